```python
import math
import jax
import jax.numpy as jnp
from jax import lax
import numpy as np

D_MODEL = 1024
BATCH = 16
SEQ = 4096
DEPTH = 2

GRID_W = 64
CTX_LEN = 256
HEAD_DIM = 64
ROPE_BASE = 10000.0
RMS_EPS = 1e-6
QBLOCK = 128
NEG_INF = -1e30
N_MOD = 6

NA_HEADS = 8
NA_KR = 8
NA_KC = 16
SWA_HEADS = 8
SWA_KV_HEADS = 2
SWA_GROUP = SWA_HEADS // SWA_KV_HEADS
SWA_WINDOW = 128
MLA_HEADS = 8
MLA_Q_RANK = 256
MLA_KV_RANK = 128
MLA_NOPE = 64
MLA_ROPE = 32
MLA_V = 64
DIFF_HEADS = 4
DIFF_QK = 64
DIFF_V = 2 * DIFF_QK
PEER_HEADS = 8
PEER_NKEYS = 128
PEER_EXPERTS = PEER_NKEYS * PEER_NKEYS
PEER_DQ = 256
PEER_TOPK = 16
PEER_TBLOCK = 128

AB_SPLITS = (NA_HEADS * HEAD_DIM,) * 3 + (SWA_HEADS * HEAD_DIM, SWA_KV_HEADS * HEAD_DIM, SWA_KV_HEADS * HEAD_DIM)
AB_IN = sum(AB_SPLITS)
CD_SPLITS = (MLA_Q_RANK, MLA_KV_RANK, MLA_ROPE, DIFF_HEADS * 2 * DIFF_QK, DIFF_HEADS * 2 * DIFF_QK, DIFF_HEADS * DIFF_V)
CD_IN = sum(CD_SPLITS)
MIX_WIDTH = NA_HEADS * HEAD_DIM + SWA_HEADS * HEAD_DIM

kernel_name = 'hybrid_prefix_na_swa_mla_diff_peer'


def rmsnorm(x, g):
    xf = x.astype(jnp.float32)
    y = xf * lax.rsqrt(jnp.mean(jnp.square(xf), axis=-1, keepdims=True) + RMS_EPS)
    return (y * g.astype(jnp.float32)).astype(x.dtype)


def split_cols(z, sizes):
    out, o = [], 0
    for s in sizes:
        out.append(z[..., o:o + s])
        o += s
    return out


def split_heads(t, n):
    return t.reshape(t.shape[:-1] + (n, t.shape[-1] // n))


def rope1d(x, pos):
    half = x.shape[-1] // 2
    freq = ROPE_BASE ** (-jnp.arange(half, dtype=jnp.float32) / half)
    ang = pos.astype(jnp.float32)[:, None] * freq[None, :]
    bshape = (pos.shape[0],) + (1,) * (x.ndim - 3) + (half,)
    cos = jnp.cos(ang).reshape(bshape).astype(x.dtype)
    sin = jnp.sin(ang).reshape(bshape).astype(x.dtype)
    x1, x2 = x[..., :half], x[..., half:]
    return jnp.concatenate([x1 * cos - x2 * sin, x1 * sin + x2 * cos], axis=-1)


def axial_rope(x, row, col):
    h = x.shape[-1] // 2
    return jnp.concatenate([rope1d(x[..., :h], row), rope1d(x[..., h:], col)], axis=-1)


def map_blocks(fn, q):
    B, T = q.shape[:2]
    nb = T // QBLOCK
    qb = jnp.moveaxis(q.reshape((B, nb, QBLOCK) + q.shape[2:]), 1, 0)
    out = lax.map(lambda a: fn(a[0], a[1]), (jnp.arange(nb, dtype=jnp.int32), qb))
    out = jnp.moveaxis(out, 0, 1)
    return out.reshape((B, T) + out.shape[3:])


def global_attention(q, k, v, scale, sink=None):
    if sink is not None:
        sink_l = sink.astype(jnp.float32).reshape(1, k.shape[2], q.shape[3], 1, 1)

    def blk(i, qb):
        s = jnp.einsum('bqkgd,bskd->bkgqs', qb, k).astype(jnp.float32) * scale
        if sink is not None:
            s = jnp.concatenate([s, jnp.broadcast_to(sink_l, s.shape[:-1] + (1,))], axis=-1)
        p = jax.nn.softmax(s, axis=-1)
        if sink is not None:
            p = p[..., :-1]
        return jnp.einsum('bkgqs,bskd->bqkgd', p.astype(v.dtype), v)

    return map_blocks(blk, q)


def neighbourhood_attention(q, k, v, kc, vc, rpb):
    B, S, H, d = q.shape
    rows = S // GRID_W
    kr = min(NA_KR, rows)
    qg = q.reshape(B, rows, GRID_W, H, d)
    kg = k.reshape(B, rows, GRID_W, H, d)
    vg = v.reshape(B, rows, GRID_W, H, d)
    cols = jnp.arange(GRID_W, dtype=jnp.int32)
    col_start = jnp.clip(cols - NA_KC // 2, 0, GRID_W - NA_KC)
    col_valid = (cols[None, :] >= col_start[:, None]) & (cols[None, :] < col_start[:, None] + NA_KC)
    col_idx = jnp.clip(cols[None, :] - cols[:, None] + NA_KC - 1, 0, 2 * NA_KC - 2)
    rpb_cols = rpb[:, :, col_idx]
    scale = d ** -0.5
    nlat = kr * GRID_W

    def row_fn(r):
        r0 = jnp.clip(r - kr // 2, 0, rows - kr)
        qr = lax.dynamic_index_in_dim(qg, r, axis=1, keepdims=False)
        kw = lax.dynamic_slice_in_dim(kg, r0, kr, axis=1)
        vw = lax.dynamic_slice_in_dim(vg, r0, kr, axis=1)
        bias = jnp.take(rpb_cols, r0 + jnp.arange(kr, dtype=jnp.int32) - r + NA_KR - 1, axis=1)
        bias = jnp.transpose(bias, (0, 2, 1, 3)).astype(jnp.float32)
        s_lat = jnp.einsum('bqhd,brkhd->bhqrk', qr, kw).astype(jnp.float32) * scale + bias
        s_lat = jnp.where(col_valid[:, None, :], s_lat, NEG_INF).reshape(B, H, GRID_W, nlat)
        s_ctx = jnp.einsum('bqhd,bchd->bhqc', qr, kc).astype(jnp.float32) * scale
        p = jax.nn.softmax(jnp.concatenate([s_lat, s_ctx], axis=-1), axis=-1).astype(v.dtype)
        p_lat = p[..., :nlat].reshape(B, H, GRID_W, kr, GRID_W)
        return (jnp.einsum('bhqrk,brkhd->bqhd', p_lat, vw)
                + jnp.einsum('bhqc,bchd->bqhd', p[..., nlat:], vc))

    out = lax.map(row_fn, jnp.arange(rows, dtype=jnp.int32))
    return jnp.moveaxis(out, 0, 1).reshape(B, S, H, d)


def window_attention(q, k, v, kc, vc, sink):
    S = q.shape[1]
    kb = QBLOCK + 2 * SWA_WINDOW
    pad = ((0, 0), (SWA_WINDOW, SWA_WINDOW), (0, 0), (0, 0))
    kp = jnp.pad(k, pad)
    vp = jnp.pad(v, pad)
    scale = q.shape[-1] ** -0.5
    nc = kc.shape[1]
    sink_l = sink.astype(jnp.float32).reshape(1, SWA_KV_HEADS, SWA_GROUP, 1, 1)

    def blk(i, qb):
        start = i * QBLOCK
        kw = lax.dynamic_slice_in_dim(kp, start, kb, axis=1)
        vw = lax.dynamic_slice_in_dim(vp, start, kb, axis=1)
        qpos = start + jnp.arange(QBLOCK, dtype=jnp.int32)
        kpos = start - SWA_WINDOW + jnp.arange(kb, dtype=jnp.int32)
        valid = ((kpos[None, :] >= 0) & (kpos[None, :] < S)
                 & (jnp.abs(kpos[None, :] - qpos[:, None]) <= SWA_WINDOW))
        s_lat = jnp.einsum('bqkgd,bskd->bkgqs', qb, kw).astype(jnp.float32) * scale
        s_lat = jnp.where(valid, s_lat, NEG_INF)
        s_ctx = jnp.einsum('bqkgd,bckd->bkgqc', qb, kc).astype(jnp.float32) * scale
        s_sink = jnp.broadcast_to(sink_l, s_ctx.shape[:-1] + (1,))
        p = jax.nn.softmax(jnp.concatenate([s_lat, s_ctx, s_sink], axis=-1), axis=-1).astype(v.dtype)
        return (jnp.einsum('bkgqs,bskd->bqkgd', p[..., :kb], vw)
                + jnp.einsum('bkgqc,bckd->bqkgd', p[..., kb:kb + nc], vc))

    return map_blocks(blk, q)


def mixer_ab(h, hc, w_in, rpb, sink, w_out, pos, need_ctx):
    B, S, _ = h.shape

    def project(z, rope_pos):
        T = z.shape[1]
        aq, ak, av, bq, bk, bv = split_cols(z, AB_SPLITS)
        aq, ak, av = split_heads(aq, NA_HEADS), split_heads(ak, NA_HEADS), split_heads(av, NA_HEADS)
        bq, bk, bv = split_heads(bq, SWA_HEADS), split_heads(bk, SWA_KV_HEADS), split_heads(bv, SWA_KV_HEADS)
        if rope_pos is not None:
            bq = axial_rope(bq, *rope_pos)
            bk = axial_rope(bk, *rope_pos)
        bq = bq.reshape(B, T, SWA_KV_HEADS, SWA_GROUP, HEAD_DIM)
        return aq, ak, av, bq, bk, bv

    aq, ak, av, bq, bk, bv = project(h @ w_in, pos)
    caq, cak, cav, cbq, cbk, cbv = project(hc @ w_in, None)
    ya = neighbourhood_attention(aq, ak, av, cak, cav, rpb)
    yb = window_attention(bq, bk, bv, cbk, cbv, sink)
    y = jnp.concatenate([ya.reshape(B, S, -1), yb.reshape(B, S, -1)], axis=-1) @ w_out
    yc = None
    if need_ctx:
        C = hc.shape[1]
        yca = global_attention(caq[:, :, :, None], cak, cav, HEAD_DIM ** -0.5)
        ycb = global_attention(cbq, cbk, cbv, HEAD_DIM ** -0.5, sink)
        yc = jnp.concatenate([yca.reshape(B, C, -1), ycb.reshape(B, C, -1)], axis=-1) @ w_out
    return y, yc


def mixer_cd(h, hc, w_in, q_norm_g, w_uq, kv_norm_g, w_ukv, lam_vecs, subln_g, w_out, pos, lam_init, need_ctx):
    B = h.shape[0]
    lv = lam_vecs.astype(jnp.float32)
    lam = jnp.exp(jnp.sum(lv[0] * lv[1])) - jnp.exp(jnp.sum(lv[2] * lv[3])) + lam_init

    def project(z, rope_pos):
        cq, ckv, kr, dq, dk, dv = split_cols(z, CD_SPLITS)
        q = split_heads(rmsnorm(cq, q_norm_g) @ w_uq, MLA_HEADS)
        kv = split_heads(rmsnorm(ckv, kv_norm_g) @ w_ukv, MLA_HEADS)
        q_nope, q_rope = q[..., :MLA_NOPE], q[..., MLA_NOPE:]
        k_nope, v = kv[..., :MLA_NOPE], kv[..., MLA_NOPE:]
        k_rope = kr[:, :, None, :]
        dq = dq.reshape(dq.shape[:2] + (DIFF_HEADS, 2, DIFF_QK))
        dk = dk.reshape(dk.shape[:2] + (DIFF_HEADS, 2, DIFF_QK))
        dv = split_heads(dv, DIFF_HEADS)
        if rope_pos is not None:
            q_rope = axial_rope(q_rope, *rope_pos)
            k_rope = axial_rope(k_rope, *rope_pos)
            dq = axial_rope(dq, *rope_pos)
            dk = axial_rope(dk, *rope_pos)
        q_mla = jnp.concatenate([q_nope, q_rope], axis=-1)
        k_mla = jnp.concatenate([k_nope, jnp.broadcast_to(k_rope, k_nope.shape[:-1] + (MLA_ROPE,))], axis=-1)
        return q_mla, k_mla, v, dq, dk, dv

    def attend(q_mla, dq, k_mla, v_mla, dk, dv):
        T = q_mla.shape[1]
        y_c = global_attention(q_mla[:, :, :, None], k_mla, v_mla, (MLA_NOPE + MLA_ROPE) ** -0.5)[:, :, :, 0]
        o1 = global_attention(dq[:, :, :, 0:1], dk[:, :, :, 0], dv, DIFF_QK ** -0.5)[:, :, :, 0]
        o2 = global_attention(dq[:, :, :, 1:2], dk[:, :, :, 1], dv, DIFF_QK ** -0.5)[:, :, :, 0]
        o = rmsnorm(o1 - lam.astype(o1.dtype) * o2, subln_g) * (1.0 - lam_init)
        return jnp.concatenate([y_c.reshape(B, T, -1), o.reshape(B, T, -1)], axis=-1) @ w_out

    q_l, k_l, v_l, dq_l, dk_l, dv_l = project(h @ w_in, pos)
    q_c, k_c, v_c, dq_c, dk_c, dv_c = project(hc @ w_in, None)
    y = attend(q_l, dq_l,
               jnp.concatenate([k_l, k_c], axis=1), jnp.concatenate([v_l, v_c], axis=1),
               jnp.concatenate([dk_l, dk_c], axis=1), jnp.concatenate([dv_l, dv_c], axis=1))
    yc = attend(q_c, dq_c, k_c, v_c, dk_c, dv_c) if need_ctx else None
    return y, yc


def peer_ffn(hx, wq, sub_keys, u, v):
    n, d = hx.shape
    hb = hx.reshape(n // PEER_TBLOCK, PEER_TBLOCK, d)

    def blk(xb):
        q = (xb @ wq).reshape(PEER_TBLOCK, PEER_HEADS, 2, PEER_DQ // 2)
        s = jnp.einsum('thpd,pnd->thpn', q, sub_keys).astype(jnp.float32)
        sv, si = lax.top_k(s, PEER_TOPK)
        cand = (sv[:, :, 0, :, None] + sv[:, :, 1, None, :]).reshape(PEER_TBLOCK, PEER_HEADS, PEER_TOPK * PEER_TOPK)
        cidx = (si[:, :, 0, :, None] * PEER_NKEYS + si[:, :, 1, None, :]).reshape(PEER_TBLOCK, PEER_HEADS, PEER_TOPK * PEER_TOPK)
        fv, fp = lax.top_k(cand, PEER_TOPK)
        eidx = jnp.take_along_axis(cidx, fp, axis=-1)
        g = jax.nn.softmax(fv, axis=-1)
        ue = jnp.take(u, eidx, axis=0)
        ve = jnp.take(v, eidx, axis=0)
        a = jnp.einsum('thkd,td->thk', ue, xb).astype(jnp.float32)
        wgt = (jax.nn.gelu(a, approximate=False) * g).astype(xb.dtype)
        return jnp.einsum('thk,thkd->td', wgt, ve)

    return lax.map(blk, hb).reshape(n, d)


def diff_lambda_init(layer_idx):
    return 0.8 - 0.6 * math.exp(-0.3 * layer_idx)


def setup_inputs(seed: int = 0) -> dict:
    key = jax.random.key(seed)
    ks = iter(jax.random.split(key, 32))
    n_even = (DEPTH + 1) // 2
    n_odd = DEPTH // 2

    def normal(shape, scale):
        return jax.random.normal(next(ks), shape, jnp.float32) * scale

    def gain(shape):
        return 1.0 + 0.02 * jax.random.normal(next(ks), shape, jnp.float32)

    D = D_MODEL
    return {
        'x': normal((BATCH, SEQ, D), 1.0),
        'c': normal((BATCH, D), 1.0),
        'ctx': normal((BATCH, CTX_LEN, D), 1.0),
        'c_ctx': normal((D,), 1.0),
        'ada_w': normal((DEPTH, D, N_MOD * D), 0.5 * D ** -0.5),
        'ada_b': normal((DEPTH, N_MOD * D), 0.02),
        'norm1_g': gain((DEPTH, D)),
        'norm2_g': gain((DEPTH, D)),
        'w_out': normal((DEPTH, MIX_WIDTH, D), MIX_WIDTH ** -0.5),
        'peer_wq': normal((DEPTH, D, PEER_HEADS * PEER_DQ), D ** -0.5),
        'peer_keys': normal((DEPTH, 2, PEER_NKEYS, PEER_DQ // 2), (PEER_DQ // 2) ** -0.5),
        'peer_u': normal((DEPTH, PEER_EXPERTS, D), D ** -0.5),
        'peer_v': normal((DEPTH, PEER_EXPERTS, D), PEER_HEADS ** -0.5),
        'ab_w_in': normal((n_even, D, AB_IN), D ** -0.5),
        'na_rpb': normal((n_even, NA_HEADS, 2 * NA_KR - 1, 2 * NA_KC - 1), 0.2),
        'swa_sink': normal((n_even, SWA_HEADS), 0.5),
        'cd_w_in': normal((n_odd, D, CD_IN), D ** -0.5),
        'mla_q_norm_g': gain((n_odd, MLA_Q_RANK)),
        'mla_w_uq': normal((n_odd, MLA_Q_RANK, MLA_HEADS * (MLA_NOPE + MLA_ROPE)), MLA_Q_RANK ** -0.5),
        'mla_kv_norm_g': gain((n_odd, MLA_KV_RANK)),
        'mla_w_ukv': normal((n_odd, MLA_KV_RANK, MLA_HEADS * (MLA_NOPE + MLA_V)), MLA_KV_RANK ** -0.5),
        'diff_lambda': normal((n_odd, 4, DIFF_QK), 0.1),
        'diff_subln_g': gain((n_odd, DIFF_V)),
        'final_norm_g': gain((D,)),
    }


def reference(x, c, ctx, c_ctx, ada_w, ada_b, norm1_g, norm2_g, w_out, peer_wq, peer_keys, peer_u, peer_v,
              ab_w_in, na_rpb, swa_sink, cd_w_in, mla_q_norm_g, mla_w_uq, mla_kv_norm_g, mla_w_ukv,
              diff_lambda, diff_subln_g, final_norm_g):
    B, S, D = x.shape
    C = ctx.shape[1]
    t = jnp.arange(S, dtype=jnp.int32)
    pos = (t // GRID_W, t % GRID_W)
    for l in range(DEPTH):
        last = l == DEPTH - 1
        m = jax.nn.silu(c) @ ada_w[l] + ada_b[l]
        mc = jax.nn.silu(c_ctx) @ ada_w[l] + ada_b[l]
        sh1, sc1, g1, sh2, sc2, g2 = split_cols(m[:, None, :], (D,) * N_MOD)
        csh1, csc1, cg1, csh2, csc2, cg2 = split_cols(mc, (D,) * N_MOD)
        h = rmsnorm(x, norm1_g[l]) * (1 + sc1) + sh1
        hc = rmsnorm(ctx, norm1_g[l]) * (1 + csc1) + csh1
        j = l // 2
        if l % 2 == 0:
            y, yc = mixer_ab(h, hc, ab_w_in[j], na_rpb[j], swa_sink[j], w_out[l], pos, not last)
        else:
            y, yc = mixer_cd(h, hc, cd_w_in[j], mla_q_norm_g[j], mla_w_uq[j], mla_kv_norm_g[j], mla_w_ukv[j],
                             diff_lambda[j], diff_subln_g[j], w_out[l], pos, diff_lambda_init(l), not last)
        x = x + g1 * y
        h2 = rmsnorm(x, norm2_g[l]) * (1 + sc2) + sh2
        if last:
            f = peer_ffn(h2.reshape(B * S, D), peer_wq[l], peer_keys[l], peer_u[l], peer_v[l])
            x = x + g2 * f.reshape(B, S, D)
        else:
            ctx = ctx + cg1 * yc
            h2c = rmsnorm(ctx, norm2_g[l]) * (1 + csc2) + csh2
            f = peer_ffn(jnp.concatenate([h2.reshape(B * S, D), h2c.reshape(B * C, D)], axis=0),
                         peer_wq[l], peer_keys[l], peer_u[l], peer_v[l])
            x = x + g2 * f[:B * S].reshape(B, S, D)
            ctx = ctx + cg2 * f[B * S:].reshape(B, C, D)
    return rmsnorm(x, final_norm_g)
```

```python
import functools
import math

import jax
import jax.numpy as jnp
from jax import lax
from jax.experimental import pallas as pl
from jax.experimental.pallas import tpu as pltpu

_F32 = jnp.float32
_MXU = jnp.bfloat16
_NEG = -1e30

GRID_W = 64
HEAD_DIM = 64
ROPE_BASE = 10000.0
RMS_EPS = 1e-6
N_MOD = 6
NA_KR, NA_KC = 8, 16
SWA_WINDOW = 128
MLA_NOPE, MLA_ROPE = 64, 32
PEER_HEADS, PEER_NKEYS, PEER_TOPK = 8, 128, 16

LANE = 128
TM = 512
TQ_LOCAL = 256
NA_GROUP_ROWS = TQ_LOCAL // GRID_W
NA_WIN_ROWS = NA_KR + NA_GROUP_ROWS - 1
PEER_EB = 512
VMEM_LIMIT = 56 * 1024 * 1024


def _cparams(sem, vmem=VMEM_LIMIT):
    return pltpu.CompilerParams(dimension_semantics=sem, vmem_limit_bytes=vmem)


def _dot(a, b):
    return jnp.dot(a, b, preferred_element_type=_F32)


def _dot_nt(a, b):
    return lax.dot_general(a, b, (((1,), (1,)), ((), ())), preferred_element_type=_F32)


def _rms(x, g):
    return x * lax.rsqrt(jnp.mean(x * x, axis=-1, keepdims=True) + RMS_EPS) * g


def _lane_half(shape):
    return lax.broadcasted_iota(jnp.int32, shape, len(shape) - 1) >= (LANE // 2)


def _mod_kernel(c_ref, w_ref, b_ref, o_ref):
    c = c_ref[...]
    a = c * jax.nn.sigmoid(c)
    w = w_ref[0]
    a_hi = a.astype(_MXU)
    a_lo = (a - a_hi.astype(_F32)).astype(_MXU)
    w_hi = w.astype(_MXU)
    w_lo = (w - w_hi.astype(_F32)).astype(_MXU)
    o_ref[0] = _dot(a_hi, w_hi) + _dot(a_lo, w_hi) + _dot(a_hi, w_lo) + b_ref[0]


def _modulation(cc, ada_w, ada_b):
    depth, d, n = ada_w.shape
    rows = cc.shape[0]
    tn = 768
    return pl.pallas_call(
        _mod_kernel,
        grid=(depth, n // tn),
        in_specs=[pl.BlockSpec((rows, d), lambda l, j: (0, 0)),
                  pl.BlockSpec((1, d, tn), lambda l, j: (l, 0, j)),
                  pl.BlockSpec((1, 1, tn), lambda l, j: (l, 0, j))],
        out_specs=pl.BlockSpec((1, rows, tn), lambda l, j: (l, 0, j)),
        out_shape=jax.ShapeDtypeStruct((depth, rows, n), _F32),
        compiler_params=_cparams(("parallel", "parallel")),
        name="adaln_mod",
    )(cc, ada_w, ada_b.reshape(depth, 1, n))


def _rope_tables(seq, d):
    t = jnp.arange(seq, dtype=jnp.int32)
    q = d // 4
    freq = ROPE_BASE ** (-jnp.arange(q, dtype=_F32) / q)

    def one(pos):
        ang = pos.astype(_F32)[:, None] * freq[None, :]
        return jnp.concatenate([jnp.cos(ang)] * 2, -1), jnp.concatenate([jnp.sin(ang)] * 2, -1)

    cr, sr = one(t // GRID_W)
    cc, sc = one(t % GRID_W)
    return jnp.concatenate([cr, cc], -1), jnp.concatenate([sr, sc], -1)


def _rot_cols(w, d):
    k, n = w.shape
    w5 = w.reshape(k, n // d, 2, 2, d // 4)
    return jnp.stack([-w5[:, :, :, 1], w5[:, :, :, 0]], axis=3).reshape(k, n)


def _pad_table(tab, width, ident):
    s, w = tab.shape
    if w < width:
        reps = width // w
        tab = jnp.tile(tab, (1, reps))
    return jnp.concatenate([tab, jnp.full((TM, width), ident, _F32)], axis=0)


def _proj_ab_kernel(x_ref, mod_ref, g_ref, cos_ref, sin_ref, w_ref,
                    aq_ref, ak_ref, av_ref, bq_ref, bk_ref, bv_ref):
    x = x_ref[...]
    h = _rms(x, g_ref[...]) * (1.0 + mod_ref[0, 1:2, :]) + mod_ref[0, 0:1, :]
    hb = h.astype(_MXU)
    cos = cos_ref[...]
    sin = sin_ref[...]

    def proj(lo, n):
        return _dot(hb, w_ref[:, lo:lo + n])

    scale = HEAD_DIM ** -0.5
    aq_ref[...] = (proj(0, 512) * scale).astype(aq_ref.dtype)
    ak_ref[...] = proj(512, 512).astype(ak_ref.dtype)
    av_ref[...] = proj(1024, 512).astype(av_ref.dtype)
    cos4 = jnp.tile(cos, (1, 4))
    sin4 = jnp.tile(sin, (1, 4))
    bq_ref[...] = ((proj(1536, 512) * cos4 + proj(2048, 512) * sin4) * scale).astype(bq_ref.dtype)
    cos2 = jnp.tile(cos, (1, 2))
    sin2 = jnp.tile(sin, (1, 2))
    bk_ref[...] = (proj(2560, 256) * cos2 + proj(2816, 256) * sin2).astype(bk_ref.dtype)
    bv_ref[...] = proj(3072, 256).astype(bv_ref.dtype)


def _proj_ab(x, mod, mod_row, g, cos, sin, tab_row, w):
    n, d = x.shape
    widths = (512, 512, 512, 512, 256, 256)
    return pl.pallas_call(
        _proj_ab_kernel,
        grid=(n // TM,),
        in_specs=[pl.BlockSpec((TM, d), lambda i: (i, 0)),
                  pl.BlockSpec((1, N_MOD, d), lambda i: (mod_row(i), 0, 0)),
                  pl.BlockSpec((1, d), lambda i: (0, 0)),
                  pl.BlockSpec((TM, LANE), lambda i: (tab_row(i), 0)),
                  pl.BlockSpec((TM, LANE), lambda i: (tab_row(i), 0)),
                  pl.BlockSpec(w.shape, lambda i: (0, 0))],
        out_specs=[pl.BlockSpec((TM, wd), lambda i: (i, 0)) for wd in widths],
        out_shape=[jax.ShapeDtypeStruct((n, wd), _MXU) for wd in widths],
        compiler_params=_cparams(("parallel",)),
        name="proj_ab",
    )(x, mod, g, cos, sin, w)


def _na_kernel(q_ref, k_ref, v_ref, kc_ref, vc_ref, bias_ref, o_ref, *, rows):
    g = pl.program_id(1)
    ks = jnp.clip(NA_GROUP_ROWS * g - NA_KR // 2, 0, rows - NA_WIN_ROWS)
    start = pl.multiple_of(ks * GRID_W, GRID_W)
    nwin = NA_WIN_ROWS * GRID_W
    hi = _lane_half((TQ_LOCAL, LANE))
    for m in range(4):
        cols = slice(m * LANE, (m + 1) * LANE)
        q2 = q_ref[:, cols]
        k2 = k_ref[pl.ds(start, nwin), cols]
        v2 = v_ref[pl.ds(start, nwin), cols]
        kc2 = kc_ref[:, cols]
        vc2 = vc_ref[:, cols]
        outs = []
        for half in range(2):
            qh = jnp.where(hi == (half == 1), q2, jnp.zeros_like(q2))
            s_lat = _dot_nt(qh, k2) + bias_ref[0, 2 * m + half]
            s_ctx = _dot_nt(qh, kc2)
            mx = jnp.maximum(jnp.max(s_lat, axis=-1, keepdims=True), jnp.max(s_ctx, axis=-1, keepdims=True))
            p_lat = jnp.exp(s_lat - mx)
            p_ctx = jnp.exp(s_ctx - mx)
            den = jnp.sum(p_lat, axis=-1, keepdims=True) + jnp.sum(p_ctx, axis=-1, keepdims=True)
            o = _dot(p_lat.astype(_MXU), v2) + _dot(p_ctx.astype(_MXU), vc2)
            outs.append(o / den)
        o_ref[:, cols] = jnp.where(hi, outs[1], outs[0]).astype(o_ref.dtype)


def _na_bias_tables(rpb, rows):
    gq = NA_GROUP_ROWS
    n_groups = rows // gq
    ql = jnp.arange(gq)[:, None]
    kl = jnp.arange(NA_WIN_ROWS)[None, :]
    qc = jnp.arange(GRID_W)[:, None]
    kc = jnp.arange(GRID_W)[None, :]
    cs = jnp.clip(qc - NA_KC // 2, 0, GRID_W - NA_KC)
    col_valid = (kc >= cs) & (kc < cs + NA_KC)
    col_idx = jnp.clip(kc - qc + NA_KC - 1, 0, 2 * NA_KC - 2)
    tabs = []
    for g in (0, 1, n_groups - 1):
        ks = min(max(gq * g - NA_KR // 2, 0), rows - NA_WIN_ROWS)
        rq = gq * g + ql
        rk = ks + kl
        r0 = jnp.clip(rq - NA_KR // 2, 0, rows - NA_KR)
        row_valid = (rk >= r0) & (rk < r0 + NA_KR)
        row_idx = jnp.clip(rk - rq + NA_KR - 1, 0, 2 * NA_KR - 2)
        b = rpb[:, row_idx[:, :, None, None], col_idx[None, None, :, :]]
        valid = row_valid[:, :, None, None] & col_valid[None, None, :, :]
        b = jnp.where(valid[None], b.astype(_F32), _NEG)
        b = jnp.transpose(b, (0, 1, 3, 2, 4)).reshape(rpb.shape[0], gq * GRID_W, NA_WIN_ROWS * GRID_W)
        tabs.append(b)
    return jnp.stack(tabs)


def _na_attention(aq, ak, av, cak, cav, bias, batch, seq, ctx_len):
    rows = seq // GRID_W
    n_groups = seq // TQ_LOCAL
    nwin = NA_WIN_ROWS * GRID_W

    def bias_row(b, g):
        return (jnp.where(g == 0, 0, jnp.where(g == n_groups - 1, 2, 1)), 0, 0, 0)

    return pl.pallas_call(
        functools.partial(_na_kernel, rows=rows),
        grid=(batch, n_groups),
        in_specs=[pl.BlockSpec((TQ_LOCAL, 512), lambda b, g: (b * n_groups + g, 0)),
                  pl.BlockSpec((seq, 512), lambda b, g: (b, 0)),
                  pl.BlockSpec((seq, 512), lambda b, g: (b, 0)),
                  pl.BlockSpec((ctx_len, 512), lambda b, g: (b, 0)),
                  pl.BlockSpec((ctx_len, 512), lambda b, g: (b, 0)),
                  pl.BlockSpec((1, 8, TQ_LOCAL, nwin), bias_row)],
        out_specs=pl.BlockSpec((TQ_LOCAL, 512), lambda b, g: (b * n_groups + g, 0)),
        out_shape=jax.ShapeDtypeStruct((batch * seq, 512), _MXU),
        compiler_params=_cparams(("parallel", "arbitrary")),
        name="na_attention",
    )(aq, ak, av, cak, cav, bias)


def _swa_kernel(sink_ref, q_ref, k_ref, v_ref, kc_ref, vc_ref, o_ref, *, seq):
    t = pl.program_id(1)
    kwin = TQ_LOCAL + 2 * SWA_WINDOW
    start = t * TQ_LOCAL
    kstart = pl.multiple_of(jnp.clip(start - SWA_WINDOW, 0, seq - kwin), LANE)
    qpos = start + lax.broadcasted_iota(jnp.int32, (TQ_LOCAL, kwin), 0)
    kpos = kstart + lax.broadcasted_iota(jnp.int32, (TQ_LOCAL, kwin), 1)
    mask = jnp.where(jnp.abs(kpos - qpos) <= SWA_WINDOW, 0.0, _NEG).astype(_F32)
    hi = _lane_half((TQ_LOCAL, LANE))
    for m in range(4):
        cols = slice(m * LANE, (m + 1) * LANE)
        kv = m // 2
        kcols = slice(kv * LANE, (kv + 1) * LANE)
        q2 = q_ref[:, cols]
        k2 = k_ref[pl.ds(kstart, kwin), kcols]
        v2 = v_ref[pl.ds(kstart, kwin), kcols]
        kc2 = kc_ref[:, kcols]
        vc2 = vc_ref[:, kcols]
        outs = []
        for half in range(2):
            sink = sink_ref[2 * m + half]
            qh = jnp.where(hi == (half == 1), q2, jnp.zeros_like(q2))
            s_lat = _dot_nt(qh, k2) + mask
            s_ctx = _dot_nt(qh, kc2)
            mx = jnp.maximum(jnp.max(s_lat, axis=-1, keepdims=True), jnp.max(s_ctx, axis=-1, keepdims=True))
            mx = jnp.maximum(mx, sink)
            p_lat = jnp.exp(s_lat - mx)
            p_ctx = jnp.exp(s_ctx - mx)
            den = (jnp.sum(p_lat, axis=-1, keepdims=True) + jnp.sum(p_ctx, axis=-1, keepdims=True)
                   + jnp.exp(sink - mx))
            o = _dot(p_lat.astype(_MXU), v2) + _dot(p_ctx.astype(_MXU), vc2)
            outs.append(o / den)
        o_ref[:, cols] = jnp.where(hi, outs[1], outs[0]).astype(o_ref.dtype)


def _swa_attention(sink, bq, bk2, bv2, cbk2, cbv2, batch, seq, ctx_len):
    n_t = seq // TQ_LOCAL
    return pl.pallas_call(
        functools.partial(_swa_kernel, seq=seq),
        grid=(batch, n_t),
        in_specs=[pl.BlockSpec(memory_space=pltpu.SMEM),
                  pl.BlockSpec((TQ_LOCAL, 512), lambda b, t: (b * n_t + t, 0)),
                  pl.BlockSpec((seq, 256), lambda b, t: (b, 0)),
                  pl.BlockSpec((seq, 256), lambda b, t: (b, 0)),
                  pl.BlockSpec((ctx_len, 256), lambda b, t: (b, 0)),
                  pl.BlockSpec((ctx_len, 256), lambda b, t: (b, 0))],
        out_specs=pl.BlockSpec((TQ_LOCAL, 512), lambda b, t: (b * n_t + t, 0)),
        out_shape=jax.ShapeDtypeStruct((batch * seq, 512), _MXU),
        compiler_params=_cparams(("parallel", "arbitrary")),
        name="swa_attention",
    )(sink, bq, bk2, bv2, cbk2, cbv2)


def _flash_kernel(*refs, heads, outs, has_ctx, has_sink, diff_cfg, n_kv):
    it = iter(refs)
    sink_ref = next(it) if has_sink else None
    q_ref, k_ref, v_ref = next(it), next(it), next(it)
    kc_ref = next(it) if has_ctx else None
    vc_ref = next(it) if has_ctx else None
    lam_ref = next(it) if diff_cfg else None
    sg_ref = next(it) if diff_cfg else None
    o_ref = next(it)
    m_ref, l_ref, acc_ref = next(it), next(it), next(it)
    kv = pl.program_id(2)
    tq = q_ref.shape[0]
    hi = _lane_half((tq, LANE))

    @pl.when(kv == 0)
    def _init():
        for h in range(len(heads)):
            if has_sink:
                m_ref[h] = jnp.full((tq, LANE), sink_ref[h], _F32)
                l_ref[h] = jnp.ones((tq, LANE), _F32)
            else:
                m_ref[h] = jnp.full((tq, LANE), _NEG, _F32)
                l_ref[h] = jnp.zeros((tq, LANE), _F32)
            acc_ref[h] = jnp.zeros((tq, LANE), _F32)

    def attend(kr, vr):
        for h, (qb, qhalf, kb, vb) in enumerate(heads):
            q2 = q_ref[:, qb * LANE:(qb + 1) * LANE]
            if qhalf is not None:
                q2 = jnp.where(hi == (qhalf == 1), q2, jnp.zeros_like(q2))
            s = _dot_nt(q2, kr[:, kb * LANE:(kb + 1) * LANE])
            m_old = m_ref[h][:, :1]
            m_new = jnp.maximum(m_old, jnp.max(s, axis=-1, keepdims=True))
            alpha = jnp.exp(m_old - m_new)
            p = jnp.exp(s - m_new)
            l_ref[h] = jnp.broadcast_to(alpha * l_ref[h][:, :1] + jnp.sum(p, axis=-1, keepdims=True), (tq, LANE))
            acc_ref[h] = alpha * acc_ref[h] + _dot(p.astype(_MXU), vr[:, vb * LANE:(vb + 1) * LANE])
            m_ref[h] = jnp.broadcast_to(m_new, (tq, LANE))

    if has_ctx:
        @pl.when(kv == 0)
        def _ctx():
            attend(kc_ref, vc_ref)

    attend(k_ref, v_ref)

    @pl.when(kv == n_kv - 1)
    def _fin():
        def head_out(h):
            return acc_ref[h] / l_ref[h][:, :1]

        for j, spec in enumerate(outs):
            if spec[0] == "full":
                o = head_out(spec[1])
            elif spec[0] == "pair":
                o = jnp.where(hi, head_out(spec[2]), head_out(spec[1]))
            else:
                lv = lam_ref[...]
                lam = (jnp.exp(jnp.sum(lv[0:1] * lv[1:2], axis=-1, keepdims=True))
                       - jnp.exp(jnp.sum(lv[2:3] * lv[3:4], axis=-1, keepdims=True)) + diff_cfg)
                o = _rms(head_out(spec[1]) - lam * head_out(spec[2]), sg_ref[...]) * (1.0 - diff_cfg)
            o_ref[:, j * LANE:(j + 1) * LANE] = o.astype(o_ref.dtype)


def _flash_attention(q, k, v, *, heads, outs, batch, q_per_batch, kv_per_batch, tq, tk,
                     q_row0=0, kv_row0=0, ctx=None, ctx_len=0, ctx_row0=0, sink=None, diff=None, name="flash"):
    n_q = q_per_batch // tq
    n_kv = kv_per_batch // tk
    qw, kw, vw = q.shape[1], k.shape[1], v.shape[1]
    args, in_specs = [], []
    if sink is not None:
        args.append(sink)
        in_specs.append(pl.BlockSpec(memory_space=pltpu.SMEM))
    q0, k0 = q_row0 // tq, kv_row0 // tk
    args += [q, k, v]
    in_specs += [pl.BlockSpec((tq, qw), lambda b, i, j: (q0 + b * n_q + i, 0)),
                 pl.BlockSpec((tk, kw), lambda b, i, j: (k0 + b * n_kv + j, 0)),
                 pl.BlockSpec((tk, vw), lambda b, i, j: (k0 + b * n_kv + j, 0))]
    if ctx is not None:
        c0 = ctx_row0 // ctx_len
        args += [ctx[0], ctx[1]]
        in_specs += [pl.BlockSpec((ctx_len, kw), lambda b, i, j: (c0 + b, 0)),
                     pl.BlockSpec((ctx_len, vw), lambda b, i, j: (c0 + b, 0))]
    diff_cfg = None
    if diff is not None:
        lam_vecs, subln_g, diff_cfg = diff
        args += [lam_vecs, subln_g]
        in_specs += [pl.BlockSpec(lam_vecs.shape, lambda b, i, j: (0, 0)),
                     pl.BlockSpec(subln_g.shape, lambda b, i, j: (0, 0))]
    nh = len(heads)
    ow = len(outs) * LANE
    return pl.pallas_call(
        functools.partial(_flash_kernel, heads=tuple(heads), outs=tuple(outs), has_ctx=ctx is not None,
                          has_sink=sink is not None, diff_cfg=diff_cfg, n_kv=n_kv),
        grid=(batch, n_q, n_kv),
        in_specs=in_specs,
        out_specs=pl.BlockSpec((tq, ow), lambda b, i, j: (b * n_q + i, 0)),
        out_shape=jax.ShapeDtypeStruct((batch * q_per_batch, ow), _MXU),
        scratch_shapes=[pltpu.VMEM((nh, tq, LANE), _F32)] * 3,
        compiler_params=_cparams(("parallel", "parallel", "arbitrary")),
        name=name,
    )(*args)


def _out_kernel(x_ref, ya_ref, yb_ref, mod_ref, g2_ref, wa_ref, wb_ref, xo_ref, h2_ref):
    y = _dot(ya_ref[...], wa_ref[...]) + _dot(yb_ref[...], wb_ref[...])
    x = x_ref[...] + mod_ref[0, 2:3, :] * y
    xo_ref[...] = x
    h2 = _rms(x, g2_ref[...]) * (1.0 + mod_ref[0, 4:5, :]) + mod_ref[0, 3:4, :]
    h2_ref[...] = h2.astype(h2_ref.dtype)


def _out_proj(x, ya, yb, mod, mod_row, g2, wa, wb):
    n, d = x.shape
    return pl.pallas_call(
        _out_kernel,
        grid=(n // TM,),
        in_specs=[pl.BlockSpec((TM, d), lambda i: (i, 0)),
                  pl.BlockSpec((TM, ya.shape[1]), lambda i: (i, 0)),
                  pl.BlockSpec((TM, yb.shape[1]), lambda i: (i, 0)),
                  pl.BlockSpec((1, N_MOD, d), lambda i: (mod_row(i), 0, 0)),
                  pl.BlockSpec((1, d), lambda i: (0, 0)),
                  pl.BlockSpec(wa.shape, lambda i: (0, 0)),
                  pl.BlockSpec(wb.shape, lambda i: (0, 0))],
        out_specs=[pl.BlockSpec((TM, d), lambda i: (i, 0)), pl.BlockSpec((TM, d), lambda i: (i, 0))],
        out_shape=[jax.ShapeDtypeStruct((n, d), _F32), jax.ShapeDtypeStruct((n, d), _MXU)],
        compiler_params=_cparams(("parallel",)),
        name="out_proj",
    )(x, ya, yb, mod, g2, wa, wb)


def _proj_cd_kernel(x_ref, mod_ref, g_ref, cq_ref, sq_ref, cd_ref, sd_ref, w_ref, qg_ref, kvg_ref,
                    wqa_ref, wqb_ref, wk_ref, wv_ref, place_ref,
                    qm_ref, km_ref, vm_ref, dq_ref, dk_ref, dv_ref):
    x = x_ref[...]
    h = _rms(x, g_ref[...]) * (1.0 + mod_ref[0, 1:2, :]) + mod_ref[0, 0:1, :]
    hb = h.astype(_MXU)

    def proj(lo, n):
        return _dot(hb, w_ref[:, lo:lo + n])

    cosq, sinq = cq_ref[...], sq_ref[...]
    cosd, sind = cd_ref[...], sd_ref[...]
    cqn = _rms(proj(0, 256), qg_ref[...]).astype(_MXU)
    ckvn = _rms(proj(256, 128), kvg_ref[...]).astype(_MXU)
    cos8, sin8 = jnp.tile(cosq, (1, 8)), jnp.tile(sinq, (1, 8))
    qm = _dot(cqn, wqa_ref[...]) * cos8 + _dot(cqn, wqb_ref[...]) * sin8
    qm_ref[...] = (qm * ((MLA_NOPE + MLA_ROPE) ** -0.5)).astype(qm_ref.dtype)
    kr = (proj(384, 128) * cosq + proj(512, 128) * sinq).astype(_MXU)
    km_ref[...] = (_dot(ckvn, wk_ref[...]) + _dot(kr, place_ref[...])).astype(km_ref.dtype)
    vm_ref[...] = _dot(ckvn, wv_ref[...]).astype(vm_ref.dtype)
    cos4, sin4 = jnp.tile(cosd, (1, 4)), jnp.tile(sind, (1, 4))
    dq_ref[...] = ((proj(640, 512) * cos4 + proj(1152, 512) * sin4) * (HEAD_DIM ** -0.5)).astype(dq_ref.dtype)
    dk_ref[...] = (proj(1664, 512) * cos4 + proj(2176, 512) * sin4).astype(dk_ref.dtype)
    dv_ref[...] = proj(2688, 512).astype(dv_ref.dtype)


def _proj_cd(x, mod, mod_row, g, tabs, tab_row, w, qg, kvg, wqa, wqb, wk, wv, place):
    n, d = x.shape
    widths = (1024, 1024, 512, 512, 512, 512)
    full = lambda a: pl.BlockSpec(a.shape, lambda i: (0, 0))
    return pl.pallas_call(
        _proj_cd_kernel,
        grid=(n // TM,),
        in_specs=[pl.BlockSpec((TM, d), lambda i: (i, 0)),
                  pl.BlockSpec((1, N_MOD, d), lambda i: (mod_row(i), 0, 0)),
                  pl.BlockSpec((1, d), lambda i: (0, 0))]
                 + [pl.BlockSpec((TM, LANE), lambda i: (tab_row(i), 0))] * 4
                 + [full(a) for a in (w, qg, kvg, wqa, wqb, wk, wv, place)],
        out_specs=[pl.BlockSpec((TM, wd), lambda i: (i, 0)) for wd in widths],
        out_shape=[jax.ShapeDtypeStruct((n, wd), _MXU) for wd in widths],
        compiler_params=_cparams(("parallel",)),
        name="proj_cd",
    )(x, mod, g, *tabs, w, qg, kvg, wqa, wqb, wk, wv, place)


def _top16(x):
    r = x.shape[0]
    row = lax.broadcasted_iota(jnp.int32, x.shape, 0)
    row16 = lax.broadcasted_iota(jnp.int32, (PEER_TOPK, LANE), 0)
    rank = jnp.full(x.shape, float(PEER_TOPK), _F32)
    vals = jnp.zeros((PEER_TOPK, LANE), _F32)
    for k in range(PEER_TOPK):
        m = jnp.max(x, axis=0, keepdims=True)
        idx = jnp.min(jnp.where(x == m, row, r), axis=0, keepdims=True)
        hit = row == idx
        x = jnp.where(hit, -jnp.inf, x)
        rank = jnp.where(hit, float(k), rank)
        vals = jnp.where(row16 == k, m, vals)
    return rank, vals


def _staircase():
    return [(r, c) for r in range(PEER_TOPK) for c in range(PEER_TOPK) if (r + 1) * (c + 1) <= PEER_TOPK]


def _peer_select(sa, sb):
    ra, av = _top16(sa)
    rb, bv = _top16(sb)
    pairs = _staircase()
    n = len(pairs)
    npad = -(-n // 8) * 8
    rows = [av[r:r + 1] + bv[c:c + 1] for r, c in pairs]
    rows += [jnp.full((1, LANE), -jnp.inf, _F32)] * (npad - n)
    cand = jnp.concatenate(rows, axis=0)
    top = av[0:1] + bv[0:1]
    e_cand = jnp.exp(cand - top)
    crow = lax.broadcasted_iota(jnp.int32, cand.shape, 0)
    sel = jnp.zeros(cand.shape, _F32)
    for _ in range(PEER_TOPK):
        m = jnp.max(cand, axis=0, keepdims=True)
        idx = jnp.min(jnp.where(cand == m, crow, npad), axis=0, keepdims=True)
        hit = crow == idx
        cand = jnp.where(hit, -jnp.inf, cand)
        sel = jnp.where(hit, 1.0, sel)
    z = jnp.sum(sel * e_cand, axis=0, keepdims=True)
    lr = jnp.zeros(sa.shape, _F32)
    pos = 0
    for r in range(PEER_TOPK):
        cnt = sum(1 for p in pairs if p[0] == r)
        l_r = jnp.sum(sel[pos:pos + cnt], axis=0, keepdims=True)
        pos += cnt
        lr = jnp.where(ra == float(r), l_r, lr)
    ea = jnp.exp(sa - av[0:1]) / z
    eb = jnp.exp(sb - bv[0:1])
    return lr, ea, rb, eb


def _peer_kernel(h2_ref, x_ref, mod_ref, wqt_ref, keys_ref, u_ref, vt_ref, *rest, n_eb, final):
    if final:
        fg_ref, o_ref, h2t_ref, lr_ref, ea_ref, rb_ref, eb_ref, ft_ref, w_ref = rest
    else:
        o_ref, h2t_ref, lr_ref, ea_ref, rb_ref, eb_ref, ft_ref, w_ref = rest
    e = pl.program_id(1)
    n_chunks = TM // LANE
    i_per = PEER_EB // PEER_NKEYS

    @pl.when(e == 0)
    def _select():
        h2t = h2_ref[...].astype(_F32).T.astype(_MXU)
        h2t_ref[...] = h2t
        ft_ref[...] = jnp.zeros_like(ft_ref)
        for hp in range(2 * PEER_HEADS):
            qt = _dot(wqt_ref[hp * LANE:(hp + 1) * LANE, :], h2t).astype(_MXU)
            st = _dot(keys_ref[hp % 2], qt)
            for c in range(n_chunks):
                (lr_ref if hp % 2 == 0 else rb_ref)[c, hp // 2] = st[:, c * LANE:(c + 1) * LANE]

        def body(t, carry):
            c, h = t // PEER_HEADS, t % PEER_HEADS
            lr, ea, rb, eb = _peer_select(lr_ref[c, h], rb_ref[c, h])
            lr_ref[c, h] = lr
            ea_ref[c, h] = ea
            rb_ref[c, h] = rb
            eb_ref[c, h] = eb
            return carry

        lax.fori_loop(0, n_chunks * PEER_HEADS, body, 0)

    at = _dot(u_ref[...], h2t_ref[...])
    for ii in range(i_per):
        i = e * i_per + ii
        for c in range(n_chunks):
            a = at[ii * PEER_NKEYS:(ii + 1) * PEER_NKEYS, c * LANE:(c + 1) * LANE]
            g = jnp.zeros((PEER_NKEYS, LANE), _F32)
            for h in range(PEER_HEADS):
                lr_i = lr_ref[c, h, pl.ds(i, 1), :]
                ea_i = ea_ref[c, h, pl.ds(i, 1), :]
                g = g + jnp.where(rb_ref[c, h] < lr_i, eb_ref[c, h], 0.0) * ea_i
            gelu = 0.5 * a * (1.0 + lax.erf(a * (2.0 ** -0.5)))
            w_ref[ii * PEER_NKEYS:(ii + 1) * PEER_NKEYS, c * LANE:(c + 1) * LANE] = (gelu * g).astype(w_ref.dtype)
    ft_ref[...] += _dot(vt_ref[...], w_ref[...])

    @pl.when(e == n_eb - 1)
    def _fin():
        x = x_ref[...] + mod_ref[0, 5:6, :] * ft_ref[...].T
        if final:
            x = _rms(x, fg_ref[...])
        o_ref[...] = x


def _peer(h2, x, mod, mod_row, wqt, keys, u, vt, final_g=None):
    n, d = x.shape
    n_exp = u.shape[0]
    n_eb = n_exp // PEER_EB
    n_chunks = TM // LANE
    final = final_g is not None
    args = [h2, x, mod, wqt, keys, u, vt]
    in_specs = [pl.BlockSpec((TM, d), lambda i, e: (i, 0)),
                pl.BlockSpec((TM, d), lambda i, e: (i, 0)),
                pl.BlockSpec((1, N_MOD, d), lambda i, e: (mod_row(i), 0, 0)),
                pl.BlockSpec(wqt.shape, lambda i, e: (0, 0)),
                pl.BlockSpec(keys.shape, lambda i, e: (0, 0, 0)),
                pl.BlockSpec((PEER_EB, d), lambda i, e: (e, 0)),
                pl.BlockSpec((d, PEER_EB), lambda i, e: (0, e))]
    if final:
        args.append(final_g)
        in_specs.append(pl.BlockSpec((1, d), lambda i, e: (0, 0)))
    tab = pltpu.VMEM((n_chunks, PEER_HEADS, PEER_NKEYS, LANE), _F32)
    return pl.pallas_call(
        functools.partial(_peer_kernel, n_eb=n_eb, final=final),
        grid=(n // TM, n_eb),
        in_specs=in_specs,
        out_specs=pl.BlockSpec((TM, d), lambda i, e: (i, 0)),
        out_shape=jax.ShapeDtypeStruct((n, d), _F32),
        scratch_shapes=[pltpu.VMEM((d, TM), _MXU), tab, tab, tab, tab,
                        pltpu.VMEM((d, TM), _F32), pltpu.VMEM((PEER_EB, TM), _MXU)],
        compiler_params=_cparams(("parallel", "arbitrary")),
        name="peer_ffn",
    )(*args)


def _dup_halves(w):
    a, b = w[:, :HEAD_DIM], w[:, HEAD_DIM:]
    return jnp.concatenate([a, a, b, b], axis=1)


def _prep_ab(w_in):
    aq, ak, av = w_in[:, 0:512], w_in[:, 512:1024], w_in[:, 1024:1536]
    bq, bk, bv = w_in[:, 1536:2048], w_in[:, 2048:2176], w_in[:, 2176:2304]
    cat = [aq, ak, av, bq, _rot_cols(bq, HEAD_DIM), _dup_halves(bk), _dup_halves(_rot_cols(bk, HEAD_DIM)),
           _dup_halves(bv)]
    return jnp.concatenate(cat, axis=1).astype(_MXU)


def _prep_cd(w_in, w_uq, w_ukv):
    d = w_in.shape[0]
    cq, ckv, kr = w_in[:, 0:256], w_in[:, 256:384], w_in[:, 384:416]
    dq, dk, dv = w_in[:, 416:928], w_in[:, 928:1440], w_in[:, 1440:1952]
    z64, z32 = jnp.zeros((d, 64), _F32), jnp.zeros((d, 32), _F32)
    kr128 = jnp.concatenate([z64, kr, z32], axis=1)
    krrot128 = jnp.concatenate([z64, _rot_cols(kr, MLA_ROPE), z32], axis=1)
    w = jnp.concatenate([cq, ckv, kr128, krrot128, dq, _rot_cols(dq, HEAD_DIM), dk, _rot_cols(dk, HEAD_DIM), dv],
                        axis=1).astype(_MXU)
    r = w_uq.shape[0]
    uq = w_uq.reshape(r, 8, MLA_NOPE + MLA_ROPE)
    nope, rope = uq[:, :, :MLA_NOPE], uq[:, :, MLA_NOPE:]
    rope_rot = _rot_cols(rope.reshape(r, 8 * MLA_ROPE), MLA_ROPE).reshape(r, 8, MLA_ROPE)
    zq = jnp.zeros((r, 8, 32), _F32)
    wqa = jnp.concatenate([nope, rope, zq], axis=2).reshape(r, 1024).astype(_MXU)
    wqb = jnp.concatenate([jnp.zeros_like(nope), rope_rot, zq], axis=2).reshape(r, 1024).astype(_MXU)
    rk = w_ukv.shape[0]
    ukv = w_ukv.reshape(rk, 8, 128)
    wk = jnp.concatenate([ukv[:, :, :MLA_NOPE], jnp.zeros((rk, 8, 64), _F32)], axis=2).reshape(rk, 1024).astype(_MXU)
    wv = ukv[:, :, MLA_NOPE:].reshape(rk, 512).astype(_MXU)
    lane = jnp.arange(LANE)
    src = (lane >= MLA_NOPE) & (lane < MLA_NOPE + MLA_ROPE)
    place = (src[:, None] & (lane[:, None] == (jnp.arange(1024)[None, :] % LANE))).astype(_MXU)
    return w, wqa, wqb, wk, wv, place


def kernel(x, c, ctx, c_ctx, ada_w, ada_b, norm1_g, norm2_g, w_out, peer_wq, peer_keys, peer_u, peer_v,
           ab_w_in, na_rpb, swa_sink, cd_w_in, mla_q_norm_g, mla_w_uq, mla_kv_norm_g, mla_w_ukv,
           diff_lambda, diff_subln_g, final_norm_g):
    batch, seq, d = x.shape
    ctx_len = ctx.shape[1]
    depth = ada_w.shape[0]
    assert seq % TM == 0 and (batch * ctx_len) % TM == 0 and seq % TQ_LOCAL == 0
    assert depth == 2, "even layers keep a context stream, the single odd layer is the last one"
    n_lat, n_ctx = batch * seq, batch * ctx_len
    xs = x.reshape(n_lat, d)
    cs = ctx.reshape(n_ctx, d)

    mod_rows = -(-(batch + 1) // 16) * 16
    cc = jnp.zeros((mod_rows, d), _F32).at[:batch].set(c).at[batch].set(c_ctx)
    mod_all = _modulation(cc, ada_w, ada_b)

    tiles_per_seq = seq // TM
    lat_row = lambda i: i // tiles_per_seq
    ctx_row = lambda i: batch
    lat_tab = lambda i: i % tiles_per_seq
    ctx_tab = lambda i: tiles_per_seq

    cos64, sin64 = _rope_tables(seq, HEAD_DIM)
    cos64p, sin64p = _pad_table(cos64, LANE, 1.0), _pad_table(sin64, LANE, 0.0)

    for l in range(depth):
        last = l == depth - 1
        j = l // 2
        mod = mod_all[l, :batch + 1].reshape(batch + 1, N_MOD, d)
        g1 = norm1_g[l].reshape(1, d)
        g2 = norm2_g[l].reshape(1, d)
        wo = w_out[l].astype(_MXU)
        wqt = peer_wq[l].T.astype(_MXU)
        keys = peer_keys[l].astype(_MXU)
        u = peer_u[l].astype(_MXU)
        vt = peer_v[l].T.astype(_MXU)
        if l % 2 == 0:
            w = _prep_ab(ab_w_in[j])
            aq, ak, av, bq, bk2, bv2 = _proj_ab(xs, mod, lat_row, g1, cos64p, sin64p, lat_tab, w)
            caq, cak, cav, cbq, cbk2, cbv2 = _proj_ab(cs, mod, ctx_row, g1, cos64p, sin64p, ctx_tab, w)
            bias = _na_bias_tables(na_rpb[j], seq // GRID_W)
            ya = _na_attention(aq, ak, av, cak, cav, bias, batch, seq, ctx_len)
            sink = swa_sink[j].astype(_F32)
            yb = _swa_attention(sink, bq, bk2, bv2, cbk2, cbv2, batch, seq, ctx_len)
            xs, h2 = _out_proj(xs, ya, yb, mod, lat_row, g2, wo[:512], wo[512:])
            if not last:
                pair_heads = [(m, half, m, m) for m in range(4) for half in range(2)]
                pair_outs = [("pair", 2 * m, 2 * m + 1) for m in range(4)]
                yca = _flash_attention(caq, cak, cav, heads=pair_heads, outs=pair_outs, batch=batch,
                                       q_per_batch=ctx_len, kv_per_batch=ctx_len, tq=ctx_len, tk=ctx_len,
                                       name="ctx_attn_a")
                gqa_heads = [(m, half, m // 2, m // 2) for m in range(4) for half in range(2)]
                ycb = _flash_attention(cbq, cbk2, cbv2, heads=gqa_heads, outs=pair_outs, batch=batch,
                                       q_per_batch=ctx_len, kv_per_batch=ctx_len, tq=ctx_len, tk=ctx_len,
                                       sink=sink, name="ctx_attn_b")
                cs, h2c = _out_proj(cs, yca, ycb, mod, ctx_row, g2, wo[:512], wo[512:])
        else:
            lam_init = 0.8 - 0.6 * math.exp(-0.3 * l)
            w, wqa, wqb, wk, wv, place = _prep_cd(cd_w_in[j], mla_w_uq[j], mla_w_ukv[j])
            cos32, sin32 = _rope_tables(seq, MLA_ROPE)
            ones64, zeros64 = jnp.ones((seq, 64), _F32), jnp.zeros((seq, 64), _F32)
            cosq = _pad_table(jnp.concatenate([ones64, cos32, ones64[:, :32]], 1), LANE, 1.0)
            sinq = _pad_table(jnp.concatenate([zeros64, sin32, zeros64[:, :32]], 1), LANE, 0.0)
            tabs = (cosq, sinq, cos64p, sin64p)
            qg = mla_q_norm_g[j].reshape(1, -1)
            kvg = mla_kv_norm_g[j].reshape(1, -1)
            qm, km, vm, dq, dk, dv = _proj_cd(xs, mod, lat_row, g1, tabs, lat_tab, w, qg, kvg, wqa, wqb, wk, wv, place)
            _, ckm, cvm, _, cdk, cdv = _proj_cd(cs, mod, ctx_row, g1, tabs, ctx_tab, w, qg, kvg, wqa, wqb, wk, wv, place)
            tq = 512
            tk = 512
            mla_heads = [(h, None, h, h // 2) for h in range(8)]
            pair_outs = [("pair", 2 * m, 2 * m + 1) for m in range(4)]
            yc = _flash_attention(qm, km, vm, heads=mla_heads, outs=pair_outs, batch=batch, q_per_batch=seq,
                                  kv_per_batch=seq, tq=tq, tk=tk, ctx=(ckm, cvm), ctx_len=ctx_len, name="mla_attn")
            diff_heads = [(h, a, h, h) for h in range(4) for a in range(2)]
            diff_outs = [("diff", 2 * h, 2 * h + 1) for h in range(4)]
            od = _flash_attention(dq, dk, dv, heads=diff_heads, outs=diff_outs, batch=batch, q_per_batch=seq,
                                  kv_per_batch=seq, tq=tq, tk=tk, ctx=(cdk, cdv), ctx_len=ctx_len,
                                  diff=(diff_lambda[j].astype(_F32), diff_subln_g[j].reshape(1, -1), lam_init),
                                  name="diff_attn")
            xs, h2 = _out_proj(xs, yc, od, mod, lat_row, g2, wo[:512], wo[512:])
        fg = final_norm_g.reshape(1, d) if last else None
        xs = _peer(h2, xs, mod, lat_row, wqt, keys, u, vt, final_g=fg)
        if not last:
            cs = _peer(h2c, cs, mod, ctx_row, wqt, keys, u, vt)
    return xs.reshape(batch, seq, d)
```

```python
import functools
import math

import jax
import jax.numpy as jnp
from jax import lax
from jax.experimental import pallas as pl
from jax.experimental.pallas import tpu as pltpu

_F32 = jnp.float32
_MXU = jnp.bfloat16
_NEG = -1e30

GRID_W = 64
HEAD_DIM = 64
ROPE_BASE = 10000.0
RMS_EPS = 1e-6
N_MOD = 6
NA_KR, NA_KC = 8, 16
SWA_WINDOW = 128
MLA_NOPE, MLA_ROPE = 64, 32
PEER_HEADS, PEER_NKEYS, PEER_TOPK = 8, 128, 16

LANE = 128
TM = 512
TQ_LOCAL = 256
NA_GROUP_ROWS = TQ_LOCAL // GRID_W
NA_WIN_ROWS = NA_KR + NA_GROUP_ROWS - 1
GLOBAL_TQ = 512
GLOBAL_TK = 512
PEER_EB = 512
VMEM_LIMIT = 56 * 1024 * 1024


def _cparams(sem, vmem=VMEM_LIMIT):
    return pltpu.CompilerParams(dimension_semantics=sem, vmem_limit_bytes=vmem)


def _dot(a, b):
    return jnp.dot(a, b, preferred_element_type=_F32)


def _dot_nt(a, b):
    return lax.dot_general(a, b, (((1,), (1,)), ((), ())), preferred_element_type=_F32)


def _rms(x, g):
    return x * lax.rsqrt(jnp.mean(x * x, axis=-1, keepdims=True) + RMS_EPS) * g


def _lane_half(shape):
    return lax.broadcasted_iota(jnp.int32, shape, len(shape) - 1) >= (LANE // 2)


def _mod_kernel(c_ref, w_ref, b_ref, o_ref):
    c = c_ref[...]
    a = c * jax.nn.sigmoid(c)
    w = w_ref[0]
    a_hi = a.astype(_MXU)
    a_lo = (a - a_hi.astype(_F32)).astype(_MXU)
    w_hi = w.astype(_MXU)
    w_lo = (w - w_hi.astype(_F32)).astype(_MXU)
    o_ref[0] = _dot(a_hi, w_hi) + _dot(a_lo, w_hi) + _dot(a_hi, w_lo) + b_ref[0]


def _modulation(cc, ada_w, ada_b):
    depth, d, n = ada_w.shape
    rows = cc.shape[0]
    tn = 768
    return pl.pallas_call(
        _mod_kernel,
        grid=(depth, n // tn),
        in_specs=[pl.BlockSpec((rows, d), lambda l, j: (0, 0)),
                  pl.BlockSpec((1, d, tn), lambda l, j: (l, 0, j)),
                  pl.BlockSpec((1, 1, tn), lambda l, j: (l, 0, j))],
        out_specs=pl.BlockSpec((1, rows, tn), lambda l, j: (l, 0, j)),
        out_shape=jax.ShapeDtypeStruct((depth, rows, n), _F32),
        compiler_params=_cparams(("parallel", "parallel")),
        name="adaln_mod",
    )(cc, ada_w, ada_b.reshape(depth, 1, n))


def _rope_tables(seq, d):
    t = jnp.arange(seq, dtype=jnp.int32)
    q = d // 4
    freq = ROPE_BASE ** (-jnp.arange(q, dtype=_F32) / q)

    def one(pos):
        ang = pos.astype(_F32)[:, None] * freq[None, :]
        return jnp.concatenate([jnp.cos(ang)] * 2, -1), jnp.concatenate([jnp.sin(ang)] * 2, -1)

    cr, sr = one(t // GRID_W)
    cc, sc = one(t % GRID_W)
    return jnp.concatenate([cr, cc], -1), jnp.concatenate([sr, sc], -1)


def _rot_cols(w, d):
    k, n = w.shape
    w5 = w.reshape(k, n // d, 2, 2, d // 4)
    return jnp.stack([-w5[:, :, :, 1], w5[:, :, :, 0]], axis=3).reshape(k, n)


def _pad_table(tab, width, ident):
    s, w = tab.shape
    if w < width:
        reps = width // w
        tab = jnp.tile(tab, (1, reps))
    return jnp.concatenate([tab, jnp.full((TM, width), ident, _F32)], axis=0)


def _proj_ab_kernel(x_ref, mod_ref, g_ref, cos_ref, sin_ref, w_ref,
                    aq_ref, ak_ref, av_ref, bq_ref, bk_ref, bv_ref):
    x = x_ref[...]
    h = _rms(x, g_ref[...]) * (1.0 + mod_ref[0, 1:2, :]) + mod_ref[0, 0:1, :]
    hb = h.astype(_MXU)
    cos = cos_ref[...]
    sin = sin_ref[...]

    def proj(lo, n):
        return _dot(hb, w_ref[:, lo:lo + n])

    scale = HEAD_DIM ** -0.5
    aq_ref[...] = (proj(0, 512) * scale).astype(aq_ref.dtype)
    ak_ref[...] = proj(512, 512).astype(ak_ref.dtype)
    av_ref[...] = proj(1024, 512).astype(av_ref.dtype)
    cos4 = jnp.tile(cos, (1, 4))
    sin4 = jnp.tile(sin, (1, 4))
    bq_ref[...] = ((proj(1536, 512) * cos4 + proj(2048, 512) * sin4) * scale).astype(bq_ref.dtype)
    cos2 = jnp.tile(cos, (1, 2))
    sin2 = jnp.tile(sin, (1, 2))
    bk_ref[...] = (proj(2560, 256) * cos2 + proj(2816, 256) * sin2).astype(bk_ref.dtype)
    bv_ref[...] = proj(3072, 256).astype(bv_ref.dtype)


def _proj_ab(x, mod, mod_row, g, cos, sin, tab_row, w):
    n, d = x.shape
    widths = (512, 512, 512, 512, 256, 256)
    return pl.pallas_call(
        _proj_ab_kernel,
        grid=(n // TM,),
        in_specs=[pl.BlockSpec((TM, d), lambda i: (i, 0)),
                  pl.BlockSpec((1, N_MOD, d), lambda i: (mod_row(i), 0, 0)),
                  pl.BlockSpec((1, d), lambda i: (0, 0)),
                  pl.BlockSpec((TM, LANE), lambda i: (tab_row(i), 0)),
                  pl.BlockSpec((TM, LANE), lambda i: (tab_row(i), 0)),
                  pl.BlockSpec(w.shape, lambda i: (0, 0))],
        out_specs=[pl.BlockSpec((TM, wd), lambda i: (i, 0)) for wd in widths],
        out_shape=[jax.ShapeDtypeStruct((n, wd), _MXU) for wd in widths],
        compiler_params=_cparams(("parallel",)),
        name="proj_ab",
    )(x, mod, g, cos, sin, w)


def _na_kernel(q_ref, k_ref, v_ref, kc_ref, vc_ref, bias_ref, o_ref, *, rows):
    g = pl.program_id(1)
    ks = jnp.clip(NA_GROUP_ROWS * g - NA_KR // 2, 0, rows - NA_WIN_ROWS)
    start = pl.multiple_of(ks * GRID_W, GRID_W)
    nwin = NA_WIN_ROWS * GRID_W
    hi = _lane_half((TQ_LOCAL, LANE))
    for m in range(4):
        cols = slice(m * LANE, (m + 1) * LANE)
        q2 = q_ref[:, cols]
        k2 = k_ref[pl.ds(start, nwin), cols]
        v2 = v_ref[pl.ds(start, nwin), cols]
        kc2 = kc_ref[:, cols]
        vc2 = vc_ref[:, cols]
        outs = []
        for half in range(2):
            qh = jnp.where(hi == (half == 1), q2, jnp.zeros_like(q2))
            s_lat = _dot_nt(qh, k2) + bias_ref[0, 2 * m + half]
            s_ctx = _dot_nt(qh, kc2)
            mx = jnp.maximum(jnp.max(s_lat, axis=-1, keepdims=True), jnp.max(s_ctx, axis=-1, keepdims=True))
            p_lat = jnp.exp(s_lat - mx)
            p_ctx = jnp.exp(s_ctx - mx)
            den = jnp.sum(p_lat, axis=-1, keepdims=True) + jnp.sum(p_ctx, axis=-1, keepdims=True)
            o = _dot(p_lat.astype(_MXU), v2) + _dot(p_ctx.astype(_MXU), vc2)
            outs.append(o / den)
        o_ref[:, cols] = jnp.where(hi, outs[1], outs[0]).astype(o_ref.dtype)


def _na_bias_tables(rpb, rows):
    gq = NA_GROUP_ROWS
    n_groups = rows // gq
    ql = jnp.arange(gq)[:, None]
    kl = jnp.arange(NA_WIN_ROWS)[None, :]
    qc = jnp.arange(GRID_W)[:, None]
    kc = jnp.arange(GRID_W)[None, :]
    cs = jnp.clip(qc - NA_KC // 2, 0, GRID_W - NA_KC)
    col_valid = (kc >= cs) & (kc < cs + NA_KC)
    col_idx = jnp.clip(kc - qc + NA_KC - 1, 0, 2 * NA_KC - 2)
    exact = lax.Precision.HIGHEST
    oh_c = jax.nn.one_hot(col_idx, 2 * NA_KC - 1, dtype=_F32)
    rpb_cols = jnp.einsum("hab,xyb->haxy", rpb.astype(_F32), oh_c, precision=exact)
    tabs = []
    for g in (0, 1, n_groups - 1):
        ks = min(max(gq * g - NA_KR // 2, 0), rows - NA_WIN_ROWS)
        rq = gq * g + ql
        rk = ks + kl
        r0 = jnp.clip(rq - NA_KR // 2, 0, rows - NA_KR)
        row_valid = (rk >= r0) & (rk < r0 + NA_KR)
        row_idx = jnp.clip(rk - rq + NA_KR - 1, 0, 2 * NA_KR - 2)
        oh_r = jax.nn.one_hot(row_idx, 2 * NA_KR - 1, dtype=_F32)
        b = jnp.einsum("qka,haxy->hqxky", oh_r, rpb_cols, precision=exact)
        valid = row_valid[:, None, :, None] & col_valid[None, :, None, :]
        b = jnp.where(valid[None], b, _NEG)
        tabs.append(b.reshape(rpb.shape[0], gq * GRID_W, NA_WIN_ROWS * GRID_W))
    return jnp.stack(tabs)


def _na_attention(aq, ak, av, cak, cav, bias, batch, seq, ctx_len):
    rows = seq // GRID_W
    n_groups = seq // TQ_LOCAL
    nwin = NA_WIN_ROWS * GRID_W

    def bias_row(b, g):
        return (jnp.where(g == 0, 0, jnp.where(g == n_groups - 1, 2, 1)), 0, 0, 0)

    return pl.pallas_call(
        functools.partial(_na_kernel, rows=rows),
        grid=(batch, n_groups),
        in_specs=[pl.BlockSpec((TQ_LOCAL, 512), lambda b, g: (b * n_groups + g, 0)),
                  pl.BlockSpec((seq, 512), lambda b, g: (b, 0)),
                  pl.BlockSpec((seq, 512), lambda b, g: (b, 0)),
                  pl.BlockSpec((ctx_len, 512), lambda b, g: (b, 0)),
                  pl.BlockSpec((ctx_len, 512), lambda b, g: (b, 0)),
                  pl.BlockSpec((1, 8, TQ_LOCAL, nwin), bias_row)],
        out_specs=pl.BlockSpec((TQ_LOCAL, 512), lambda b, g: (b * n_groups + g, 0)),
        out_shape=jax.ShapeDtypeStruct((batch * seq, 512), _MXU),
        compiler_params=_cparams(("parallel", "arbitrary")),
        name="na_attention",
    )(aq, ak, av, cak, cav, bias)


def _swa_kernel(sink_ref, q_ref, k_ref, v_ref, kc_ref, vc_ref, o_ref, *, seq):
    t = pl.program_id(1)
    kwin = TQ_LOCAL + 2 * SWA_WINDOW
    start = t * TQ_LOCAL
    kstart = pl.multiple_of(jnp.clip(start - SWA_WINDOW, 0, seq - kwin), LANE)
    qpos = start + lax.broadcasted_iota(jnp.int32, (TQ_LOCAL, kwin), 0)
    kpos = kstart + lax.broadcasted_iota(jnp.int32, (TQ_LOCAL, kwin), 1)
    mask = jnp.where(jnp.abs(kpos - qpos) <= SWA_WINDOW, 0.0, _NEG).astype(_F32)
    hi = _lane_half((TQ_LOCAL, LANE))
    for m in range(4):
        cols = slice(m * LANE, (m + 1) * LANE)
        kv = m // 2
        kcols = slice(kv * LANE, (kv + 1) * LANE)
        q2 = q_ref[:, cols]
        k2 = k_ref[pl.ds(kstart, kwin), kcols]
        v2 = v_ref[pl.ds(kstart, kwin), kcols]
        kc2 = kc_ref[:, kcols]
        vc2 = vc_ref[:, kcols]
        outs = []
        for half in range(2):
            sink = sink_ref[2 * m + half]
            qh = jnp.where(hi == (half == 1), q2, jnp.zeros_like(q2))
            s_lat = _dot_nt(qh, k2) + mask
            s_ctx = _dot_nt(qh, kc2)
            mx = jnp.maximum(jnp.max(s_lat, axis=-1, keepdims=True), jnp.max(s_ctx, axis=-1, keepdims=True))
            mx = jnp.maximum(mx, sink)
            p_lat = jnp.exp(s_lat - mx)
            p_ctx = jnp.exp(s_ctx - mx)
            den = (jnp.sum(p_lat, axis=-1, keepdims=True) + jnp.sum(p_ctx, axis=-1, keepdims=True)
                   + jnp.exp(sink - mx))
            o = _dot(p_lat.astype(_MXU), v2) + _dot(p_ctx.astype(_MXU), vc2)
            outs.append(o / den)
        o_ref[:, cols] = jnp.where(hi, outs[1], outs[0]).astype(o_ref.dtype)


def _swa_attention(sink, bq, bk2, bv2, cbk2, cbv2, batch, seq, ctx_len):
    n_t = seq // TQ_LOCAL
    return pl.pallas_call(
        functools.partial(_swa_kernel, seq=seq),
        grid=(batch, n_t),
        in_specs=[pl.BlockSpec(memory_space=pltpu.SMEM),
                  pl.BlockSpec((TQ_LOCAL, 512), lambda b, t: (b * n_t + t, 0)),
                  pl.BlockSpec((seq, 256), lambda b, t: (b, 0)),
                  pl.BlockSpec((seq, 256), lambda b, t: (b, 0)),
                  pl.BlockSpec((ctx_len, 256), lambda b, t: (b, 0)),
                  pl.BlockSpec((ctx_len, 256), lambda b, t: (b, 0))],
        out_specs=pl.BlockSpec((TQ_LOCAL, 512), lambda b, t: (b * n_t + t, 0)),
        out_shape=jax.ShapeDtypeStruct((batch * seq, 512), _MXU),
        compiler_params=_cparams(("parallel", "arbitrary")),
        name="swa_attention",
    )(sink, bq, bk2, bv2, cbk2, cbv2)


def _flash_kernel(*refs, heads, outs, has_ctx, has_sink, diff_cfg, n_kv):
    it = iter(refs)
    sink_ref = next(it) if has_sink else None
    q_ref, k_ref, v_ref = next(it), next(it), next(it)
    kc_ref = next(it) if has_ctx else None
    vc_ref = next(it) if has_ctx else None
    lam_ref = next(it) if diff_cfg else None
    sg_ref = next(it) if diff_cfg else None
    o_ref = next(it)
    m_ref, l_ref, acc_ref = next(it), next(it), next(it)
    kv = pl.program_id(2)
    tq = q_ref.shape[0]
    hi = _lane_half((tq, LANE))

    @pl.when(kv == 0)
    def _init():
        for h in range(len(heads)):
            if has_sink:
                m_ref[h] = jnp.full((tq, LANE), sink_ref[h], _F32)
                l_ref[h] = jnp.ones((tq, LANE), _F32)
            else:
                m_ref[h] = jnp.full((tq, LANE), _NEG, _F32)
                l_ref[h] = jnp.zeros((tq, LANE), _F32)
            acc_ref[h] = jnp.zeros((tq, LANE), _F32)

    def attend(kr, vr):
        for h, (qb, qhalf, kb, vb) in enumerate(heads):
            q2 = q_ref[:, qb * LANE:(qb + 1) * LANE]
            if qhalf is not None:
                q2 = jnp.where(hi == (qhalf == 1), q2, jnp.zeros_like(q2))
            s = _dot_nt(q2, kr[:, kb * LANE:(kb + 1) * LANE])
            m_old = m_ref[h][:, :1]
            m_new = jnp.maximum(m_old, jnp.max(s, axis=-1, keepdims=True))
            alpha = jnp.exp(m_old - m_new)
            p = jnp.exp(s - m_new)
            l_ref[h] = jnp.broadcast_to(alpha * l_ref[h][:, :1] + jnp.sum(p, axis=-1, keepdims=True), (tq, LANE))
            acc_ref[h] = alpha * acc_ref[h] + _dot(p.astype(_MXU), vr[:, vb * LANE:(vb + 1) * LANE])
            m_ref[h] = jnp.broadcast_to(m_new, (tq, LANE))

    if has_ctx:
        @pl.when(kv == 0)
        def _ctx():
            attend(kc_ref, vc_ref)

    attend(k_ref, v_ref)

    @pl.when(kv == n_kv - 1)
    def _fin():
        def head_out(h):
            return acc_ref[h] / l_ref[h][:, :1]

        for j, spec in enumerate(outs):
            if spec[0] == "full":
                o = head_out(spec[1])
            elif spec[0] == "pair":
                o = jnp.where(hi, head_out(spec[2]), head_out(spec[1]))
            else:
                lv = lam_ref[...]
                lam = (jnp.exp(jnp.sum(lv[0:1] * lv[1:2], axis=-1, keepdims=True))
                       - jnp.exp(jnp.sum(lv[2:3] * lv[3:4], axis=-1, keepdims=True)) + diff_cfg)
                o = _rms(head_out(spec[1]) - lam * head_out(spec[2]), sg_ref[...]) * (1.0 - diff_cfg)
            o_ref[:, j * LANE:(j + 1) * LANE] = o.astype(o_ref.dtype)


def _flash_attention(q, k, v, *, heads, outs, batch, q_per_batch, kv_per_batch, tq, tk,
                     q_row0=0, kv_row0=0, ctx=None, ctx_len=0, ctx_row0=0, sink=None, diff=None, name="flash"):
    n_q = q_per_batch // tq
    n_kv = kv_per_batch // tk
    qw, kw, vw = q.shape[1], k.shape[1], v.shape[1]
    args, in_specs = [], []
    if sink is not None:
        args.append(sink)
        in_specs.append(pl.BlockSpec(memory_space=pltpu.SMEM))
    q0, k0 = q_row0 // tq, kv_row0 // tk
    args += [q, k, v]
    in_specs += [pl.BlockSpec((tq, qw), lambda b, i, j: (q0 + b * n_q + i, 0)),
                 pl.BlockSpec((tk, kw), lambda b, i, j: (k0 + b * n_kv + j, 0)),
                 pl.BlockSpec((tk, vw), lambda b, i, j: (k0 + b * n_kv + j, 0))]
    if ctx is not None:
        c0 = ctx_row0 // ctx_len
        args += [ctx[0], ctx[1]]
        in_specs += [pl.BlockSpec((ctx_len, kw), lambda b, i, j: (c0 + b, 0)),
                     pl.BlockSpec((ctx_len, vw), lambda b, i, j: (c0 + b, 0))]
    diff_cfg = None
    if diff is not None:
        lam_vecs, subln_g, diff_cfg = diff
        args += [lam_vecs, subln_g]
        in_specs += [pl.BlockSpec(lam_vecs.shape, lambda b, i, j: (0, 0)),
                     pl.BlockSpec(subln_g.shape, lambda b, i, j: (0, 0))]
    nh = len(heads)
    ow = len(outs) * LANE
    return pl.pallas_call(
        functools.partial(_flash_kernel, heads=tuple(heads), outs=tuple(outs), has_ctx=ctx is not None,
                          has_sink=sink is not None, diff_cfg=diff_cfg, n_kv=n_kv),
        grid=(batch, n_q, n_kv),
        in_specs=in_specs,
        out_specs=pl.BlockSpec((tq, ow), lambda b, i, j: (b * n_q + i, 0)),
        out_shape=jax.ShapeDtypeStruct((batch * q_per_batch, ow), _MXU),
        scratch_shapes=[pltpu.VMEM((nh, tq, LANE), _F32)] * 3,
        compiler_params=_cparams(("parallel", "parallel", "arbitrary")),
        name=name,
    )(*args)


def _flasht_kernel(*refs, heads, outs, diff_cfg, n_kv):
    it = iter(refs)
    qt_ref, k_ref, vt_ref, kc_ref, vct_ref = next(it), next(it), next(it), next(it), next(it)
    lam_ref = next(it) if diff_cfg else None
    sg_ref = next(it) if diff_cfg else None
    o_ref = next(it)
    m_ref, l_ref, acc_ref = next(it), next(it), next(it)
    kv = pl.program_id(2)
    tq = qt_ref.shape[1]
    row_hi = lax.broadcasted_iota(jnp.int32, (LANE, tq), 0) >= (LANE // 2)

    @pl.when(kv == 0)
    def _init():
        m_ref[...] = jnp.full(m_ref.shape, _NEG, _F32)
        l_ref[...] = jnp.zeros(l_ref.shape, _F32)
        acc_ref[...] = jnp.zeros(acc_ref.shape, _F32)

    def attend(kr, vtr):
        for h, (qb, qhalf, kb, vb) in enumerate(heads):
            qt = qt_ref[qb * LANE:(qb + 1) * LANE, :]
            if qhalf is not None:
                qt = jnp.where(row_hi == (qhalf == 1), qt, jnp.zeros_like(qt))
            s = _dot(kr[:, kb * LANE:(kb + 1) * LANE], qt)
            m_old = m_ref[h, 0:1, :]
            m_new = jnp.maximum(m_old, jnp.max(s, axis=0, keepdims=True))
            alpha = jnp.exp(m_old - m_new)
            p = jnp.exp(s - m_new)
            l_new = alpha * l_ref[h, 0:1, :] + jnp.sum(p, axis=0, keepdims=True)
            acc_ref[h] = alpha * acc_ref[h] + _dot(vtr[vb * LANE:(vb + 1) * LANE, :], p.astype(_MXU))
            l_ref[h] = jnp.broadcast_to(l_new, (8, tq))
            m_ref[h] = jnp.broadcast_to(m_new, (8, tq))

    @pl.when(kv == 0)
    def _ctx():
        attend(kc_ref, vct_ref)

    attend(k_ref, vt_ref)

    @pl.when(kv == n_kv - 1)
    def _fin():
        def head_out(h):
            return acc_ref[h] / l_ref[h, 0:1, :]

        for j, spec in enumerate(outs):
            if spec[0] == "full":
                o = head_out(spec[1]).T
            elif spec[0] == "pair":
                o = jnp.where(row_hi, head_out(spec[2]), head_out(spec[1])).T
            else:
                lv = lam_ref[...]
                lam = (jnp.exp(jnp.sum(lv[0:1] * lv[1:2], axis=-1, keepdims=True))
                       - jnp.exp(jnp.sum(lv[2:3] * lv[3:4], axis=-1, keepdims=True)) + diff_cfg)
                o = _rms((head_out(spec[1]) - lam * head_out(spec[2])).T, sg_ref[...]) * (1.0 - diff_cfg)
            o_ref[:, j * LANE:(j + 1) * LANE] = o.astype(o_ref.dtype)


def _flasht_attention(qt, k, vt, kc, vct, *, heads, outs, batch, seq, ctx_len, tq, tk, diff=None, name="flasht"):
    n_q, n_kv = seq // tq, seq // tk
    qw, kw, vw = qt.shape[0], k.shape[1], vt.shape[0]
    args = [qt, k, vt, kc, vct]
    in_specs = [pl.BlockSpec((qw, tq), lambda b, i, j: (0, b * n_q + i)),
                pl.BlockSpec((tk, kw), lambda b, i, j: (b * n_kv + j, 0)),
                pl.BlockSpec((vw, tk), lambda b, i, j: (0, b * n_kv + j)),
                pl.BlockSpec((ctx_len, kw), lambda b, i, j: (b, 0)),
                pl.BlockSpec((vw, ctx_len), lambda b, i, j: (0, b))]
    diff_cfg = None
    if diff is not None:
        lam_vecs, subln_g, diff_cfg = diff
        args += [lam_vecs, subln_g]
        in_specs += [pl.BlockSpec(lam_vecs.shape, lambda b, i, j: (0, 0)),
                     pl.BlockSpec(subln_g.shape, lambda b, i, j: (0, 0))]
    nh = len(heads)
    ow = len(outs) * LANE
    return pl.pallas_call(
        functools.partial(_flasht_kernel, heads=tuple(heads), outs=tuple(outs), diff_cfg=diff_cfg, n_kv=n_kv),
        grid=(batch, n_q, n_kv),
        in_specs=in_specs,
        out_specs=pl.BlockSpec((tq, ow), lambda b, i, j: (b * n_q + i, 0)),
        out_shape=jax.ShapeDtypeStruct((batch * seq, ow), _MXU),
        scratch_shapes=[pltpu.VMEM((nh, 8, tq), _F32), pltpu.VMEM((nh, 8, tq), _F32),
                        pltpu.VMEM((nh, LANE, tq), _F32)],
        compiler_params=_cparams(("parallel", "parallel", "arbitrary")),
        name=name,
    )(*args)


def _out_kernel(x_ref, ya_ref, yb_ref, mod_ref, g2_ref, wa_ref, wb_ref, xo_ref, h2_ref):
    y = _dot(ya_ref[...], wa_ref[...]) + _dot(yb_ref[...], wb_ref[...])
    x = x_ref[...] + mod_ref[0, 2:3, :] * y
    xo_ref[...] = x
    h2 = _rms(x, g2_ref[...]) * (1.0 + mod_ref[0, 4:5, :]) + mod_ref[0, 3:4, :]
    h2_ref[...] = h2.astype(h2_ref.dtype)


def _out_proj(x, ya, yb, mod, mod_row, g2, wa, wb):
    n, d = x.shape
    return pl.pallas_call(
        _out_kernel,
        grid=(n // TM,),
        in_specs=[pl.BlockSpec((TM, d), lambda i: (i, 0)),
                  pl.BlockSpec((TM, ya.shape[1]), lambda i: (i, 0)),
                  pl.BlockSpec((TM, yb.shape[1]), lambda i: (i, 0)),
                  pl.BlockSpec((1, N_MOD, d), lambda i: (mod_row(i), 0, 0)),
                  pl.BlockSpec((1, d), lambda i: (0, 0)),
                  pl.BlockSpec(wa.shape, lambda i: (0, 0)),
                  pl.BlockSpec(wb.shape, lambda i: (0, 0))],
        out_specs=[pl.BlockSpec((TM, d), lambda i: (i, 0)), pl.BlockSpec((TM, d), lambda i: (i, 0))],
        out_shape=[jax.ShapeDtypeStruct((n, d), _F32), jax.ShapeDtypeStruct((n, d), _MXU)],
        compiler_params=_cparams(("parallel",)),
        name="out_proj",
    )(x, ya, yb, mod, g2, wa, wb)


def _proj_cd_kernel(x_ref, mod_ref, g_ref, cq_ref, sq_ref, cd_ref, sd_ref, w_ref, qg_ref, kvg_ref,
                    wqa_ref, wqb_ref, wk_ref, wv_ref, place_ref,
                    qm_ref, km_ref, vm_ref, dq_ref, dk_ref, dv_ref):
    x = x_ref[...]
    h = _rms(x, g_ref[...]) * (1.0 + mod_ref[0, 1:2, :]) + mod_ref[0, 0:1, :]
    hb = h.astype(_MXU)

    def proj(lo, n):
        return _dot(hb, w_ref[:, lo:lo + n])

    cosq, sinq = cq_ref[...], sq_ref[...]
    cosd, sind = cd_ref[...], sd_ref[...]
    cqn = _rms(proj(0, 256), qg_ref[...]).astype(_MXU)
    ckvn = _rms(proj(256, 128), kvg_ref[...]).astype(_MXU)
    cos8, sin8 = jnp.tile(cosq, (1, 8)), jnp.tile(sinq, (1, 8))
    qm = _dot(cqn, wqa_ref[...]) * cos8 + _dot(cqn, wqb_ref[...]) * sin8
    qm_ref[...] = (qm * ((MLA_NOPE + MLA_ROPE) ** -0.5)).T.astype(qm_ref.dtype)
    kr = (proj(384, 128) * cosq + proj(512, 128) * sinq).astype(_MXU)
    km_ref[...] = (_dot(ckvn, wk_ref[...]) + _dot(kr, place_ref[...])).astype(km_ref.dtype)
    vm_ref[...] = _dot(ckvn, wv_ref[...]).T.astype(vm_ref.dtype)
    cos4, sin4 = jnp.tile(cosd, (1, 4)), jnp.tile(sind, (1, 4))
    dq_ref[...] = ((proj(640, 512) * cos4 + proj(1152, 512) * sin4) * (HEAD_DIM ** -0.5)).T.astype(dq_ref.dtype)
    dk_ref[...] = (proj(1664, 512) * cos4 + proj(2176, 512) * sin4).astype(dk_ref.dtype)
    dv_ref[...] = proj(2688, 512).T.astype(dv_ref.dtype)


def _proj_cd(x, mod, mod_row, g, tabs, tab_row, w, qg, kvg, wqa, wqb, wk, wv, place):
    n, d = x.shape
    outs = ((1024, True), (1024, False), (512, True), (512, True), (512, False), (512, True))
    full = lambda a: pl.BlockSpec(a.shape, lambda i: (0, 0))
    return pl.pallas_call(
        _proj_cd_kernel,
        grid=(n // TM,),
        in_specs=[pl.BlockSpec((TM, d), lambda i: (i, 0)),
                  pl.BlockSpec((1, N_MOD, d), lambda i: (mod_row(i), 0, 0)),
                  pl.BlockSpec((1, d), lambda i: (0, 0))]
                 + [pl.BlockSpec((TM, LANE), lambda i: (tab_row(i), 0))] * 4
                 + [full(a) for a in (w, qg, kvg, wqa, wqb, wk, wv, place)],
        out_specs=[pl.BlockSpec((wd, TM), lambda i: (0, i)) if fm else pl.BlockSpec((TM, wd), lambda i: (i, 0))
                   for wd, fm in outs],
        out_shape=[jax.ShapeDtypeStruct((wd, n) if fm else (n, wd), _MXU) for wd, fm in outs],
        compiler_params=_cparams(("parallel",)),
        name="proj_cd",
    )(x, mod, g, *tabs, w, qg, kvg, wqa, wqb, wk, wv, place)


def _top16(x):
    r = x.shape[0]
    row = lax.broadcasted_iota(jnp.int32, x.shape, 0)
    row16 = lax.broadcasted_iota(jnp.int32, (PEER_TOPK, LANE), 0)
    rank = jnp.full(x.shape, float(PEER_TOPK), _F32)
    vals = jnp.zeros((PEER_TOPK, LANE), _F32)
    for k in range(PEER_TOPK):
        m = jnp.max(x, axis=0, keepdims=True)
        idx = jnp.min(jnp.where(x == m, row, r), axis=0, keepdims=True)
        hit = row == idx
        x = jnp.where(hit, -jnp.inf, x)
        rank = jnp.where(hit, float(k), rank)
        vals = jnp.where(row16 == k, m, vals)
    return rank, vals


def _staircase():
    return [(r, c) for r in range(PEER_TOPK) for c in range(PEER_TOPK) if (r + 1) * (c + 1) <= PEER_TOPK]


def _peer_select(sa, sb):
    ra, av = _top16(sa)
    rb, bv = _top16(sb)
    pairs = _staircase()
    n = len(pairs)
    npad = -(-n // 8) * 8
    rows = [av[r:r + 1] + bv[c:c + 1] for r, c in pairs]
    rows += [jnp.full((1, LANE), -jnp.inf, _F32)] * (npad - n)
    cand = jnp.concatenate(rows, axis=0)
    top = av[0:1] + bv[0:1]
    e_cand = jnp.exp(cand - top)
    crow = lax.broadcasted_iota(jnp.int32, cand.shape, 0)
    sel = jnp.zeros(cand.shape, _F32)
    for _ in range(PEER_TOPK):
        m = jnp.max(cand, axis=0, keepdims=True)
        idx = jnp.min(jnp.where(cand == m, crow, npad), axis=0, keepdims=True)
        hit = crow == idx
        cand = jnp.where(hit, -jnp.inf, cand)
        sel = jnp.where(hit, 1.0, sel)
    z = jnp.sum(sel * e_cand, axis=0, keepdims=True)
    lr = jnp.zeros(sa.shape, _F32)
    pos = 0
    for r in range(PEER_TOPK):
        cnt = sum(1 for p in pairs if p[0] == r)
        l_r = jnp.sum(sel[pos:pos + cnt], axis=0, keepdims=True)
        pos += cnt
        lr = jnp.where(ra == float(r), l_r, lr)
    ea = jnp.exp(sa - av[0:1]) / z
    eb = jnp.exp(sb - bv[0:1])
    return lr, ea, rb, eb


def _peer_kernel(h2_ref, x_ref, mod_ref, wqt_ref, keys_ref, u_ref, vt_ref, *rest, n_eb, final):
    if final:
        fg_ref, o_ref, h2t_ref, lr_ref, ea_ref, rb_ref, eb_ref, ft_ref, w_ref = rest
    else:
        o_ref, h2t_ref, lr_ref, ea_ref, rb_ref, eb_ref, ft_ref, w_ref = rest
    e = pl.program_id(1)
    n_chunks = TM // LANE
    i_per = PEER_EB // PEER_NKEYS

    @pl.when(e == 0)
    def _select():
        h2t = h2_ref[...].astype(_F32).T.astype(_MXU)
        h2t_ref[...] = h2t
        ft_ref[...] = jnp.zeros_like(ft_ref)
        for hp in range(2 * PEER_HEADS):
            qt = _dot(wqt_ref[hp * LANE:(hp + 1) * LANE, :], h2t).astype(_MXU)
            st = _dot(keys_ref[hp % 2], qt)
            for c in range(n_chunks):
                (lr_ref if hp % 2 == 0 else rb_ref)[c, hp // 2] = st[:, c * LANE:(c + 1) * LANE]

        def body(t, carry):
            c, h = t // PEER_HEADS, t % PEER_HEADS
            lr, ea, rb, eb = _peer_select(lr_ref[c, h], rb_ref[c, h])
            lr_ref[c, h] = lr
            ea_ref[c, h] = ea
            rb_ref[c, h] = rb
            eb_ref[c, h] = eb
            return carry

        lax.fori_loop(0, n_chunks * PEER_HEADS, body, 0)

    at = _dot(u_ref[...], h2t_ref[...])
    for ii in range(i_per):
        i = e * i_per + ii
        for c in range(n_chunks):
            a = at[ii * PEER_NKEYS:(ii + 1) * PEER_NKEYS, c * LANE:(c + 1) * LANE]
            g = jnp.zeros((PEER_NKEYS, LANE), _F32)
            for h in range(PEER_HEADS):
                lr_i = lr_ref[c, h, pl.ds(i, 1), :]
                ea_i = ea_ref[c, h, pl.ds(i, 1), :]
                g = g + jnp.where(rb_ref[c, h] < lr_i, eb_ref[c, h], 0.0) * ea_i
            gelu = 0.5 * a * (1.0 + lax.erf(a * (2.0 ** -0.5)))
            w_ref[ii * PEER_NKEYS:(ii + 1) * PEER_NKEYS, c * LANE:(c + 1) * LANE] = (gelu * g).astype(w_ref.dtype)
    ft_ref[...] += _dot(vt_ref[...], w_ref[...])

    @pl.when(e == n_eb - 1)
    def _fin():
        x = x_ref[...] + mod_ref[0, 5:6, :] * ft_ref[...].T
        if final:
            x = _rms(x, fg_ref[...])
        o_ref[...] = x


def _peer(h2, x, mod, mod_row, wqt, keys, u, vt, final_g=None):
    n, d = x.shape
    n_exp = u.shape[0]
    n_eb = n_exp // PEER_EB
    n_chunks = TM // LANE
    final = final_g is not None
    args = [h2, x, mod, wqt, keys, u, vt]
    in_specs = [pl.BlockSpec((TM, d), lambda i, e: (i, 0)),
                pl.BlockSpec((TM, d), lambda i, e: (i, 0)),
                pl.BlockSpec((1, N_MOD, d), lambda i, e: (mod_row(i), 0, 0)),
                pl.BlockSpec(wqt.shape, lambda i, e: (0, 0)),
                pl.BlockSpec(keys.shape, lambda i, e: (0, 0, 0)),
                pl.BlockSpec((PEER_EB, d), lambda i, e: (e, 0)),
                pl.BlockSpec((d, PEER_EB), lambda i, e: (0, e))]
    if final:
        args.append(final_g)
        in_specs.append(pl.BlockSpec((1, d), lambda i, e: (0, 0)))
    tab = pltpu.VMEM((n_chunks, PEER_HEADS, PEER_NKEYS, LANE), _F32)
    return pl.pallas_call(
        functools.partial(_peer_kernel, n_eb=n_eb, final=final),
        grid=(n // TM, n_eb),
        in_specs=in_specs,
        out_specs=pl.BlockSpec((TM, d), lambda i, e: (i, 0)),
        out_shape=jax.ShapeDtypeStruct((n, d), _F32),
        scratch_shapes=[pltpu.VMEM((d, TM), _MXU), tab, tab, tab, tab,
                        pltpu.VMEM((d, TM), _F32), pltpu.VMEM((PEER_EB, TM), _MXU)],
        compiler_params=_cparams(("parallel", "arbitrary")),
        name="peer_ffn",
    )(*args)


def _dup_halves(w):
    a, b = w[:, :HEAD_DIM], w[:, HEAD_DIM:]
    return jnp.concatenate([a, a, b, b], axis=1)


def _prep_ab(w_in):
    aq, ak, av = w_in[:, 0:512], w_in[:, 512:1024], w_in[:, 1024:1536]
    bq, bk, bv = w_in[:, 1536:2048], w_in[:, 2048:2176], w_in[:, 2176:2304]
    cat = [aq, ak, av, bq, _rot_cols(bq, HEAD_DIM), _dup_halves(bk), _dup_halves(_rot_cols(bk, HEAD_DIM)),
           _dup_halves(bv)]
    return jnp.concatenate(cat, axis=1).astype(_MXU)


def _prep_cd(w_in, w_uq, w_ukv):
    d = w_in.shape[0]
    cq, ckv, kr = w_in[:, 0:256], w_in[:, 256:384], w_in[:, 384:416]
    dq, dk, dv = w_in[:, 416:928], w_in[:, 928:1440], w_in[:, 1440:1952]
    z64, z32 = jnp.zeros((d, 64), _F32), jnp.zeros((d, 32), _F32)
    kr128 = jnp.concatenate([z64, kr, z32], axis=1)
    krrot128 = jnp.concatenate([z64, _rot_cols(kr, MLA_ROPE), z32], axis=1)
    w = jnp.concatenate([cq, ckv, kr128, krrot128, dq, _rot_cols(dq, HEAD_DIM), dk, _rot_cols(dk, HEAD_DIM), dv],
                        axis=1).astype(_MXU)
    r = w_uq.shape[0]
    uq = w_uq.reshape(r, 8, MLA_NOPE + MLA_ROPE)
    nope, rope = uq[:, :, :MLA_NOPE], uq[:, :, MLA_NOPE:]
    rope_rot = _rot_cols(rope.reshape(r, 8 * MLA_ROPE), MLA_ROPE).reshape(r, 8, MLA_ROPE)
    zq = jnp.zeros((r, 8, 32), _F32)
    wqa = jnp.concatenate([nope, rope, zq], axis=2).reshape(r, 1024).astype(_MXU)
    wqb = jnp.concatenate([jnp.zeros_like(nope), rope_rot, zq], axis=2).reshape(r, 1024).astype(_MXU)
    rk = w_ukv.shape[0]
    ukv = w_ukv.reshape(rk, 8, 128)
    wk = jnp.concatenate([ukv[:, :, :MLA_NOPE], jnp.zeros((rk, 8, 64), _F32)], axis=2).reshape(rk, 1024).astype(_MXU)
    wv = ukv[:, :, MLA_NOPE:].reshape(rk, 512).astype(_MXU)
    lane = jnp.arange(LANE)
    src = (lane >= MLA_NOPE) & (lane < MLA_NOPE + MLA_ROPE)
    place = (src[:, None] & (lane[:, None] == (jnp.arange(1024)[None, :] % LANE))).astype(_MXU)
    return w, wqa, wqb, wk, wv, place


def kernel(x, c, ctx, c_ctx, ada_w, ada_b, norm1_g, norm2_g, w_out, peer_wq, peer_keys, peer_u, peer_v,
           ab_w_in, na_rpb, swa_sink, cd_w_in, mla_q_norm_g, mla_w_uq, mla_kv_norm_g, mla_w_ukv,
           diff_lambda, diff_subln_g, final_norm_g):
    batch, seq, d = x.shape
    ctx_len = ctx.shape[1]
    depth = ada_w.shape[0]
    assert seq % TM == 0 and (batch * ctx_len) % TM == 0 and seq % TQ_LOCAL == 0
    assert depth == 2, "even layers keep a context stream, the single odd layer is the last one"
    n_lat, n_ctx = batch * seq, batch * ctx_len
    xs = x.reshape(n_lat, d)
    cs = ctx.reshape(n_ctx, d)

    mod_rows = -(-(batch + 1) // 16) * 16
    cc = jnp.zeros((mod_rows, d), _F32).at[:batch].set(c).at[batch].set(c_ctx)
    mod_all = _modulation(cc, ada_w, ada_b)

    tiles_per_seq = seq // TM
    lat_row = lambda i: i // tiles_per_seq
    ctx_row = lambda i: batch
    lat_tab = lambda i: i % tiles_per_seq
    ctx_tab = lambda i: tiles_per_seq

    cos64, sin64 = _rope_tables(seq, HEAD_DIM)
    cos64p, sin64p = _pad_table(cos64, LANE, 1.0), _pad_table(sin64, LANE, 0.0)

    for l in range(depth):
        last = l == depth - 1
        j = l // 2
        mod = mod_all[l, :batch + 1].reshape(batch + 1, N_MOD, d)
        g1 = norm1_g[l].reshape(1, d)
        g2 = norm2_g[l].reshape(1, d)
        wo = w_out[l].astype(_MXU)
        wqt = peer_wq[l].T.astype(_MXU)
        keys = peer_keys[l].astype(_MXU)
        u = peer_u[l].astype(_MXU)
        vt = peer_v[l].T.astype(_MXU)
        if l % 2 == 0:
            w = _prep_ab(ab_w_in[j])
            aq, ak, av, bq, bk2, bv2 = _proj_ab(xs, mod, lat_row, g1, cos64p, sin64p, lat_tab, w)
            caq, cak, cav, cbq, cbk2, cbv2 = _proj_ab(cs, mod, ctx_row, g1, cos64p, sin64p, ctx_tab, w)
            bias = _na_bias_tables(na_rpb[j], seq // GRID_W)
            ya = _na_attention(aq, ak, av, cak, cav, bias, batch, seq, ctx_len)
            sink = swa_sink[j].astype(_F32)
            yb = _swa_attention(sink, bq, bk2, bv2, cbk2, cbv2, batch, seq, ctx_len)
            xs, h2 = _out_proj(xs, ya, yb, mod, lat_row, g2, wo[:512], wo[512:])
            if not last:
                pair_heads = [(m, half, m, m) for m in range(4) for half in range(2)]
                pair_outs = [("pair", 2 * m, 2 * m + 1) for m in range(4)]
                yca = _flash_attention(caq, cak, cav, heads=pair_heads, outs=pair_outs, batch=batch,
                                       q_per_batch=ctx_len, kv_per_batch=ctx_len, tq=ctx_len, tk=ctx_len,
                                       name="ctx_attn_a")
                gqa_heads = [(m, half, m // 2, m // 2) for m in range(4) for half in range(2)]
                ycb = _flash_attention(cbq, cbk2, cbv2, heads=gqa_heads, outs=pair_outs, batch=batch,
                                       q_per_batch=ctx_len, kv_per_batch=ctx_len, tq=ctx_len, tk=ctx_len,
                                       sink=sink, name="ctx_attn_b")
                cs, h2c = _out_proj(cs, yca, ycb, mod, ctx_row, g2, wo[:512], wo[512:])
        else:
            lam_init = 0.8 - 0.6 * math.exp(-0.3 * l)
            w, wqa, wqb, wk, wv, place = _prep_cd(cd_w_in[j], mla_w_uq[j], mla_w_ukv[j])
            cos32, sin32 = _rope_tables(seq, MLA_ROPE)
            ones64, zeros64 = jnp.ones((seq, 64), _F32), jnp.zeros((seq, 64), _F32)
            cosq = _pad_table(jnp.concatenate([ones64, cos32, ones64[:, :32]], 1), LANE, 1.0)
            sinq = _pad_table(jnp.concatenate([zeros64, sin32, zeros64[:, :32]], 1), LANE, 0.0)
            tabs = (cosq, sinq, cos64p, sin64p)
            qg = mla_q_norm_g[j].reshape(1, -1)
            kvg = mla_kv_norm_g[j].reshape(1, -1)
            qmt, km, vmt, dqt, dk, dvt = _proj_cd(xs, mod, lat_row, g1, tabs, lat_tab, w, qg, kvg, wqa, wqb, wk, wv, place)
            _, ckm, cvmt, _, cdk, cdvt = _proj_cd(cs, mod, ctx_row, g1, tabs, ctx_tab, w, qg, kvg, wqa, wqb, wk, wv, place)
            mla_heads = [(h, None, h, h // 2) for h in range(8)]
            pair_outs = [("pair", 2 * m, 2 * m + 1) for m in range(4)]
            yc = _flasht_attention(qmt, km, vmt, ckm, cvmt, heads=mla_heads, outs=pair_outs, batch=batch, seq=seq,
                                   ctx_len=ctx_len, tq=GLOBAL_TQ, tk=GLOBAL_TK, name="mla_attn")
            diff_heads = [(h, a, h, h) for h in range(4) for a in range(2)]
            diff_outs = [("diff", 2 * h, 2 * h + 1) for h in range(4)]
            od = _flasht_attention(dqt, dk, dvt, cdk, cdvt, heads=diff_heads, outs=diff_outs, batch=batch, seq=seq,
                                   ctx_len=ctx_len, tq=GLOBAL_TQ, tk=GLOBAL_TK,
                                   diff=(diff_lambda[j].astype(_F32), diff_subln_g[j].reshape(1, -1), lam_init),
                                   name="diff_attn")
            xs, h2 = _out_proj(xs, yc, od, mod, lat_row, g2, wo[:512], wo[512:])
        fg = final_norm_g.reshape(1, d) if last else None
        xs = _peer(h2, xs, mod, lat_row, wqt, keys, u, vt, final_g=fg)
        if not last:
            cs = _peer(h2c, cs, mod, ctx_row, wqt, keys, u, vt)
    return xs.reshape(batch, seq, d)
```

```python
import functools
import math

import jax
import jax.numpy as jnp
from jax import lax
from jax.experimental import pallas as pl
from jax.experimental.pallas import tpu as pltpu

_F32 = jnp.float32
_MXU = jnp.bfloat16
_NEG = -1e30

GRID_W = 64
HEAD_DIM = 64
ROPE_BASE = 10000.0
RMS_EPS = 1e-6
N_MOD = 6
NA_KR, NA_KC = 8, 16
SWA_WINDOW = 128
MLA_NOPE, MLA_ROPE = 64, 32
PEER_HEADS, PEER_NKEYS, PEER_TOPK = 8, 128, 16

LANE = 128
TM = 512
TQ_LOCAL = 256
NA_GROUP_ROWS = TQ_LOCAL // GRID_W
NA_WIN_ROWS = NA_KR + NA_GROUP_ROWS - 1
GLOBAL_TQ = 512
GLOBAL_TK = 512
PEER_EB = 1024
MXU_DEPTH = 256
PEER_MXU_ROWS = 512
VMEM_LIMIT = 56 * 1024 * 1024


def _cparams(sem, vmem=VMEM_LIMIT):
    return pltpu.CompilerParams(dimension_semantics=sem, vmem_limit_bytes=vmem)


def _dot(a, b):
    return jnp.dot(a, b, preferred_element_type=_F32)


def _dot_nt(a, b):
    return lax.dot_general(a, b, (((1,), (1,)), ((), ())), preferred_element_type=_F32)


def _rms(x, g):
    return x * lax.rsqrt(jnp.mean(x * x, axis=-1, keepdims=True) + RMS_EPS) * g


def _lane_half(shape):
    return lax.broadcasted_iota(jnp.int32, shape, len(shape) - 1) >= (LANE // 2)


def _mod_kernel(c_ref, w_ref, b_ref, o_ref):
    c = c_ref[...]
    a = c * jax.nn.sigmoid(c)
    w = w_ref[0]
    a_hi = a.astype(_MXU)
    a_lo = (a - a_hi.astype(_F32)).astype(_MXU)
    w_hi = w.astype(_MXU)
    w_lo = (w - w_hi.astype(_F32)).astype(_MXU)
    o_ref[0] = _dot(a_hi, w_hi) + _dot(a_lo, w_hi) + _dot(a_hi, w_lo) + b_ref[0]


def _modulation(cc, ada_w, ada_b):
    depth, d, n = ada_w.shape
    rows = cc.shape[0]
    tn = 768
    return pl.pallas_call(
        _mod_kernel,
        grid=(depth, n // tn),
        in_specs=[pl.BlockSpec((rows, d), lambda l, j: (0, 0)),
                  pl.BlockSpec((1, d, tn), lambda l, j: (l, 0, j)),
                  pl.BlockSpec((1, 1, tn), lambda l, j: (l, 0, j))],
        out_specs=pl.BlockSpec((1, rows, tn), lambda l, j: (l, 0, j)),
        out_shape=jax.ShapeDtypeStruct((depth, rows, n), _F32),
        compiler_params=_cparams(("parallel", "parallel")),
        name="adaln_mod",
    )(cc, ada_w, ada_b.reshape(depth, 1, n))


def _rope_tables(seq, d):
    t = jnp.arange(seq, dtype=jnp.int32)
    q = d // 4
    freq = ROPE_BASE ** (-jnp.arange(q, dtype=_F32) / q)

    def one(pos):
        ang = pos.astype(_F32)[:, None] * freq[None, :]
        return jnp.concatenate([jnp.cos(ang)] * 2, -1), jnp.concatenate([jnp.sin(ang)] * 2, -1)

    cr, sr = one(t // GRID_W)
    cc, sc = one(t % GRID_W)
    return jnp.concatenate([cr, cc], -1), jnp.concatenate([sr, sc], -1)


def _rot_cols(w, d):
    k, n = w.shape
    w5 = w.reshape(k, n // d, 2, 2, d // 4)
    return jnp.stack([-w5[:, :, :, 1], w5[:, :, :, 0]], axis=3).reshape(k, n)


def _pad_table(tab, width, ident):
    s, w = tab.shape
    if w < width:
        reps = width // w
        tab = jnp.tile(tab, (1, reps))
    return jnp.concatenate([tab, jnp.full((TM, width), ident, _F32)], axis=0)


def _proj_ab_kernel(x_ref, mod_ref, g_ref, cos_ref, sin_ref, w_ref,
                    aq_ref, ak_ref, av_ref, bq_ref, bk_ref, bv_ref):
    x = x_ref[...]
    h = _rms(x, g_ref[...]) * (1.0 + mod_ref[0, 1:2, :]) + mod_ref[0, 0:1, :]
    hb = h.astype(_MXU)
    cos = cos_ref[...]
    sin = sin_ref[...]

    def proj(lo, n):
        return _dot(hb, w_ref[:, lo:lo + n])

    scale = HEAD_DIM ** -0.5
    aq_ref[...] = (proj(0, 512) * scale).astype(aq_ref.dtype)
    ak_ref[...] = proj(512, 512).astype(ak_ref.dtype)
    av_ref[...] = proj(1024, 512).astype(av_ref.dtype)
    cos4 = jnp.tile(cos, (1, 4))
    sin4 = jnp.tile(sin, (1, 4))
    bq_ref[...] = ((proj(1536, 512) * cos4 + proj(2048, 512) * sin4) * scale).astype(bq_ref.dtype)
    cos2 = jnp.tile(cos, (1, 2))
    sin2 = jnp.tile(sin, (1, 2))
    bk_ref[...] = (proj(2560, 256) * cos2 + proj(2816, 256) * sin2).astype(bk_ref.dtype)
    bv_ref[...] = proj(3072, 256).astype(bv_ref.dtype)


def _proj_ab(x, mod, mod_row, g, cos, sin, tab_row, w):
    n, d = x.shape
    widths = (512, 512, 512, 512, 256, 256)
    return pl.pallas_call(
        _proj_ab_kernel,
        grid=(n // TM,),
        in_specs=[pl.BlockSpec((TM, d), lambda i: (i, 0)),
                  pl.BlockSpec((1, N_MOD, d), lambda i: (mod_row(i), 0, 0)),
                  pl.BlockSpec((1, d), lambda i: (0, 0)),
                  pl.BlockSpec((TM, LANE), lambda i: (tab_row(i), 0)),
                  pl.BlockSpec((TM, LANE), lambda i: (tab_row(i), 0)),
                  pl.BlockSpec(w.shape, lambda i: (0, 0))],
        out_specs=[pl.BlockSpec((TM, wd), lambda i: (i, 0)) for wd in widths],
        out_shape=[jax.ShapeDtypeStruct((n, wd), _MXU) for wd in widths],
        compiler_params=_cparams(("parallel",)),
        name="proj_ab",
    )(x, mod, g, cos, sin, w)


def _na_kernel(q_ref, k_ref, v_ref, kc_ref, vc_ref, bias_ref, o_ref, *, rows):
    g = pl.program_id(1)
    ks = jnp.clip(NA_GROUP_ROWS * g - NA_KR // 2, 0, rows - NA_WIN_ROWS)
    start = pl.multiple_of(ks * GRID_W, GRID_W)
    nwin = NA_WIN_ROWS * GRID_W
    hi = _lane_half((TQ_LOCAL, LANE))
    for m in range(4):
        cols = slice(m * LANE, (m + 1) * LANE)
        q2 = q_ref[:, cols]
        k2 = k_ref[pl.ds(start, nwin), cols]
        v2 = v_ref[pl.ds(start, nwin), cols]
        kc2 = kc_ref[:, cols]
        vc2 = vc_ref[:, cols]
        outs = []
        for half in range(2):
            qh = jnp.where(hi == (half == 1), q2, jnp.zeros_like(q2))
            s_lat = _dot_nt(qh, k2) + bias_ref[0, 2 * m + half]
            s_ctx = _dot_nt(qh, kc2)
            mx = jnp.maximum(jnp.max(s_lat, axis=-1, keepdims=True), jnp.max(s_ctx, axis=-1, keepdims=True))
            p_lat = jnp.exp(s_lat - mx)
            p_ctx = jnp.exp(s_ctx - mx)
            den = jnp.sum(p_lat, axis=-1, keepdims=True) + jnp.sum(p_ctx, axis=-1, keepdims=True)
            o = _dot(p_lat.astype(_MXU), v2) + _dot(p_ctx.astype(_MXU), vc2)
            outs.append(o / den)
        o_ref[:, cols] = jnp.where(hi, outs[1], outs[0]).astype(o_ref.dtype)


def _na_bias_tables(rpb, rows):
    gq = NA_GROUP_ROWS
    n_groups = rows // gq
    ql = jnp.arange(gq)[:, None]
    kl = jnp.arange(NA_WIN_ROWS)[None, :]
    qc = jnp.arange(GRID_W)[:, None]
    kc = jnp.arange(GRID_W)[None, :]
    cs = jnp.clip(qc - NA_KC // 2, 0, GRID_W - NA_KC)
    col_valid = (kc >= cs) & (kc < cs + NA_KC)
    col_idx = jnp.clip(kc - qc + NA_KC - 1, 0, 2 * NA_KC - 2)
    exact = lax.Precision.HIGHEST
    oh_c = jax.nn.one_hot(col_idx, 2 * NA_KC - 1, dtype=_F32)
    rpb_cols = jnp.einsum("hab,xyb->haxy", rpb.astype(_F32), oh_c, precision=exact)
    tabs = []
    for g in (0, 1, n_groups - 1):
        ks = min(max(gq * g - NA_KR // 2, 0), rows - NA_WIN_ROWS)
        rq = gq * g + ql
        rk = ks + kl
        r0 = jnp.clip(rq - NA_KR // 2, 0, rows - NA_KR)
        row_valid = (rk >= r0) & (rk < r0 + NA_KR)
        row_idx = jnp.clip(rk - rq + NA_KR - 1, 0, 2 * NA_KR - 2)
        oh_r = jax.nn.one_hot(row_idx, 2 * NA_KR - 1, dtype=_F32)
        b = jnp.einsum("qka,haxy->hqxky", oh_r, rpb_cols, precision=exact)
        valid = row_valid[:, None, :, None] & col_valid[None, :, None, :]
        b = jnp.where(valid[None], b, _NEG)
        tabs.append(b.reshape(rpb.shape[0], gq * GRID_W, NA_WIN_ROWS * GRID_W))
    return jnp.stack(tabs)


def _na_attention(aq, ak, av, cak, cav, bias, batch, seq, ctx_len):
    rows = seq // GRID_W
    n_groups = seq // TQ_LOCAL
    nwin = NA_WIN_ROWS * GRID_W

    def bias_row(b, g):
        return (jnp.where(g == 0, 0, jnp.where(g == n_groups - 1, 2, 1)), 0, 0, 0)

    return pl.pallas_call(
        functools.partial(_na_kernel, rows=rows),
        grid=(batch, n_groups),
        in_specs=[pl.BlockSpec((TQ_LOCAL, 512), lambda b, g: (b * n_groups + g, 0)),
                  pl.BlockSpec((seq, 512), lambda b, g: (b, 0)),
                  pl.BlockSpec((seq, 512), lambda b, g: (b, 0)),
                  pl.BlockSpec((ctx_len, 512), lambda b, g: (b, 0)),
                  pl.BlockSpec((ctx_len, 512), lambda b, g: (b, 0)),
                  pl.BlockSpec((1, 8, TQ_LOCAL, nwin), bias_row)],
        out_specs=pl.BlockSpec((TQ_LOCAL, 512), lambda b, g: (b * n_groups + g, 0)),
        out_shape=jax.ShapeDtypeStruct((batch * seq, 512), _MXU),
        compiler_params=_cparams(("parallel", "arbitrary")),
        name="na_attention",
    )(aq, ak, av, cak, cav, bias)


def _swa_kernel(sink_ref, q_ref, k_ref, v_ref, kc_ref, vc_ref, o_ref, *, seq):
    t = pl.program_id(1)
    kwin = TQ_LOCAL + 2 * SWA_WINDOW
    start = t * TQ_LOCAL
    kstart = pl.multiple_of(jnp.clip(start - SWA_WINDOW, 0, seq - kwin), LANE)
    qpos = start + lax.broadcasted_iota(jnp.int32, (TQ_LOCAL, kwin), 0)
    kpos = kstart + lax.broadcasted_iota(jnp.int32, (TQ_LOCAL, kwin), 1)
    mask = jnp.where(jnp.abs(kpos - qpos) <= SWA_WINDOW, 0.0, _NEG).astype(_F32)
    hi = _lane_half((TQ_LOCAL, LANE))
    for m in range(4):
        cols = slice(m * LANE, (m + 1) * LANE)
        kv = m // 2
        kcols = slice(kv * LANE, (kv + 1) * LANE)
        q2 = q_ref[:, cols]
        k2 = k_ref[pl.ds(kstart, kwin), kcols]
        v2 = v_ref[pl.ds(kstart, kwin), kcols]
        kc2 = kc_ref[:, kcols]
        vc2 = vc_ref[:, kcols]
        outs = []
        for half in range(2):
            sink = sink_ref[2 * m + half]
            qh = jnp.where(hi == (half == 1), q2, jnp.zeros_like(q2))
            s_lat = _dot_nt(qh, k2) + mask
            s_ctx = _dot_nt(qh, kc2)
            mx = jnp.maximum(jnp.max(s_lat, axis=-1, keepdims=True), jnp.max(s_ctx, axis=-1, keepdims=True))
            mx = jnp.maximum(mx, sink)
            p_lat = jnp.exp(s_lat - mx)
            p_ctx = jnp.exp(s_ctx - mx)
            den = (jnp.sum(p_lat, axis=-1, keepdims=True) + jnp.sum(p_ctx, axis=-1, keepdims=True)
                   + jnp.exp(sink - mx))
            o = _dot(p_lat.astype(_MXU), v2) + _dot(p_ctx.astype(_MXU), vc2)
            outs.append(o / den)
        o_ref[:, cols] = jnp.where(hi, outs[1], outs[0]).astype(o_ref.dtype)


def _swa_attention(sink, bq, bk2, bv2, cbk2, cbv2, batch, seq, ctx_len):
    n_t = seq // TQ_LOCAL
    return pl.pallas_call(
        functools.partial(_swa_kernel, seq=seq),
        grid=(batch, n_t),
        in_specs=[pl.BlockSpec(memory_space=pltpu.SMEM),
                  pl.BlockSpec((TQ_LOCAL, 512), lambda b, t: (b * n_t + t, 0)),
                  pl.BlockSpec((seq, 256), lambda b, t: (b, 0)),
                  pl.BlockSpec((seq, 256), lambda b, t: (b, 0)),
                  pl.BlockSpec((ctx_len, 256), lambda b, t: (b, 0)),
                  pl.BlockSpec((ctx_len, 256), lambda b, t: (b, 0))],
        out_specs=pl.BlockSpec((TQ_LOCAL, 512), lambda b, t: (b * n_t + t, 0)),
        out_shape=jax.ShapeDtypeStruct((batch * seq, 512), _MXU),
        compiler_params=_cparams(("parallel", "arbitrary")),
        name="swa_attention",
    )(sink, bq, bk2, bv2, cbk2, cbv2)


def _flash_kernel(*refs, heads, outs, has_ctx, has_sink, diff_cfg, n_kv):
    it = iter(refs)
    sink_ref = next(it) if has_sink else None
    q_ref, k_ref, v_ref = next(it), next(it), next(it)
    kc_ref = next(it) if has_ctx else None
    vc_ref = next(it) if has_ctx else None
    lam_ref = next(it) if diff_cfg else None
    sg_ref = next(it) if diff_cfg else None
    o_ref = next(it)
    m_ref, l_ref, acc_ref = next(it), next(it), next(it)
    kv = pl.program_id(2)
    tq = q_ref.shape[0]
    hi = _lane_half((tq, LANE))

    @pl.when(kv == 0)
    def _init():
        for h in range(len(heads)):
            if has_sink:
                m_ref[h] = jnp.full((tq, LANE), sink_ref[h], _F32)
                l_ref[h] = jnp.ones((tq, LANE), _F32)
            else:
                m_ref[h] = jnp.full((tq, LANE), _NEG, _F32)
                l_ref[h] = jnp.zeros((tq, LANE), _F32)
            acc_ref[h] = jnp.zeros((tq, LANE), _F32)

    def attend(kr, vr):
        for h, (qb, qhalf, kb, vb) in enumerate(heads):
            q2 = q_ref[:, qb * LANE:(qb + 1) * LANE]
            if qhalf is not None:
                q2 = jnp.where(hi == (qhalf == 1), q2, jnp.zeros_like(q2))
            s = _dot_nt(q2, kr[:, kb * LANE:(kb + 1) * LANE])
            m_old = m_ref[h][:, :1]
            m_new = jnp.maximum(m_old, jnp.max(s, axis=-1, keepdims=True))
            alpha = jnp.exp(m_old - m_new)
            p = jnp.exp(s - m_new)
            l_ref[h] = jnp.broadcast_to(alpha * l_ref[h][:, :1] + jnp.sum(p, axis=-1, keepdims=True), (tq, LANE))
            acc_ref[h] = alpha * acc_ref[h] + _dot(p.astype(_MXU), vr[:, vb * LANE:(vb + 1) * LANE])
            m_ref[h] = jnp.broadcast_to(m_new, (tq, LANE))

    if has_ctx:
        @pl.when(kv == 0)
        def _ctx():
            attend(kc_ref, vc_ref)

    attend(k_ref, v_ref)

    @pl.when(kv == n_kv - 1)
    def _fin():
        def head_out(h):
            return acc_ref[h] / l_ref[h][:, :1]

        for j, spec in enumerate(outs):
            if spec[0] == "full":
                o = head_out(spec[1])
            elif spec[0] == "pair":
                o = jnp.where(hi, head_out(spec[2]), head_out(spec[1]))
            else:
                lv = lam_ref[...]
                lam = (jnp.exp(jnp.sum(lv[0:1] * lv[1:2], axis=-1, keepdims=True))
                       - jnp.exp(jnp.sum(lv[2:3] * lv[3:4], axis=-1, keepdims=True)) + diff_cfg)
                o = _rms(head_out(spec[1]) - lam * head_out(spec[2]), sg_ref[...]) * (1.0 - diff_cfg)
            o_ref[:, j * LANE:(j + 1) * LANE] = o.astype(o_ref.dtype)


def _flash_attention(q, k, v, *, heads, outs, batch, q_per_batch, kv_per_batch, tq, tk,
                     q_row0=0, kv_row0=0, ctx=None, ctx_len=0, ctx_row0=0, sink=None, diff=None, name="flash"):
    n_q = q_per_batch // tq
    n_kv = kv_per_batch // tk
    qw, kw, vw = q.shape[1], k.shape[1], v.shape[1]
    args, in_specs = [], []
    if sink is not None:
        args.append(sink)
        in_specs.append(pl.BlockSpec(memory_space=pltpu.SMEM))
    q0, k0 = q_row0 // tq, kv_row0 // tk
    args += [q, k, v]
    in_specs += [pl.BlockSpec((tq, qw), lambda b, i, j: (q0 + b * n_q + i, 0)),
                 pl.BlockSpec((tk, kw), lambda b, i, j: (k0 + b * n_kv + j, 0)),
                 pl.BlockSpec((tk, vw), lambda b, i, j: (k0 + b * n_kv + j, 0))]
    if ctx is not None:
        c0 = ctx_row0 // ctx_len
        args += [ctx[0], ctx[1]]
        in_specs += [pl.BlockSpec((ctx_len, kw), lambda b, i, j: (c0 + b, 0)),
                     pl.BlockSpec((ctx_len, vw), lambda b, i, j: (c0 + b, 0))]
    diff_cfg = None
    if diff is not None:
        lam_vecs, subln_g, diff_cfg = diff
        args += [lam_vecs, subln_g]
        in_specs += [pl.BlockSpec(lam_vecs.shape, lambda b, i, j: (0, 0)),
                     pl.BlockSpec(subln_g.shape, lambda b, i, j: (0, 0))]
    nh = len(heads)
    ow = len(outs) * LANE
    return pl.pallas_call(
        functools.partial(_flash_kernel, heads=tuple(heads), outs=tuple(outs), has_ctx=ctx is not None,
                          has_sink=sink is not None, diff_cfg=diff_cfg, n_kv=n_kv),
        grid=(batch, n_q, n_kv),
        in_specs=in_specs,
        out_specs=pl.BlockSpec((tq, ow), lambda b, i, j: (b * n_q + i, 0)),
        out_shape=jax.ShapeDtypeStruct((batch * q_per_batch, ow), _MXU),
        scratch_shapes=[pltpu.VMEM((nh, tq, LANE), _F32)] * 3,
        compiler_params=_cparams(("parallel", "parallel", "arbitrary")),
        name=name,
    )(*args)


def _flasht_kernel(*refs, heads, outs, diff_cfg, n_kv):
    it = iter(refs)
    qt_ref, k_ref, vt_ref, kc_ref, vct_ref = next(it), next(it), next(it), next(it), next(it)
    lam_ref = next(it) if diff_cfg else None
    sg_ref = next(it) if diff_cfg else None
    o_ref = next(it)
    m_ref, l_ref, acc_ref = next(it), next(it), next(it)
    kv = pl.program_id(2)
    tq = qt_ref.shape[1]
    row_hi = lax.broadcasted_iota(jnp.int32, (LANE, tq), 0) >= (LANE // 2)

    @pl.when(kv == 0)
    def _init():
        m_ref[...] = jnp.full(m_ref.shape, _NEG, _F32)
        l_ref[...] = jnp.zeros(l_ref.shape, _F32)
        acc_ref[...] = jnp.zeros(acc_ref.shape, _F32)

    def attend(kr, vtr):
        for h, (qb, qhalf, kb, vb) in enumerate(heads):
            qt = qt_ref[qb * LANE:(qb + 1) * LANE, :]
            if qhalf is not None:
                qt = jnp.where(row_hi == (qhalf == 1), qt, jnp.zeros_like(qt))
            s = _dot(kr[:, kb * LANE:(kb + 1) * LANE], qt)
            m_old = m_ref[h, 0:1, :]
            m_new = jnp.maximum(m_old, jnp.max(s, axis=0, keepdims=True))
            alpha = jnp.exp(m_old - m_new)
            p = jnp.exp(s - m_new)
            l_new = alpha * l_ref[h, 0:1, :] + jnp.sum(p, axis=0, keepdims=True)
            acc_ref[h] = alpha * acc_ref[h] + _dot(vtr[vb * LANE:(vb + 1) * LANE, :], p.astype(_MXU))
            l_ref[h] = jnp.broadcast_to(l_new, (8, tq))
            m_ref[h] = jnp.broadcast_to(m_new, (8, tq))

    @pl.when(kv == 0)
    def _ctx():
        attend(kc_ref, vct_ref)

    attend(k_ref, vt_ref)

    @pl.when(kv == n_kv - 1)
    def _fin():
        def head_out(h):
            return acc_ref[h] / l_ref[h, 0:1, :]

        for j, spec in enumerate(outs):
            if spec[0] == "full":
                o = head_out(spec[1]).T
            elif spec[0] == "pair":
                o = jnp.where(row_hi, head_out(spec[2]), head_out(spec[1])).T
            else:
                lv = lam_ref[...]
                lam = (jnp.exp(jnp.sum(lv[0:1] * lv[1:2], axis=-1, keepdims=True))
                       - jnp.exp(jnp.sum(lv[2:3] * lv[3:4], axis=-1, keepdims=True)) + diff_cfg)
                o = _rms((head_out(spec[1]) - lam * head_out(spec[2])).T, sg_ref[...]) * (1.0 - diff_cfg)
            o_ref[:, j * LANE:(j + 1) * LANE] = o.astype(o_ref.dtype)


def _flasht_attention(qt, k, vt, kc, vct, *, heads, outs, batch, seq, ctx_len, tq, tk, diff=None, name="flasht"):
    n_q, n_kv = seq // tq, seq // tk
    qw, kw, vw = qt.shape[0], k.shape[1], vt.shape[0]
    args = [qt, k, vt, kc, vct]
    in_specs = [pl.BlockSpec((qw, tq), lambda b, i, j: (0, b * n_q + i)),
                pl.BlockSpec((tk, kw), lambda b, i, j: (b * n_kv + j, 0)),
                pl.BlockSpec((vw, tk), lambda b, i, j: (0, b * n_kv + j)),
                pl.BlockSpec((ctx_len, kw), lambda b, i, j: (b, 0)),
                pl.BlockSpec((vw, ctx_len), lambda b, i, j: (0, b))]
    diff_cfg = None
    if diff is not None:
        lam_vecs, subln_g, diff_cfg = diff
        args += [lam_vecs, subln_g]
        in_specs += [pl.BlockSpec(lam_vecs.shape, lambda b, i, j: (0, 0)),
                     pl.BlockSpec(subln_g.shape, lambda b, i, j: (0, 0))]
    nh = len(heads)
    ow = len(outs) * LANE
    return pl.pallas_call(
        functools.partial(_flasht_kernel, heads=tuple(heads), outs=tuple(outs), diff_cfg=diff_cfg, n_kv=n_kv),
        grid=(batch, n_q, n_kv),
        in_specs=in_specs,
        out_specs=pl.BlockSpec((tq, ow), lambda b, i, j: (b * n_q + i, 0)),
        out_shape=jax.ShapeDtypeStruct((batch * seq, ow), _MXU),
        scratch_shapes=[pltpu.VMEM((nh, 8, tq), _F32), pltpu.VMEM((nh, 8, tq), _F32),
                        pltpu.VMEM((nh, LANE, tq), _F32)],
        compiler_params=_cparams(("parallel", "parallel", "arbitrary")),
        name=name,
    )(*args)


def _out_kernel(x_ref, ya_ref, yb_ref, mod_ref, g2_ref, wa_ref, wb_ref, xo_ref, h2_ref):
    y = _dot(ya_ref[...], wa_ref[...]) + _dot(yb_ref[...], wb_ref[...])
    x = x_ref[...] + mod_ref[0, 2:3, :] * y
    xo_ref[...] = x
    h2 = _rms(x, g2_ref[...]) * (1.0 + mod_ref[0, 4:5, :]) + mod_ref[0, 3:4, :]
    h2_ref[...] = h2.astype(h2_ref.dtype)


def _out_proj(x, ya, yb, mod, mod_row, g2, wa, wb):
    n, d = x.shape
    return pl.pallas_call(
        _out_kernel,
        grid=(n // TM,),
        in_specs=[pl.BlockSpec((TM, d), lambda i: (i, 0)),
                  pl.BlockSpec((TM, ya.shape[1]), lambda i: (i, 0)),
                  pl.BlockSpec((TM, yb.shape[1]), lambda i: (i, 0)),
                  pl.BlockSpec((1, N_MOD, d), lambda i: (mod_row(i), 0, 0)),
                  pl.BlockSpec((1, d), lambda i: (0, 0)),
                  pl.BlockSpec(wa.shape, lambda i: (0, 0)),
                  pl.BlockSpec(wb.shape, lambda i: (0, 0))],
        out_specs=[pl.BlockSpec((TM, d), lambda i: (i, 0)), pl.BlockSpec((TM, d), lambda i: (i, 0))],
        out_shape=[jax.ShapeDtypeStruct((n, d), _F32), jax.ShapeDtypeStruct((n, d), _MXU)],
        compiler_params=_cparams(("parallel",)),
        name="out_proj",
    )(x, ya, yb, mod, g2, wa, wb)


def _proj_cd_kernel(x_ref, mod_ref, g_ref, cq_ref, sq_ref, cd_ref, sd_ref, w_ref, qg_ref, kvg_ref,
                    wqa_ref, wqb_ref, wk_ref, wv_ref, place_ref,
                    qm_ref, km_ref, vm_ref, dq_ref, dk_ref, dv_ref):
    x = x_ref[...]
    h = _rms(x, g_ref[...]) * (1.0 + mod_ref[0, 1:2, :]) + mod_ref[0, 0:1, :]
    hb = h.astype(_MXU)

    def proj(lo, n):
        return _dot(hb, w_ref[:, lo:lo + n])

    cosq, sinq = cq_ref[...], sq_ref[...]
    cosd, sind = cd_ref[...], sd_ref[...]
    cqn = _rms(proj(0, 256), qg_ref[...]).astype(_MXU)
    ckvn = _rms(proj(256, 128), kvg_ref[...]).astype(_MXU)
    cos8, sin8 = jnp.tile(cosq, (1, 8)), jnp.tile(sinq, (1, 8))
    qm = _dot(cqn, wqa_ref[...]) * cos8 + _dot(cqn, wqb_ref[...]) * sin8
    qm_ref[...] = (qm * ((MLA_NOPE + MLA_ROPE) ** -0.5)).T.astype(qm_ref.dtype)
    kr = (proj(384, 128) * cosq + proj(512, 128) * sinq).astype(_MXU)
    km_ref[...] = (_dot(ckvn, wk_ref[...]) + _dot(kr, place_ref[...])).astype(km_ref.dtype)
    vm_ref[...] = _dot(ckvn, wv_ref[...]).T.astype(vm_ref.dtype)
    cos4, sin4 = jnp.tile(cosd, (1, 4)), jnp.tile(sind, (1, 4))
    dq_ref[...] = ((proj(640, 512) * cos4 + proj(1152, 512) * sin4) * (HEAD_DIM ** -0.5)).T.astype(dq_ref.dtype)
    dk_ref[...] = (proj(1664, 512) * cos4 + proj(2176, 512) * sin4).astype(dk_ref.dtype)
    dv_ref[...] = proj(2688, 512).T.astype(dv_ref.dtype)


def _proj_cd(x, mod, mod_row, g, tabs, tab_row, w, qg, kvg, wqa, wqb, wk, wv, place):
    n, d = x.shape
    outs = ((1024, True), (1024, False), (512, True), (512, True), (512, False), (512, True))
    full = lambda a: pl.BlockSpec(a.shape, lambda i: (0, 0))
    return pl.pallas_call(
        _proj_cd_kernel,
        grid=(n // TM,),
        in_specs=[pl.BlockSpec((TM, d), lambda i: (i, 0)),
                  pl.BlockSpec((1, N_MOD, d), lambda i: (mod_row(i), 0, 0)),
                  pl.BlockSpec((1, d), lambda i: (0, 0))]
                 + [pl.BlockSpec((TM, LANE), lambda i: (tab_row(i), 0))] * 4
                 + [full(a) for a in (w, qg, kvg, wqa, wqb, wk, wv, place)],
        out_specs=[pl.BlockSpec((wd, TM), lambda i: (0, i)) if fm else pl.BlockSpec((TM, wd), lambda i: (i, 0))
                   for wd, fm in outs],
        out_shape=[jax.ShapeDtypeStruct((wd, n) if fm else (n, wd), _MXU) for wd, fm in outs],
        compiler_params=_cparams(("parallel",)),
        name="proj_cd",
    )(x, mod, g, *tabs, w, qg, kvg, wqa, wqb, wk, wv, place)


def _extract16(x, first_only):
    r = x.shape[0]
    row = lax.broadcasted_iota(jnp.int32, x.shape, 0)
    row16 = lax.broadcasted_iota(jnp.int32, (PEER_TOPK, LANE), 0)
    order = jnp.full(x.shape, float(PEER_TOPK), _F32)
    vals = jnp.zeros((PEER_TOPK, LANE), _F32)
    for k in range(PEER_TOPK):
        m = jnp.max(x, axis=0, keepdims=True)
        hit = x == m
        if first_only:
            hit = row == jnp.min(jnp.where(hit, row, r), axis=0, keepdims=True)
        x = jnp.where(hit, -jnp.inf, x)
        order = jnp.where(hit, float(k), order)
        vals = jnp.where(row16 == k, m, vals)
    return order, vals


def _staircase_rows():
    groups = []
    for r in range(PEER_TOPK // 2):
        n = PEER_TOPK // (r + 1)
        for c0 in range(0, n, 8):
            groups.append((r, c0, min(8, n - c0)))
    groups.append((None, 0, 8))
    return groups


def _peer_select(sa, sb, first_only):
    ra, av = _extract16(sa, first_only)
    rb, bv = _extract16(sb, first_only)
    groups = _staircase_rows()
    sub = lax.broadcasted_iota(jnp.int32, (8, LANE), 0)
    pieces = []
    for r, c0, nv in groups:
        if r is None:
            piece = av[8:16] + bv[0:1]
        else:
            piece = av[r:r + 1] + bv[c0:c0 + 8]
            if nv < 8:
                piece = jnp.where(sub < nv, piece, -jnp.inf)
        pieces.append(piece)
    cand = jnp.concatenate(pieces, axis=0)
    e_cand = jnp.exp(cand - (av[0:1] + bv[0:1]))
    sel, _ = _extract16(cand, first_only)
    sel = jnp.where(sel < float(PEER_TOPK), 1.0, 0.0)
    z = jnp.sum(sel * e_cand, axis=0, keepdims=True)
    lr = jnp.zeros(sa.shape, _F32)
    for g, (r, c0, nv) in enumerate(groups):
        blk = sel[8 * g:8 * g + 8]
        if r is None:
            for q in range(8):
                lr = jnp.where(ra == float(8 + q), blk[q:q + 1], lr)
        elif c0 == 0:
            cnt = jnp.sum(blk, axis=0, keepdims=True)
            if PEER_TOPK // (r + 1) > 8:
                cnt = cnt + jnp.sum(sel[8 * g + 8:8 * g + 16], axis=0, keepdims=True)
            lr = jnp.where(ra == float(r), cnt, lr)
    ea = jnp.exp(sa - av[0:1]) / z
    eb = jnp.exp(sb - bv[0:1])
    n_sel = (jnp.sum(jnp.where(ra < float(PEER_TOPK), 1.0, 0.0), axis=0, keepdims=True)
             + jnp.sum(jnp.where(rb < float(PEER_TOPK), 1.0, 0.0), axis=0, keepdims=True)
             + jnp.sum(sel, axis=0, keepdims=True))
    return lr, ea, rb, eb, n_sel


def _peer_kernel(h2_ref, x_ref, mod_ref, wqt_ref, keys_ref, u_ref, vt_ref, *rest, n_eb, final):
    if final:
        fg_ref, o_ref, h2t_ref, lr_ref, ea_ref, rb_ref, eb_ref, ft_ref, a0_ref, a1_ref, w0_ref, w1_ref = rest
    else:
        o_ref, h2t_ref, lr_ref, ea_ref, rb_ref, eb_ref, ft_ref, a0_ref, a1_ref, w0_ref, w1_ref = rest
    s = pl.program_id(1)
    n_chunks = TM // LANE
    i_per = PEER_EB // PEER_NKEYS
    gdt = rb_ref.dtype

    @pl.when(s == 0)
    def _select():
        h2t = h2_ref[...].astype(_F32).T.astype(_MXU)
        h2t_ref[...] = h2t
        ft_ref[...] = jnp.zeros_like(ft_ref)
        a1_ref[...] = jnp.zeros_like(a1_ref)
        w0_ref[...] = jnp.zeros_like(w0_ref)
        for hp in range(2 * PEER_HEADS):
            qt = _dot(wqt_ref[hp * LANE:(hp + 1) * LANE, :], h2t).astype(_MXU)
            st = _dot(keys_ref[hp % 2], qt)
            for c in range(n_chunks):
                (lr_ref if hp % 2 == 0 else ea_ref)[c, hp // 2] = st[:, c * LANE:(c + 1) * LANE]

        def unit(c, h):
            sa, sb = lr_ref[c, h], ea_ref[c, h]

            def put(lr, ea, rb, eb):
                lr_ref[c, h] = lr
                ea_ref[c, h] = ea
                rb_ref[c, h] = rb.astype(gdt)
                eb_ref[c, h] = eb.astype(gdt)

            lr, ea, rb, eb, n_sel = _peer_select(sa, sb, first_only=False)
            put(lr, ea, rb, eb)

            @pl.when(jnp.max(n_sel) > 3.0 * PEER_TOPK)
            def _ties():
                put(*_peer_select(sa, sb, first_only=True)[:4])

        def body(t, carry):
            c, h2 = t // (PEER_HEADS // 2), t % (PEER_HEADS // 2)
            unit(c, 2 * h2)
            unit(c, 2 * h2 + 1)
            return carry

        lax.fori_loop(0, n_chunks * PEER_HEADS // 2, body, 0)

    def step(a_new, a_prev, w_new, w_prev):
        blk = jnp.clip(s - 1, 0, n_eb - 1)

        def gate(ii, c):
            i = blk * i_per + ii
            rows = slice(ii * PEER_NKEYS, (ii + 1) * PEER_NKEYS)
            cols = slice(c * LANE, (c + 1) * LANE)
            g = jnp.zeros((PEER_NKEYS, LANE), gdt)
            for h in range(PEER_HEADS):
                lr_i = lr_ref[c, h, pl.ds(i, 1), :].astype(gdt)
                ea_i = ea_ref[c, h, pl.ds(i, 1), :].astype(gdt)
                g = g + jnp.where(rb_ref[c, h] < lr_i, eb_ref[c, h], jnp.zeros((), gdt)) * ea_i
            a = a_prev[rows, cols]
            gelu = 0.5 * a * (1.0 + lax.erf(a * (2.0 ** -0.5)))
            w_new[rows, cols] = gelu.astype(gdt) * g

        units = [(ii, c) for ii in range(i_per) for c in range(n_chunks)]
        kd, mr = MXU_DEPTH, PEER_MXU_ROWS
        d_model = u_ref.shape[1]
        n_pieces = (PEER_EB // mr) * (d_model // kd) + (d_model // mr) * (PEER_EB // kd)
        per = len(units) // n_pieces
        done = 0
        for r0 in range(0, PEER_EB, mr):
            acc = None
            for k0 in range(0, d_model, kd):
                part = _dot(u_ref[r0:r0 + mr, k0:k0 + kd], h2t_ref[k0:k0 + kd, :])
                acc = part if acc is None else acc + part
                for ii, c in units[done:done + per]:
                    gate(ii, c)
                done += per
            a_new[r0:r0 + mr, :] = acc
        for r0 in range(0, d_model, mr):
            acc = ft_ref[r0:r0 + mr, :]
            for k0 in range(0, PEER_EB, kd):
                acc = acc + _dot(vt_ref[r0:r0 + mr, k0:k0 + kd], w_prev[k0:k0 + kd, :])
                for ii, c in units[done:done + per]:
                    gate(ii, c)
                done += per
            ft_ref[r0:r0 + mr, :] = acc
        for ii, c in units[done:]:
            gate(ii, c)

    @pl.when(s % 2 == 0)
    def _even():
        step(a0_ref, a1_ref, w1_ref, w0_ref)

    @pl.when(s % 2 == 1)
    def _odd():
        step(a1_ref, a0_ref, w0_ref, w1_ref)

    @pl.when(s == n_eb + 1)
    def _fin():
        x = x_ref[...] + mod_ref[0, 5:6, :] * ft_ref[...].T
        if final:
            x = _rms(x, fg_ref[...])
        o_ref[...] = x


def _peer(h2, x, mod, mod_row, wqt, keys, u, vt, final_g=None):
    n, d = x.shape
    n_exp = u.shape[0]
    n_eb = n_exp // PEER_EB
    n_chunks = TM // LANE
    final = final_g is not None
    args = [h2, x, mod, wqt, keys, u, vt]
    in_specs = [pl.BlockSpec((TM, d), lambda i, e: (i, 0)),
                pl.BlockSpec((TM, d), lambda i, e: (i, 0)),
                pl.BlockSpec((1, N_MOD, d), lambda i, e: (mod_row(i), 0, 0)),
                pl.BlockSpec(wqt.shape, lambda i, e: (0, 0)),
                pl.BlockSpec(keys.shape, lambda i, e: (0, 0, 0)),
                pl.BlockSpec((PEER_EB, d), lambda i, e: (jnp.minimum(e, n_eb - 1), 0)),
                pl.BlockSpec((d, PEER_EB), lambda i, e: (0, jnp.clip(e - 2, 0, n_eb - 1)))]
    if final:
        args.append(final_g)
        in_specs.append(pl.BlockSpec((1, d), lambda i, e: (0, 0)))
    tab32 = pltpu.VMEM((n_chunks, PEER_HEADS, PEER_NKEYS, LANE), _F32)
    tab16 = pltpu.VMEM((n_chunks, PEER_HEADS, PEER_NKEYS, LANE), _MXU)
    abuf = pltpu.VMEM((PEER_EB, TM), _F32)
    wbuf = pltpu.VMEM((PEER_EB, TM), _MXU)
    return pl.pallas_call(
        functools.partial(_peer_kernel, n_eb=n_eb, final=final),
        grid=(n // TM, n_eb + 2),
        in_specs=in_specs,
        out_specs=pl.BlockSpec((TM, d), lambda i, e: (i, 0)),
        out_shape=jax.ShapeDtypeStruct((n, d), _F32),
        scratch_shapes=[pltpu.VMEM((d, TM), _MXU), tab32, tab32, tab16, tab16,
                        pltpu.VMEM((d, TM), _F32), abuf, abuf, wbuf, wbuf],
        compiler_params=_cparams(("parallel", "arbitrary")),
        name="peer_ffn",
    )(*args)


def _dup_halves(w):
    a, b = w[:, :HEAD_DIM], w[:, HEAD_DIM:]
    return jnp.concatenate([a, a, b, b], axis=1)


def _prep_ab(w_in):
    aq, ak, av = w_in[:, 0:512], w_in[:, 512:1024], w_in[:, 1024:1536]
    bq, bk, bv = w_in[:, 1536:2048], w_in[:, 2048:2176], w_in[:, 2176:2304]
    cat = [aq, ak, av, bq, _rot_cols(bq, HEAD_DIM), _dup_halves(bk), _dup_halves(_rot_cols(bk, HEAD_DIM)),
           _dup_halves(bv)]
    return jnp.concatenate(cat, axis=1).astype(_MXU)


def _prep_cd(w_in, w_uq, w_ukv):
    d = w_in.shape[0]
    cq, ckv, kr = w_in[:, 0:256], w_in[:, 256:384], w_in[:, 384:416]
    dq, dk, dv = w_in[:, 416:928], w_in[:, 928:1440], w_in[:, 1440:1952]
    z64, z32 = jnp.zeros((d, 64), _F32), jnp.zeros((d, 32), _F32)
    kr128 = jnp.concatenate([z64, kr, z32], axis=1)
    krrot128 = jnp.concatenate([z64, _rot_cols(kr, MLA_ROPE), z32], axis=1)
    w = jnp.concatenate([cq, ckv, kr128, krrot128, dq, _rot_cols(dq, HEAD_DIM), dk, _rot_cols(dk, HEAD_DIM), dv],
                        axis=1).astype(_MXU)
    r = w_uq.shape[0]
    uq = w_uq.reshape(r, 8, MLA_NOPE + MLA_ROPE)
    nope, rope = uq[:, :, :MLA_NOPE], uq[:, :, MLA_NOPE:]
    rope_rot = _rot_cols(rope.reshape(r, 8 * MLA_ROPE), MLA_ROPE).reshape(r, 8, MLA_ROPE)
    zq = jnp.zeros((r, 8, 32), _F32)
    wqa = jnp.concatenate([nope, rope, zq], axis=2).reshape(r, 1024).astype(_MXU)
    wqb = jnp.concatenate([jnp.zeros_like(nope), rope_rot, zq], axis=2).reshape(r, 1024).astype(_MXU)
    rk = w_ukv.shape[0]
    ukv = w_ukv.reshape(rk, 8, 128)
    wk = jnp.concatenate([ukv[:, :, :MLA_NOPE], jnp.zeros((rk, 8, 64), _F32)], axis=2).reshape(rk, 1024).astype(_MXU)
    wv = ukv[:, :, MLA_NOPE:].reshape(rk, 512).astype(_MXU)
    lane = jnp.arange(LANE)
    src = (lane >= MLA_NOPE) & (lane < MLA_NOPE + MLA_ROPE)
    place = (src[:, None] & (lane[:, None] == (jnp.arange(1024)[None, :] % LANE))).astype(_MXU)
    return w, wqa, wqb, wk, wv, place


def kernel(x, c, ctx, c_ctx, ada_w, ada_b, norm1_g, norm2_g, w_out, peer_wq, peer_keys, peer_u, peer_v,
           ab_w_in, na_rpb, swa_sink, cd_w_in, mla_q_norm_g, mla_w_uq, mla_kv_norm_g, mla_w_ukv,
           diff_lambda, diff_subln_g, final_norm_g):
    batch, seq, d = x.shape
    ctx_len = ctx.shape[1]
    depth = ada_w.shape[0]
    assert seq % TM == 0 and (batch * ctx_len) % TM == 0 and seq % TQ_LOCAL == 0
    assert depth == 2, "even layers keep a context stream, the single odd layer is the last one"
    n_lat, n_ctx = batch * seq, batch * ctx_len
    xs = x.reshape(n_lat, d)
    cs = ctx.reshape(n_ctx, d)

    mod_rows = -(-(batch + 1) // 16) * 16
    cc = jnp.zeros((mod_rows, d), _F32).at[:batch].set(c).at[batch].set(c_ctx)
    mod_all = _modulation(cc, ada_w, ada_b)

    tiles_per_seq = seq // TM
    lat_row = lambda i: i // tiles_per_seq
    ctx_row = lambda i: batch
    lat_tab = lambda i: i % tiles_per_seq
    ctx_tab = lambda i: tiles_per_seq

    cos64, sin64 = _rope_tables(seq, HEAD_DIM)
    cos64p, sin64p = _pad_table(cos64, LANE, 1.0), _pad_table(sin64, LANE, 0.0)

    for l in range(depth):
        last = l == depth - 1
        j = l // 2
        mod = mod_all[l, :batch + 1].reshape(batch + 1, N_MOD, d)
        g1 = norm1_g[l].reshape(1, d)
        g2 = norm2_g[l].reshape(1, d)
        wo = w_out[l].astype(_MXU)
        wqt = peer_wq[l].T.astype(_MXU)
        keys = peer_keys[l].astype(_MXU)
        u = peer_u[l].astype(_MXU)
        vt = peer_v[l].T.astype(_MXU)
        if l % 2 == 0:
            w = _prep_ab(ab_w_in[j])
            aq, ak, av, bq, bk2, bv2 = _proj_ab(xs, mod, lat_row, g1, cos64p, sin64p, lat_tab, w)
            caq, cak, cav, cbq, cbk2, cbv2 = _proj_ab(cs, mod, ctx_row, g1, cos64p, sin64p, ctx_tab, w)
            bias = _na_bias_tables(na_rpb[j], seq // GRID_W)
            ya = _na_attention(aq, ak, av, cak, cav, bias, batch, seq, ctx_len)
            sink = swa_sink[j].astype(_F32)
            yb = _swa_attention(sink, bq, bk2, bv2, cbk2, cbv2, batch, seq, ctx_len)
            xs, h2 = _out_proj(xs, ya, yb, mod, lat_row, g2, wo[:512], wo[512:])
            if not last:
                pair_heads = [(m, half, m, m) for m in range(4) for half in range(2)]
                pair_outs = [("pair", 2 * m, 2 * m + 1) for m in range(4)]
                yca = _flash_attention(caq, cak, cav, heads=pair_heads, outs=pair_outs, batch=batch,
                                       q_per_batch=ctx_len, kv_per_batch=ctx_len, tq=ctx_len, tk=ctx_len,
                                       name="ctx_attn_a")
                gqa_heads = [(m, half, m // 2, m // 2) for m in range(4) for half in range(2)]
                ycb = _flash_attention(cbq, cbk2, cbv2, heads=gqa_heads, outs=pair_outs, batch=batch,
                                       q_per_batch=ctx_len, kv_per_batch=ctx_len, tq=ctx_len, tk=ctx_len,
                                       sink=sink, name="ctx_attn_b")
                cs, h2c = _out_proj(cs, yca, ycb, mod, ctx_row, g2, wo[:512], wo[512:])
        else:
            lam_init = 0.8 - 0.6 * math.exp(-0.3 * l)
            w, wqa, wqb, wk, wv, place = _prep_cd(cd_w_in[j], mla_w_uq[j], mla_w_ukv[j])
            cos32, sin32 = _rope_tables(seq, MLA_ROPE)
            ones64, zeros64 = jnp.ones((seq, 64), _F32), jnp.zeros((seq, 64), _F32)
            cosq = _pad_table(jnp.concatenate([ones64, cos32, ones64[:, :32]], 1), LANE, 1.0)
            sinq = _pad_table(jnp.concatenate([zeros64, sin32, zeros64[:, :32]], 1), LANE, 0.0)
            tabs = (cosq, sinq, cos64p, sin64p)
            qg = mla_q_norm_g[j].reshape(1, -1)
            kvg = mla_kv_norm_g[j].reshape(1, -1)
            qmt, km, vmt, dqt, dk, dvt = _proj_cd(xs, mod, lat_row, g1, tabs, lat_tab, w, qg, kvg, wqa, wqb, wk, wv, place)
            _, ckm, cvmt, _, cdk, cdvt = _proj_cd(cs, mod, ctx_row, g1, tabs, ctx_tab, w, qg, kvg, wqa, wqb, wk, wv, place)
            mla_heads = [(h, None, h, h // 2) for h in range(8)]
            pair_outs = [("pair", 2 * m, 2 * m + 1) for m in range(4)]
            yc = _flasht_attention(qmt, km, vmt, ckm, cvmt, heads=mla_heads, outs=pair_outs, batch=batch, seq=seq,
                                   ctx_len=ctx_len, tq=GLOBAL_TQ, tk=GLOBAL_TK, name="mla_attn")
            diff_heads = [(h, a, h, h) for h in range(4) for a in range(2)]
            diff_outs = [("diff", 2 * h, 2 * h + 1) for h in range(4)]
            od = _flasht_attention(dqt, dk, dvt, cdk, cdvt, heads=diff_heads, outs=diff_outs, batch=batch, seq=seq,
                                   ctx_len=ctx_len, tq=GLOBAL_TQ, tk=GLOBAL_TK,
                                   diff=(diff_lambda[j].astype(_F32), diff_subln_g[j].reshape(1, -1), lam_init),
                                   name="diff_attn")
            xs, h2 = _out_proj(xs, yc, od, mod, lat_row, g2, wo[:512], wo[512:])
        fg = final_norm_g.reshape(1, d) if last else None
        xs = _peer(h2, xs, mod, lat_row, wqt, keys, u, vt, final_g=fg)
        if not last:
            cs = _peer(h2c, cs, mod, ctx_row, wqt, keys, u, vt)
    return xs.reshape(batch, seq, d)
```

```python
import functools
import math

import jax
import jax.numpy as jnp
from jax import lax
from jax.experimental import pallas as pl
from jax.experimental.pallas import tpu as pltpu

_F32 = jnp.float32
_MXU = jnp.bfloat16
_NEG = -1e30
_LOG2E = math.log2(math.e)

GRID_W = 64
HEAD_DIM = 64
ROPE_BASE = 10000.0
RMS_EPS = 1e-6
N_MOD = 6
NA_KR, NA_KC = 8, 16
SWA_WINDOW = 128
MLA_NOPE, MLA_ROPE = 64, 32
PEER_HEADS, PEER_NKEYS, PEER_TOPK = 8, 128, 16

LANE = 128
TM = 512
TQ_LOCAL = 256
NA_GROUP_ROWS = TQ_LOCAL // GRID_W
NA_WIN_ROWS = NA_KR + NA_GROUP_ROWS - 1
GLOBAL_TQ = 512
GLOBAL_TK = 512
PEER_EB = 1024
MXU_DEPTH = 256
PEER_MXU_ROWS = 512
PEER_SELECT_UNROLL = 8
VMEM_LIMIT = 56 * 1024 * 1024


def _cparams(sem, vmem=VMEM_LIMIT):
    return pltpu.CompilerParams(dimension_semantics=sem, vmem_limit_bytes=vmem)


def _dot(a, b):
    return jnp.dot(a, b, preferred_element_type=_F32)


def _dot_nt(a, b):
    return lax.dot_general(a, b, (((1,), (1,)), ((), ())), preferred_element_type=_F32)


def _rms(x, g):
    return x * lax.rsqrt(jnp.mean(x * x, axis=-1, keepdims=True) + RMS_EPS) * g


def _lane_half(shape):
    return lax.broadcasted_iota(jnp.int32, shape, len(shape) - 1) >= (LANE // 2)


def _mod_kernel(c_ref, w_ref, b_ref, o_ref):
    c = c_ref[...]
    a = c * jax.nn.sigmoid(c)
    w = w_ref[0]
    a_hi = a.astype(_MXU)
    a_lo = (a - a_hi.astype(_F32)).astype(_MXU)
    w_hi = w.astype(_MXU)
    w_lo = (w - w_hi.astype(_F32)).astype(_MXU)
    o_ref[0] = _dot(a_hi, w_hi) + _dot(a_lo, w_hi) + _dot(a_hi, w_lo) + b_ref[0]


def _modulation(cc, ada_w, ada_b):
    depth, d, n = ada_w.shape
    rows = cc.shape[0]
    tn = 768
    return pl.pallas_call(
        _mod_kernel,
        grid=(depth, n // tn),
        in_specs=[pl.BlockSpec((rows, d), lambda l, j: (0, 0)),
                  pl.BlockSpec((1, d, tn), lambda l, j: (l, 0, j)),
                  pl.BlockSpec((1, 1, tn), lambda l, j: (l, 0, j))],
        out_specs=pl.BlockSpec((1, rows, tn), lambda l, j: (l, 0, j)),
        out_shape=jax.ShapeDtypeStruct((depth, rows, n), _F32),
        compiler_params=_cparams(("parallel", "parallel")),
        name="adaln_mod",
    )(cc, ada_w, ada_b.reshape(depth, 1, n))


def _rope_tables(seq, d):
    t = jnp.arange(seq, dtype=jnp.int32)
    q = d // 4
    freq = ROPE_BASE ** (-jnp.arange(q, dtype=_F32) / q)

    def one(pos):
        ang = pos.astype(_F32)[:, None] * freq[None, :]
        return jnp.concatenate([jnp.cos(ang)] * 2, -1), jnp.concatenate([jnp.sin(ang)] * 2, -1)

    cr, sr = one(t // GRID_W)
    cc, sc = one(t % GRID_W)
    return jnp.concatenate([cr, cc], -1), jnp.concatenate([sr, sc], -1)


def _rot_cols(w, d):
    k, n = w.shape
    w5 = w.reshape(k, n // d, 2, 2, d // 4)
    return jnp.stack([-w5[:, :, :, 1], w5[:, :, :, 0]], axis=3).reshape(k, n)


def _pad_table(tab, width, ident):
    s, w = tab.shape
    if w < width:
        reps = width // w
        tab = jnp.tile(tab, (1, reps))
    return jnp.concatenate([tab, jnp.full((TM, width), ident, _F32)], axis=0)


def _proj_ab_kernel(x_ref, mod_ref, g_ref, cos_ref, sin_ref, w_ref,
                    aq_ref, ak_ref, av_ref, bq_ref, bk_ref, bv_ref):
    x = x_ref[...]
    h = _rms(x, g_ref[...]) * (1.0 + mod_ref[0, 1:2, :]) + mod_ref[0, 0:1, :]
    hb = h.astype(_MXU)
    cos = cos_ref[...]
    sin = sin_ref[...]

    def proj(lo, n):
        return _dot(hb, w_ref[:, lo:lo + n])

    scale = HEAD_DIM ** -0.5
    aq_ref[...] = (proj(0, 512) * scale).astype(aq_ref.dtype)
    ak_ref[...] = proj(512, 512).astype(ak_ref.dtype)
    av_ref[...] = proj(1024, 512).astype(av_ref.dtype)
    cos4 = jnp.tile(cos, (1, 4))
    sin4 = jnp.tile(sin, (1, 4))
    bq_ref[...] = ((proj(1536, 512) * cos4 + proj(2048, 512) * sin4) * scale).astype(bq_ref.dtype)
    cos2 = jnp.tile(cos, (1, 2))
    sin2 = jnp.tile(sin, (1, 2))
    bk_ref[...] = (proj(2560, 256) * cos2 + proj(2816, 256) * sin2).astype(bk_ref.dtype)
    bv_ref[...] = proj(3072, 256).astype(bv_ref.dtype)


def _proj_ab(x, mod, mod_row, g, cos, sin, tab_row, w):
    n, d = x.shape
    widths = (512, 512, 512, 512, 256, 256)
    return pl.pallas_call(
        _proj_ab_kernel,
        grid=(n // TM,),
        in_specs=[pl.BlockSpec((TM, d), lambda i: (i, 0)),
                  pl.BlockSpec((1, N_MOD, d), lambda i: (mod_row(i), 0, 0)),
                  pl.BlockSpec((1, d), lambda i: (0, 0)),
                  pl.BlockSpec((TM, LANE), lambda i: (tab_row(i), 0)),
                  pl.BlockSpec((TM, LANE), lambda i: (tab_row(i), 0)),
                  pl.BlockSpec(w.shape, lambda i: (0, 0))],
        out_specs=[pl.BlockSpec((TM, wd), lambda i: (i, 0)) for wd in widths],
        out_shape=[jax.ShapeDtypeStruct((n, wd), _MXU) for wd in widths],
        compiler_params=_cparams(("parallel",)),
        name="proj_ab",
    )(x, mod, g, cos, sin, w)


def _na_kernel(q_ref, k_ref, v_ref, kc_ref, vc_ref, bias_ref, o_ref, *, rows):
    g = pl.program_id(1)
    ks = jnp.clip(NA_GROUP_ROWS * g - NA_KR // 2, 0, rows - NA_WIN_ROWS)
    start = pl.multiple_of(ks * GRID_W, GRID_W)
    nwin = NA_WIN_ROWS * GRID_W
    hi = _lane_half((TQ_LOCAL, LANE))
    for m in range(4):
        cols = slice(m * LANE, (m + 1) * LANE)
        q2 = q_ref[:, cols]
        k2 = k_ref[pl.ds(start, nwin), cols]
        v2 = v_ref[pl.ds(start, nwin), cols]
        kc2 = kc_ref[:, cols]
        vc2 = vc_ref[:, cols]
        outs = []
        for half in range(2):
            qh = jnp.where(hi == (half == 1), q2, jnp.zeros_like(q2))
            s_lat = _dot_nt(qh, k2) + bias_ref[0, 2 * m + half]
            s_ctx = _dot_nt(qh, kc2)
            mx = jnp.maximum(jnp.max(s_lat, axis=-1, keepdims=True), jnp.max(s_ctx, axis=-1, keepdims=True))
            p_lat = jnp.exp(s_lat - mx)
            p_ctx = jnp.exp(s_ctx - mx)
            den = jnp.sum(p_lat, axis=-1, keepdims=True) + jnp.sum(p_ctx, axis=-1, keepdims=True)
            o = _dot(p_lat.astype(_MXU), v2) + _dot(p_ctx.astype(_MXU), vc2)
            outs.append(o / den)
        o_ref[:, cols] = jnp.where(hi, outs[1], outs[0]).astype(o_ref.dtype)


def _na_bias_tables(rpb, rows):
    gq = NA_GROUP_ROWS
    n_groups = rows // gq
    ql = jnp.arange(gq)[:, None]
    kl = jnp.arange(NA_WIN_ROWS)[None, :]
    qc = jnp.arange(GRID_W)[:, None]
    kc = jnp.arange(GRID_W)[None, :]
    cs = jnp.clip(qc - NA_KC // 2, 0, GRID_W - NA_KC)
    col_valid = (kc >= cs) & (kc < cs + NA_KC)
    col_idx = jnp.clip(kc - qc + NA_KC - 1, 0, 2 * NA_KC - 2)
    exact = lax.Precision.HIGHEST
    oh_c = jax.nn.one_hot(col_idx, 2 * NA_KC - 1, dtype=_F32)
    rpb_cols = jnp.einsum("hab,xyb->haxy", rpb.astype(_F32), oh_c, precision=exact)
    tabs = []
    for g in (0, 1, n_groups - 1):
        ks = min(max(gq * g - NA_KR // 2, 0), rows - NA_WIN_ROWS)
        rq = gq * g + ql
        rk = ks + kl
        r0 = jnp.clip(rq - NA_KR // 2, 0, rows - NA_KR)
        row_valid = (rk >= r0) & (rk < r0 + NA_KR)
        row_idx = jnp.clip(rk - rq + NA_KR - 1, 0, 2 * NA_KR - 2)
        oh_r = jax.nn.one_hot(row_idx, 2 * NA_KR - 1, dtype=_F32)
        b = jnp.einsum("qka,haxy->hqxky", oh_r, rpb_cols, precision=exact)
        valid = row_valid[:, None, :, None] & col_valid[None, :, None, :]
        b = jnp.where(valid[None], b, _NEG)
        tabs.append(b.reshape(rpb.shape[0], gq * GRID_W, NA_WIN_ROWS * GRID_W))
    return jnp.stack(tabs)


def _na_attention(aq, ak, av, cak, cav, bias, batch, seq, ctx_len):
    rows = seq // GRID_W
    n_groups = seq // TQ_LOCAL
    nwin = NA_WIN_ROWS * GRID_W

    def bias_row(b, g):
        return (jnp.where(g == 0, 0, jnp.where(g == n_groups - 1, 2, 1)), 0, 0, 0)

    return pl.pallas_call(
        functools.partial(_na_kernel, rows=rows),
        grid=(batch, n_groups),
        in_specs=[pl.BlockSpec((TQ_LOCAL, 512), lambda b, g: (b * n_groups + g, 0)),
                  pl.BlockSpec((seq, 512), lambda b, g: (b, 0)),
                  pl.BlockSpec((seq, 512), lambda b, g: (b, 0)),
                  pl.BlockSpec((ctx_len, 512), lambda b, g: (b, 0)),
                  pl.BlockSpec((ctx_len, 512), lambda b, g: (b, 0)),
                  pl.BlockSpec((1, 8, TQ_LOCAL, nwin), bias_row)],
        out_specs=pl.BlockSpec((TQ_LOCAL, 512), lambda b, g: (b * n_groups + g, 0)),
        out_shape=jax.ShapeDtypeStruct((batch * seq, 512), _MXU),
        compiler_params=_cparams(("parallel", "arbitrary")),
        name="na_attention",
    )(aq, ak, av, cak, cav, bias)


def _swa_kernel(sink_ref, q_ref, k_ref, v_ref, kc_ref, vc_ref, o_ref, *, seq):
    t = pl.program_id(1)
    kwin = TQ_LOCAL + 2 * SWA_WINDOW
    start = t * TQ_LOCAL
    kstart = pl.multiple_of(jnp.clip(start - SWA_WINDOW, 0, seq - kwin), LANE)
    qpos = start + lax.broadcasted_iota(jnp.int32, (TQ_LOCAL, kwin), 0)
    kpos = kstart + lax.broadcasted_iota(jnp.int32, (TQ_LOCAL, kwin), 1)
    mask = jnp.where(jnp.abs(kpos - qpos) <= SWA_WINDOW, 0.0, _NEG).astype(_F32)
    hi = _lane_half((TQ_LOCAL, LANE))
    for m in range(4):
        cols = slice(m * LANE, (m + 1) * LANE)
        kv = m // 2
        kcols = slice(kv * LANE, (kv + 1) * LANE)
        q2 = q_ref[:, cols]
        k2 = k_ref[pl.ds(kstart, kwin), kcols]
        v2 = v_ref[pl.ds(kstart, kwin), kcols]
        kc2 = kc_ref[:, kcols]
        vc2 = vc_ref[:, kcols]
        outs = []
        for half in range(2):
            sink = sink_ref[2 * m + half]
            qh = jnp.where(hi == (half == 1), q2, jnp.zeros_like(q2))
            s_lat = _dot_nt(qh, k2) + mask
            s_ctx = _dot_nt(qh, kc2)
            mx = jnp.maximum(jnp.max(s_lat, axis=-1, keepdims=True), jnp.max(s_ctx, axis=-1, keepdims=True))
            mx = jnp.maximum(mx, sink)
            p_lat = jnp.exp(s_lat - mx)
            p_ctx = jnp.exp(s_ctx - mx)
            den = (jnp.sum(p_lat, axis=-1, keepdims=True) + jnp.sum(p_ctx, axis=-1, keepdims=True)
                   + jnp.exp(sink - mx))
            o = _dot(p_lat.astype(_MXU), v2) + _dot(p_ctx.astype(_MXU), vc2)
            outs.append(o / den)
        o_ref[:, cols] = jnp.where(hi, outs[1], outs[0]).astype(o_ref.dtype)


def _swa_attention(sink, bq, bk2, bv2, cbk2, cbv2, batch, seq, ctx_len):
    n_t = seq // TQ_LOCAL
    return pl.pallas_call(
        functools.partial(_swa_kernel, seq=seq),
        grid=(batch, n_t),
        in_specs=[pl.BlockSpec(memory_space=pltpu.SMEM),
                  pl.BlockSpec((TQ_LOCAL, 512), lambda b, t: (b * n_t + t, 0)),
                  pl.BlockSpec((seq, 256), lambda b, t: (b, 0)),
                  pl.BlockSpec((seq, 256), lambda b, t: (b, 0)),
                  pl.BlockSpec((ctx_len, 256), lambda b, t: (b, 0)),
                  pl.BlockSpec((ctx_len, 256), lambda b, t: (b, 0))],
        out_specs=pl.BlockSpec((TQ_LOCAL, 512), lambda b, t: (b * n_t + t, 0)),
        out_shape=jax.ShapeDtypeStruct((batch * seq, 512), _MXU),
        compiler_params=_cparams(("parallel", "arbitrary")),
        name="swa_attention",
    )(sink, bq, bk2, bv2, cbk2, cbv2)


def _flash_kernel(*refs, heads, outs, has_ctx, has_sink, diff_cfg, n_kv):
    it = iter(refs)
    sink_ref = next(it) if has_sink else None
    q_ref, k_ref, v_ref = next(it), next(it), next(it)
    kc_ref = next(it) if has_ctx else None
    vc_ref = next(it) if has_ctx else None
    lam_ref = next(it) if diff_cfg else None
    sg_ref = next(it) if diff_cfg else None
    o_ref = next(it)
    m_ref, l_ref, acc_ref = next(it), next(it), next(it)
    kv = pl.program_id(2)
    tq = q_ref.shape[0]
    hi = _lane_half((tq, LANE))

    @pl.when(kv == 0)
    def _init():
        for h in range(len(heads)):
            if has_sink:
                m_ref[h] = jnp.full((tq, LANE), sink_ref[h], _F32)
                l_ref[h] = jnp.ones((tq, LANE), _F32)
            else:
                m_ref[h] = jnp.full((tq, LANE), _NEG, _F32)
                l_ref[h] = jnp.zeros((tq, LANE), _F32)
            acc_ref[h] = jnp.zeros((tq, LANE), _F32)

    def attend(kr, vr):
        for h, (qb, qhalf, kb, vb) in enumerate(heads):
            q2 = q_ref[:, qb * LANE:(qb + 1) * LANE]
            if qhalf is not None:
                q2 = jnp.where(hi == (qhalf == 1), q2, jnp.zeros_like(q2))
            s = _dot_nt(q2, kr[:, kb * LANE:(kb + 1) * LANE])
            m_old = m_ref[h][:, :1]
            m_new = jnp.maximum(m_old, jnp.max(s, axis=-1, keepdims=True))
            alpha = jnp.exp(m_old - m_new)
            p = jnp.exp(s - m_new)
            l_ref[h] = jnp.broadcast_to(alpha * l_ref[h][:, :1] + jnp.sum(p, axis=-1, keepdims=True), (tq, LANE))
            acc_ref[h] = alpha * acc_ref[h] + _dot(p.astype(_MXU), vr[:, vb * LANE:(vb + 1) * LANE])
            m_ref[h] = jnp.broadcast_to(m_new, (tq, LANE))

    if has_ctx:
        @pl.when(kv == 0)
        def _ctx():
            attend(kc_ref, vc_ref)

    attend(k_ref, v_ref)

    @pl.when(kv == n_kv - 1)
    def _fin():
        def head_out(h):
            return acc_ref[h] / l_ref[h][:, :1]

        for j, spec in enumerate(outs):
            if spec[0] == "full":
                o = head_out(spec[1])
            elif spec[0] == "pair":
                o = jnp.where(hi, head_out(spec[2]), head_out(spec[1]))
            else:
                lv = lam_ref[...]
                lam = (jnp.exp(jnp.sum(lv[0:1] * lv[1:2], axis=-1, keepdims=True))
                       - jnp.exp(jnp.sum(lv[2:3] * lv[3:4], axis=-1, keepdims=True)) + diff_cfg)
                o = _rms(head_out(spec[1]) - lam * head_out(spec[2]), sg_ref[...]) * (1.0 - diff_cfg)
            o_ref[:, j * LANE:(j + 1) * LANE] = o.astype(o_ref.dtype)


def _flash_attention(q, k, v, *, heads, outs, batch, q_per_batch, kv_per_batch, tq, tk,
                     q_row0=0, kv_row0=0, ctx=None, ctx_len=0, ctx_row0=0, sink=None, diff=None, name="flash"):
    n_q = q_per_batch // tq
    n_kv = kv_per_batch // tk
    qw, kw, vw = q.shape[1], k.shape[1], v.shape[1]
    args, in_specs = [], []
    if sink is not None:
        args.append(sink)
        in_specs.append(pl.BlockSpec(memory_space=pltpu.SMEM))
    q0, k0 = q_row0 // tq, kv_row0 // tk
    args += [q, k, v]
    in_specs += [pl.BlockSpec((tq, qw), lambda b, i, j: (q0 + b * n_q + i, 0)),
                 pl.BlockSpec((tk, kw), lambda b, i, j: (k0 + b * n_kv + j, 0)),
                 pl.BlockSpec((tk, vw), lambda b, i, j: (k0 + b * n_kv + j, 0))]
    if ctx is not None:
        c0 = ctx_row0 // ctx_len
        args += [ctx[0], ctx[1]]
        in_specs += [pl.BlockSpec((ctx_len, kw), lambda b, i, j: (c0 + b, 0)),
                     pl.BlockSpec((ctx_len, vw), lambda b, i, j: (c0 + b, 0))]
    diff_cfg = None
    if diff is not None:
        lam_vecs, subln_g, diff_cfg = diff
        args += [lam_vecs, subln_g]
        in_specs += [pl.BlockSpec(lam_vecs.shape, lambda b, i, j: (0, 0)),
                     pl.BlockSpec(subln_g.shape, lambda b, i, j: (0, 0))]
    nh = len(heads)
    ow = len(outs) * LANE
    return pl.pallas_call(
        functools.partial(_flash_kernel, heads=tuple(heads), outs=tuple(outs), has_ctx=ctx is not None,
                          has_sink=sink is not None, diff_cfg=diff_cfg, n_kv=n_kv),
        grid=(batch, n_q, n_kv),
        in_specs=in_specs,
        out_specs=pl.BlockSpec((tq, ow), lambda b, i, j: (b * n_q + i, 0)),
        out_shape=jax.ShapeDtypeStruct((batch * q_per_batch, ow), _MXU),
        scratch_shapes=[pltpu.VMEM((nh, tq, LANE), _F32)] * 3,
        compiler_params=_cparams(("parallel", "parallel", "arbitrary")),
        name=name,
    )(*args)


def _flasht_kernel(*refs, heads, outs, diff_cfg, n_kv):
    it = iter(refs)
    qt_ref, k_ref, vt_ref, kc_ref, vct_ref = next(it), next(it), next(it), next(it), next(it)
    lam_ref = next(it) if diff_cfg else None
    sg_ref = next(it) if diff_cfg else None
    o_ref = next(it)
    m_ref, l_ref, acc_ref = next(it), next(it), next(it)
    kv = pl.program_id(2)
    tq = qt_ref.shape[1]
    row_hi = lax.broadcasted_iota(jnp.int32, (LANE, tq), 0) >= (LANE // 2)

    @pl.when(kv == 0)
    def _init():
        m_ref[...] = jnp.full(m_ref.shape, _NEG, _F32)
        l_ref[...] = jnp.zeros(l_ref.shape, _F32)
        acc_ref[...] = jnp.zeros(acc_ref.shape, _F32)

    def attend(kr, vtr):
        def scores(h):
            qb, qhalf, kb, _ = heads[h]
            qt = qt_ref[qb * LANE:(qb + 1) * LANE, :]
            if qhalf is not None:
                qt = jnp.where(row_hi == (qhalf == 1), qt, jnp.zeros_like(qt))
            return _dot(kr[:, kb * LANE:(kb + 1) * LANE], qt)

        s_next = scores(0)
        for h in range(len(heads)):
            s = s_next
            if h + 1 < len(heads):
                s_next = scores(h + 1)
            vb = heads[h][3]
            m_old = m_ref[h, 0:1, :]
            m_new = jnp.maximum(m_old, jnp.max(s, axis=0, keepdims=True))
            alpha = jnp.exp2(m_old - m_new)
            p = jnp.exp2(s - m_new)
            l_new = alpha * l_ref[h, 0:1, :] + jnp.sum(p, axis=0, keepdims=True)
            acc_ref[h] = alpha * acc_ref[h] + _dot(vtr[vb * LANE:(vb + 1) * LANE, :], p.astype(_MXU))
            l_ref[h] = jnp.broadcast_to(l_new, (8, tq))
            m_ref[h] = jnp.broadcast_to(m_new, (8, tq))

    @pl.when(kv == 0)
    def _ctx():
        attend(kc_ref, vct_ref)

    attend(k_ref, vt_ref)

    @pl.when(kv == n_kv - 1)
    def _fin():
        def head_out(h):
            return acc_ref[h] / l_ref[h, 0:1, :]

        for j, spec in enumerate(outs):
            if spec[0] == "full":
                o = head_out(spec[1]).T
            elif spec[0] == "pair":
                o = jnp.where(row_hi, head_out(spec[2]), head_out(spec[1])).T
            else:
                lv = lam_ref[...]
                lam = (jnp.exp(jnp.sum(lv[0:1] * lv[1:2], axis=-1, keepdims=True))
                       - jnp.exp(jnp.sum(lv[2:3] * lv[3:4], axis=-1, keepdims=True)) + diff_cfg)
                o = _rms((head_out(spec[1]) - lam * head_out(spec[2])).T, sg_ref[...]) * (1.0 - diff_cfg)
            o_ref[:, j * LANE:(j + 1) * LANE] = o.astype(o_ref.dtype)


def _flasht_attention(qt, k, vt, kc, vct, *, heads, outs, batch, seq, ctx_len, tq, tk, diff=None, name="flasht"):
    n_q, n_kv = seq // tq, seq // tk
    qw, kw, vw = qt.shape[0], k.shape[1], vt.shape[0]
    args = [qt, k, vt, kc, vct]
    in_specs = [pl.BlockSpec((qw, tq), lambda b, i, j: (0, b * n_q + i)),
                pl.BlockSpec((tk, kw), lambda b, i, j: (b * n_kv + j, 0)),
                pl.BlockSpec((vw, tk), lambda b, i, j: (0, b * n_kv + j)),
                pl.BlockSpec((ctx_len, kw), lambda b, i, j: (b, 0)),
                pl.BlockSpec((vw, ctx_len), lambda b, i, j: (0, b))]
    diff_cfg = None
    if diff is not None:
        lam_vecs, subln_g, diff_cfg = diff
        args += [lam_vecs, subln_g]
        in_specs += [pl.BlockSpec(lam_vecs.shape, lambda b, i, j: (0, 0)),
                     pl.BlockSpec(subln_g.shape, lambda b, i, j: (0, 0))]
    nh = len(heads)
    ow = len(outs) * LANE
    return pl.pallas_call(
        functools.partial(_flasht_kernel, heads=tuple(heads), outs=tuple(outs), diff_cfg=diff_cfg, n_kv=n_kv),
        grid=(batch, n_q, n_kv),
        in_specs=in_specs,
        out_specs=pl.BlockSpec((tq, ow), lambda b, i, j: (b * n_q + i, 0)),
        out_shape=jax.ShapeDtypeStruct((batch * seq, ow), _MXU),
        scratch_shapes=[pltpu.VMEM((nh, 8, tq), _F32), pltpu.VMEM((nh, 8, tq), _F32),
                        pltpu.VMEM((nh, LANE, tq), _F32)],
        compiler_params=_cparams(("parallel", "parallel", "arbitrary")),
        name=name,
    )(*args)


def _out_kernel(x_ref, ya_ref, yb_ref, mod_ref, g2_ref, wa_ref, wb_ref, xo_ref, h2_ref):
    y = _dot(ya_ref[...], wa_ref[...]) + _dot(yb_ref[...], wb_ref[...])
    x = x_ref[...] + mod_ref[0, 2:3, :] * y
    xo_ref[...] = x
    h2 = _rms(x, g2_ref[...]) * (1.0 + mod_ref[0, 4:5, :]) + mod_ref[0, 3:4, :]
    h2_ref[...] = h2.astype(h2_ref.dtype)


def _out_proj(x, ya, yb, mod, mod_row, g2, wa, wb):
    n, d = x.shape
    return pl.pallas_call(
        _out_kernel,
        grid=(n // TM,),
        in_specs=[pl.BlockSpec((TM, d), lambda i: (i, 0)),
                  pl.BlockSpec((TM, ya.shape[1]), lambda i: (i, 0)),
                  pl.BlockSpec((TM, yb.shape[1]), lambda i: (i, 0)),
                  pl.BlockSpec((1, N_MOD, d), lambda i: (mod_row(i), 0, 0)),
                  pl.BlockSpec((1, d), lambda i: (0, 0)),
                  pl.BlockSpec(wa.shape, lambda i: (0, 0)),
                  pl.BlockSpec(wb.shape, lambda i: (0, 0))],
        out_specs=[pl.BlockSpec((TM, d), lambda i: (i, 0)), pl.BlockSpec((TM, d), lambda i: (i, 0))],
        out_shape=[jax.ShapeDtypeStruct((n, d), _F32), jax.ShapeDtypeStruct((n, d), _MXU)],
        compiler_params=_cparams(("parallel",)),
        name="out_proj",
    )(x, ya, yb, mod, g2, wa, wb)


def _proj_cd_kernel(x_ref, mod_ref, g_ref, cq_ref, sq_ref, cd_ref, sd_ref, w_ref, qg_ref, kvg_ref,
                    wqa_ref, wqb_ref, wk_ref, wv_ref, place_ref,
                    qm_ref, km_ref, vm_ref, dq_ref, dk_ref, dv_ref):
    x = x_ref[...]
    h = _rms(x, g_ref[...]) * (1.0 + mod_ref[0, 1:2, :]) + mod_ref[0, 0:1, :]
    hb = h.astype(_MXU)

    def proj(lo, n):
        return _dot(hb, w_ref[:, lo:lo + n])

    cosq, sinq = cq_ref[...], sq_ref[...]
    cosd, sind = cd_ref[...], sd_ref[...]
    cqn = _rms(proj(0, 256), qg_ref[...]).astype(_MXU)
    ckvn = _rms(proj(256, 128), kvg_ref[...]).astype(_MXU)
    cos8, sin8 = jnp.tile(cosq, (1, 8)), jnp.tile(sinq, (1, 8))
    qm = _dot(cqn, wqa_ref[...]) * cos8 + _dot(cqn, wqb_ref[...]) * sin8
    qm_ref[...] = (qm * ((MLA_NOPE + MLA_ROPE) ** -0.5 * _LOG2E)).T.astype(qm_ref.dtype)
    kr = (proj(384, 128) * cosq + proj(512, 128) * sinq).astype(_MXU)
    km_ref[...] = (_dot(ckvn, wk_ref[...]) + _dot(kr, place_ref[...])).astype(km_ref.dtype)
    vm_ref[...] = _dot(ckvn, wv_ref[...]).T.astype(vm_ref.dtype)
    cos4, sin4 = jnp.tile(cosd, (1, 4)), jnp.tile(sind, (1, 4))
    dq_ref[...] = ((proj(640, 512) * cos4 + proj(1152, 512) * sin4) * (HEAD_DIM ** -0.5 * _LOG2E)).T.astype(dq_ref.dtype)
    dk_ref[...] = (proj(1664, 512) * cos4 + proj(2176, 512) * sin4).astype(dk_ref.dtype)
    dv_ref[...] = proj(2688, 512).T.astype(dv_ref.dtype)


def _proj_cd(x, mod, mod_row, g, tabs, tab_row, w, qg, kvg, wqa, wqb, wk, wv, place):
    n, d = x.shape
    outs = ((1024, True), (1024, False), (512, True), (512, True), (512, False), (512, True))
    full = lambda a: pl.BlockSpec(a.shape, lambda i: (0, 0))
    return pl.pallas_call(
        _proj_cd_kernel,
        grid=(n // TM,),
        in_specs=[pl.BlockSpec((TM, d), lambda i: (i, 0)),
                  pl.BlockSpec((1, N_MOD, d), lambda i: (mod_row(i), 0, 0)),
                  pl.BlockSpec((1, d), lambda i: (0, 0))]
                 + [pl.BlockSpec((TM, LANE), lambda i: (tab_row(i), 0))] * 4
                 + [full(a) for a in (w, qg, kvg, wqa, wqb, wk, wv, place)],
        out_specs=[pl.BlockSpec((wd, TM), lambda i: (0, i)) if fm else pl.BlockSpec((TM, wd), lambda i: (i, 0))
                   for wd, fm in outs],
        out_shape=[jax.ShapeDtypeStruct((wd, n) if fm else (n, wd), _MXU) for wd, fm in outs],
        compiler_params=_cparams(("parallel",)),
        name="proj_cd",
    )(x, mod, g, *tabs, w, qg, kvg, wqa, wqb, wk, wv, place)


_MARK = 2.0 ** 121


def _extract16(x, first_only):
    r = x.shape[0]
    row = lax.broadcasted_iota(jnp.int32, x.shape, 0)
    row16 = lax.broadcasted_iota(jnp.int32, (PEER_TOPK, LANE), 0)
    vals = jnp.zeros((PEER_TOPK, LANE), _F32)
    for k in range(PEER_TOPK):
        m = jnp.max(x, axis=0, keepdims=True)
        hit = x == m
        if first_only:
            hit = row == jnp.min(jnp.where(hit, row, r), axis=0, keepdims=True)
        x = jnp.where(hit, -_MARK * (32 + k), x)
        vals = jnp.where(row16 == k, m, vals)
    order = jnp.where(x < -16.0 * _MARK, x * (-1.0 / _MARK) - 32.0, float(PEER_TOPK))
    return order, vals


def _staircase_rows():
    groups = []
    for r in range(PEER_TOPK // 2):
        n = PEER_TOPK // (r + 1)
        for c0 in range(0, n, 8):
            groups.append((r, c0, min(8, n - c0)))
    groups.append((None, 0, 8))
    return groups


def _peer_select(sa, sb, first_only):
    ra, av = _extract16(sa, first_only)
    rb, bv = _extract16(sb, first_only)
    groups = _staircase_rows()
    sub = lax.broadcasted_iota(jnp.int32, (8, LANE), 0)
    pieces = []
    for r, c0, nv in groups:
        if r is None:
            piece = av[8:16] + bv[0:1]
        else:
            piece = av[r:r + 1] + bv[c0:c0 + 8]
            if nv < 8:
                piece = jnp.where(sub < nv, piece, -jnp.inf)
        pieces.append(piece)
    cand = jnp.concatenate(pieces, axis=0)
    e_cand = jnp.exp(cand - (av[0:1] + bv[0:1]))
    sel, _ = _extract16(cand, first_only)
    sel = jnp.where(sel < float(PEER_TOPK), 1.0, 0.0)
    z = jnp.sum(sel * e_cand, axis=0, keepdims=True)
    lr = jnp.zeros(sa.shape, _F32)
    for g, (r, c0, nv) in enumerate(groups):
        blk = sel[8 * g:8 * g + 8]
        if r is None:
            for q in range(8):
                lr = jnp.where(ra == float(8 + q), blk[q:q + 1], lr)
        elif c0 == 0:
            cnt = jnp.sum(blk, axis=0, keepdims=True)
            if PEER_TOPK // (r + 1) > 8:
                cnt = cnt + jnp.sum(sel[8 * g + 8:8 * g + 16], axis=0, keepdims=True)
            lr = jnp.where(ra == float(r), cnt, lr)
    ea = jnp.exp(sa - av[0:1]) / z
    eb = jnp.exp(sb - bv[0:1])
    n_sel = (jnp.sum(jnp.where(ra < float(PEER_TOPK), 1.0, 0.0), axis=0, keepdims=True)
             + jnp.sum(jnp.where(rb < float(PEER_TOPK), 1.0, 0.0), axis=0, keepdims=True)
             + jnp.sum(sel, axis=0, keepdims=True))
    return lr, ea, rb, eb, n_sel


def _peer_kernel(h2_ref, x_ref, mod_ref, wqt_ref, keys_ref, u_ref, vt_ref, *rest, n_eb, final):
    if final:
        fg_ref, o_ref, h2t_ref, lr_ref, ea_ref, rb_ref, eb_ref, ft_ref, a0_ref, a1_ref, w0_ref, w1_ref = rest
    else:
        o_ref, h2t_ref, lr_ref, ea_ref, rb_ref, eb_ref, ft_ref, a0_ref, a1_ref, w0_ref, w1_ref = rest
    s = pl.program_id(1)
    n_chunks = TM // LANE
    i_per = PEER_EB // PEER_NKEYS
    gdt = rb_ref.dtype

    @pl.when(s == 0)
    def _select():
        h2t = h2_ref[...].astype(_F32).T.astype(_MXU)
        h2t_ref[...] = h2t
        ft_ref[...] = jnp.zeros_like(ft_ref)
        a1_ref[...] = jnp.zeros_like(a1_ref)
        w0_ref[...] = jnp.zeros_like(w0_ref)

        def stash_scores():
            for hp in range(2 * PEER_HEADS):
                qt = _dot(wqt_ref[hp * LANE:(hp + 1) * LANE, :], h2t_ref[...]).astype(_MXU)
                st = _dot(keys_ref[hp % 2], qt)
                for c in range(n_chunks):
                    (lr_ref if hp % 2 == 0 else ea_ref)[c, hp // 2] = st[:, c * LANE:(c + 1) * LANE]

        def unit(c, h, first_only):
            lr, ea, rb, eb, n_sel = _peer_select(lr_ref[c, h], ea_ref[c, h], first_only)
            lr_ref[c, h] = lr
            ea_ref[c, h] = ea
            rb_ref[c, h] = rb.astype(gdt)
            eb_ref[c, h] = eb.astype(gdt)
            return n_sel

        per_trip = PEER_SELECT_UNROLL

        def fast(t, worst):
            c, h0 = t // (PEER_HEADS // per_trip), (t % (PEER_HEADS // per_trip)) * per_trip
            n = [unit(c, h0 + u, False) for u in range(per_trip)]
            return jnp.maximum(worst, jnp.max(functools.reduce(jnp.maximum, n)))

        stash_scores()
        worst = lax.fori_loop(0, n_chunks * PEER_HEADS // per_trip, fast, jnp.float32(0.0))

        @pl.when(worst > 3.0 * PEER_TOPK)
        def _ties():
            stash_scores()

            def exact(t, carry):
                unit(t // PEER_HEADS, t % PEER_HEADS, True)
                return carry

            lax.fori_loop(0, n_chunks * PEER_HEADS, exact, 0)

    def step(a_new, a_prev, w_new, w_prev):
        blk = jnp.clip(s - 1, 0, n_eb - 1)

        def gate(ii, c):
            i = blk * i_per + ii
            rows = slice(ii * PEER_NKEYS, (ii + 1) * PEER_NKEYS)
            cols = slice(c * LANE, (c + 1) * LANE)
            g = jnp.zeros((PEER_NKEYS, LANE), gdt)
            for h in range(PEER_HEADS):
                lr_i = lr_ref[c, h, pl.ds(i, 1), :].astype(gdt)
                ea_i = ea_ref[c, h, pl.ds(i, 1), :].astype(gdt)
                g = g + jnp.where(rb_ref[c, h] < lr_i, eb_ref[c, h], jnp.zeros((), gdt)) * ea_i
            a = a_prev[rows, cols]
            gelu = 0.5 * a * (1.0 + lax.erf(a * (2.0 ** -0.5)))
            w_new[rows, cols] = gelu.astype(gdt) * g

        units = [(ii, c) for ii in range(i_per) for c in range(n_chunks)]
        kd, mr = MXU_DEPTH, PEER_MXU_ROWS
        d_model = u_ref.shape[1]
        n_pieces = (PEER_EB // mr) * (d_model // kd) + (d_model // mr) * (PEER_EB // kd)
        per = len(units) // n_pieces
        done = 0
        for r0 in range(0, PEER_EB, mr):
            acc = None
            for k0 in range(0, d_model, kd):
                part = _dot(u_ref[r0:r0 + mr, k0:k0 + kd], h2t_ref[k0:k0 + kd, :])
                acc = part if acc is None else acc + part
                for ii, c in units[done:done + per]:
                    gate(ii, c)
                done += per
            a_new[r0:r0 + mr, :] = acc
        for r0 in range(0, d_model, mr):
            acc = ft_ref[r0:r0 + mr, :]
            for k0 in range(0, PEER_EB, kd):
                acc = acc + _dot(vt_ref[r0:r0 + mr, k0:k0 + kd], w_prev[k0:k0 + kd, :])
                for ii, c in units[done:done + per]:
                    gate(ii, c)
                done += per
            ft_ref[r0:r0 + mr, :] = acc
        for ii, c in units[done:]:
            gate(ii, c)

    @pl.when(s % 2 == 0)
    def _even():
        step(a0_ref, a1_ref, w1_ref, w0_ref)

    @pl.when(s % 2 == 1)
    def _odd():
        step(a1_ref, a0_ref, w0_ref, w1_ref)

    @pl.when(s == n_eb + 1)
    def _fin():
        x = x_ref[...] + mod_ref[0, 5:6, :] * ft_ref[...].T
        if final:
            x = _rms(x, fg_ref[...])
        o_ref[...] = x


def _peer(h2, x, mod, mod_row, wqt, keys, u, vt, final_g=None):
    n, d = x.shape
    n_exp = u.shape[0]
    n_eb = n_exp // PEER_EB
    n_chunks = TM // LANE
    final = final_g is not None
    args = [h2, x, mod, wqt, keys, u, vt]
    in_specs = [pl.BlockSpec((TM, d), lambda i, e: (i, 0)),
                pl.BlockSpec((TM, d), lambda i, e: (i, 0)),
                pl.BlockSpec((1, N_MOD, d), lambda i, e: (mod_row(i), 0, 0)),
                pl.BlockSpec(wqt.shape, lambda i, e: (0, 0)),
                pl.BlockSpec(keys.shape, lambda i, e: (0, 0, 0)),
                pl.BlockSpec((PEER_EB, d), lambda i, e: (jnp.minimum(e, n_eb - 1), 0)),
                pl.BlockSpec((d, PEER_EB), lambda i, e: (0, jnp.clip(e - 2, 0, n_eb - 1)))]
    if final:
        args.append(final_g)
        in_specs.append(pl.BlockSpec((1, d), lambda i, e: (0, 0)))
    tab32 = pltpu.VMEM((n_chunks, PEER_HEADS, PEER_NKEYS, LANE), _F32)
    tab16 = pltpu.VMEM((n_chunks, PEER_HEADS, PEER_NKEYS, LANE), _MXU)
    abuf = pltpu.VMEM((PEER_EB, TM), _F32)
    wbuf = pltpu.VMEM((PEER_EB, TM), _MXU)
    return pl.pallas_call(
        functools.partial(_peer_kernel, n_eb=n_eb, final=final),
        grid=(n // TM, n_eb + 2),
        in_specs=in_specs,
        out_specs=pl.BlockSpec((TM, d), lambda i, e: (i, 0)),
        out_shape=jax.ShapeDtypeStruct((n, d), _F32),
        scratch_shapes=[pltpu.VMEM((d, TM), _MXU), tab32, tab32, tab16, tab16,
                        pltpu.VMEM((d, TM), _F32), abuf, abuf, wbuf, wbuf],
        compiler_params=_cparams(("parallel", "arbitrary")),
        name="peer_ffn",
    )(*args)


def _dup_halves(w):
    a, b = w[:, :HEAD_DIM], w[:, HEAD_DIM:]
    return jnp.concatenate([a, a, b, b], axis=1)


def _prep_ab(w_in):
    aq, ak, av = w_in[:, 0:512], w_in[:, 512:1024], w_in[:, 1024:1536]
    bq, bk, bv = w_in[:, 1536:2048], w_in[:, 2048:2176], w_in[:, 2176:2304]
    cat = [aq, ak, av, bq, _rot_cols(bq, HEAD_DIM), _dup_halves(bk), _dup_halves(_rot_cols(bk, HEAD_DIM)),
           _dup_halves(bv)]
    return jnp.concatenate(cat, axis=1).astype(_MXU)


def _prep_cd(w_in, w_uq, w_ukv):
    d = w_in.shape[0]
    cq, ckv, kr = w_in[:, 0:256], w_in[:, 256:384], w_in[:, 384:416]
    dq, dk, dv = w_in[:, 416:928], w_in[:, 928:1440], w_in[:, 1440:1952]
    z64, z32 = jnp.zeros((d, 64), _F32), jnp.zeros((d, 32), _F32)
    kr128 = jnp.concatenate([z64, kr, z32], axis=1)
    krrot128 = jnp.concatenate([z64, _rot_cols(kr, MLA_ROPE), z32], axis=1)
    w = jnp.concatenate([cq, ckv, kr128, krrot128, dq, _rot_cols(dq, HEAD_DIM), dk, _rot_cols(dk, HEAD_DIM), dv],
                        axis=1).astype(_MXU)
    r = w_uq.shape[0]
    uq = w_uq.reshape(r, 8, MLA_NOPE + MLA_ROPE)
    nope, rope = uq[:, :, :MLA_NOPE], uq[:, :, MLA_NOPE:]
    rope_rot = _rot_cols(rope.reshape(r, 8 * MLA_ROPE), MLA_ROPE).reshape(r, 8, MLA_ROPE)
    zq = jnp.zeros((r, 8, 32), _F32)
    wqa = jnp.concatenate([nope, rope, zq], axis=2).reshape(r, 1024).astype(_MXU)
    wqb = jnp.concatenate([jnp.zeros_like(nope), rope_rot, zq], axis=2).reshape(r, 1024).astype(_MXU)
    rk = w_ukv.shape[0]
    ukv = w_ukv.reshape(rk, 8, 128)
    wk = jnp.concatenate([ukv[:, :, :MLA_NOPE], jnp.zeros((rk, 8, 64), _F32)], axis=2).reshape(rk, 1024).astype(_MXU)
    wv = ukv[:, :, MLA_NOPE:].reshape(rk, 512).astype(_MXU)
    lane = jnp.arange(LANE)
    src = (lane >= MLA_NOPE) & (lane < MLA_NOPE + MLA_ROPE)
    place = (src[:, None] & (lane[:, None] == (jnp.arange(1024)[None, :] % LANE))).astype(_MXU)
    return w, wqa, wqb, wk, wv, place


def kernel(x, c, ctx, c_ctx, ada_w, ada_b, norm1_g, norm2_g, w_out, peer_wq, peer_keys, peer_u, peer_v,
           ab_w_in, na_rpb, swa_sink, cd_w_in, mla_q_norm_g, mla_w_uq, mla_kv_norm_g, mla_w_ukv,
           diff_lambda, diff_subln_g, final_norm_g):
    batch, seq, d = x.shape
    ctx_len = ctx.shape[1]
    depth = ada_w.shape[0]
    assert seq % TM == 0 and (batch * ctx_len) % TM == 0 and seq % TQ_LOCAL == 0
    assert depth == 2, "even layers keep a context stream, the single odd layer is the last one"
    n_lat, n_ctx = batch * seq, batch * ctx_len
    xs = x.reshape(n_lat, d)
    cs = ctx.reshape(n_ctx, d)

    mod_rows = -(-(batch + 1) // 16) * 16
    cc = jnp.zeros((mod_rows, d), _F32).at[:batch].set(c).at[batch].set(c_ctx)
    mod_all = _modulation(cc, ada_w, ada_b)

    tiles_per_seq = seq // TM
    lat_row = lambda i: i // tiles_per_seq
    ctx_row = lambda i: batch
    lat_tab = lambda i: i % tiles_per_seq
    ctx_tab = lambda i: tiles_per_seq

    cos64, sin64 = _rope_tables(seq, HEAD_DIM)
    cos64p, sin64p = _pad_table(cos64, LANE, 1.0), _pad_table(sin64, LANE, 0.0)

    for l in range(depth):
        last = l == depth - 1
        j = l // 2
        mod = mod_all[l, :batch + 1].reshape(batch + 1, N_MOD, d)
        g1 = norm1_g[l].reshape(1, d)
        g2 = norm2_g[l].reshape(1, d)
        wo = w_out[l].astype(_MXU)
        wqt = peer_wq[l].T.astype(_MXU)
        keys = peer_keys[l].astype(_MXU)
        u = peer_u[l].astype(_MXU)
        vt = peer_v[l].T.astype(_MXU)
        if l % 2 == 0:
            w = _prep_ab(ab_w_in[j])
            aq, ak, av, bq, bk2, bv2 = _proj_ab(xs, mod, lat_row, g1, cos64p, sin64p, lat_tab, w)
            caq, cak, cav, cbq, cbk2, cbv2 = _proj_ab(cs, mod, ctx_row, g1, cos64p, sin64p, ctx_tab, w)
            bias = _na_bias_tables(na_rpb[j], seq // GRID_W)
            ya = _na_attention(aq, ak, av, cak, cav, bias, batch, seq, ctx_len)
            sink = swa_sink[j].astype(_F32)
            yb = _swa_attention(sink, bq, bk2, bv2, cbk2, cbv2, batch, seq, ctx_len)
            xs, h2 = _out_proj(xs, ya, yb, mod, lat_row, g2, wo[:512], wo[512:])
            if not last:
                pair_heads = [(m, half, m, m) for m in range(4) for half in range(2)]
                pair_outs = [("pair", 2 * m, 2 * m + 1) for m in range(4)]
                yca = _flash_attention(caq, cak, cav, heads=pair_heads, outs=pair_outs, batch=batch,
                                       q_per_batch=ctx_len, kv_per_batch=ctx_len, tq=ctx_len, tk=ctx_len,
                                       name="ctx_attn_a")
                gqa_heads = [(m, half, m // 2, m // 2) for m in range(4) for half in range(2)]
                ycb = _flash_attention(cbq, cbk2, cbv2, heads=gqa_heads, outs=pair_outs, batch=batch,
                                       q_per_batch=ctx_len, kv_per_batch=ctx_len, tq=ctx_len, tk=ctx_len,
                                       sink=sink, name="ctx_attn_b")
                cs, h2c = _out_proj(cs, yca, ycb, mod, ctx_row, g2, wo[:512], wo[512:])
        else:
            lam_init = 0.8 - 0.6 * math.exp(-0.3 * l)
            w, wqa, wqb, wk, wv, place = _prep_cd(cd_w_in[j], mla_w_uq[j], mla_w_ukv[j])
            cos32, sin32 = _rope_tables(seq, MLA_ROPE)
            ones64, zeros64 = jnp.ones((seq, 64), _F32), jnp.zeros((seq, 64), _F32)
            cosq = _pad_table(jnp.concatenate([ones64, cos32, ones64[:, :32]], 1), LANE, 1.0)
            sinq = _pad_table(jnp.concatenate([zeros64, sin32, zeros64[:, :32]], 1), LANE, 0.0)
            tabs = (cosq, sinq, cos64p, sin64p)
            qg = mla_q_norm_g[j].reshape(1, -1)
            kvg = mla_kv_norm_g[j].reshape(1, -1)
            qmt, km, vmt, dqt, dk, dvt = _proj_cd(xs, mod, lat_row, g1, tabs, lat_tab, w, qg, kvg, wqa, wqb, wk, wv, place)
            _, ckm, cvmt, _, cdk, cdvt = _proj_cd(cs, mod, ctx_row, g1, tabs, ctx_tab, w, qg, kvg, wqa, wqb, wk, wv, place)
            mla_heads = [(h, None, h, h // 2) for h in range(8)]
            pair_outs = [("pair", 2 * m, 2 * m + 1) for m in range(4)]
            yc = _flasht_attention(qmt, km, vmt, ckm, cvmt, heads=mla_heads, outs=pair_outs, batch=batch, seq=seq,
                                   ctx_len=ctx_len, tq=GLOBAL_TQ, tk=GLOBAL_TK, name="mla_attn")
            diff_heads = [(h, a, h, h) for h in range(4) for a in range(2)]
            diff_outs = [("diff", 2 * h, 2 * h + 1) for h in range(4)]
            od = _flasht_attention(dqt, dk, dvt, cdk, cdvt, heads=diff_heads, outs=diff_outs, batch=batch, seq=seq,
                                   ctx_len=ctx_len, tq=GLOBAL_TQ, tk=GLOBAL_TK,
                                   diff=(diff_lambda[j].astype(_F32), diff_subln_g[j].reshape(1, -1), lam_init),
                                   name="diff_attn")
            xs, h2 = _out_proj(xs, yc, od, mod, lat_row, g2, wo[:512], wo[512:])
        fg = final_norm_g.reshape(1, d) if last else None
        xs = _peer(h2, xs, mod, lat_row, wqt, keys, u, vt, final_g=fg)
        if not last:
            cs = _peer(h2c, cs, mod, ctx_row, wqt, keys, u, vt)
    return xs.reshape(batch, seq, d)
```

```python
import functools
import math

import jax
import jax.numpy as jnp
from jax import lax
from jax.experimental import pallas as pl
from jax.experimental.pallas import tpu as pltpu

_F32 = jnp.float32
_MXU = jnp.bfloat16
_NEG = -1e30
_LOG2E = math.log2(math.e)

GRID_W = 64
HEAD_DIM = 64
ROPE_BASE = 10000.0
RMS_EPS = 1e-6
N_MOD = 6
NA_KR, NA_KC = 8, 16
SWA_WINDOW = 128
MLA_NOPE, MLA_ROPE = 64, 32
PEER_HEADS, PEER_NKEYS, PEER_TOPK = 8, 128, 16

LANE = 128
TM = 512
TQ_LOCAL = 256
NA_GROUP_ROWS = TQ_LOCAL // GRID_W
NA_WIN_ROWS = NA_KR + NA_GROUP_ROWS - 1
GLOBAL_TQ = 512
GLOBAL_TK = 512
PEER_TM = 1024
PEER_EB = 512
MXU_DEPTH = 256
PEER_MXU_ROWS = 512
PEER_SELECT_UNROLL = 8
VMEM_LIMIT = 56 * 1024 * 1024


def _cparams(sem, vmem=VMEM_LIMIT):
    return pltpu.CompilerParams(dimension_semantics=sem, vmem_limit_bytes=vmem)


def _dot(a, b):
    return jnp.dot(a, b, preferred_element_type=_F32)


def _dot_nt(a, b):
    return lax.dot_general(a, b, (((1,), (1,)), ((), ())), preferred_element_type=_F32)


def _rms(x, g):
    return x * lax.rsqrt(jnp.mean(x * x, axis=-1, keepdims=True) + RMS_EPS) * g


def _lane_half(shape):
    return lax.broadcasted_iota(jnp.int32, shape, len(shape) - 1) >= (LANE // 2)


def _mod_kernel(c_ref, w_ref, b_ref, o_ref):
    c = c_ref[...]
    a = c * jax.nn.sigmoid(c)
    w = w_ref[0]
    a_hi = a.astype(_MXU)
    a_lo = (a - a_hi.astype(_F32)).astype(_MXU)
    w_hi = w.astype(_MXU)
    w_lo = (w - w_hi.astype(_F32)).astype(_MXU)
    o_ref[0] = _dot(a_hi, w_hi) + _dot(a_lo, w_hi) + _dot(a_hi, w_lo) + b_ref[0]


def _modulation(cc, ada_w, ada_b):
    depth, d, n = ada_w.shape
    rows = cc.shape[0]
    tn = 768
    return pl.pallas_call(
        _mod_kernel,
        grid=(depth, n // tn),
        in_specs=[pl.BlockSpec((rows, d), lambda l, j: (0, 0)),
                  pl.BlockSpec((1, d, tn), lambda l, j: (l, 0, j)),
                  pl.BlockSpec((1, 1, tn), lambda l, j: (l, 0, j))],
        out_specs=pl.BlockSpec((1, rows, tn), lambda l, j: (l, 0, j)),
        out_shape=jax.ShapeDtypeStruct((depth, rows, n), _F32),
        compiler_params=_cparams(("parallel", "parallel")),
        name="adaln_mod",
    )(cc, ada_w, ada_b.reshape(depth, 1, n))


def _rope_tables(seq, d):
    t = jnp.arange(seq, dtype=jnp.int32)
    q = d // 4
    freq = ROPE_BASE ** (-jnp.arange(q, dtype=_F32) / q)

    def one(pos):
        ang = pos.astype(_F32)[:, None] * freq[None, :]
        return jnp.concatenate([jnp.cos(ang)] * 2, -1), jnp.concatenate([jnp.sin(ang)] * 2, -1)

    cr, sr = one(t // GRID_W)
    cc, sc = one(t % GRID_W)
    return jnp.concatenate([cr, cc], -1), jnp.concatenate([sr, sc], -1)


def _rot_cols(w, d):
    k, n = w.shape
    w5 = w.reshape(k, n // d, 2, 2, d // 4)
    return jnp.stack([-w5[:, :, :, 1], w5[:, :, :, 0]], axis=3).reshape(k, n)


def _pad_table(tab, width, ident):
    s, w = tab.shape
    if w < width:
        reps = width // w
        tab = jnp.tile(tab, (1, reps))
    return jnp.concatenate([tab, jnp.full((TM, width), ident, _F32)], axis=0)


def _proj_ab_kernel(x_ref, mod_ref, g_ref, cos_ref, sin_ref, w_ref,
                    aq_ref, ak_ref, av_ref, bq_ref, bk_ref, bv_ref):
    x = x_ref[...]
    h = _rms(x, g_ref[...]) * (1.0 + mod_ref[0, 1:2, :]) + mod_ref[0, 0:1, :]
    hb = h.astype(_MXU)
    cos = cos_ref[...]
    sin = sin_ref[...]

    def proj(lo, n):
        return _dot(hb, w_ref[:, lo:lo + n])

    scale = HEAD_DIM ** -0.5
    aq_ref[...] = (proj(0, 512) * scale).astype(aq_ref.dtype)
    ak_ref[...] = proj(512, 512).astype(ak_ref.dtype)
    av_ref[...] = proj(1024, 512).astype(av_ref.dtype)
    cos4 = jnp.tile(cos, (1, 4))
    sin4 = jnp.tile(sin, (1, 4))
    bq_ref[...] = ((proj(1536, 512) * cos4 + proj(2048, 512) * sin4) * scale).astype(bq_ref.dtype)
    cos2 = jnp.tile(cos, (1, 2))
    sin2 = jnp.tile(sin, (1, 2))
    bk_ref[...] = (proj(2560, 256) * cos2 + proj(2816, 256) * sin2).astype(bk_ref.dtype)
    bv_ref[...] = proj(3072, 256).astype(bv_ref.dtype)


def _proj_ab(x, mod, mod_row, g, cos, sin, tab_row, w):
    n, d = x.shape
    widths = (512, 512, 512, 512, 256, 256)
    return pl.pallas_call(
        _proj_ab_kernel,
        grid=(n // TM,),
        in_specs=[pl.BlockSpec((TM, d), lambda i: (i, 0)),
                  pl.BlockSpec((1, N_MOD, d), lambda i: (mod_row(i), 0, 0)),
                  pl.BlockSpec((1, d), lambda i: (0, 0)),
                  pl.BlockSpec((TM, LANE), lambda i: (tab_row(i), 0)),
                  pl.BlockSpec((TM, LANE), lambda i: (tab_row(i), 0)),
                  pl.BlockSpec(w.shape, lambda i: (0, 0))],
        out_specs=[pl.BlockSpec((TM, wd), lambda i: (i, 0)) for wd in widths],
        out_shape=[jax.ShapeDtypeStruct((n, wd), _MXU) for wd in widths],
        compiler_params=_cparams(("parallel",)),
        name="proj_ab",
    )(x, mod, g, cos, sin, w)


def _na_kernel(q_ref, k_ref, v_ref, kc_ref, vc_ref, bias_ref, o_ref, *, rows):
    g = pl.program_id(1)
    ks = jnp.clip(NA_GROUP_ROWS * g - NA_KR // 2, 0, rows - NA_WIN_ROWS)
    start = pl.multiple_of(ks * GRID_W, GRID_W)
    nwin = NA_WIN_ROWS * GRID_W
    hi = _lane_half((TQ_LOCAL, LANE))
    for m in range(4):
        cols = slice(m * LANE, (m + 1) * LANE)
        q2 = q_ref[:, cols]
        k2 = k_ref[pl.ds(start, nwin), cols]
        v2 = v_ref[pl.ds(start, nwin), cols]
        kc2 = kc_ref[:, cols]
        vc2 = vc_ref[:, cols]
        outs = []
        for half in range(2):
            qh = jnp.where(hi == (half == 1), q2, jnp.zeros_like(q2))
            s_lat = _dot_nt(qh, k2) + bias_ref[0, 2 * m + half]
            s_ctx = _dot_nt(qh, kc2)
            mx = jnp.maximum(jnp.max(s_lat, axis=-1, keepdims=True), jnp.max(s_ctx, axis=-1, keepdims=True))
            p_lat = jnp.exp(s_lat - mx)
            p_ctx = jnp.exp(s_ctx - mx)
            den = jnp.sum(p_lat, axis=-1, keepdims=True) + jnp.sum(p_ctx, axis=-1, keepdims=True)
            o = _dot(p_lat.astype(_MXU), v2) + _dot(p_ctx.astype(_MXU), vc2)
            outs.append(o / den)
        o_ref[:, cols] = jnp.where(hi, outs[1], outs[0]).astype(o_ref.dtype)


def _na_bias_tables(rpb, rows):
    gq = NA_GROUP_ROWS
    n_groups = rows // gq
    ql = jnp.arange(gq)[:, None]
    kl = jnp.arange(NA_WIN_ROWS)[None, :]
    qc = jnp.arange(GRID_W)[:, None]
    kc = jnp.arange(GRID_W)[None, :]
    cs = jnp.clip(qc - NA_KC // 2, 0, GRID_W - NA_KC)
    col_valid = (kc >= cs) & (kc < cs + NA_KC)
    col_idx = jnp.clip(kc - qc + NA_KC - 1, 0, 2 * NA_KC - 2)
    exact = lax.Precision.HIGHEST
    oh_c = jax.nn.one_hot(col_idx, 2 * NA_KC - 1, dtype=_F32)
    rpb_cols = jnp.einsum("hab,xyb->haxy", rpb.astype(_F32), oh_c, precision=exact)
    tabs = []
    for g in (0, 1, n_groups - 1):
        ks = min(max(gq * g - NA_KR // 2, 0), rows - NA_WIN_ROWS)
        rq = gq * g + ql
        rk = ks + kl
        r0 = jnp.clip(rq - NA_KR // 2, 0, rows - NA_KR)
        row_valid = (rk >= r0) & (rk < r0 + NA_KR)
        row_idx = jnp.clip(rk - rq + NA_KR - 1, 0, 2 * NA_KR - 2)
        oh_r = jax.nn.one_hot(row_idx, 2 * NA_KR - 1, dtype=_F32)
        b = jnp.einsum("qka,haxy->hqxky", oh_r, rpb_cols, precision=exact)
        valid = row_valid[:, None, :, None] & col_valid[None, :, None, :]
        b = jnp.where(valid[None], b, _NEG)
        tabs.append(b.reshape(rpb.shape[0], gq * GRID_W, NA_WIN_ROWS * GRID_W))
    return jnp.stack(tabs)


def _na_attention(aq, ak, av, cak, cav, bias, batch, seq, ctx_len):
    rows = seq // GRID_W
    n_groups = seq // TQ_LOCAL
    nwin = NA_WIN_ROWS * GRID_W

    def bias_row(b, g):
        return (jnp.where(g == 0, 0, jnp.where(g == n_groups - 1, 2, 1)), 0, 0, 0)

    return pl.pallas_call(
        functools.partial(_na_kernel, rows=rows),
        grid=(batch, n_groups),
        in_specs=[pl.BlockSpec((TQ_LOCAL, 512), lambda b, g: (b * n_groups + g, 0)),
                  pl.BlockSpec((seq, 512), lambda b, g: (b, 0)),
                  pl.BlockSpec((seq, 512), lambda b, g: (b, 0)),
                  pl.BlockSpec((ctx_len, 512), lambda b, g: (b, 0)),
                  pl.BlockSpec((ctx_len, 512), lambda b, g: (b, 0)),
                  pl.BlockSpec((1, 8, TQ_LOCAL, nwin), bias_row)],
        out_specs=pl.BlockSpec((TQ_LOCAL, 512), lambda b, g: (b * n_groups + g, 0)),
        out_shape=jax.ShapeDtypeStruct((batch * seq, 512), _MXU),
        compiler_params=_cparams(("parallel", "arbitrary")),
        name="na_attention",
    )(aq, ak, av, cak, cav, bias)


def _swa_kernel(sink_ref, q_ref, k_ref, v_ref, kc_ref, vc_ref, o_ref, *, seq):
    t = pl.program_id(1)
    kwin = TQ_LOCAL + 2 * SWA_WINDOW
    start = t * TQ_LOCAL
    kstart = pl.multiple_of(jnp.clip(start - SWA_WINDOW, 0, seq - kwin), LANE)
    qpos = start + lax.broadcasted_iota(jnp.int32, (TQ_LOCAL, kwin), 0)
    kpos = kstart + lax.broadcasted_iota(jnp.int32, (TQ_LOCAL, kwin), 1)
    mask = jnp.where(jnp.abs(kpos - qpos) <= SWA_WINDOW, 0.0, _NEG).astype(_F32)
    hi = _lane_half((TQ_LOCAL, LANE))
    for m in range(4):
        cols = slice(m * LANE, (m + 1) * LANE)
        kv = m // 2
        kcols = slice(kv * LANE, (kv + 1) * LANE)
        q2 = q_ref[:, cols]
        k2 = k_ref[pl.ds(kstart, kwin), kcols]
        v2 = v_ref[pl.ds(kstart, kwin), kcols]
        kc2 = kc_ref[:, kcols]
        vc2 = vc_ref[:, kcols]
        outs = []
        for half in range(2):
            sink = sink_ref[2 * m + half]
            qh = jnp.where(hi == (half == 1), q2, jnp.zeros_like(q2))
            s_lat = _dot_nt(qh, k2) + mask
            s_ctx = _dot_nt(qh, kc2)
            mx = jnp.maximum(jnp.max(s_lat, axis=-1, keepdims=True), jnp.max(s_ctx, axis=-1, keepdims=True))
            mx = jnp.maximum(mx, sink)
            p_lat = jnp.exp(s_lat - mx)
            p_ctx = jnp.exp(s_ctx - mx)
            den = (jnp.sum(p_lat, axis=-1, keepdims=True) + jnp.sum(p_ctx, axis=-1, keepdims=True)
                   + jnp.exp(sink - mx))
            o = _dot(p_lat.astype(_MXU), v2) + _dot(p_ctx.astype(_MXU), vc2)
            outs.append(o / den)
        o_ref[:, cols] = jnp.where(hi, outs[1], outs[0]).astype(o_ref.dtype)


def _swa_attention(sink, bq, bk2, bv2, cbk2, cbv2, batch, seq, ctx_len):
    n_t = seq // TQ_LOCAL
    return pl.pallas_call(
        functools.partial(_swa_kernel, seq=seq),
        grid=(batch, n_t),
        in_specs=[pl.BlockSpec(memory_space=pltpu.SMEM),
                  pl.BlockSpec((TQ_LOCAL, 512), lambda b, t: (b * n_t + t, 0)),
                  pl.BlockSpec((seq, 256), lambda b, t: (b, 0)),
                  pl.BlockSpec((seq, 256), lambda b, t: (b, 0)),
                  pl.BlockSpec((ctx_len, 256), lambda b, t: (b, 0)),
                  pl.BlockSpec((ctx_len, 256), lambda b, t: (b, 0))],
        out_specs=pl.BlockSpec((TQ_LOCAL, 512), lambda b, t: (b * n_t + t, 0)),
        out_shape=jax.ShapeDtypeStruct((batch * seq, 512), _MXU),
        compiler_params=_cparams(("parallel", "arbitrary")),
        name="swa_attention",
    )(sink, bq, bk2, bv2, cbk2, cbv2)


def _flash_kernel(*refs, heads, outs, has_ctx, has_sink, diff_cfg, n_kv):
    it = iter(refs)
    sink_ref = next(it) if has_sink else None
    q_ref, k_ref, v_ref = next(it), next(it), next(it)
    kc_ref = next(it) if has_ctx else None
    vc_ref = next(it) if has_ctx else None
    lam_ref = next(it) if diff_cfg else None
    sg_ref = next(it) if diff_cfg else None
    o_ref = next(it)
    m_ref, l_ref, acc_ref = next(it), next(it), next(it)
    kv = pl.program_id(2)
    tq = q_ref.shape[0]
    hi = _lane_half((tq, LANE))

    @pl.when(kv == 0)
    def _init():
        for h in range(len(heads)):
            if has_sink:
                m_ref[h] = jnp.full((tq, LANE), sink_ref[h], _F32)
                l_ref[h] = jnp.ones((tq, LANE), _F32)
            else:
                m_ref[h] = jnp.full((tq, LANE), _NEG, _F32)
                l_ref[h] = jnp.zeros((tq, LANE), _F32)
            acc_ref[h] = jnp.zeros((tq, LANE), _F32)

    def attend(kr, vr):
        for h, (qb, qhalf, kb, vb) in enumerate(heads):
            q2 = q_ref[:, qb * LANE:(qb + 1) * LANE]
            if qhalf is not None:
                q2 = jnp.where(hi == (qhalf == 1), q2, jnp.zeros_like(q2))
            s = _dot_nt(q2, kr[:, kb * LANE:(kb + 1) * LANE])
            m_old = m_ref[h][:, :1]
            m_new = jnp.maximum(m_old, jnp.max(s, axis=-1, keepdims=True))
            alpha = jnp.exp(m_old - m_new)
            p = jnp.exp(s - m_new)
            l_ref[h] = jnp.broadcast_to(alpha * l_ref[h][:, :1] + jnp.sum(p, axis=-1, keepdims=True), (tq, LANE))
            acc_ref[h] = alpha * acc_ref[h] + _dot(p.astype(_MXU), vr[:, vb * LANE:(vb + 1) * LANE])
            m_ref[h] = jnp.broadcast_to(m_new, (tq, LANE))

    if has_ctx:
        @pl.when(kv == 0)
        def _ctx():
            attend(kc_ref, vc_ref)

    attend(k_ref, v_ref)

    @pl.when(kv == n_kv - 1)
    def _fin():
        def head_out(h):
            return acc_ref[h] / l_ref[h][:, :1]

        for j, spec in enumerate(outs):
            if spec[0] == "full":
                o = head_out(spec[1])
            elif spec[0] == "pair":
                o = jnp.where(hi, head_out(spec[2]), head_out(spec[1]))
            else:
                lv = lam_ref[...]
                lam = (jnp.exp(jnp.sum(lv[0:1] * lv[1:2], axis=-1, keepdims=True))
                       - jnp.exp(jnp.sum(lv[2:3] * lv[3:4], axis=-1, keepdims=True)) + diff_cfg)
                o = _rms(head_out(spec[1]) - lam * head_out(spec[2]), sg_ref[...]) * (1.0 - diff_cfg)
            o_ref[:, j * LANE:(j + 1) * LANE] = o.astype(o_ref.dtype)


def _flash_attention(q, k, v, *, heads, outs, batch, q_per_batch, kv_per_batch, tq, tk,
                     q_row0=0, kv_row0=0, ctx=None, ctx_len=0, ctx_row0=0, sink=None, diff=None, name="flash"):
    n_q = q_per_batch // tq
    n_kv = kv_per_batch // tk
    qw, kw, vw = q.shape[1], k.shape[1], v.shape[1]
    args, in_specs = [], []
    if sink is not None:
        args.append(sink)
        in_specs.append(pl.BlockSpec(memory_space=pltpu.SMEM))
    q0, k0 = q_row0 // tq, kv_row0 // tk
    args += [q, k, v]
    in_specs += [pl.BlockSpec((tq, qw), lambda b, i, j: (q0 + b * n_q + i, 0)),
                 pl.BlockSpec((tk, kw), lambda b, i, j: (k0 + b * n_kv + j, 0)),
                 pl.BlockSpec((tk, vw), lambda b, i, j: (k0 + b * n_kv + j, 0))]
    if ctx is not None:
        c0 = ctx_row0 // ctx_len
        args += [ctx[0], ctx[1]]
        in_specs += [pl.BlockSpec((ctx_len, kw), lambda b, i, j: (c0 + b, 0)),
                     pl.BlockSpec((ctx_len, vw), lambda b, i, j: (c0 + b, 0))]
    diff_cfg = None
    if diff is not None:
        lam_vecs, subln_g, diff_cfg = diff
        args += [lam_vecs, subln_g]
        in_specs += [pl.BlockSpec(lam_vecs.shape, lambda b, i, j: (0, 0)),
                     pl.BlockSpec(subln_g.shape, lambda b, i, j: (0, 0))]
    nh = len(heads)
    ow = len(outs) * LANE
    return pl.pallas_call(
        functools.partial(_flash_kernel, heads=tuple(heads), outs=tuple(outs), has_ctx=ctx is not None,
                          has_sink=sink is not None, diff_cfg=diff_cfg, n_kv=n_kv),
        grid=(batch, n_q, n_kv),
        in_specs=in_specs,
        out_specs=pl.BlockSpec((tq, ow), lambda b, i, j: (b * n_q + i, 0)),
        out_shape=jax.ShapeDtypeStruct((batch * q_per_batch, ow), _MXU),
        scratch_shapes=[pltpu.VMEM((nh, tq, LANE), _F32)] * 3,
        compiler_params=_cparams(("parallel", "parallel", "arbitrary")),
        name=name,
    )(*args)


def _flasht_kernel(*refs, heads, outs, diff_cfg, n_kv):
    it = iter(refs)
    qt_ref, k_ref, vt_ref, kc_ref, vct_ref = next(it), next(it), next(it), next(it), next(it)
    lam_ref = next(it) if diff_cfg else None
    sg_ref = next(it) if diff_cfg else None
    o_ref = next(it)
    m_ref, l_ref, acc_ref = next(it), next(it), next(it)
    kv = pl.program_id(2)
    tq = qt_ref.shape[1]
    row_hi = lax.broadcasted_iota(jnp.int32, (LANE, tq), 0) >= (LANE // 2)

    @pl.when(kv == 0)
    def _init():
        m_ref[...] = jnp.full(m_ref.shape, _NEG, _F32)
        l_ref[...] = jnp.zeros(l_ref.shape, _F32)
        acc_ref[...] = jnp.zeros(acc_ref.shape, _F32)

    def attend(kr, vtr):
        def scores(h):
            qb, qhalf, kb, _ = heads[h]
            qt = qt_ref[qb * LANE:(qb + 1) * LANE, :]
            if qhalf is not None:
                qt = jnp.where(row_hi == (qhalf == 1), qt, jnp.zeros_like(qt))
            return _dot(kr[:, kb * LANE:(kb + 1) * LANE], qt)

        s_next = scores(0)
        for h in range(len(heads)):
            s = s_next
            if h + 1 < len(heads):
                s_next = scores(h + 1)
            vb = heads[h][3]
            m_old = m_ref[h, 0:1, :]
            m_new = jnp.maximum(m_old, jnp.max(s, axis=0, keepdims=True))
            alpha = jnp.exp2(m_old - m_new)
            p = jnp.exp2(s - m_new)
            l_new = alpha * l_ref[h, 0:1, :] + jnp.sum(p, axis=0, keepdims=True)
            acc_ref[h] = alpha * acc_ref[h] + _dot(vtr[vb * LANE:(vb + 1) * LANE, :], p.astype(_MXU))
            l_ref[h] = jnp.broadcast_to(l_new, (8, tq))
            m_ref[h] = jnp.broadcast_to(m_new, (8, tq))

    @pl.when(kv == 0)
    def _ctx():
        attend(kc_ref, vct_ref)

    attend(k_ref, vt_ref)

    @pl.when(kv == n_kv - 1)
    def _fin():
        def head_out(h):
            return acc_ref[h] / l_ref[h, 0:1, :]

        for j, spec in enumerate(outs):
            if spec[0] == "full":
                o = head_out(spec[1]).T
            elif spec[0] == "pair":
                o = jnp.where(row_hi, head_out(spec[2]), head_out(spec[1])).T
            else:
                lv = lam_ref[...]
                lam = (jnp.exp(jnp.sum(lv[0:1] * lv[1:2], axis=-1, keepdims=True))
                       - jnp.exp(jnp.sum(lv[2:3] * lv[3:4], axis=-1, keepdims=True)) + diff_cfg)
                o = _rms((head_out(spec[1]) - lam * head_out(spec[2])).T, sg_ref[...]) * (1.0 - diff_cfg)
            o_ref[:, j * LANE:(j + 1) * LANE] = o.astype(o_ref.dtype)


def _flasht_attention(qt, k, vt, kc, vct, *, heads, outs, batch, seq, ctx_len, tq, tk, diff=None, name="flasht"):
    n_q, n_kv = seq // tq, seq // tk
    qw, kw, vw = qt.shape[0], k.shape[1], vt.shape[0]
    args = [qt, k, vt, kc, vct]
    in_specs = [pl.BlockSpec((qw, tq), lambda b, i, j: (0, b * n_q + i)),
                pl.BlockSpec((tk, kw), lambda b, i, j: (b * n_kv + j, 0)),
                pl.BlockSpec((vw, tk), lambda b, i, j: (0, b * n_kv + j)),
                pl.BlockSpec((ctx_len, kw), lambda b, i, j: (b, 0)),
                pl.BlockSpec((vw, ctx_len), lambda b, i, j: (0, b))]
    diff_cfg = None
    if diff is not None:
        lam_vecs, subln_g, diff_cfg = diff
        args += [lam_vecs, subln_g]
        in_specs += [pl.BlockSpec(lam_vecs.shape, lambda b, i, j: (0, 0)),
                     pl.BlockSpec(subln_g.shape, lambda b, i, j: (0, 0))]
    nh = len(heads)
    ow = len(outs) * LANE
    return pl.pallas_call(
        functools.partial(_flasht_kernel, heads=tuple(heads), outs=tuple(outs), diff_cfg=diff_cfg, n_kv=n_kv),
        grid=(batch, n_q, n_kv),
        in_specs=in_specs,
        out_specs=pl.BlockSpec((tq, ow), lambda b, i, j: (b * n_q + i, 0)),
        out_shape=jax.ShapeDtypeStruct((batch * seq, ow), _MXU),
        scratch_shapes=[pltpu.VMEM((nh, 8, tq), _F32), pltpu.VMEM((nh, 8, tq), _F32),
                        pltpu.VMEM((nh, LANE, tq), _F32)],
        compiler_params=_cparams(("parallel", "parallel", "arbitrary")),
        name=name,
    )(*args)


def _out_kernel(x_ref, ya_ref, yb_ref, mod_ref, g2_ref, wa_ref, wb_ref, xo_ref, h2_ref):
    y = _dot(ya_ref[...], wa_ref[...]) + _dot(yb_ref[...], wb_ref[...])
    x = x_ref[...] + mod_ref[0, 2:3, :] * y
    xo_ref[...] = x
    h2 = _rms(x, g2_ref[...]) * (1.0 + mod_ref[0, 4:5, :]) + mod_ref[0, 3:4, :]
    h2_ref[...] = h2.astype(h2_ref.dtype)


def _out_proj(x, ya, yb, mod, mod_row, g2, wa, wb):
    n, d = x.shape
    return pl.pallas_call(
        _out_kernel,
        grid=(n // TM,),
        in_specs=[pl.BlockSpec((TM, d), lambda i: (i, 0)),
                  pl.BlockSpec((TM, ya.shape[1]), lambda i: (i, 0)),
                  pl.BlockSpec((TM, yb.shape[1]), lambda i: (i, 0)),
                  pl.BlockSpec((1, N_MOD, d), lambda i: (mod_row(i), 0, 0)),
                  pl.BlockSpec((1, d), lambda i: (0, 0)),
                  pl.BlockSpec(wa.shape, lambda i: (0, 0)),
                  pl.BlockSpec(wb.shape, lambda i: (0, 0))],
        out_specs=[pl.BlockSpec((TM, d), lambda i: (i, 0)), pl.BlockSpec((TM, d), lambda i: (i, 0))],
        out_shape=[jax.ShapeDtypeStruct((n, d), _F32), jax.ShapeDtypeStruct((n, d), _MXU)],
        compiler_params=_cparams(("parallel",)),
        name="out_proj",
    )(x, ya, yb, mod, g2, wa, wb)


def _proj_cd_kernel(x_ref, mod_ref, g_ref, cq_ref, sq_ref, cd_ref, sd_ref, w_ref, qg_ref, kvg_ref,
                    wqa_ref, wqb_ref, wk_ref, wv_ref, place_ref,
                    qm_ref, km_ref, vm_ref, dq_ref, dk_ref, dv_ref):
    x = x_ref[...]
    h = _rms(x, g_ref[...]) * (1.0 + mod_ref[0, 1:2, :]) + mod_ref[0, 0:1, :]
    hb = h.astype(_MXU)

    def proj(lo, n):
        return _dot(hb, w_ref[:, lo:lo + n])

    cosq, sinq = cq_ref[...], sq_ref[...]
    cosd, sind = cd_ref[...], sd_ref[...]
    cqn = _rms(proj(0, 256), qg_ref[...]).astype(_MXU)
    ckvn = _rms(proj(256, 128), kvg_ref[...]).astype(_MXU)
    cos8, sin8 = jnp.tile(cosq, (1, 8)), jnp.tile(sinq, (1, 8))
    qm = _dot(cqn, wqa_ref[...]) * cos8 + _dot(cqn, wqb_ref[...]) * sin8
    qm_ref[...] = (qm * ((MLA_NOPE + MLA_ROPE) ** -0.5 * _LOG2E)).T.astype(qm_ref.dtype)
    kr = (proj(384, 128) * cosq + proj(512, 128) * sinq).astype(_MXU)
    km_ref[...] = (_dot(ckvn, wk_ref[...]) + _dot(kr, place_ref[...])).astype(km_ref.dtype)
    vm_ref[...] = _dot(ckvn, wv_ref[...]).T.astype(vm_ref.dtype)
    cos4, sin4 = jnp.tile(cosd, (1, 4)), jnp.tile(sind, (1, 4))
    dq_ref[...] = ((proj(640, 512) * cos4 + proj(1152, 512) * sin4) * (HEAD_DIM ** -0.5 * _LOG2E)).T.astype(dq_ref.dtype)
    dk_ref[...] = (proj(1664, 512) * cos4 + proj(2176, 512) * sin4).astype(dk_ref.dtype)
    dv_ref[...] = proj(2688, 512).T.astype(dv_ref.dtype)


def _proj_cd(x, mod, mod_row, g, tabs, tab_row, w, qg, kvg, wqa, wqb, wk, wv, place):
    n, d = x.shape
    outs = ((1024, True), (1024, False), (512, True), (512, True), (512, False), (512, True))
    full = lambda a: pl.BlockSpec(a.shape, lambda i: (0, 0))
    return pl.pallas_call(
        _proj_cd_kernel,
        grid=(n // TM,),
        in_specs=[pl.BlockSpec((TM, d), lambda i: (i, 0)),
                  pl.BlockSpec((1, N_MOD, d), lambda i: (mod_row(i), 0, 0)),
                  pl.BlockSpec((1, d), lambda i: (0, 0))]
                 + [pl.BlockSpec((TM, LANE), lambda i: (tab_row(i), 0))] * 4
                 + [full(a) for a in (w, qg, kvg, wqa, wqb, wk, wv, place)],
        out_specs=[pl.BlockSpec((wd, TM), lambda i: (0, i)) if fm else pl.BlockSpec((TM, wd), lambda i: (i, 0))
                   for wd, fm in outs],
        out_shape=[jax.ShapeDtypeStruct((wd, n) if fm else (n, wd), _MXU) for wd, fm in outs],
        compiler_params=_cparams(("parallel",)),
        name="proj_cd",
    )(x, mod, g, *tabs, w, qg, kvg, wqa, wqb, wk, wv, place)


_MARK = 2.0 ** 121


def _extract16(x, first_only):
    r = x.shape[0]
    row = lax.broadcasted_iota(jnp.int32, x.shape, 0)
    row16 = lax.broadcasted_iota(jnp.int32, (PEER_TOPK, LANE), 0)
    vals = jnp.zeros((PEER_TOPK, LANE), _F32)
    for k in range(PEER_TOPK):
        m = jnp.max(x, axis=0, keepdims=True)
        hit = x == m
        if first_only:
            hit = row == jnp.min(jnp.where(hit, row, r), axis=0, keepdims=True)
        x = jnp.where(hit, -_MARK * (32 + k), x)
        vals = jnp.where(row16 == k, m, vals)
    order = jnp.where(x < -16.0 * _MARK, x * (-1.0 / _MARK) - 32.0, float(PEER_TOPK))
    return order, vals


def _staircase_rows():
    groups = []
    for r in range(PEER_TOPK // 2):
        n = PEER_TOPK // (r + 1)
        for c0 in range(0, n, 8):
            groups.append((r, c0, min(8, n - c0)))
    groups.append((None, 0, 8))
    return groups


def _peer_select(sa, sb, first_only):
    ra, av = _extract16(sa, first_only)
    rb, bv = _extract16(sb, first_only)
    groups = _staircase_rows()
    sub = lax.broadcasted_iota(jnp.int32, (8, LANE), 0)
    pieces = []
    for r, c0, nv in groups:
        if r is None:
            piece = av[8:16] + bv[0:1]
        else:
            piece = av[r:r + 1] + bv[c0:c0 + 8]
            if nv < 8:
                piece = jnp.where(sub < nv, piece, -jnp.inf)
        pieces.append(piece)
    cand = jnp.concatenate(pieces, axis=0)
    e_cand = jnp.exp(cand - (av[0:1] + bv[0:1]))
    sel, _ = _extract16(cand, first_only)
    sel = jnp.where(sel < float(PEER_TOPK), 1.0, 0.0)
    z = jnp.sum(sel * e_cand, axis=0, keepdims=True)
    lr = jnp.zeros(sa.shape, _F32)
    for g, (r, c0, nv) in enumerate(groups):
        blk = sel[8 * g:8 * g + 8]
        if r is None:
            for q in range(8):
                lr = jnp.where(ra == float(8 + q), blk[q:q + 1], lr)
        elif c0 == 0:
            cnt = jnp.sum(blk, axis=0, keepdims=True)
            if PEER_TOPK // (r + 1) > 8:
                cnt = cnt + jnp.sum(sel[8 * g + 8:8 * g + 16], axis=0, keepdims=True)
            lr = jnp.where(ra == float(r), cnt, lr)
    ea = jnp.exp(sa - av[0:1]) / z
    eb = jnp.exp(sb - bv[0:1])
    n_sel = (jnp.sum(jnp.where(ra < float(PEER_TOPK), 1.0, 0.0), axis=0, keepdims=True)
             + jnp.sum(jnp.where(rb < float(PEER_TOPK), 1.0, 0.0), axis=0, keepdims=True)
             + jnp.sum(sel, axis=0, keepdims=True))
    return lr, ea, rb, eb, n_sel


def _peer_kernel(h2_ref, x_ref, mod_ref, wqt_ref, keys_ref, u_ref, vt_ref, *rest, n_eb, final):
    if final:
        fg_ref, o_ref, h2t_ref, lr_ref, ea_ref, rb_ref, eb_ref, ft_ref, a0_ref, a1_ref, w0_ref, w1_ref = rest
    else:
        o_ref, h2t_ref, lr_ref, ea_ref, rb_ref, eb_ref, ft_ref, a0_ref, a1_ref, w0_ref, w1_ref = rest
    s = pl.program_id(1)
    n_chunks = PEER_TM // LANE
    i_per = PEER_EB // PEER_NKEYS
    gdt = rb_ref.dtype

    @pl.when(s == 0)
    def _select():
        h2t = h2_ref[...].astype(_F32).T.astype(_MXU)
        h2t_ref[...] = h2t
        ft_ref[...] = jnp.zeros_like(ft_ref)
        a1_ref[...] = jnp.zeros_like(a1_ref)
        w0_ref[...] = jnp.zeros_like(w0_ref)

        def stash_scores():
            for hp in range(2 * PEER_HEADS):
                qt = _dot(wqt_ref[hp * LANE:(hp + 1) * LANE, :], h2t_ref[...]).astype(_MXU)
                st = _dot(keys_ref[hp % 2], qt)
                for c in range(n_chunks):
                    (lr_ref if hp % 2 == 0 else ea_ref)[c, hp // 2] = st[:, c * LANE:(c + 1) * LANE]

        def unit(c, h, first_only):
            lr, ea, rb, eb, n_sel = _peer_select(lr_ref[c, h], ea_ref[c, h], first_only)
            lr_ref[c, h] = lr
            ea_ref[c, h] = ea
            rb_ref[c, h] = rb.astype(gdt)
            eb_ref[c, h] = eb.astype(gdt)
            return n_sel

        per_trip = PEER_SELECT_UNROLL

        def fast(t, worst):
            c, h0 = t // (PEER_HEADS // per_trip), (t % (PEER_HEADS // per_trip)) * per_trip
            n = [unit(c, h0 + u, False) for u in range(per_trip)]
            return jnp.maximum(worst, jnp.max(functools.reduce(jnp.maximum, n)))

        stash_scores()
        worst = lax.fori_loop(0, n_chunks * PEER_HEADS // per_trip, fast, jnp.float32(0.0))

        @pl.when(worst > 3.0 * PEER_TOPK)
        def _ties():
            stash_scores()

            def exact(t, carry):
                unit(t // PEER_HEADS, t % PEER_HEADS, True)
                return carry

            lax.fori_loop(0, n_chunks * PEER_HEADS, exact, 0)

    def step(a_new, a_prev, w_new, w_prev):
        blk = jnp.clip(s - 1, 0, n_eb - 1)

        def gate(ii, c):
            i = blk * i_per + ii
            rows = slice(ii * PEER_NKEYS, (ii + 1) * PEER_NKEYS)
            cols = slice(c * LANE, (c + 1) * LANE)
            g = jnp.zeros((PEER_NKEYS, LANE), gdt)
            for h in range(PEER_HEADS):
                lr_i = lr_ref[c, h, pl.ds(i, 1), :].astype(gdt)
                ea_i = ea_ref[c, h, pl.ds(i, 1), :].astype(gdt)
                g = g + jnp.where(rb_ref[c, h] < lr_i, eb_ref[c, h], jnp.zeros((), gdt)) * ea_i
            a = a_prev[rows, cols]
            gelu = 0.5 * a * (1.0 + lax.erf(a * (2.0 ** -0.5)))
            w_new[rows, cols] = gelu.astype(gdt) * g

        units = [(ii, c) for ii in range(i_per) for c in range(n_chunks)]
        kd, mr = MXU_DEPTH, PEER_MXU_ROWS
        d_model = u_ref.shape[1]
        n_pieces = (PEER_EB // mr) * (d_model // kd) + (d_model // mr) * (PEER_EB // kd)
        per = len(units) // n_pieces
        done = 0
        for r0 in range(0, PEER_EB, mr):
            acc = None
            for k0 in range(0, d_model, kd):
                part = _dot(u_ref[r0:r0 + mr, k0:k0 + kd], h2t_ref[k0:k0 + kd, :])
                acc = part if acc is None else acc + part
                for ii, c in units[done:done + per]:
                    gate(ii, c)
                done += per
            a_new[r0:r0 + mr, :] = acc
        for r0 in range(0, d_model, mr):
            acc = ft_ref[r0:r0 + mr, :]
            for k0 in range(0, PEER_EB, kd):
                acc = acc + _dot(vt_ref[r0:r0 + mr, k0:k0 + kd], w_prev[k0:k0 + kd, :])
                for ii, c in units[done:done + per]:
                    gate(ii, c)
                done += per
            ft_ref[r0:r0 + mr, :] = acc
        for ii, c in units[done:]:
            gate(ii, c)

    @pl.when(s % 2 == 0)
    def _even():
        step(a0_ref, a1_ref, w1_ref, w0_ref)

    @pl.when(s % 2 == 1)
    def _odd():
        step(a1_ref, a0_ref, w0_ref, w1_ref)

    @pl.when(s == n_eb + 1)
    def _fin():
        x = x_ref[...] + mod_ref[0, 5:6, :] * ft_ref[...].T
        if final:
            x = _rms(x, fg_ref[...])
        o_ref[...] = x


def _peer(h2, x, mod, mod_row, wqt, keys, u, vt, final_g=None):
    n, d = x.shape
    n_exp = u.shape[0]
    n_eb = n_exp // PEER_EB
    tm = PEER_TM
    n_chunks = tm // LANE
    final = final_g is not None
    once = pl.Buffered(1)
    args = [h2, x, mod, wqt, keys, u, vt]
    in_specs = [pl.BlockSpec((tm, d), lambda i, e: (i, 0), pipeline_mode=once),
                pl.BlockSpec((tm, d), lambda i, e: (i, 0), pipeline_mode=once),
                pl.BlockSpec((1, N_MOD, d), lambda i, e: (mod_row(i), 0, 0)),
                pl.BlockSpec(wqt.shape, lambda i, e: (0, 0), pipeline_mode=once),
                pl.BlockSpec(keys.shape, lambda i, e: (0, 0, 0)),
                pl.BlockSpec((PEER_EB, d), lambda i, e: (jnp.minimum(e, n_eb - 1), 0)),
                pl.BlockSpec((d, PEER_EB), lambda i, e: (0, jnp.clip(e - 2, 0, n_eb - 1)))]
    if final:
        args.append(final_g)
        in_specs.append(pl.BlockSpec((1, d), lambda i, e: (0, 0)))
    tab32 = pltpu.VMEM((n_chunks, PEER_HEADS, PEER_NKEYS, LANE), _F32)
    tab16 = pltpu.VMEM((n_chunks, PEER_HEADS, PEER_NKEYS, LANE), _MXU)
    abuf = pltpu.VMEM((PEER_EB, tm), _F32)
    wbuf = pltpu.VMEM((PEER_EB, tm), _MXU)
    return pl.pallas_call(
        functools.partial(_peer_kernel, n_eb=n_eb, final=final),
        grid=(n // tm, n_eb + 2),
        in_specs=in_specs,
        out_specs=pl.BlockSpec((tm, d), lambda i, e: (i, 0)),
        out_shape=jax.ShapeDtypeStruct((n, d), _F32),
        scratch_shapes=[pltpu.VMEM((d, tm), _MXU), tab32, tab32, tab16, tab16,
                        pltpu.VMEM((d, tm), _F32), abuf, abuf, wbuf, wbuf],
        compiler_params=_cparams(("parallel", "arbitrary")),
        name="peer_ffn",
    )(*args)


def _dup_halves(w):
    a, b = w[:, :HEAD_DIM], w[:, HEAD_DIM:]
    return jnp.concatenate([a, a, b, b], axis=1)


def _prep_ab(w_in):
    aq, ak, av = w_in[:, 0:512], w_in[:, 512:1024], w_in[:, 1024:1536]
    bq, bk, bv = w_in[:, 1536:2048], w_in[:, 2048:2176], w_in[:, 2176:2304]
    cat = [aq, ak, av, bq, _rot_cols(bq, HEAD_DIM), _dup_halves(bk), _dup_halves(_rot_cols(bk, HEAD_DIM)),
           _dup_halves(bv)]
    return jnp.concatenate(cat, axis=1).astype(_MXU)


def _prep_cd(w_in, w_uq, w_ukv):
    d = w_in.shape[0]
    cq, ckv, kr = w_in[:, 0:256], w_in[:, 256:384], w_in[:, 384:416]
    dq, dk, dv = w_in[:, 416:928], w_in[:, 928:1440], w_in[:, 1440:1952]
    z64, z32 = jnp.zeros((d, 64), _F32), jnp.zeros((d, 32), _F32)
    kr128 = jnp.concatenate([z64, kr, z32], axis=1)
    krrot128 = jnp.concatenate([z64, _rot_cols(kr, MLA_ROPE), z32], axis=1)
    w = jnp.concatenate([cq, ckv, kr128, krrot128, dq, _rot_cols(dq, HEAD_DIM), dk, _rot_cols(dk, HEAD_DIM), dv],
                        axis=1).astype(_MXU)
    r = w_uq.shape[0]
    uq = w_uq.reshape(r, 8, MLA_NOPE + MLA_ROPE)
    nope, rope = uq[:, :, :MLA_NOPE], uq[:, :, MLA_NOPE:]
    rope_rot = _rot_cols(rope.reshape(r, 8 * MLA_ROPE), MLA_ROPE).reshape(r, 8, MLA_ROPE)
    zq = jnp.zeros((r, 8, 32), _F32)
    wqa = jnp.concatenate([nope, rope, zq], axis=2).reshape(r, 1024).astype(_MXU)
    wqb = jnp.concatenate([jnp.zeros_like(nope), rope_rot, zq], axis=2).reshape(r, 1024).astype(_MXU)
    rk = w_ukv.shape[0]
    ukv = w_ukv.reshape(rk, 8, 128)
    wk = jnp.concatenate([ukv[:, :, :MLA_NOPE], jnp.zeros((rk, 8, 64), _F32)], axis=2).reshape(rk, 1024).astype(_MXU)
    wv = ukv[:, :, MLA_NOPE:].reshape(rk, 512).astype(_MXU)
    lane = jnp.arange(LANE)
    src = (lane >= MLA_NOPE) & (lane < MLA_NOPE + MLA_ROPE)
    place = (src[:, None] & (lane[:, None] == (jnp.arange(1024)[None, :] % LANE))).astype(_MXU)
    return w, wqa, wqb, wk, wv, place


def kernel(x, c, ctx, c_ctx, ada_w, ada_b, norm1_g, norm2_g, w_out, peer_wq, peer_keys, peer_u, peer_v,
           ab_w_in, na_rpb, swa_sink, cd_w_in, mla_q_norm_g, mla_w_uq, mla_kv_norm_g, mla_w_ukv,
           diff_lambda, diff_subln_g, final_norm_g):
    batch, seq, d = x.shape
    ctx_len = ctx.shape[1]
    depth = ada_w.shape[0]
    assert seq % TM == 0 and (batch * ctx_len) % TM == 0 and seq % TQ_LOCAL == 0
    assert seq % PEER_TM == 0 and (batch * ctx_len) % PEER_TM == 0
    assert depth == 2, "even layers keep a context stream, the single odd layer is the last one"
    n_lat, n_ctx = batch * seq, batch * ctx_len
    xs = x.reshape(n_lat, d)
    cs = ctx.reshape(n_ctx, d)

    mod_rows = -(-(batch + 1) // 16) * 16
    cc = jnp.zeros((mod_rows, d), _F32).at[:batch].set(c).at[batch].set(c_ctx)
    mod_all = _modulation(cc, ada_w, ada_b)

    tiles_per_seq = seq // TM
    lat_row = lambda i: i // tiles_per_seq
    ctx_row = lambda i: batch
    lat_tab = lambda i: i % tiles_per_seq
    ctx_tab = lambda i: tiles_per_seq

    cos64, sin64 = _rope_tables(seq, HEAD_DIM)
    cos64p, sin64p = _pad_table(cos64, LANE, 1.0), _pad_table(sin64, LANE, 0.0)

    for l in range(depth):
        last = l == depth - 1
        j = l // 2
        mod = mod_all[l, :batch + 1].reshape(batch + 1, N_MOD, d)
        g1 = norm1_g[l].reshape(1, d)
        g2 = norm2_g[l].reshape(1, d)
        wo = w_out[l].astype(_MXU)
        wqt = peer_wq[l].T.astype(_MXU)
        keys = peer_keys[l].astype(_MXU)
        u = peer_u[l].astype(_MXU)
        vt = peer_v[l].T.astype(_MXU)
        if l % 2 == 0:
            w = _prep_ab(ab_w_in[j])
            aq, ak, av, bq, bk2, bv2 = _proj_ab(xs, mod, lat_row, g1, cos64p, sin64p, lat_tab, w)
            caq, cak, cav, cbq, cbk2, cbv2 = _proj_ab(cs, mod, ctx_row, g1, cos64p, sin64p, ctx_tab, w)
            bias = _na_bias_tables(na_rpb[j], seq // GRID_W)
            ya = _na_attention(aq, ak, av, cak, cav, bias, batch, seq, ctx_len)
            sink = swa_sink[j].astype(_F32)
            yb = _swa_attention(sink, bq, bk2, bv2, cbk2, cbv2, batch, seq, ctx_len)
            xs, h2 = _out_proj(xs, ya, yb, mod, lat_row, g2, wo[:512], wo[512:])
            if not last:
                pair_heads = [(m, half, m, m) for m in range(4) for half in range(2)]
                pair_outs = [("pair", 2 * m, 2 * m + 1) for m in range(4)]
                yca = _flash_attention(caq, cak, cav, heads=pair_heads, outs=pair_outs, batch=batch,
                                       q_per_batch=ctx_len, kv_per_batch=ctx_len, tq=ctx_len, tk=ctx_len,
                                       name="ctx_attn_a")
                gqa_heads = [(m, half, m // 2, m // 2) for m in range(4) for half in range(2)]
                ycb = _flash_attention(cbq, cbk2, cbv2, heads=gqa_heads, outs=pair_outs, batch=batch,
                                       q_per_batch=ctx_len, kv_per_batch=ctx_len, tq=ctx_len, tk=ctx_len,
                                       sink=sink, name="ctx_attn_b")
                cs, h2c = _out_proj(cs, yca, ycb, mod, ctx_row, g2, wo[:512], wo[512:])
        else:
            lam_init = 0.8 - 0.6 * math.exp(-0.3 * l)
            w, wqa, wqb, wk, wv, place = _prep_cd(cd_w_in[j], mla_w_uq[j], mla_w_ukv[j])
            cos32, sin32 = _rope_tables(seq, MLA_ROPE)
            ones64, zeros64 = jnp.ones((seq, 64), _F32), jnp.zeros((seq, 64), _F32)
            cosq = _pad_table(jnp.concatenate([ones64, cos32, ones64[:, :32]], 1), LANE, 1.0)
            sinq = _pad_table(jnp.concatenate([zeros64, sin32, zeros64[:, :32]], 1), LANE, 0.0)
            tabs = (cosq, sinq, cos64p, sin64p)
            qg = mla_q_norm_g[j].reshape(1, -1)
            kvg = mla_kv_norm_g[j].reshape(1, -1)
            qmt, km, vmt, dqt, dk, dvt = _proj_cd(xs, mod, lat_row, g1, tabs, lat_tab, w, qg, kvg, wqa, wqb, wk, wv, place)
            _, ckm, cvmt, _, cdk, cdvt = _proj_cd(cs, mod, ctx_row, g1, tabs, ctx_tab, w, qg, kvg, wqa, wqb, wk, wv, place)
            mla_heads = [(h, None, h, h // 2) for h in range(8)]
            pair_outs = [("pair", 2 * m, 2 * m + 1) for m in range(4)]
            yc = _flasht_attention(qmt, km, vmt, ckm, cvmt, heads=mla_heads, outs=pair_outs, batch=batch, seq=seq,
                                   ctx_len=ctx_len, tq=GLOBAL_TQ, tk=GLOBAL_TK, name="mla_attn")
            diff_heads = [(h, a, h, h) for h in range(4) for a in range(2)]
            diff_outs = [("diff", 2 * h, 2 * h + 1) for h in range(4)]
            od = _flasht_attention(dqt, dk, dvt, cdk, cdvt, heads=diff_heads, outs=diff_outs, batch=batch, seq=seq,
                                   ctx_len=ctx_len, tq=GLOBAL_TQ, tk=GLOBAL_TK,
                                   diff=(diff_lambda[j].astype(_F32), diff_subln_g[j].reshape(1, -1), lam_init),
                                   name="diff_attn")
            xs, h2 = _out_proj(xs, yc, od, mod, lat_row, g2, wo[:512], wo[512:])
        fg = final_norm_g.reshape(1, d) if last else None
        xs = _peer(h2, xs, mod, lambda i: i // (seq // PEER_TM), wqt, keys, u, vt, final_g=fg)
        if not last:
            cs = _peer(h2c, cs, mod, ctx_row, wqt, keys, u, vt)
    return xs.reshape(batch, seq, d)
```

```python
import functools
import math

import jax
import jax.numpy as jnp
from jax import lax
from jax.experimental import pallas as pl
from jax.experimental.pallas import tpu as pltpu

_F32 = jnp.float32
_MXU = jnp.bfloat16
_NEG = -1e30
_LOG2E = math.log2(math.e)

GRID_W = 64
HEAD_DIM = 64
ROPE_BASE = 10000.0
RMS_EPS = 1e-6
N_MOD = 6
NA_KR, NA_KC = 8, 16
SWA_WINDOW = 128
MLA_NOPE, MLA_ROPE = 64, 32
PEER_HEADS, PEER_NKEYS, PEER_TOPK = 8, 128, 16

LANE = 128
TM = 512
TQ_LOCAL = 256
NA_GROUP_ROWS = TQ_LOCAL // GRID_W
NA_WIN_ROWS = NA_KR + NA_GROUP_ROWS - 1
GLOBAL_TQ = 512
GLOBAL_TK = 512
PEER_TM = 1024
PEER_EB = 512
MXU_DEPTH = 256
PEER_MXU_ROWS = 512
PEER_SELECT_UNROLL = 8
VMEM_LIMIT = 56 * 1024 * 1024


def _cparams(sem, vmem=VMEM_LIMIT):
    return pltpu.CompilerParams(dimension_semantics=sem, vmem_limit_bytes=vmem)


def _dot(a, b):
    return jnp.dot(a, b, preferred_element_type=_F32)


def _dot_nt(a, b):
    return lax.dot_general(a, b, (((1,), (1,)), ((), ())), preferred_element_type=_F32)


def _rms(x, g):
    return x * lax.rsqrt(jnp.mean(x * x, axis=-1, keepdims=True) + RMS_EPS) * g


def _lane_half(shape):
    return lax.broadcasted_iota(jnp.int32, shape, len(shape) - 1) >= (LANE // 2)


def _mod_kernel(c_ref, w_ref, b_ref, o_ref):
    c = c_ref[...]
    a = c * jax.nn.sigmoid(c)
    w = w_ref[0]
    a_hi = a.astype(_MXU)
    a_lo = (a - a_hi.astype(_F32)).astype(_MXU)
    w_hi = w.astype(_MXU)
    w_lo = (w - w_hi.astype(_F32)).astype(_MXU)
    o_ref[0] = _dot(a_hi, w_hi) + _dot(a_lo, w_hi) + _dot(a_hi, w_lo) + b_ref[0]


def _modulation(cc, ada_w, ada_b):
    depth, d, n = ada_w.shape
    rows = cc.shape[0]
    tn = 768
    return pl.pallas_call(
        _mod_kernel,
        grid=(depth, n // tn),
        in_specs=[pl.BlockSpec((rows, d), lambda l, j: (0, 0)),
                  pl.BlockSpec((1, d, tn), lambda l, j: (l, 0, j)),
                  pl.BlockSpec((1, 1, tn), lambda l, j: (l, 0, j))],
        out_specs=pl.BlockSpec((1, rows, tn), lambda l, j: (l, 0, j)),
        out_shape=jax.ShapeDtypeStruct((depth, rows, n), _F32),
        compiler_params=_cparams(("parallel", "parallel")),
        name="adaln_mod",
    )(cc, ada_w, ada_b.reshape(depth, 1, n))


def _rope_tables(seq, d):
    t = jnp.arange(seq, dtype=jnp.int32)
    q = d // 4
    freq = ROPE_BASE ** (-jnp.arange(q, dtype=_F32) / q)

    def one(pos):
        ang = pos.astype(_F32)[:, None] * freq[None, :]
        return jnp.concatenate([jnp.cos(ang)] * 2, -1), jnp.concatenate([jnp.sin(ang)] * 2, -1)

    cr, sr = one(t // GRID_W)
    cc, sc = one(t % GRID_W)
    return jnp.concatenate([cr, cc], -1), jnp.concatenate([sr, sc], -1)


def _rot_cols(w, d):
    k, n = w.shape
    w5 = w.reshape(k, n // d, 2, 2, d // 4)
    return jnp.stack([-w5[:, :, :, 1], w5[:, :, :, 0]], axis=3).reshape(k, n)


def _pad_table(tab, width, ident):
    s, w = tab.shape
    if w < width:
        reps = width // w
        tab = jnp.tile(tab, (1, reps))
    return jnp.concatenate([tab, jnp.full((TM, width), ident, _F32)], axis=0)


def _proj_ab_kernel(x_ref, mod_ref, g_ref, cos_ref, sin_ref, w_ref,
                    aq_ref, ak_ref, av_ref, bq_ref, bk_ref, bv_ref):
    x = x_ref[...]
    h = _rms(x, g_ref[...]) * (1.0 + mod_ref[0, 1:2, :]) + mod_ref[0, 0:1, :]
    hb = h.astype(_MXU)
    cos = cos_ref[...]
    sin = sin_ref[...]

    def proj(lo, n):
        return _dot(hb, w_ref[:, lo:lo + n])

    scale = HEAD_DIM ** -0.5
    aq_ref[...] = (proj(0, 512) * scale).astype(aq_ref.dtype)
    ak_ref[...] = proj(512, 512).astype(ak_ref.dtype)
    av_ref[...] = proj(1024, 512).astype(av_ref.dtype)
    cos4 = jnp.tile(cos, (1, 4))
    sin4 = jnp.tile(sin, (1, 4))
    bq_ref[...] = ((proj(1536, 512) * cos4 + proj(2048, 512) * sin4) * scale).astype(bq_ref.dtype)
    cos2 = jnp.tile(cos, (1, 2))
    sin2 = jnp.tile(sin, (1, 2))
    bk_ref[...] = (proj(2560, 256) * cos2 + proj(2816, 256) * sin2).astype(bk_ref.dtype)
    bv_ref[...] = proj(3072, 256).astype(bv_ref.dtype)


def _proj_ab(x, mod, mod_row, g, cos, sin, tab_row, w):
    n, d = x.shape
    widths = (512, 512, 512, 512, 256, 256)
    return pl.pallas_call(
        _proj_ab_kernel,
        grid=(n // TM,),
        in_specs=[pl.BlockSpec((TM, d), lambda i: (i, 0)),
                  pl.BlockSpec((1, N_MOD, d), lambda i: (mod_row(i), 0, 0)),
                  pl.BlockSpec((1, d), lambda i: (0, 0)),
                  pl.BlockSpec((TM, LANE), lambda i: (tab_row(i), 0)),
                  pl.BlockSpec((TM, LANE), lambda i: (tab_row(i), 0)),
                  pl.BlockSpec(w.shape, lambda i: (0, 0))],
        out_specs=[pl.BlockSpec((TM, wd), lambda i: (i, 0)) for wd in widths],
        out_shape=[jax.ShapeDtypeStruct((n, wd), _MXU) for wd in widths],
        compiler_params=_cparams(("parallel",)),
        name="proj_ab",
    )(x, mod, g, cos, sin, w)


def _na_kernel(q_ref, k_ref, v_ref, kc_ref, vc_ref, bias_ref, o_ref, *, rows):
    g = pl.program_id(1)
    ks = jnp.clip(NA_GROUP_ROWS * g - NA_KR // 2, 0, rows - NA_WIN_ROWS)
    start = pl.multiple_of(ks * GRID_W, GRID_W)
    nwin = NA_WIN_ROWS * GRID_W
    hi = _lane_half((TQ_LOCAL, LANE))
    for m in range(4):
        cols = slice(m * LANE, (m + 1) * LANE)
        q2 = q_ref[:, cols]
        k2 = k_ref[pl.ds(start, nwin), cols]
        v2 = v_ref[pl.ds(start, nwin), cols]
        kc2 = kc_ref[:, cols]
        vc2 = vc_ref[:, cols]
        outs = []
        for half in range(2):
            qh = jnp.where(hi == (half == 1), q2, jnp.zeros_like(q2))
            s_lat = _dot_nt(qh, k2) + bias_ref[0, 2 * m + half]
            s_ctx = _dot_nt(qh, kc2)
            mx = jnp.maximum(jnp.max(s_lat, axis=-1, keepdims=True), jnp.max(s_ctx, axis=-1, keepdims=True))
            p_lat = jnp.exp(s_lat - mx)
            p_ctx = jnp.exp(s_ctx - mx)
            den = jnp.sum(p_lat, axis=-1, keepdims=True) + jnp.sum(p_ctx, axis=-1, keepdims=True)
            o = _dot(p_lat.astype(_MXU), v2) + _dot(p_ctx.astype(_MXU), vc2)
            outs.append(o / den)
        o_ref[:, cols] = jnp.where(hi, outs[1], outs[0]).astype(o_ref.dtype)


def _na_bias_tables(rpb, rows):
    gq = NA_GROUP_ROWS
    n_groups = rows // gq
    ql = jnp.arange(gq)[:, None]
    kl = jnp.arange(NA_WIN_ROWS)[None, :]
    qc = jnp.arange(GRID_W)[:, None]
    kc = jnp.arange(GRID_W)[None, :]
    cs = jnp.clip(qc - NA_KC // 2, 0, GRID_W - NA_KC)
    col_valid = (kc >= cs) & (kc < cs + NA_KC)
    col_idx = jnp.clip(kc - qc + NA_KC - 1, 0, 2 * NA_KC - 2)
    exact = lax.Precision.HIGHEST
    oh_c = jax.nn.one_hot(col_idx, 2 * NA_KC - 1, dtype=_F32)
    rpb_cols = jnp.einsum("hab,xyb->haxy", rpb.astype(_F32), oh_c, precision=exact)
    tabs = []
    for g in (0, 1, n_groups - 1):
        ks = min(max(gq * g - NA_KR // 2, 0), rows - NA_WIN_ROWS)
        rq = gq * g + ql
        rk = ks + kl
        r0 = jnp.clip(rq - NA_KR // 2, 0, rows - NA_KR)
        row_valid = (rk >= r0) & (rk < r0 + NA_KR)
        row_idx = jnp.clip(rk - rq + NA_KR - 1, 0, 2 * NA_KR - 2)
        oh_r = jax.nn.one_hot(row_idx, 2 * NA_KR - 1, dtype=_F32)
        b = jnp.einsum("qka,haxy->hqxky", oh_r, rpb_cols, precision=exact)
        valid = row_valid[:, None, :, None] & col_valid[None, :, None, :]
        b = jnp.where(valid[None], b, _NEG)
        tabs.append(b.reshape(rpb.shape[0], gq * GRID_W, NA_WIN_ROWS * GRID_W))
    return jnp.stack(tabs)


def _na_attention(aq, ak, av, cak, cav, bias, batch, seq, ctx_len):
    rows = seq // GRID_W
    n_groups = seq // TQ_LOCAL
    nwin = NA_WIN_ROWS * GRID_W

    def bias_row(b, g):
        return (jnp.where(g == 0, 0, jnp.where(g == n_groups - 1, 2, 1)), 0, 0, 0)

    return pl.pallas_call(
        functools.partial(_na_kernel, rows=rows),
        grid=(batch, n_groups),
        in_specs=[pl.BlockSpec((TQ_LOCAL, 512), lambda b, g: (b * n_groups + g, 0)),
                  pl.BlockSpec((seq, 512), lambda b, g: (b, 0)),
                  pl.BlockSpec((seq, 512), lambda b, g: (b, 0)),
                  pl.BlockSpec((ctx_len, 512), lambda b, g: (b, 0)),
                  pl.BlockSpec((ctx_len, 512), lambda b, g: (b, 0)),
                  pl.BlockSpec((1, 8, TQ_LOCAL, nwin), bias_row)],
        out_specs=pl.BlockSpec((TQ_LOCAL, 512), lambda b, g: (b * n_groups + g, 0)),
        out_shape=jax.ShapeDtypeStruct((batch * seq, 512), _MXU),
        compiler_params=_cparams(("parallel", "arbitrary")),
        name="na_attention",
    )(aq, ak, av, cak, cav, bias)


def _swa_kernel(sink_ref, q_ref, k_ref, v_ref, kc_ref, vc_ref, o_ref, *, seq):
    t = pl.program_id(1)
    kwin = TQ_LOCAL + 2 * SWA_WINDOW
    start = t * TQ_LOCAL
    kstart = pl.multiple_of(jnp.clip(start - SWA_WINDOW, 0, seq - kwin), LANE)
    qpos = start + lax.broadcasted_iota(jnp.int32, (TQ_LOCAL, kwin), 0)
    kpos = kstart + lax.broadcasted_iota(jnp.int32, (TQ_LOCAL, kwin), 1)
    mask = jnp.where(jnp.abs(kpos - qpos) <= SWA_WINDOW, 0.0, _NEG).astype(_F32)
    hi = _lane_half((TQ_LOCAL, LANE))
    for m in range(4):
        cols = slice(m * LANE, (m + 1) * LANE)
        kv = m // 2
        kcols = slice(kv * LANE, (kv + 1) * LANE)
        q2 = q_ref[:, cols]
        k2 = k_ref[pl.ds(kstart, kwin), kcols]
        v2 = v_ref[pl.ds(kstart, kwin), kcols]
        kc2 = kc_ref[:, kcols]
        vc2 = vc_ref[:, kcols]
        outs = []
        for half in range(2):
            sink = sink_ref[2 * m + half]
            qh = jnp.where(hi == (half == 1), q2, jnp.zeros_like(q2))
            s_lat = _dot_nt(qh, k2) + mask
            s_ctx = _dot_nt(qh, kc2)
            mx = jnp.maximum(jnp.max(s_lat, axis=-1, keepdims=True), jnp.max(s_ctx, axis=-1, keepdims=True))
            mx = jnp.maximum(mx, sink)
            p_lat = jnp.exp(s_lat - mx)
            p_ctx = jnp.exp(s_ctx - mx)
            den = (jnp.sum(p_lat, axis=-1, keepdims=True) + jnp.sum(p_ctx, axis=-1, keepdims=True)
                   + jnp.exp(sink - mx))
            o = _dot(p_lat.astype(_MXU), v2) + _dot(p_ctx.astype(_MXU), vc2)
            outs.append(o / den)
        o_ref[:, cols] = jnp.where(hi, outs[1], outs[0]).astype(o_ref.dtype)


def _swa_attention(sink, bq, bk2, bv2, cbk2, cbv2, batch, seq, ctx_len):
    n_t = seq // TQ_LOCAL
    return pl.pallas_call(
        functools.partial(_swa_kernel, seq=seq),
        grid=(batch, n_t),
        in_specs=[pl.BlockSpec(memory_space=pltpu.SMEM),
                  pl.BlockSpec((TQ_LOCAL, 512), lambda b, t: (b * n_t + t, 0)),
                  pl.BlockSpec((seq, 256), lambda b, t: (b, 0)),
                  pl.BlockSpec((seq, 256), lambda b, t: (b, 0)),
                  pl.BlockSpec((ctx_len, 256), lambda b, t: (b, 0)),
                  pl.BlockSpec((ctx_len, 256), lambda b, t: (b, 0))],
        out_specs=pl.BlockSpec((TQ_LOCAL, 512), lambda b, t: (b * n_t + t, 0)),
        out_shape=jax.ShapeDtypeStruct((batch * seq, 512), _MXU),
        compiler_params=_cparams(("parallel", "arbitrary")),
        name="swa_attention",
    )(sink, bq, bk2, bv2, cbk2, cbv2)


def _flash_kernel(*refs, heads, outs, has_ctx, has_sink, diff_cfg, n_kv):
    it = iter(refs)
    sink_ref = next(it) if has_sink else None
    q_ref, k_ref, v_ref = next(it), next(it), next(it)
    kc_ref = next(it) if has_ctx else None
    vc_ref = next(it) if has_ctx else None
    lam_ref = next(it) if diff_cfg else None
    sg_ref = next(it) if diff_cfg else None
    o_ref = next(it)
    m_ref, l_ref, acc_ref = next(it), next(it), next(it)
    kv = pl.program_id(2)
    tq = q_ref.shape[0]
    hi = _lane_half((tq, LANE))

    @pl.when(kv == 0)
    def _init():
        for h in range(len(heads)):
            if has_sink:
                m_ref[h] = jnp.full((tq, LANE), sink_ref[h], _F32)
                l_ref[h] = jnp.ones((tq, LANE), _F32)
            else:
                m_ref[h] = jnp.full((tq, LANE), _NEG, _F32)
                l_ref[h] = jnp.zeros((tq, LANE), _F32)
            acc_ref[h] = jnp.zeros((tq, LANE), _F32)

    def attend(kr, vr):
        for h, (qb, qhalf, kb, vb) in enumerate(heads):
            q2 = q_ref[:, qb * LANE:(qb + 1) * LANE]
            if qhalf is not None:
                q2 = jnp.where(hi == (qhalf == 1), q2, jnp.zeros_like(q2))
            s = _dot_nt(q2, kr[:, kb * LANE:(kb + 1) * LANE])
            m_old = m_ref[h][:, :1]
            m_new = jnp.maximum(m_old, jnp.max(s, axis=-1, keepdims=True))
            alpha = jnp.exp(m_old - m_new)
            p = jnp.exp(s - m_new)
            l_ref[h] = jnp.broadcast_to(alpha * l_ref[h][:, :1] + jnp.sum(p, axis=-1, keepdims=True), (tq, LANE))
            acc_ref[h] = alpha * acc_ref[h] + _dot(p.astype(_MXU), vr[:, vb * LANE:(vb + 1) * LANE])
            m_ref[h] = jnp.broadcast_to(m_new, (tq, LANE))

    if has_ctx:
        @pl.when(kv == 0)
        def _ctx():
            attend(kc_ref, vc_ref)

    attend(k_ref, v_ref)

    @pl.when(kv == n_kv - 1)
    def _fin():
        def head_out(h):
            return acc_ref[h] / l_ref[h][:, :1]

        for j, spec in enumerate(outs):
            if spec[0] == "full":
                o = head_out(spec[1])
            elif spec[0] == "pair":
                o = jnp.where(hi, head_out(spec[2]), head_out(spec[1]))
            else:
                lv = lam_ref[...]
                lam = (jnp.exp(jnp.sum(lv[0:1] * lv[1:2], axis=-1, keepdims=True))
                       - jnp.exp(jnp.sum(lv[2:3] * lv[3:4], axis=-1, keepdims=True)) + diff_cfg)
                o = _rms(head_out(spec[1]) - lam * head_out(spec[2]), sg_ref[...]) * (1.0 - diff_cfg)
            o_ref[:, j * LANE:(j + 1) * LANE] = o.astype(o_ref.dtype)


def _flash_attention(q, k, v, *, heads, outs, batch, q_per_batch, kv_per_batch, tq, tk,
                     q_row0=0, kv_row0=0, ctx=None, ctx_len=0, ctx_row0=0, sink=None, diff=None, name="flash"):
    n_q = q_per_batch // tq
    n_kv = kv_per_batch // tk
    qw, kw, vw = q.shape[1], k.shape[1], v.shape[1]
    args, in_specs = [], []
    if sink is not None:
        args.append(sink)
        in_specs.append(pl.BlockSpec(memory_space=pltpu.SMEM))
    q0, k0 = q_row0 // tq, kv_row0 // tk
    args += [q, k, v]
    in_specs += [pl.BlockSpec((tq, qw), lambda b, i, j: (q0 + b * n_q + i, 0)),
                 pl.BlockSpec((tk, kw), lambda b, i, j: (k0 + b * n_kv + j, 0)),
                 pl.BlockSpec((tk, vw), lambda b, i, j: (k0 + b * n_kv + j, 0))]
    if ctx is not None:
        c0 = ctx_row0 // ctx_len
        args += [ctx[0], ctx[1]]
        in_specs += [pl.BlockSpec((ctx_len, kw), lambda b, i, j: (c0 + b, 0)),
                     pl.BlockSpec((ctx_len, vw), lambda b, i, j: (c0 + b, 0))]
    diff_cfg = None
    if diff is not None:
        lam_vecs, subln_g, diff_cfg = diff
        args += [lam_vecs, subln_g]
        in_specs += [pl.BlockSpec(lam_vecs.shape, lambda b, i, j: (0, 0)),
                     pl.BlockSpec(subln_g.shape, lambda b, i, j: (0, 0))]
    nh = len(heads)
    ow = len(outs) * LANE
    return pl.pallas_call(
        functools.partial(_flash_kernel, heads=tuple(heads), outs=tuple(outs), has_ctx=ctx is not None,
                          has_sink=sink is not None, diff_cfg=diff_cfg, n_kv=n_kv),
        grid=(batch, n_q, n_kv),
        in_specs=in_specs,
        out_specs=pl.BlockSpec((tq, ow), lambda b, i, j: (b * n_q + i, 0)),
        out_shape=jax.ShapeDtypeStruct((batch * q_per_batch, ow), _MXU),
        scratch_shapes=[pltpu.VMEM((nh, tq, LANE), _F32)] * 3,
        compiler_params=_cparams(("parallel", "parallel", "arbitrary")),
        name=name,
    )(*args)


def _flasht_kernel(*refs, heads, outs, diff_cfg, n_kv):
    it = iter(refs)
    qt_ref, k_ref, vt_ref, kc_ref, vct_ref = next(it), next(it), next(it), next(it), next(it)
    lam_ref = next(it) if diff_cfg else None
    sg_ref = next(it) if diff_cfg else None
    o_ref = next(it)
    m_ref, l_ref, acc_ref = next(it), next(it), next(it)
    kv = pl.program_id(2)
    tq = qt_ref.shape[1]
    row_hi = lax.broadcasted_iota(jnp.int32, (LANE, tq), 0) >= (LANE // 2)

    @pl.when(kv == 0)
    def _init():
        m_ref[...] = jnp.full(m_ref.shape, _NEG, _F32)
        l_ref[...] = jnp.zeros(l_ref.shape, _F32)
        acc_ref[...] = jnp.zeros(acc_ref.shape, _F32)

    def attend(kr, vtr):
        def scores(h):
            qb, qhalf, kb, _ = heads[h]
            qt = qt_ref[qb * LANE:(qb + 1) * LANE, :]
            if qhalf is not None:
                qt = jnp.where(row_hi == (qhalf == 1), qt, jnp.zeros_like(qt))
            return _dot(kr[:, kb * LANE:(kb + 1) * LANE], qt)

        s_next = scores(0)
        for h in range(len(heads)):
            s = s_next
            if h + 1 < len(heads):
                s_next = scores(h + 1)
            vb = heads[h][3]
            m_old = m_ref[h, 0:1, :]
            m_new = jnp.maximum(m_old, jnp.max(s, axis=0, keepdims=True))
            alpha = jnp.exp2(m_old - m_new)
            p = jnp.exp2(s - m_new)
            l_new = alpha * l_ref[h, 0:1, :] + jnp.sum(p, axis=0, keepdims=True)
            acc_ref[h] = alpha * acc_ref[h] + _dot(vtr[vb * LANE:(vb + 1) * LANE, :], p.astype(_MXU))
            l_ref[h] = jnp.broadcast_to(l_new, (8, tq))
            m_ref[h] = jnp.broadcast_to(m_new, (8, tq))

    @pl.when(kv == 0)
    def _ctx():
        attend(kc_ref, vct_ref)

    attend(k_ref, vt_ref)

    @pl.when(kv == n_kv - 1)
    def _fin():
        def head_out(h):
            return acc_ref[h] / l_ref[h, 0:1, :]

        for j, spec in enumerate(outs):
            if spec[0] == "full":
                o = head_out(spec[1]).T
            elif spec[0] == "pair":
                o = jnp.where(row_hi, head_out(spec[2]), head_out(spec[1])).T
            else:
                lv = lam_ref[...]
                lam = (jnp.exp(jnp.sum(lv[0:1] * lv[1:2], axis=-1, keepdims=True))
                       - jnp.exp(jnp.sum(lv[2:3] * lv[3:4], axis=-1, keepdims=True)) + diff_cfg)
                o = _rms((head_out(spec[1]) - lam * head_out(spec[2])).T, sg_ref[...]) * (1.0 - diff_cfg)
            o_ref[:, j * LANE:(j + 1) * LANE] = o.astype(o_ref.dtype)


def _flasht_attention(qt, k, vt, kc, vct, *, heads, outs, batch, seq, ctx_len, tq, tk, diff=None, name="flasht"):
    n_q, n_kv = seq // tq, seq // tk
    qw, kw, vw = qt.shape[0], k.shape[1], vt.shape[0]
    args = [qt, k, vt, kc, vct]
    in_specs = [pl.BlockSpec((qw, tq), lambda b, i, j: (0, b * n_q + i)),
                pl.BlockSpec((tk, kw), lambda b, i, j: (b * n_kv + j, 0)),
                pl.BlockSpec((vw, tk), lambda b, i, j: (0, b * n_kv + j)),
                pl.BlockSpec((ctx_len, kw), lambda b, i, j: (b, 0)),
                pl.BlockSpec((vw, ctx_len), lambda b, i, j: (0, b))]
    diff_cfg = None
    if diff is not None:
        lam_vecs, subln_g, diff_cfg = diff
        args += [lam_vecs, subln_g]
        in_specs += [pl.BlockSpec(lam_vecs.shape, lambda b, i, j: (0, 0)),
                     pl.BlockSpec(subln_g.shape, lambda b, i, j: (0, 0))]
    nh = len(heads)
    ow = len(outs) * LANE
    return pl.pallas_call(
        functools.partial(_flasht_kernel, heads=tuple(heads), outs=tuple(outs), diff_cfg=diff_cfg, n_kv=n_kv),
        grid=(batch, n_q, n_kv),
        in_specs=in_specs,
        out_specs=pl.BlockSpec((tq, ow), lambda b, i, j: (b * n_q + i, 0)),
        out_shape=jax.ShapeDtypeStruct((batch * seq, ow), _MXU),
        scratch_shapes=[pltpu.VMEM((nh, 8, tq), _F32), pltpu.VMEM((nh, 8, tq), _F32),
                        pltpu.VMEM((nh, LANE, tq), _F32)],
        compiler_params=_cparams(("parallel", "parallel", "arbitrary")),
        name=name,
    )(*args)


def _out_kernel(x_ref, ya_ref, yb_ref, mod_ref, g2_ref, wa_ref, wb_ref, xo_ref, h2_ref):
    y = _dot(ya_ref[...], wa_ref[...]) + _dot(yb_ref[...], wb_ref[...])
    x = x_ref[...] + mod_ref[0, 2:3, :] * y
    xo_ref[...] = x
    h2 = _rms(x, g2_ref[...]) * (1.0 + mod_ref[0, 4:5, :]) + mod_ref[0, 3:4, :]
    h2_ref[...] = h2.astype(h2_ref.dtype)


def _out_proj(x, ya, yb, mod, mod_row, g2, wa, wb):
    n, d = x.shape
    return pl.pallas_call(
        _out_kernel,
        grid=(n // TM,),
        in_specs=[pl.BlockSpec((TM, d), lambda i: (i, 0)),
                  pl.BlockSpec((TM, ya.shape[1]), lambda i: (i, 0)),
                  pl.BlockSpec((TM, yb.shape[1]), lambda i: (i, 0)),
                  pl.BlockSpec((1, N_MOD, d), lambda i: (mod_row(i), 0, 0)),
                  pl.BlockSpec((1, d), lambda i: (0, 0)),
                  pl.BlockSpec(wa.shape, lambda i: (0, 0)),
                  pl.BlockSpec(wb.shape, lambda i: (0, 0))],
        out_specs=[pl.BlockSpec((TM, d), lambda i: (i, 0)), pl.BlockSpec((TM, d), lambda i: (i, 0))],
        out_shape=[jax.ShapeDtypeStruct((n, d), _F32), jax.ShapeDtypeStruct((n, d), _MXU)],
        compiler_params=_cparams(("parallel",)),
        name="out_proj",
    )(x, ya, yb, mod, g2, wa, wb)


def _proj_cd_kernel(x_ref, mod_ref, g_ref, cq_ref, sq_ref, cd_ref, sd_ref, w_ref, qg_ref, kvg_ref,
                    wqa_ref, wqb_ref, wk_ref, wv_ref, place_ref,
                    qm_ref, km_ref, vm_ref, dq_ref, dk_ref, dv_ref):
    x = x_ref[...]
    h = _rms(x, g_ref[...]) * (1.0 + mod_ref[0, 1:2, :]) + mod_ref[0, 0:1, :]
    hb = h.astype(_MXU)

    def proj(lo, n):
        return _dot(hb, w_ref[:, lo:lo + n])

    cosq, sinq = cq_ref[...], sq_ref[...]
    cosd, sind = cd_ref[...], sd_ref[...]
    cqn = _rms(proj(0, 256), qg_ref[...]).astype(_MXU)
    ckvn = _rms(proj(256, 128), kvg_ref[...]).astype(_MXU)
    cos8, sin8 = jnp.tile(cosq, (1, 8)), jnp.tile(sinq, (1, 8))
    qm = _dot(cqn, wqa_ref[...]) * cos8 + _dot(cqn, wqb_ref[...]) * sin8
    qm_ref[...] = (qm * ((MLA_NOPE + MLA_ROPE) ** -0.5 * _LOG2E)).T.astype(qm_ref.dtype)
    kr = (proj(384, 128) * cosq + proj(512, 128) * sinq).astype(_MXU)
    km_ref[...] = (_dot(ckvn, wk_ref[...]) + _dot(kr, place_ref[...])).astype(km_ref.dtype)
    vm_ref[...] = _dot(ckvn, wv_ref[...]).T.astype(vm_ref.dtype)
    cos4, sin4 = jnp.tile(cosd, (1, 4)), jnp.tile(sind, (1, 4))
    dq_ref[...] = ((proj(640, 512) * cos4 + proj(1152, 512) * sin4) * (HEAD_DIM ** -0.5 * _LOG2E)).T.astype(dq_ref.dtype)
    dk_ref[...] = (proj(1664, 512) * cos4 + proj(2176, 512) * sin4).astype(dk_ref.dtype)
    dv_ref[...] = proj(2688, 512).T.astype(dv_ref.dtype)


def _proj_cd(x, mod, mod_row, g, tabs, tab_row, w, qg, kvg, wqa, wqb, wk, wv, place):
    n, d = x.shape
    outs = ((1024, True), (1024, False), (512, True), (512, True), (512, False), (512, True))
    full = lambda a: pl.BlockSpec(a.shape, lambda i: (0, 0))
    return pl.pallas_call(
        _proj_cd_kernel,
        grid=(n // TM,),
        in_specs=[pl.BlockSpec((TM, d), lambda i: (i, 0)),
                  pl.BlockSpec((1, N_MOD, d), lambda i: (mod_row(i), 0, 0)),
                  pl.BlockSpec((1, d), lambda i: (0, 0))]
                 + [pl.BlockSpec((TM, LANE), lambda i: (tab_row(i), 0))] * 4
                 + [full(a) for a in (w, qg, kvg, wqa, wqb, wk, wv, place)],
        out_specs=[pl.BlockSpec((wd, TM), lambda i: (0, i)) if fm else pl.BlockSpec((TM, wd), lambda i: (i, 0))
                   for wd, fm in outs],
        out_shape=[jax.ShapeDtypeStruct((wd, n) if fm else (n, wd), _MXU) for wd, fm in outs],
        compiler_params=_cparams(("parallel",)),
        name="proj_cd",
    )(x, mod, g, *tabs, w, qg, kvg, wqa, wqb, wk, wv, place)


_ERF_ALPHA = (-2.72614225801306e-10, 2.77068142495902e-08, -2.10102402082508e-06, -5.69250639462346e-05,
              -7.34990630326855e-04, -2.95459980854025e-03, -1.60960333262415e-02)
_ERF_BETA = (-1.45660718464996e-05, -2.13374055278905e-04, -1.68282697438203e-03, -7.37332916720468e-03,
             -1.42647390514189e-02)


def _erf(x):
    x = jnp.clip(x, -4.0, 4.0)
    x2 = x * x
    p = jnp.full_like(x, _ERF_ALPHA[0])
    for c in _ERF_ALPHA[1:]:
        p = p * x2 + c
    q = jnp.full_like(x, _ERF_BETA[0])
    for c in _ERF_BETA[1:]:
        q = q * x2 + c
    return x * p / q


_MARK = 2.0 ** 121


def _extract16(x, first_only):
    r = x.shape[0]
    row = lax.broadcasted_iota(jnp.int32, x.shape, 0)
    row16 = lax.broadcasted_iota(jnp.int32, (PEER_TOPK, LANE), 0)
    vals = jnp.zeros((PEER_TOPK, LANE), _F32)
    for k in range(PEER_TOPK):
        m = jnp.max(x, axis=0, keepdims=True)
        hit = x == m
        if first_only:
            hit = row == jnp.min(jnp.where(hit, row, r), axis=0, keepdims=True)
        x = jnp.where(hit, -_MARK * (32 + k), x)
        vals = jnp.where(row16 == k, m, vals)
    order = jnp.where(x < -16.0 * _MARK, x * (-1.0 / _MARK) - 32.0, float(PEER_TOPK))
    return order, vals


def _staircase_rows():
    groups = []
    for r in range(PEER_TOPK // 2):
        n = PEER_TOPK // (r + 1)
        for c0 in range(0, n, 8):
            groups.append((r, c0, min(8, n - c0)))
    groups.append((None, 0, 8))
    return groups


def _peer_select(sa, sb, first_only):
    ra, av = _extract16(sa, first_only)
    rb, bv = _extract16(sb, first_only)
    groups = _staircase_rows()
    sub = lax.broadcasted_iota(jnp.int32, (8, LANE), 0)
    pieces = []
    for r, c0, nv in groups:
        if r is None:
            piece = av[8:16] + bv[0:1]
        else:
            piece = av[r:r + 1] + bv[c0:c0 + 8]
            if nv < 8:
                piece = jnp.where(sub < nv, piece, -jnp.inf)
        pieces.append(piece)
    cand = jnp.concatenate(pieces, axis=0)
    e_cand = jnp.exp(cand - (av[0:1] + bv[0:1]))
    sel, _ = _extract16(cand, first_only)
    sel = jnp.where(sel < float(PEER_TOPK), 1.0, 0.0)
    z = jnp.sum(sel * e_cand, axis=0, keepdims=True)
    lr = jnp.zeros(sa.shape, _F32)
    for g, (r, c0, nv) in enumerate(groups):
        blk = sel[8 * g:8 * g + 8]
        if r is None:
            for q in range(8):
                lr = jnp.where(ra == float(8 + q), blk[q:q + 1], lr)
        elif c0 == 0:
            cnt = jnp.sum(blk, axis=0, keepdims=True)
            if PEER_TOPK // (r + 1) > 8:
                cnt = cnt + jnp.sum(sel[8 * g + 8:8 * g + 16], axis=0, keepdims=True)
            lr = jnp.where(ra == float(r), cnt, lr)
    ea = jnp.exp(sa - av[0:1]) / z
    eb = jnp.exp(sb - bv[0:1])
    n_sel = (jnp.sum(jnp.where(ra < float(PEER_TOPK), 1.0, 0.0), axis=0, keepdims=True)
             + jnp.sum(jnp.where(rb < float(PEER_TOPK), 1.0, 0.0), axis=0, keepdims=True)
             + jnp.sum(sel, axis=0, keepdims=True))
    return lr, ea, rb, eb, n_sel


def _peer_kernel(h2_ref, x_ref, mod_ref, wqt_ref, keys_ref, u_ref, vt_ref, *rest, n_eb, final):
    if final:
        fg_ref, o_ref, h2t_ref, lr_ref, ea_ref, rb_ref, eb_ref, ft_ref, a0_ref, a1_ref, w0_ref, w1_ref = rest
    else:
        o_ref, h2t_ref, lr_ref, ea_ref, rb_ref, eb_ref, ft_ref, a0_ref, a1_ref, w0_ref, w1_ref = rest
    s = pl.program_id(1)
    n_chunks = PEER_TM // LANE
    i_per = PEER_EB // PEER_NKEYS
    gdt = rb_ref.dtype

    @pl.when(s == 0)
    def _select():
        h2t = h2_ref[...].astype(_F32).T.astype(_MXU)
        h2t_ref[...] = h2t
        ft_ref[...] = jnp.zeros_like(ft_ref)
        a1_ref[...] = jnp.zeros_like(a1_ref)
        w0_ref[...] = jnp.zeros_like(w0_ref)

        def stash_scores():
            for hp in range(2 * PEER_HEADS):
                qt = _dot(wqt_ref[hp * LANE:(hp + 1) * LANE, :], h2t_ref[...]).astype(_MXU)
                st = _dot(keys_ref[hp % 2], qt)
                for c in range(n_chunks):
                    (lr_ref if hp % 2 == 0 else ea_ref)[c, hp // 2] = st[:, c * LANE:(c + 1) * LANE]

        def unit(c, h, first_only):
            lr, ea, rb, eb, n_sel = _peer_select(lr_ref[c, h], ea_ref[c, h], first_only)
            lr_ref[c, h] = lr
            ea_ref[c, h] = ea
            rb_ref[c, h] = rb.astype(gdt)
            eb_ref[c, h] = eb.astype(gdt)
            return n_sel

        per_trip = PEER_SELECT_UNROLL

        def fast(t, worst):
            c, h0 = t // (PEER_HEADS // per_trip), (t % (PEER_HEADS // per_trip)) * per_trip
            n = [unit(c, h0 + u, False) for u in range(per_trip)]
            return jnp.maximum(worst, jnp.max(functools.reduce(jnp.maximum, n)))

        stash_scores()
        worst = lax.fori_loop(0, n_chunks * PEER_HEADS // per_trip, fast, jnp.float32(0.0))

        @pl.when(worst > 3.0 * PEER_TOPK)
        def _ties():
            stash_scores()

            def exact(t, carry):
                unit(t // PEER_HEADS, t % PEER_HEADS, True)
                return carry

            lax.fori_loop(0, n_chunks * PEER_HEADS, exact, 0)

    def step(a_new, a_prev, w_new, w_prev):
        blk = jnp.clip(s - 1, 0, n_eb - 1)

        def gate(ii, c):
            i = blk * i_per + ii
            rows = slice(ii * PEER_NKEYS, (ii + 1) * PEER_NKEYS)
            cols = slice(c * LANE, (c + 1) * LANE)
            g = jnp.zeros((PEER_NKEYS, LANE), gdt)
            for h in range(PEER_HEADS):
                lr_i = lr_ref[c, h, pl.ds(i, 1), :].astype(gdt)
                ea_i = ea_ref[c, h, pl.ds(i, 1), :].astype(gdt)
                g = g + jnp.where(rb_ref[c, h] < lr_i, eb_ref[c, h], jnp.zeros((), gdt)) * ea_i
            a = a_prev[rows, cols]
            gelu = 0.5 * a * (1.0 + _erf(a * (2.0 ** -0.5)))
            w_new[rows, cols] = gelu.astype(gdt) * g

        units = [(ii, c) for ii in range(i_per) for c in range(n_chunks)]
        kd, mr = MXU_DEPTH, PEER_MXU_ROWS
        d_model = u_ref.shape[1]
        n_pieces = (PEER_EB // mr) * (d_model // kd) + (d_model // mr) * (PEER_EB // kd)
        per = len(units) // n_pieces
        done = 0
        for r0 in range(0, PEER_EB, mr):
            acc = None
            for k0 in range(0, d_model, kd):
                part = _dot(u_ref[r0:r0 + mr, k0:k0 + kd], h2t_ref[k0:k0 + kd, :])
                acc = part if acc is None else acc + part
                for ii, c in units[done:done + per]:
                    gate(ii, c)
                done += per
            a_new[r0:r0 + mr, :] = acc
        for r0 in range(0, d_model, mr):
            acc = ft_ref[r0:r0 + mr, :]
            for k0 in range(0, PEER_EB, kd):
                acc = acc + _dot(vt_ref[r0:r0 + mr, k0:k0 + kd], w_prev[k0:k0 + kd, :])
                for ii, c in units[done:done + per]:
                    gate(ii, c)
                done += per
            ft_ref[r0:r0 + mr, :] = acc
        for ii, c in units[done:]:
            gate(ii, c)

    @pl.when(s % 2 == 0)
    def _even():
        step(a0_ref, a1_ref, w1_ref, w0_ref)

    @pl.when(s % 2 == 1)
    def _odd():
        step(a1_ref, a0_ref, w0_ref, w1_ref)

    @pl.when(s == n_eb + 1)
    def _fin():
        x = x_ref[...] + mod_ref[0, 5:6, :] * ft_ref[...].T
        if final:
            x = _rms(x, fg_ref[...])
        o_ref[...] = x


def _peer(h2, x, mod, mod_row, wqt, keys, u, vt, final_g=None):
    n, d = x.shape
    n_exp = u.shape[0]
    n_eb = n_exp // PEER_EB
    tm = PEER_TM
    n_chunks = tm // LANE
    final = final_g is not None
    once = pl.Buffered(1)
    args = [h2, x, mod, wqt, keys, u, vt]
    in_specs = [pl.BlockSpec((tm, d), lambda i, e: (i, 0), pipeline_mode=once),
                pl.BlockSpec((tm, d), lambda i, e: (i, 0), pipeline_mode=once),
                pl.BlockSpec((1, N_MOD, d), lambda i, e: (mod_row(i), 0, 0)),
                pl.BlockSpec(wqt.shape, lambda i, e: (0, 0), pipeline_mode=once),
                pl.BlockSpec(keys.shape, lambda i, e: (0, 0, 0)),
                pl.BlockSpec((PEER_EB, d), lambda i, e: (jnp.minimum(e, n_eb - 1), 0)),
                pl.BlockSpec((d, PEER_EB), lambda i, e: (0, jnp.clip(e - 2, 0, n_eb - 1)))]
    if final:
        args.append(final_g)
        in_specs.append(pl.BlockSpec((1, d), lambda i, e: (0, 0)))
    tab32 = pltpu.VMEM((n_chunks, PEER_HEADS, PEER_NKEYS, LANE), _F32)
    tab16 = pltpu.VMEM((n_chunks, PEER_HEADS, PEER_NKEYS, LANE), _MXU)
    abuf = pltpu.VMEM((PEER_EB, tm), _F32)
    wbuf = pltpu.VMEM((PEER_EB, tm), _MXU)
    return pl.pallas_call(
        functools.partial(_peer_kernel, n_eb=n_eb, final=final),
        grid=(n // tm, n_eb + 2),
        in_specs=in_specs,
        out_specs=pl.BlockSpec((tm, d), lambda i, e: (i, 0)),
        out_shape=jax.ShapeDtypeStruct((n, d), _F32),
        scratch_shapes=[pltpu.VMEM((d, tm), _MXU), tab32, tab32, tab16, tab16,
                        pltpu.VMEM((d, tm), _F32), abuf, abuf, wbuf, wbuf],
        compiler_params=_cparams(("parallel", "arbitrary")),
        name="peer_ffn",
    )(*args)


def _dup_halves(w):
    a, b = w[:, :HEAD_DIM], w[:, HEAD_DIM:]
    return jnp.concatenate([a, a, b, b], axis=1)


def _prep_ab(w_in):
    aq, ak, av = w_in[:, 0:512], w_in[:, 512:1024], w_in[:, 1024:1536]
    bq, bk, bv = w_in[:, 1536:2048], w_in[:, 2048:2176], w_in[:, 2176:2304]
    cat = [aq, ak, av, bq, _rot_cols(bq, HEAD_DIM), _dup_halves(bk), _dup_halves(_rot_cols(bk, HEAD_DIM)),
           _dup_halves(bv)]
    return jnp.concatenate(cat, axis=1).astype(_MXU)


def _prep_cd(w_in, w_uq, w_ukv):
    d = w_in.shape[0]
    cq, ckv, kr = w_in[:, 0:256], w_in[:, 256:384], w_in[:, 384:416]
    dq, dk, dv = w_in[:, 416:928], w_in[:, 928:1440], w_in[:, 1440:1952]
    z64, z32 = jnp.zeros((d, 64), _F32), jnp.zeros((d, 32), _F32)
    kr128 = jnp.concatenate([z64, kr, z32], axis=1)
    krrot128 = jnp.concatenate([z64, _rot_cols(kr, MLA_ROPE), z32], axis=1)
    w = jnp.concatenate([cq, ckv, kr128, krrot128, dq, _rot_cols(dq, HEAD_DIM), dk, _rot_cols(dk, HEAD_DIM), dv],
                        axis=1).astype(_MXU)
    r = w_uq.shape[0]
    uq = w_uq.reshape(r, 8, MLA_NOPE + MLA_ROPE)
    nope, rope = uq[:, :, :MLA_NOPE], uq[:, :, MLA_NOPE:]
    rope_rot = _rot_cols(rope.reshape(r, 8 * MLA_ROPE), MLA_ROPE).reshape(r, 8, MLA_ROPE)
    zq = jnp.zeros((r, 8, 32), _F32)
    wqa = jnp.concatenate([nope, rope, zq], axis=2).reshape(r, 1024).astype(_MXU)
    wqb = jnp.concatenate([jnp.zeros_like(nope), rope_rot, zq], axis=2).reshape(r, 1024).astype(_MXU)
    rk = w_ukv.shape[0]
    ukv = w_ukv.reshape(rk, 8, 128)
    wk = jnp.concatenate([ukv[:, :, :MLA_NOPE], jnp.zeros((rk, 8, 64), _F32)], axis=2).reshape(rk, 1024).astype(_MXU)
    wv = ukv[:, :, MLA_NOPE:].reshape(rk, 512).astype(_MXU)
    lane = jnp.arange(LANE)
    src = (lane >= MLA_NOPE) & (lane < MLA_NOPE + MLA_ROPE)
    place = (src[:, None] & (lane[:, None] == (jnp.arange(1024)[None, :] % LANE))).astype(_MXU)
    return w, wqa, wqb, wk, wv, place


def kernel(x, c, ctx, c_ctx, ada_w, ada_b, norm1_g, norm2_g, w_out, peer_wq, peer_keys, peer_u, peer_v,
           ab_w_in, na_rpb, swa_sink, cd_w_in, mla_q_norm_g, mla_w_uq, mla_kv_norm_g, mla_w_ukv,
           diff_lambda, diff_subln_g, final_norm_g):
    batch, seq, d = x.shape
    ctx_len = ctx.shape[1]
    depth = ada_w.shape[0]
    assert seq % TM == 0 and (batch * ctx_len) % TM == 0 and seq % TQ_LOCAL == 0
    assert seq % PEER_TM == 0 and (batch * ctx_len) % PEER_TM == 0
    assert depth == 2, "even layers keep a context stream, the single odd layer is the last one"
    n_lat, n_ctx = batch * seq, batch * ctx_len
    xs = x.reshape(n_lat, d)
    cs = ctx.reshape(n_ctx, d)

    mod_rows = -(-(batch + 1) // 16) * 16
    cc = jnp.zeros((mod_rows, d), _F32).at[:batch].set(c).at[batch].set(c_ctx)
    mod_all = _modulation(cc, ada_w, ada_b)

    tiles_per_seq = seq // TM
    lat_row = lambda i: i // tiles_per_seq
    ctx_row = lambda i: batch
    lat_tab = lambda i: i % tiles_per_seq
    ctx_tab = lambda i: tiles_per_seq

    cos64, sin64 = _rope_tables(seq, HEAD_DIM)
    cos64p, sin64p = _pad_table(cos64, LANE, 1.0), _pad_table(sin64, LANE, 0.0)

    for l in range(depth):
        last = l == depth - 1
        j = l // 2
        mod = mod_all[l, :batch + 1].reshape(batch + 1, N_MOD, d)
        g1 = norm1_g[l].reshape(1, d)
        g2 = norm2_g[l].reshape(1, d)
        wo = w_out[l].astype(_MXU)
        wqt = peer_wq[l].T.astype(_MXU)
        keys = peer_keys[l].astype(_MXU)
        u = peer_u[l].astype(_MXU)
        vt = peer_v[l].T.astype(_MXU)
        if l % 2 == 0:
            w = _prep_ab(ab_w_in[j])
            aq, ak, av, bq, bk2, bv2 = _proj_ab(xs, mod, lat_row, g1, cos64p, sin64p, lat_tab, w)
            caq, cak, cav, cbq, cbk2, cbv2 = _proj_ab(cs, mod, ctx_row, g1, cos64p, sin64p, ctx_tab, w)
            bias = _na_bias_tables(na_rpb[j], seq // GRID_W)
            ya = _na_attention(aq, ak, av, cak, cav, bias, batch, seq, ctx_len)
            sink = swa_sink[j].astype(_F32)
            yb = _swa_attention(sink, bq, bk2, bv2, cbk2, cbv2, batch, seq, ctx_len)
            xs, h2 = _out_proj(xs, ya, yb, mod, lat_row, g2, wo[:512], wo[512:])
            if not last:
                pair_heads = [(m, half, m, m) for m in range(4) for half in range(2)]
                pair_outs = [("pair", 2 * m, 2 * m + 1) for m in range(4)]
                yca = _flash_attention(caq, cak, cav, heads=pair_heads, outs=pair_outs, batch=batch,
                                       q_per_batch=ctx_len, kv_per_batch=ctx_len, tq=ctx_len, tk=ctx_len,
                                       name="ctx_attn_a")
                gqa_heads = [(m, half, m // 2, m // 2) for m in range(4) for half in range(2)]
                ycb = _flash_attention(cbq, cbk2, cbv2, heads=gqa_heads, outs=pair_outs, batch=batch,
                                       q_per_batch=ctx_len, kv_per_batch=ctx_len, tq=ctx_len, tk=ctx_len,
                                       sink=sink, name="ctx_attn_b")
                cs, h2c = _out_proj(cs, yca, ycb, mod, ctx_row, g2, wo[:512], wo[512:])
        else:
            lam_init = 0.8 - 0.6 * math.exp(-0.3 * l)
            w, wqa, wqb, wk, wv, place = _prep_cd(cd_w_in[j], mla_w_uq[j], mla_w_ukv[j])
            cos32, sin32 = _rope_tables(seq, MLA_ROPE)
            ones64, zeros64 = jnp.ones((seq, 64), _F32), jnp.zeros((seq, 64), _F32)
            cosq = _pad_table(jnp.concatenate([ones64, cos32, ones64[:, :32]], 1), LANE, 1.0)
            sinq = _pad_table(jnp.concatenate([zeros64, sin32, zeros64[:, :32]], 1), LANE, 0.0)
            tabs = (cosq, sinq, cos64p, sin64p)
            qg = mla_q_norm_g[j].reshape(1, -1)
            kvg = mla_kv_norm_g[j].reshape(1, -1)
            qmt, km, vmt, dqt, dk, dvt = _proj_cd(xs, mod, lat_row, g1, tabs, lat_tab, w, qg, kvg, wqa, wqb, wk, wv, place)
            _, ckm, cvmt, _, cdk, cdvt = _proj_cd(cs, mod, ctx_row, g1, tabs, ctx_tab, w, qg, kvg, wqa, wqb, wk, wv, place)
            mla_heads = [(h, None, h, h // 2) for h in range(8)]
            pair_outs = [("pair", 2 * m, 2 * m + 1) for m in range(4)]
            yc = _flasht_attention(qmt, km, vmt, ckm, cvmt, heads=mla_heads, outs=pair_outs, batch=batch, seq=seq,
                                   ctx_len=ctx_len, tq=GLOBAL_TQ, tk=GLOBAL_TK, name="mla_attn")
            diff_heads = [(h, a, h, h) for h in range(4) for a in range(2)]
            diff_outs = [("diff", 2 * h, 2 * h + 1) for h in range(4)]
            od = _flasht_attention(dqt, dk, dvt, cdk, cdvt, heads=diff_heads, outs=diff_outs, batch=batch, seq=seq,
                                   ctx_len=ctx_len, tq=GLOBAL_TQ, tk=GLOBAL_TK,
                                   diff=(diff_lambda[j].astype(_F32), diff_subln_g[j].reshape(1, -1), lam_init),
                                   name="diff_attn")
            xs, h2 = _out_proj(xs, yc, od, mod, lat_row, g2, wo[:512], wo[512:])
        fg = final_norm_g.reshape(1, d) if last else None
        xs = _peer(h2, xs, mod, lambda i: i // (seq // PEER_TM), wqt, keys, u, vt, final_g=fg)
        if not last:
            cs = _peer(h2c, cs, mod, ctx_row, wqt, keys, u, vt)
    return xs.reshape(batch, seq, d)
```

```python
import functools
import math

import jax
import jax.numpy as jnp
from jax import lax
from jax.experimental import pallas as pl
from jax.experimental.pallas import tpu as pltpu

_F32 = jnp.float32
_MXU = jnp.bfloat16
_NEG = -1e30
_LOG2E = math.log2(math.e)

GRID_W = 64
HEAD_DIM = 64
ROPE_BASE = 10000.0
RMS_EPS = 1e-6
N_MOD = 6
NA_KR, NA_KC = 8, 16
SWA_WINDOW = 128
MLA_NOPE, MLA_ROPE = 64, 32
PEER_HEADS, PEER_NKEYS, PEER_TOPK = 8, 128, 16

LANE = 128
TM = 512
TQ_LOCAL = 256
NA_GROUP_ROWS = TQ_LOCAL // GRID_W
NA_WIN_ROWS = NA_KR + NA_GROUP_ROWS - 1
GLOBAL_TQ = 512
GLOBAL_TK = 512
PEER_TM = 1024
PEER_EB = 512
MXU_DEPTH = 256
PEER_MXU_ROWS = 512
PEER_SELECT_UNROLL = 8
VMEM_LIMIT = 56 * 1024 * 1024


def _cparams(sem, vmem=VMEM_LIMIT):
    return pltpu.CompilerParams(dimension_semantics=sem, vmem_limit_bytes=vmem)


def _dot(a, b):
    return jnp.dot(a, b, preferred_element_type=_F32)


def _dot_nt(a, b):
    return lax.dot_general(a, b, (((1,), (1,)), ((), ())), preferred_element_type=_F32)


def _rms(x, g):
    return x * lax.rsqrt(jnp.mean(x * x, axis=-1, keepdims=True) + RMS_EPS) * g


def _lane_half(shape):
    return lax.broadcasted_iota(jnp.int32, shape, len(shape) - 1) >= (LANE // 2)


def _mod_kernel(c_ref, w_ref, b_ref, o_ref):
    c = c_ref[...]
    a = c * jax.nn.sigmoid(c)
    w = w_ref[0]
    a_hi = a.astype(_MXU)
    a_lo = (a - a_hi.astype(_F32)).astype(_MXU)
    w_hi = w.astype(_MXU)
    w_lo = (w - w_hi.astype(_F32)).astype(_MXU)
    o_ref[0] = _dot(a_hi, w_hi) + _dot(a_lo, w_hi) + _dot(a_hi, w_lo) + b_ref[0]


def _modulation(cc, ada_w, ada_b):
    depth, d, n = ada_w.shape
    rows = cc.shape[0]
    tn = 768
    return pl.pallas_call(
        _mod_kernel,
        grid=(depth, n // tn),
        in_specs=[pl.BlockSpec((rows, d), lambda l, j: (0, 0)),
                  pl.BlockSpec((1, d, tn), lambda l, j: (l, 0, j)),
                  pl.BlockSpec((1, 1, tn), lambda l, j: (l, 0, j))],
        out_specs=pl.BlockSpec((1, rows, tn), lambda l, j: (l, 0, j)),
        out_shape=jax.ShapeDtypeStruct((depth, rows, n), _F32),
        compiler_params=_cparams(("parallel", "parallel")),
        name="adaln_mod",
    )(cc, ada_w, ada_b.reshape(depth, 1, n))


def _rope_tables(seq, d):
    t = jnp.arange(seq, dtype=jnp.int32)
    q = d // 4
    freq = ROPE_BASE ** (-jnp.arange(q, dtype=_F32) / q)

    def one(pos):
        ang = pos.astype(_F32)[:, None] * freq[None, :]
        return jnp.concatenate([jnp.cos(ang)] * 2, -1), jnp.concatenate([jnp.sin(ang)] * 2, -1)

    cr, sr = one(t // GRID_W)
    cc, sc = one(t % GRID_W)
    return jnp.concatenate([cr, cc], -1), jnp.concatenate([sr, sc], -1)


def _rot_cols(w, d):
    k, n = w.shape
    w5 = w.reshape(k, n // d, 2, 2, d // 4)
    return jnp.stack([-w5[:, :, :, 1], w5[:, :, :, 0]], axis=3).reshape(k, n)


def _pad_table(tab, width, ident):
    s, w = tab.shape
    if w < width:
        reps = width // w
        tab = jnp.tile(tab, (1, reps))
    return jnp.concatenate([tab, jnp.full((TM, width), ident, _F32)], axis=0)


def _proj_ab_kernel(x_ref, mod_ref, g_ref, cos_ref, sin_ref, w_ref,
                    aq_ref, ak_ref, av_ref, bq_ref, bk_ref, bv_ref):
    x = x_ref[...]
    h = _rms(x, g_ref[...]) * (1.0 + mod_ref[0, 1:2, :]) + mod_ref[0, 0:1, :]
    hb = h.astype(_MXU)
    cos = cos_ref[...]
    sin = sin_ref[...]

    def proj(lo, n):
        return _dot(hb, w_ref[:, lo:lo + n])

    scale = HEAD_DIM ** -0.5
    aq_ref[...] = (proj(0, 512) * scale).astype(aq_ref.dtype)
    ak_ref[...] = proj(512, 512).astype(ak_ref.dtype)
    av_ref[...] = proj(1024, 512).astype(av_ref.dtype)
    cos4 = jnp.tile(cos, (1, 4))
    sin4 = jnp.tile(sin, (1, 4))
    bq_ref[...] = ((proj(1536, 512) * cos4 + proj(2048, 512) * sin4) * scale).astype(bq_ref.dtype)
    cos2 = jnp.tile(cos, (1, 2))
    sin2 = jnp.tile(sin, (1, 2))
    bk_ref[...] = (proj(2560, 256) * cos2 + proj(2816, 256) * sin2).astype(bk_ref.dtype)
    bv_ref[...] = proj(3072, 256).astype(bv_ref.dtype)


def _proj_ab(x, mod, mod_row, g, cos, sin, tab_row, w):
    n, d = x.shape
    widths = (512, 512, 512, 512, 256, 256)
    return pl.pallas_call(
        _proj_ab_kernel,
        grid=(n // TM,),
        in_specs=[pl.BlockSpec((TM, d), lambda i: (i, 0)),
                  pl.BlockSpec((1, N_MOD, d), lambda i: (mod_row(i), 0, 0)),
                  pl.BlockSpec((1, d), lambda i: (0, 0)),
                  pl.BlockSpec((TM, LANE), lambda i: (tab_row(i), 0)),
                  pl.BlockSpec((TM, LANE), lambda i: (tab_row(i), 0)),
                  pl.BlockSpec(w.shape, lambda i: (0, 0))],
        out_specs=[pl.BlockSpec((TM, wd), lambda i: (i, 0)) for wd in widths],
        out_shape=[jax.ShapeDtypeStruct((n, wd), _MXU) for wd in widths],
        compiler_params=_cparams(("parallel",)),
        name="proj_ab",
    )(x, mod, g, cos, sin, w)


def _na_kernel(q_ref, k_ref, v_ref, kc_ref, vc_ref, bias_ref, o_ref, *, rows):
    g = pl.program_id(1)
    ks = jnp.clip(NA_GROUP_ROWS * g - NA_KR // 2, 0, rows - NA_WIN_ROWS)
    start = pl.multiple_of(ks * GRID_W, GRID_W)
    nwin = NA_WIN_ROWS * GRID_W
    hi = _lane_half((TQ_LOCAL, LANE))
    for m in range(4):
        cols = slice(m * LANE, (m + 1) * LANE)
        q2 = q_ref[:, cols]
        k2 = k_ref[pl.ds(start, nwin), cols]
        v2 = v_ref[pl.ds(start, nwin), cols]
        kc2 = kc_ref[:, cols]
        vc2 = vc_ref[:, cols]
        outs = []
        for half in range(2):
            qh = jnp.where(hi == (half == 1), q2, jnp.zeros_like(q2))
            s_lat = _dot_nt(qh, k2) + bias_ref[0, 2 * m + half]
            s_ctx = _dot_nt(qh, kc2)
            mx = jnp.maximum(jnp.max(s_lat, axis=-1, keepdims=True), jnp.max(s_ctx, axis=-1, keepdims=True))
            p_lat = jnp.exp(s_lat - mx)
            p_ctx = jnp.exp(s_ctx - mx)
            den = jnp.sum(p_lat, axis=-1, keepdims=True) + jnp.sum(p_ctx, axis=-1, keepdims=True)
            o = _dot(p_lat.astype(_MXU), v2) + _dot(p_ctx.astype(_MXU), vc2)
            outs.append(o / den)
        o_ref[:, cols] = jnp.where(hi, outs[1], outs[0]).astype(o_ref.dtype)


def _na_bias_tables(rpb, rows):
    gq = NA_GROUP_ROWS
    n_groups = rows // gq
    ql = jnp.arange(gq)[:, None]
    kl = jnp.arange(NA_WIN_ROWS)[None, :]
    qc = jnp.arange(GRID_W)[:, None]
    kc = jnp.arange(GRID_W)[None, :]
    cs = jnp.clip(qc - NA_KC // 2, 0, GRID_W - NA_KC)
    col_valid = (kc >= cs) & (kc < cs + NA_KC)
    col_idx = jnp.clip(kc - qc + NA_KC - 1, 0, 2 * NA_KC - 2)
    exact = lax.Precision.HIGHEST
    oh_c = jax.nn.one_hot(col_idx, 2 * NA_KC - 1, dtype=_F32)
    rpb_cols = jnp.einsum("hab,xyb->haxy", rpb.astype(_F32), oh_c, precision=exact)
    tabs = []
    for g in (0, 1, n_groups - 1):
        ks = min(max(gq * g - NA_KR // 2, 0), rows - NA_WIN_ROWS)
        rq = gq * g + ql
        rk = ks + kl
        r0 = jnp.clip(rq - NA_KR // 2, 0, rows - NA_KR)
        row_valid = (rk >= r0) & (rk < r0 + NA_KR)
        row_idx = jnp.clip(rk - rq + NA_KR - 1, 0, 2 * NA_KR - 2)
        oh_r = jax.nn.one_hot(row_idx, 2 * NA_KR - 1, dtype=_F32)
        b = jnp.einsum("qka,haxy->hqxky", oh_r, rpb_cols, precision=exact)
        valid = row_valid[:, None, :, None] & col_valid[None, :, None, :]
        b = jnp.where(valid[None], b, _NEG)
        tabs.append(b.reshape(rpb.shape[0], gq * GRID_W, NA_WIN_ROWS * GRID_W))
    return jnp.stack(tabs)


def _na_attention(aq, ak, av, cak, cav, bias, batch, seq, ctx_len):
    rows = seq // GRID_W
    n_groups = seq // TQ_LOCAL
    nwin = NA_WIN_ROWS * GRID_W

    def bias_row(b, g):
        return (jnp.where(g == 0, 0, jnp.where(g == n_groups - 1, 2, 1)), 0, 0, 0)

    return pl.pallas_call(
        functools.partial(_na_kernel, rows=rows),
        grid=(batch, n_groups),
        in_specs=[pl.BlockSpec((TQ_LOCAL, 512), lambda b, g: (b * n_groups + g, 0)),
                  pl.BlockSpec((seq, 512), lambda b, g: (b, 0)),
                  pl.BlockSpec((seq, 512), lambda b, g: (b, 0)),
                  pl.BlockSpec((ctx_len, 512), lambda b, g: (b, 0)),
                  pl.BlockSpec((ctx_len, 512), lambda b, g: (b, 0)),
                  pl.BlockSpec((1, 8, TQ_LOCAL, nwin), bias_row)],
        out_specs=pl.BlockSpec((TQ_LOCAL, 512), lambda b, g: (b * n_groups + g, 0)),
        out_shape=jax.ShapeDtypeStruct((batch * seq, 512), _MXU),
        compiler_params=_cparams(("parallel", "arbitrary")),
        name="na_attention",
    )(aq, ak, av, cak, cav, bias)


def _swa_kernel(sink_ref, q_ref, k_ref, v_ref, kc_ref, vc_ref, o_ref, *, seq):
    t = pl.program_id(1)
    kwin = TQ_LOCAL + 2 * SWA_WINDOW
    start = t * TQ_LOCAL
    kstart = pl.multiple_of(jnp.clip(start - SWA_WINDOW, 0, seq - kwin), LANE)
    qpos = start + lax.broadcasted_iota(jnp.int32, (TQ_LOCAL, kwin), 0)
    kpos = kstart + lax.broadcasted_iota(jnp.int32, (TQ_LOCAL, kwin), 1)
    mask = jnp.where(jnp.abs(kpos - qpos) <= SWA_WINDOW, 0.0, _NEG).astype(_F32)
    hi = _lane_half((TQ_LOCAL, LANE))
    for m in range(4):
        cols = slice(m * LANE, (m + 1) * LANE)
        kv = m // 2
        kcols = slice(kv * LANE, (kv + 1) * LANE)
        q2 = q_ref[:, cols]
        k2 = k_ref[pl.ds(kstart, kwin), kcols]
        v2 = v_ref[pl.ds(kstart, kwin), kcols]
        kc2 = kc_ref[:, kcols]
        vc2 = vc_ref[:, kcols]
        outs = []
        for half in range(2):
            sink = sink_ref[2 * m + half]
            qh = jnp.where(hi == (half == 1), q2, jnp.zeros_like(q2))
            s_lat = _dot_nt(qh, k2) + mask
            s_ctx = _dot_nt(qh, kc2)
            mx = jnp.maximum(jnp.max(s_lat, axis=-1, keepdims=True), jnp.max(s_ctx, axis=-1, keepdims=True))
            mx = jnp.maximum(mx, sink)
            p_lat = jnp.exp(s_lat - mx)
            p_ctx = jnp.exp(s_ctx - mx)
            den = (jnp.sum(p_lat, axis=-1, keepdims=True) + jnp.sum(p_ctx, axis=-1, keepdims=True)
                   + jnp.exp(sink - mx))
            o = _dot(p_lat.astype(_MXU), v2) + _dot(p_ctx.astype(_MXU), vc2)
            outs.append(o / den)
        o_ref[:, cols] = jnp.where(hi, outs[1], outs[0]).astype(o_ref.dtype)


def _swa_attention(sink, bq, bk2, bv2, cbk2, cbv2, batch, seq, ctx_len):
    n_t = seq // TQ_LOCAL
    return pl.pallas_call(
        functools.partial(_swa_kernel, seq=seq),
        grid=(batch, n_t),
        in_specs=[pl.BlockSpec(memory_space=pltpu.SMEM),
                  pl.BlockSpec((TQ_LOCAL, 512), lambda b, t: (b * n_t + t, 0)),
                  pl.BlockSpec((seq, 256), lambda b, t: (b, 0)),
                  pl.BlockSpec((seq, 256), lambda b, t: (b, 0)),
                  pl.BlockSpec((ctx_len, 256), lambda b, t: (b, 0)),
                  pl.BlockSpec((ctx_len, 256), lambda b, t: (b, 0))],
        out_specs=pl.BlockSpec((TQ_LOCAL, 512), lambda b, t: (b * n_t + t, 0)),
        out_shape=jax.ShapeDtypeStruct((batch * seq, 512), _MXU),
        compiler_params=_cparams(("parallel", "arbitrary")),
        name="swa_attention",
    )(sink, bq, bk2, bv2, cbk2, cbv2)


def _flash_kernel(*refs, heads, outs, has_ctx, has_sink, diff_cfg, n_kv):
    it = iter(refs)
    sink_ref = next(it) if has_sink else None
    q_ref, k_ref, v_ref = next(it), next(it), next(it)
    kc_ref = next(it) if has_ctx else None
    vc_ref = next(it) if has_ctx else None
    lam_ref = next(it) if diff_cfg else None
    sg_ref = next(it) if diff_cfg else None
    o_ref = next(it)
    m_ref, l_ref, acc_ref = next(it), next(it), next(it)
    kv = pl.program_id(2)
    tq = q_ref.shape[0]
    hi = _lane_half((tq, LANE))

    @pl.when(kv == 0)
    def _init():
        for h in range(len(heads)):
            if has_sink:
                m_ref[h] = jnp.full((tq, LANE), sink_ref[h], _F32)
                l_ref[h] = jnp.ones((tq, LANE), _F32)
            else:
                m_ref[h] = jnp.full((tq, LANE), _NEG, _F32)
                l_ref[h] = jnp.zeros((tq, LANE), _F32)
            acc_ref[h] = jnp.zeros((tq, LANE), _F32)

    def attend(kr, vr):
        for h, (qb, qhalf, kb, vb) in enumerate(heads):
            q2 = q_ref[:, qb * LANE:(qb + 1) * LANE]
            if qhalf is not None:
                q2 = jnp.where(hi == (qhalf == 1), q2, jnp.zeros_like(q2))
            s = _dot_nt(q2, kr[:, kb * LANE:(kb + 1) * LANE])
            m_old = m_ref[h][:, :1]
            m_new = jnp.maximum(m_old, jnp.max(s, axis=-1, keepdims=True))
            alpha = jnp.exp(m_old - m_new)
            p = jnp.exp(s - m_new)
            l_ref[h] = jnp.broadcast_to(alpha * l_ref[h][:, :1] + jnp.sum(p, axis=-1, keepdims=True), (tq, LANE))
            acc_ref[h] = alpha * acc_ref[h] + _dot(p.astype(_MXU), vr[:, vb * LANE:(vb + 1) * LANE])
            m_ref[h] = jnp.broadcast_to(m_new, (tq, LANE))

    if has_ctx:
        @pl.when(kv == 0)
        def _ctx():
            attend(kc_ref, vc_ref)

    attend(k_ref, v_ref)

    @pl.when(kv == n_kv - 1)
    def _fin():
        def head_out(h):
            return acc_ref[h] / l_ref[h][:, :1]

        for j, spec in enumerate(outs):
            if spec[0] == "full":
                o = head_out(spec[1])
            elif spec[0] == "pair":
                o = jnp.where(hi, head_out(spec[2]), head_out(spec[1]))
            else:
                lv = lam_ref[...]
                lam = (jnp.exp(jnp.sum(lv[0:1] * lv[1:2], axis=-1, keepdims=True))
                       - jnp.exp(jnp.sum(lv[2:3] * lv[3:4], axis=-1, keepdims=True)) + diff_cfg)
                o = _rms(head_out(spec[1]) - lam * head_out(spec[2]), sg_ref[...]) * (1.0 - diff_cfg)
            o_ref[:, j * LANE:(j + 1) * LANE] = o.astype(o_ref.dtype)


def _flash_attention(q, k, v, *, heads, outs, batch, q_per_batch, kv_per_batch, tq, tk,
                     q_row0=0, kv_row0=0, ctx=None, ctx_len=0, ctx_row0=0, sink=None, diff=None, name="flash"):
    n_q = q_per_batch // tq
    n_kv = kv_per_batch // tk
    qw, kw, vw = q.shape[1], k.shape[1], v.shape[1]
    args, in_specs = [], []
    if sink is not None:
        args.append(sink)
        in_specs.append(pl.BlockSpec(memory_space=pltpu.SMEM))
    q0, k0 = q_row0 // tq, kv_row0 // tk
    args += [q, k, v]
    in_specs += [pl.BlockSpec((tq, qw), lambda b, i, j: (q0 + b * n_q + i, 0)),
                 pl.BlockSpec((tk, kw), lambda b, i, j: (k0 + b * n_kv + j, 0)),
                 pl.BlockSpec((tk, vw), lambda b, i, j: (k0 + b * n_kv + j, 0))]
    if ctx is not None:
        c0 = ctx_row0 // ctx_len
        args += [ctx[0], ctx[1]]
        in_specs += [pl.BlockSpec((ctx_len, kw), lambda b, i, j: (c0 + b, 0)),
                     pl.BlockSpec((ctx_len, vw), lambda b, i, j: (c0 + b, 0))]
    diff_cfg = None
    if diff is not None:
        lam_vecs, subln_g, diff_cfg = diff
        args += [lam_vecs, subln_g]
        in_specs += [pl.BlockSpec(lam_vecs.shape, lambda b, i, j: (0, 0)),
                     pl.BlockSpec(subln_g.shape, lambda b, i, j: (0, 0))]
    nh = len(heads)
    ow = len(outs) * LANE
    return pl.pallas_call(
        functools.partial(_flash_kernel, heads=tuple(heads), outs=tuple(outs), has_ctx=ctx is not None,
                          has_sink=sink is not None, diff_cfg=diff_cfg, n_kv=n_kv),
        grid=(batch, n_q, n_kv),
        in_specs=in_specs,
        out_specs=pl.BlockSpec((tq, ow), lambda b, i, j: (b * n_q + i, 0)),
        out_shape=jax.ShapeDtypeStruct((batch * q_per_batch, ow), _MXU),
        scratch_shapes=[pltpu.VMEM((nh, tq, LANE), _F32)] * 3,
        compiler_params=_cparams(("parallel", "parallel", "arbitrary")),
        name=name,
    )(*args)


def _flasht_kernel(*refs, heads, outs, diff_cfg, n_kv):
    it = iter(refs)
    qt_ref, k_ref, vt_ref, kc_ref, vct_ref = next(it), next(it), next(it), next(it), next(it)
    lam_ref = next(it) if diff_cfg else None
    sg_ref = next(it) if diff_cfg else None
    o_ref = next(it)
    m_ref, l_ref, acc_ref = next(it), next(it), next(it)
    kv = pl.program_id(2)
    tq = qt_ref.shape[1]
    row_hi = lax.broadcasted_iota(jnp.int32, (LANE, tq), 0) >= (LANE // 2)

    @pl.when(kv == 0)
    def _init():
        m_ref[...] = jnp.full(m_ref.shape, _NEG, _F32)
        l_ref[...] = jnp.zeros(l_ref.shape, _F32)
        acc_ref[...] = jnp.zeros(acc_ref.shape, _F32)

    def attend(kr, vtr):
        def scores(h):
            qb, qhalf, kb, _ = heads[h]
            qt = qt_ref[qb * LANE:(qb + 1) * LANE, :]
            if qhalf is not None:
                qt = jnp.where(row_hi == (qhalf == 1), qt, jnp.zeros_like(qt))
            return _dot(kr[:, kb * LANE:(kb + 1) * LANE], qt)

        s_next = scores(0)
        for h in range(len(heads)):
            s = s_next
            if h + 1 < len(heads):
                s_next = scores(h + 1)
            vb = heads[h][3]
            m_old = m_ref[h, 0:1, :]
            m_new = jnp.maximum(m_old, jnp.max(s, axis=0, keepdims=True))
            alpha = jnp.exp2(m_old - m_new)
            p = jnp.exp2(s - m_new)
            l_new = alpha * l_ref[h, 0:1, :] + jnp.sum(p, axis=0, keepdims=True)
            acc_ref[h] = alpha * acc_ref[h] + _dot(vtr[vb * LANE:(vb + 1) * LANE, :], p.astype(_MXU))
            l_ref[h] = jnp.broadcast_to(l_new, (8, tq))
            m_ref[h] = jnp.broadcast_to(m_new, (8, tq))

    @pl.when(kv == 0)
    def _ctx():
        attend(kc_ref, vct_ref)

    attend(k_ref, vt_ref)

    @pl.when(kv == n_kv - 1)
    def _fin():
        def head_out(h):
            return acc_ref[h] / l_ref[h, 0:1, :]

        for j, spec in enumerate(outs):
            if spec[0] == "full":
                o = head_out(spec[1]).T
            elif spec[0] == "pair":
                o = jnp.where(row_hi, head_out(spec[2]), head_out(spec[1])).T
            else:
                lv = lam_ref[...]
                lam = (jnp.exp(jnp.sum(lv[0:1] * lv[1:2], axis=-1, keepdims=True))
                       - jnp.exp(jnp.sum(lv[2:3] * lv[3:4], axis=-1, keepdims=True)) + diff_cfg)
                o = _rms((head_out(spec[1]) - lam * head_out(spec[2])).T, sg_ref[...]) * (1.0 - diff_cfg)
            o_ref[:, j * LANE:(j + 1) * LANE] = o.astype(o_ref.dtype)


def _flasht_attention(qt, k, vt, kc, vct, *, heads, outs, batch, seq, ctx_len, tq, tk, diff=None, name="flasht"):
    n_q, n_kv = seq // tq, seq // tk
    qw, kw, vw = qt.shape[0], k.shape[1], vt.shape[0]
    args = [qt, k, vt, kc, vct]
    in_specs = [pl.BlockSpec((qw, tq), lambda b, i, j: (0, b * n_q + i)),
                pl.BlockSpec((tk, kw), lambda b, i, j: (b * n_kv + j, 0)),
                pl.BlockSpec((vw, tk), lambda b, i, j: (0, b * n_kv + j)),
                pl.BlockSpec((ctx_len, kw), lambda b, i, j: (b, 0)),
                pl.BlockSpec((vw, ctx_len), lambda b, i, j: (0, b))]
    diff_cfg = None
    if diff is not None:
        lam_vecs, subln_g, diff_cfg = diff
        args += [lam_vecs, subln_g]
        in_specs += [pl.BlockSpec(lam_vecs.shape, lambda b, i, j: (0, 0)),
                     pl.BlockSpec(subln_g.shape, lambda b, i, j: (0, 0))]
    nh = len(heads)
    ow = len(outs) * LANE
    return pl.pallas_call(
        functools.partial(_flasht_kernel, heads=tuple(heads), outs=tuple(outs), diff_cfg=diff_cfg, n_kv=n_kv),
        grid=(batch, n_q, n_kv),
        in_specs=in_specs,
        out_specs=pl.BlockSpec((tq, ow), lambda b, i, j: (b * n_q + i, 0)),
        out_shape=jax.ShapeDtypeStruct((batch * seq, ow), _MXU),
        scratch_shapes=[pltpu.VMEM((nh, 8, tq), _F32), pltpu.VMEM((nh, 8, tq), _F32),
                        pltpu.VMEM((nh, LANE, tq), _F32)],
        compiler_params=_cparams(("parallel", "parallel", "arbitrary")),
        name=name,
    )(*args)


def _out_kernel(x_ref, ya_ref, yb_ref, mod_ref, g2_ref, wa_ref, wb_ref, xo_ref, h2_ref):
    y = _dot(ya_ref[...], wa_ref[...]) + _dot(yb_ref[...], wb_ref[...])
    x = x_ref[...] + mod_ref[0, 2:3, :] * y
    xo_ref[...] = x
    h2 = _rms(x, g2_ref[...]) * (1.0 + mod_ref[0, 4:5, :]) + mod_ref[0, 3:4, :]
    h2_ref[...] = h2.astype(h2_ref.dtype)


def _out_proj(x, ya, yb, mod, mod_row, g2, wa, wb):
    n, d = x.shape
    return pl.pallas_call(
        _out_kernel,
        grid=(n // TM,),
        in_specs=[pl.BlockSpec((TM, d), lambda i: (i, 0)),
                  pl.BlockSpec((TM, ya.shape[1]), lambda i: (i, 0)),
                  pl.BlockSpec((TM, yb.shape[1]), lambda i: (i, 0)),
                  pl.BlockSpec((1, N_MOD, d), lambda i: (mod_row(i), 0, 0)),
                  pl.BlockSpec((1, d), lambda i: (0, 0)),
                  pl.BlockSpec(wa.shape, lambda i: (0, 0)),
                  pl.BlockSpec(wb.shape, lambda i: (0, 0))],
        out_specs=[pl.BlockSpec((TM, d), lambda i: (i, 0)), pl.BlockSpec((TM, d), lambda i: (i, 0))],
        out_shape=[jax.ShapeDtypeStruct((n, d), _F32), jax.ShapeDtypeStruct((n, d), _MXU)],
        compiler_params=_cparams(("parallel",)),
        name="out_proj",
    )(x, ya, yb, mod, g2, wa, wb)


def _proj_cd_kernel(x_ref, mod_ref, g_ref, cq_ref, sq_ref, cd_ref, sd_ref, w_ref, qg_ref, kvg_ref,
                    wqa_ref, wqb_ref, wk_ref, wv_ref, place_ref,
                    qm_ref, km_ref, vm_ref, dq_ref, dk_ref, dv_ref):
    x = x_ref[...]
    h = _rms(x, g_ref[...]) * (1.0 + mod_ref[0, 1:2, :]) + mod_ref[0, 0:1, :]
    hb = h.astype(_MXU)

    def proj(lo, n):
        return _dot(hb, w_ref[:, lo:lo + n])

    cosq, sinq = cq_ref[...], sq_ref[...]
    cosd, sind = cd_ref[...], sd_ref[...]
    cqn = _rms(proj(0, 256), qg_ref[...]).astype(_MXU)
    ckvn = _rms(proj(256, 128), kvg_ref[...]).astype(_MXU)
    cos8, sin8 = jnp.tile(cosq, (1, 8)), jnp.tile(sinq, (1, 8))
    qm = _dot(cqn, wqa_ref[...]) * cos8 + _dot(cqn, wqb_ref[...]) * sin8
    qm_ref[...] = (qm * ((MLA_NOPE + MLA_ROPE) ** -0.5 * _LOG2E)).T.astype(qm_ref.dtype)
    kr = (proj(384, 128) * cosq + proj(512, 128) * sinq).astype(_MXU)
    km_ref[...] = (_dot(ckvn, wk_ref[...]) + _dot(kr, place_ref[...])).astype(km_ref.dtype)
    vm_ref[...] = _dot(ckvn, wv_ref[...]).T.astype(vm_ref.dtype)
    cos4, sin4 = jnp.tile(cosd, (1, 4)), jnp.tile(sind, (1, 4))
    dq_ref[...] = ((proj(640, 512) * cos4 + proj(1152, 512) * sin4) * (HEAD_DIM ** -0.5 * _LOG2E)).T.astype(dq_ref.dtype)
    dk_ref[...] = (proj(1664, 512) * cos4 + proj(2176, 512) * sin4).astype(dk_ref.dtype)
    dv_ref[...] = proj(2688, 512).T.astype(dv_ref.dtype)


def _proj_cd(x, mod, mod_row, g, tabs, tab_row, w, qg, kvg, wqa, wqb, wk, wv, place):
    n, d = x.shape
    outs = ((1024, True), (1024, False), (512, True), (512, True), (512, False), (512, True))
    full = lambda a: pl.BlockSpec(a.shape, lambda i: (0, 0))
    return pl.pallas_call(
        _proj_cd_kernel,
        grid=(n // TM,),
        in_specs=[pl.BlockSpec((TM, d), lambda i: (i, 0)),
                  pl.BlockSpec((1, N_MOD, d), lambda i: (mod_row(i), 0, 0)),
                  pl.BlockSpec((1, d), lambda i: (0, 0))]
                 + [pl.BlockSpec((TM, LANE), lambda i: (tab_row(i), 0))] * 4
                 + [full(a) for a in (w, qg, kvg, wqa, wqb, wk, wv, place)],
        out_specs=[pl.BlockSpec((wd, TM), lambda i: (0, i)) if fm else pl.BlockSpec((TM, wd), lambda i: (i, 0))
                   for wd, fm in outs],
        out_shape=[jax.ShapeDtypeStruct((wd, n) if fm else (n, wd), _MXU) for wd, fm in outs],
        compiler_params=_cparams(("parallel",)),
        name="proj_cd",
    )(x, mod, g, *tabs, w, qg, kvg, wqa, wqb, wk, wv, place)


_MARK = 2.0 ** 121


def _extract16(x, first_only):
    r = x.shape[0]
    row = lax.broadcasted_iota(jnp.int32, x.shape, 0)
    row16 = lax.broadcasted_iota(jnp.int32, (PEER_TOPK, LANE), 0)
    vals = jnp.zeros((PEER_TOPK, LANE), _F32)
    for k in range(PEER_TOPK):
        m = jnp.max(x, axis=0, keepdims=True)
        hit = x == m
        if first_only:
            hit = row == jnp.min(jnp.where(hit, row, r), axis=0, keepdims=True)
        x = jnp.where(hit, -_MARK * (32 + k), x)
        vals = jnp.where(row16 == k, m, vals)
    order = jnp.where(x < -16.0 * _MARK, x * (-1.0 / _MARK) - 32.0, float(PEER_TOPK))
    return order, vals


def _staircase_rows():
    groups = []
    for r in range(PEER_TOPK // 2):
        n = PEER_TOPK // (r + 1)
        for c0 in range(0, n, 8):
            groups.append((r, c0, min(8, n - c0)))
    groups.append((None, 0, 8))
    return groups


def _peer_select(sa, sb, first_only):
    ra, av = _extract16(sa, first_only)
    rb, bv = _extract16(sb, first_only)
    groups = _staircase_rows()
    sub = lax.broadcasted_iota(jnp.int32, (8, LANE), 0)
    pieces = []
    for r, c0, nv in groups:
        if r is None:
            piece = av[8:16] + bv[0:1]
        else:
            piece = av[r:r + 1] + bv[c0:c0 + 8]
            if nv < 8:
                piece = jnp.where(sub < nv, piece, -jnp.inf)
        pieces.append(piece)
    cand = jnp.concatenate(pieces, axis=0)
    e_cand = jnp.exp(cand - (av[0:1] + bv[0:1]))
    sel, _ = _extract16(cand, first_only)
    sel = jnp.where(sel < float(PEER_TOPK), 1.0, 0.0)
    z = jnp.sum(sel * e_cand, axis=0, keepdims=True)
    lr = jnp.zeros(sa.shape, _F32)
    for g, (r, c0, nv) in enumerate(groups):
        blk = sel[8 * g:8 * g + 8]
        if r is None:
            for q in range(8):
                lr = jnp.where(ra == float(8 + q), blk[q:q + 1], lr)
        elif c0 == 0:
            cnt = jnp.sum(blk, axis=0, keepdims=True)
            if PEER_TOPK // (r + 1) > 8:
                cnt = cnt + jnp.sum(sel[8 * g + 8:8 * g + 16], axis=0, keepdims=True)
            lr = jnp.where(ra == float(r), cnt, lr)
    ea = jnp.exp(sa - av[0:1]) / z
    eb = jnp.exp(sb - bv[0:1])
    n_sel = (jnp.sum(jnp.where(ra < float(PEER_TOPK), 1.0, 0.0), axis=0, keepdims=True)
             + jnp.sum(jnp.where(rb < float(PEER_TOPK), 1.0, 0.0), axis=0, keepdims=True)
             + jnp.sum(sel, axis=0, keepdims=True))
    return lr, ea, rb, eb, n_sel


def _peer_kernel(h2_ref, x_ref, mod_ref, wqt_ref, keys_ref, u_ref, vt_ref, *rest, n_eb, final):
    fg_ref = rest[0] if final else None
    (o_ref, h2t_ref, lr_ref, ea_ref, rb_ref, eb_ref, ft_ref, a0_ref, a1_ref, w0_ref, w1_ref,
     h2c_ref, flag_ref) = rest[1:] if final else rest
    s = pl.program_id(1)
    n_chunks = PEER_TM // LANE
    i_per = PEER_EB // PEER_NKEYS
    gdt = rb_ref.dtype

    @pl.when(s == 0)
    def _select():
        h2t = h2_ref[...].astype(_F32).T.astype(_MXU)
        h2t_ref[...] = h2t
        ft_ref[...] = jnp.zeros_like(ft_ref)
        a1_ref[...] = jnp.zeros_like(a1_ref)
        w0_ref[...] = jnp.zeros_like(w0_ref)

        for c in range(n_chunks):
            h2c_ref[c] = h2t[:, c * LANE:(c + 1) * LANE]
        for hp in range(2 * PEER_HEADS):
            qt = _dot(wqt_ref[hp * LANE:(hp + 1) * LANE, :], h2t).astype(_MXU)
            st = _dot(keys_ref[hp % 2], qt)
            for c in range(n_chunks):
                (lr_ref if hp % 2 == 0 else ea_ref)[c, hp // 2] = st[:, c * LANE:(c + 1) * LANE]

        def unit(c, h, sa, sb, first_only):
            lr, ea, rb, eb, n_sel = _peer_select(sa, sb, first_only)
            lr_ref[c, h] = lr
            ea_ref[c, h] = ea
            rb_ref[c, h] = rb.astype(gdt)
            eb_ref[c, h] = eb.astype(gdt)
            return jnp.max(n_sel) > 3.0 * PEER_TOPK

        per_trip = PEER_SELECT_UNROLL

        def fast(t, carry):
            c, h0 = t // (PEER_HEADS // per_trip), (t % (PEER_HEADS // per_trip)) * per_trip
            for u in range(per_trip):
                tied = unit(c, h0 + u, lr_ref[c, h0 + u], ea_ref[c, h0 + u], False)
                flag_ref[c * PEER_HEADS + h0 + u] = tied.astype(jnp.int32)
            return carry

        lax.fori_loop(0, n_chunks * PEER_HEADS // per_trip, fast, 0)

        def exact(t, carry):
            @pl.when(flag_ref[t] != 0)
            def _redo():
                c, h = t // PEER_HEADS, t % PEER_HEADS
                w2 = wqt_ref[pl.ds(pl.multiple_of(h * 2 * LANE, 2 * LANE), 2 * LANE), :]
                qt = _dot(w2, h2c_ref[c]).astype(_MXU)
                unit(c, h, _dot(keys_ref[0], qt[:LANE]), _dot(keys_ref[1], qt[LANE:]), True)
            return carry

        lax.fori_loop(0, n_chunks * PEER_HEADS, exact, 0)

    def step(a_new, a_prev, w_new, w_prev):
        blk = jnp.clip(s - 1, 0, n_eb - 1)

        def gate(ii, c):
            i = blk * i_per + ii
            rows = slice(ii * PEER_NKEYS, (ii + 1) * PEER_NKEYS)
            cols = slice(c * LANE, (c + 1) * LANE)
            g = jnp.zeros((PEER_NKEYS, LANE), gdt)
            for h in range(PEER_HEADS):
                lr_i = lr_ref[c, h, pl.ds(i, 1), :].astype(gdt)
                ea_i = ea_ref[c, h, pl.ds(i, 1), :].astype(gdt)
                g = g + jnp.where(rb_ref[c, h] < lr_i, eb_ref[c, h], jnp.zeros((), gdt)) * ea_i
            a = a_prev[rows, cols]
            gelu = 0.5 * a * (1.0 + lax.erf(a * (2.0 ** -0.5)))
            w_new[rows, cols] = gelu.astype(gdt) * g

        units = [(ii, c) for ii in range(i_per) for c in range(n_chunks)]
        kd, mr = MXU_DEPTH, PEER_MXU_ROWS
        d_model = u_ref.shape[1]
        n_pieces = (PEER_EB // mr) * (d_model // kd) + (d_model // mr) * (PEER_EB // kd)
        per = len(units) // n_pieces
        done = 0
        for r0 in range(0, PEER_EB, mr):
            acc = None
            for k0 in range(0, d_model, kd):
                part = _dot(u_ref[r0:r0 + mr, k0:k0 + kd], h2t_ref[k0:k0 + kd, :])
                acc = part if acc is None else acc + part
                for ii, c in units[done:done + per]:
                    gate(ii, c)
                done += per
            a_new[r0:r0 + mr, :] = acc
        for r0 in range(0, d_model, mr):
            acc = ft_ref[r0:r0 + mr, :]
            for k0 in range(0, PEER_EB, kd):
                acc = acc + _dot(vt_ref[0, r0:r0 + mr, k0:k0 + kd], w_prev[k0:k0 + kd, :])
                for ii, c in units[done:done + per]:
                    gate(ii, c)
                done += per
            ft_ref[r0:r0 + mr, :] = acc
        for ii, c in units[done:]:
            gate(ii, c)

    @pl.when(s % 2 == 0)
    def _even():
        step(a0_ref, a1_ref, w1_ref, w0_ref)

    @pl.when(s % 2 == 1)
    def _odd():
        step(a1_ref, a0_ref, w0_ref, w1_ref)

    @pl.when(s == n_eb + 1)
    def _fin():
        x = x_ref[...] + mod_ref[0, 5:6, :] * ft_ref[...].T
        if final:
            x = _rms(x, fg_ref[...])
        o_ref[...] = x


def _peer(h2, x, mod, mod_row, wqt, keys, u, vt, final_g=None):
    n, d = x.shape
    n_exp = u.shape[0]
    n_eb = n_exp // PEER_EB
    tm = PEER_TM
    n_chunks = tm // LANE
    final = final_g is not None
    once = pl.Buffered(1)
    args = [h2, x, mod, wqt, keys, u, vt]
    in_specs = [pl.BlockSpec((tm, d), lambda i, e: (i, 0), pipeline_mode=once),
                pl.BlockSpec((tm, d), lambda i, e: (i, 0), pipeline_mode=once),
                pl.BlockSpec((1, N_MOD, d), lambda i, e: (mod_row(i), 0, 0)),
                pl.BlockSpec(wqt.shape, lambda i, e: (0, 0), pipeline_mode=once),
                pl.BlockSpec(keys.shape, lambda i, e: (0, 0, 0)),
                pl.BlockSpec((PEER_EB, d), lambda i, e: (jnp.minimum(e, n_eb - 1), 0)),
                pl.BlockSpec((1, d, PEER_EB), lambda i, e: (jnp.clip(e - 2, 0, n_eb - 1), 0, 0))]
    if final:
        args.append(final_g)
        in_specs.append(pl.BlockSpec((1, d), lambda i, e: (0, 0)))
    tab32 = pltpu.VMEM((n_chunks, PEER_HEADS, PEER_NKEYS, LANE), _F32)
    tab16 = pltpu.VMEM((n_chunks, PEER_HEADS, PEER_NKEYS, LANE), _MXU)
    abuf = pltpu.VMEM((PEER_EB, tm), _F32)
    wbuf = pltpu.VMEM((PEER_EB, tm), _MXU)
    return pl.pallas_call(
        functools.partial(_peer_kernel, n_eb=n_eb, final=final),
        grid=(n // tm, n_eb + 2),
        in_specs=in_specs,
        out_specs=pl.BlockSpec((tm, d), lambda i, e: (i, 0)),
        out_shape=jax.ShapeDtypeStruct((n, d), _F32),
        scratch_shapes=[pltpu.VMEM((d, tm), _MXU), tab32, tab32, tab16, tab16,
                        pltpu.VMEM((d, tm), _F32), abuf, abuf, wbuf, wbuf,
                        pltpu.VMEM((n_chunks, d, LANE), _MXU), pltpu.SMEM((n_chunks * PEER_HEADS,), jnp.int32)],
        compiler_params=_cparams(("parallel", "arbitrary")),
        name="peer_ffn",
    )(*args)


def _dup_halves(w):
    a, b = w[:, :HEAD_DIM], w[:, HEAD_DIM:]
    return jnp.concatenate([a, a, b, b], axis=1)


def _prep_ab(w_in):
    aq, ak, av = w_in[:, 0:512], w_in[:, 512:1024], w_in[:, 1024:1536]
    bq, bk, bv = w_in[:, 1536:2048], w_in[:, 2048:2176], w_in[:, 2176:2304]
    cat = [aq, ak, av, bq, _rot_cols(bq, HEAD_DIM), _dup_halves(bk), _dup_halves(_rot_cols(bk, HEAD_DIM)),
           _dup_halves(bv)]
    return jnp.concatenate(cat, axis=1).astype(_MXU)


def _prep_cd(w_in, w_uq, w_ukv):
    d = w_in.shape[0]
    cq, ckv, kr = w_in[:, 0:256], w_in[:, 256:384], w_in[:, 384:416]
    dq, dk, dv = w_in[:, 416:928], w_in[:, 928:1440], w_in[:, 1440:1952]
    z64, z32 = jnp.zeros((d, 64), _F32), jnp.zeros((d, 32), _F32)
    kr128 = jnp.concatenate([z64, kr, z32], axis=1)
    krrot128 = jnp.concatenate([z64, _rot_cols(kr, MLA_ROPE), z32], axis=1)
    w = jnp.concatenate([cq, ckv, kr128, krrot128, dq, _rot_cols(dq, HEAD_DIM), dk, _rot_cols(dk, HEAD_DIM), dv],
                        axis=1).astype(_MXU)
    r = w_uq.shape[0]
    uq = w_uq.reshape(r, 8, MLA_NOPE + MLA_ROPE)
    nope, rope = uq[:, :, :MLA_NOPE], uq[:, :, MLA_NOPE:]
    rope_rot = _rot_cols(rope.reshape(r, 8 * MLA_ROPE), MLA_ROPE).reshape(r, 8, MLA_ROPE)
    zq = jnp.zeros((r, 8, 32), _F32)
    wqa = jnp.concatenate([nope, rope, zq], axis=2).reshape(r, 1024).astype(_MXU)
    wqb = jnp.concatenate([jnp.zeros_like(nope), rope_rot, zq], axis=2).reshape(r, 1024).astype(_MXU)
    rk = w_ukv.shape[0]
    ukv = w_ukv.reshape(rk, 8, 128)
    wk = jnp.concatenate([ukv[:, :, :MLA_NOPE], jnp.zeros((rk, 8, 64), _F32)], axis=2).reshape(rk, 1024).astype(_MXU)
    wv = ukv[:, :, MLA_NOPE:].reshape(rk, 512).astype(_MXU)
    lane = jnp.arange(LANE)
    src = (lane >= MLA_NOPE) & (lane < MLA_NOPE + MLA_ROPE)
    place = (src[:, None] & (lane[:, None] == (jnp.arange(1024)[None, :] % LANE))).astype(_MXU)
    return w, wqa, wqb, wk, wv, place


def kernel(x, c, ctx, c_ctx, ada_w, ada_b, norm1_g, norm2_g, w_out, peer_wq, peer_keys, peer_u, peer_v,
           ab_w_in, na_rpb, swa_sink, cd_w_in, mla_q_norm_g, mla_w_uq, mla_kv_norm_g, mla_w_ukv,
           diff_lambda, diff_subln_g, final_norm_g):
    batch, seq, d = x.shape
    ctx_len = ctx.shape[1]
    depth = ada_w.shape[0]
    assert seq % TM == 0 and (batch * ctx_len) % TM == 0 and seq % TQ_LOCAL == 0
    assert seq % PEER_TM == 0 and (batch * ctx_len) % PEER_TM == 0
    assert depth == 2, "even layers keep a context stream, the single odd layer is the last one"
    n_lat, n_ctx = batch * seq, batch * ctx_len
    xs = x.reshape(n_lat, d)
    cs = ctx.reshape(n_ctx, d)

    mod_rows = -(-(batch + 1) // 16) * 16
    cc = jnp.zeros((mod_rows, d), _F32).at[:batch].set(c).at[batch].set(c_ctx)
    mod_all = _modulation(cc, ada_w, ada_b)

    tiles_per_seq = seq // TM
    lat_row = lambda i: i // tiles_per_seq
    ctx_row = lambda i: batch
    lat_tab = lambda i: i % tiles_per_seq
    ctx_tab = lambda i: tiles_per_seq

    cos64, sin64 = _rope_tables(seq, HEAD_DIM)
    cos64p, sin64p = _pad_table(cos64, LANE, 1.0), _pad_table(sin64, LANE, 0.0)

    for l in range(depth):
        last = l == depth - 1
        j = l // 2
        mod = mod_all[l, :batch + 1].reshape(batch + 1, N_MOD, d)
        g1 = norm1_g[l].reshape(1, d)
        g2 = norm2_g[l].reshape(1, d)
        wo = w_out[l].astype(_MXU)
        wqt = peer_wq[l].T.astype(_MXU)
        keys = peer_keys[l].astype(_MXU)
        u = peer_u[l].astype(_MXU)
        vt = peer_v[l].astype(_MXU).reshape(-1, PEER_EB, d).transpose(0, 2, 1)
        if l % 2 == 0:
            w = _prep_ab(ab_w_in[j])
            aq, ak, av, bq, bk2, bv2 = _proj_ab(xs, mod, lat_row, g1, cos64p, sin64p, lat_tab, w)
            caq, cak, cav, cbq, cbk2, cbv2 = _proj_ab(cs, mod, ctx_row, g1, cos64p, sin64p, ctx_tab, w)
            bias = _na_bias_tables(na_rpb[j], seq // GRID_W)
            ya = _na_attention(aq, ak, av, cak, cav, bias, batch, seq, ctx_len)
            sink = swa_sink[j].astype(_F32)
            yb = _swa_attention(sink, bq, bk2, bv2, cbk2, cbv2, batch, seq, ctx_len)
            xs, h2 = _out_proj(xs, ya, yb, mod, lat_row, g2, wo[:512], wo[512:])
            if not last:
                pair_heads = [(m, half, m, m) for m in range(4) for half in range(2)]
                pair_outs = [("pair", 2 * m, 2 * m + 1) for m in range(4)]
                yca = _flash_attention(caq, cak, cav, heads=pair_heads, outs=pair_outs, batch=batch,
                                       q_per_batch=ctx_len, kv_per_batch=ctx_len, tq=ctx_len, tk=ctx_len,
                                       name="ctx_attn_a")
                gqa_heads = [(m, half, m // 2, m // 2) for m in range(4) for half in range(2)]
                ycb = _flash_attention(cbq, cbk2, cbv2, heads=gqa_heads, outs=pair_outs, batch=batch,
                                       q_per_batch=ctx_len, kv_per_batch=ctx_len, tq=ctx_len, tk=ctx_len,
                                       sink=sink, name="ctx_attn_b")
                cs, h2c = _out_proj(cs, yca, ycb, mod, ctx_row, g2, wo[:512], wo[512:])
        else:
            lam_init = 0.8 - 0.6 * math.exp(-0.3 * l)
            w, wqa, wqb, wk, wv, place = _prep_cd(cd_w_in[j], mla_w_uq[j], mla_w_ukv[j])
            cos32, sin32 = _rope_tables(seq, MLA_ROPE)
            ones64, zeros64 = jnp.ones((seq, 64), _F32), jnp.zeros((seq, 64), _F32)
            cosq = _pad_table(jnp.concatenate([ones64, cos32, ones64[:, :32]], 1), LANE, 1.0)
            sinq = _pad_table(jnp.concatenate([zeros64, sin32, zeros64[:, :32]], 1), LANE, 0.0)
            tabs = (cosq, sinq, cos64p, sin64p)
            qg = mla_q_norm_g[j].reshape(1, -1)
            kvg = mla_kv_norm_g[j].reshape(1, -1)
            qmt, km, vmt, dqt, dk, dvt = _proj_cd(xs, mod, lat_row, g1, tabs, lat_tab, w, qg, kvg, wqa, wqb, wk, wv, place)
            _, ckm, cvmt, _, cdk, cdvt = _proj_cd(cs, mod, ctx_row, g1, tabs, ctx_tab, w, qg, kvg, wqa, wqb, wk, wv, place)
            mla_heads = [(h, None, h, h // 2) for h in range(8)]
            pair_outs = [("pair", 2 * m, 2 * m + 1) for m in range(4)]
            yc = _flasht_attention(qmt, km, vmt, ckm, cvmt, heads=mla_heads, outs=pair_outs, batch=batch, seq=seq,
                                   ctx_len=ctx_len, tq=GLOBAL_TQ, tk=GLOBAL_TK, name="mla_attn")
            diff_heads = [(h, a, h, h) for h in range(4) for a in range(2)]
            diff_outs = [("diff", 2 * h, 2 * h + 1) for h in range(4)]
            od = _flasht_attention(dqt, dk, dvt, cdk, cdvt, heads=diff_heads, outs=diff_outs, batch=batch, seq=seq,
                                   ctx_len=ctx_len, tq=GLOBAL_TQ, tk=GLOBAL_TK,
                                   diff=(diff_lambda[j].astype(_F32), diff_subln_g[j].reshape(1, -1), lam_init),
                                   name="diff_attn")
            xs, h2 = _out_proj(xs, yc, od, mod, lat_row, g2, wo[:512], wo[512:])
        fg = final_norm_g.reshape(1, d) if last else None
        xs = _peer(h2, xs, mod, lambda i: i // (seq // PEER_TM), wqt, keys, u, vt, final_g=fg)
        if not last:
            cs = _peer(h2c, cs, mod, ctx_row, wqt, keys, u, vt)
    return xs.reshape(batch, seq, d)
```

```python
import functools
import math

import jax
import jax.numpy as jnp
from jax import lax
from jax.experimental import pallas as pl
from jax.experimental.pallas import tpu as pltpu

_F32 = jnp.float32
_MXU = jnp.bfloat16
_NEG = -1e30
_LOG2E = math.log2(math.e)

GRID_W = 64
HEAD_DIM = 64
ROPE_BASE = 10000.0
RMS_EPS = 1e-6
N_MOD = 6
NA_KR, NA_KC = 8, 16
SWA_WINDOW = 128
MLA_NOPE, MLA_ROPE = 64, 32
PEER_HEADS, PEER_NKEYS, PEER_TOPK = 8, 128, 16

LANE = 128
TM = 512
TQ_LOCAL = 256
NA_GROUP_ROWS = TQ_LOCAL // GRID_W
NA_WIN_ROWS = NA_KR + NA_GROUP_ROWS - 1
GLOBAL_TQ = 512
GLOBAL_TK = 512
PEER_TM = 512
PEER_EB = 1024
MXU_DEPTH = 256
PEER_MXU_ROWS = 1024
PEER_SELECT_UNROLL = 8
VMEM_LIMIT = 56 * 1024 * 1024


def _cparams(sem, vmem=VMEM_LIMIT):
    return pltpu.CompilerParams(dimension_semantics=sem, vmem_limit_bytes=vmem)


def _dot(a, b):
    return jnp.dot(a, b, preferred_element_type=_F32)


def _dot_nt(a, b):
    return lax.dot_general(a, b, (((1,), (1,)), ((), ())), preferred_element_type=_F32)


def _rms(x, g):
    return x * lax.rsqrt(jnp.mean(x * x, axis=-1, keepdims=True) + RMS_EPS) * g


def _lane_half(shape):
    return lax.broadcasted_iota(jnp.int32, shape, len(shape) - 1) >= (LANE // 2)


def _mod_kernel(c_ref, w_ref, b_ref, o_ref):
    c = c_ref[...]
    a = c * jax.nn.sigmoid(c)
    w = w_ref[0]
    a_hi = a.astype(_MXU)
    a_lo = (a - a_hi.astype(_F32)).astype(_MXU)
    w_hi = w.astype(_MXU)
    w_lo = (w - w_hi.astype(_F32)).astype(_MXU)
    o_ref[0] = _dot(a_hi, w_hi) + _dot(a_lo, w_hi) + _dot(a_hi, w_lo) + b_ref[0]


def _modulation(cc, ada_w, ada_b):
    depth, d, n = ada_w.shape
    rows = cc.shape[0]
    tn = 768
    return pl.pallas_call(
        _mod_kernel,
        grid=(depth, n // tn),
        in_specs=[pl.BlockSpec((rows, d), lambda l, j: (0, 0)),
                  pl.BlockSpec((1, d, tn), lambda l, j: (l, 0, j)),
                  pl.BlockSpec((1, 1, tn), lambda l, j: (l, 0, j))],
        out_specs=pl.BlockSpec((1, rows, tn), lambda l, j: (l, 0, j)),
        out_shape=jax.ShapeDtypeStruct((depth, rows, n), _F32),
        compiler_params=_cparams(("parallel", "parallel")),
        name="adaln_mod",
    )(cc, ada_w, ada_b.reshape(depth, 1, n))


def _rope_tables(seq, d):
    t = jnp.arange(seq, dtype=jnp.int32)
    q = d // 4
    freq = ROPE_BASE ** (-jnp.arange(q, dtype=_F32) / q)

    def one(pos):
        ang = pos.astype(_F32)[:, None] * freq[None, :]
        return jnp.concatenate([jnp.cos(ang)] * 2, -1), jnp.concatenate([jnp.sin(ang)] * 2, -1)

    cr, sr = one(t // GRID_W)
    cc, sc = one(t % GRID_W)
    return jnp.concatenate([cr, cc], -1), jnp.concatenate([sr, sc], -1)


def _rot_cols(w, d):
    k, n = w.shape
    w5 = w.reshape(k, n // d, 2, 2, d // 4)
    return jnp.stack([-w5[:, :, :, 1], w5[:, :, :, 0]], axis=3).reshape(k, n)


def _pad_table(tab, width, ident):
    s, w = tab.shape
    if w < width:
        reps = width // w
        tab = jnp.tile(tab, (1, reps))
    return jnp.concatenate([tab, jnp.full((TM, width), ident, _F32)], axis=0)


def _proj_ab_kernel(x_ref, mod_ref, g_ref, cos_ref, sin_ref, w_ref,
                    aq_ref, ak_ref, av_ref, bq_ref, bk_ref, bv_ref):
    x = x_ref[...]
    h = _rms(x, g_ref[...]) * (1.0 + mod_ref[0, 1:2, :]) + mod_ref[0, 0:1, :]
    hb = h.astype(_MXU)
    cos = cos_ref[...]
    sin = sin_ref[...]

    def proj(lo, n):
        return _dot(hb, w_ref[:, lo:lo + n])

    scale = HEAD_DIM ** -0.5
    aq_ref[...] = (proj(0, 512) * scale).astype(aq_ref.dtype)
    ak_ref[...] = proj(512, 512).astype(ak_ref.dtype)
    av_ref[...] = proj(1024, 512).astype(av_ref.dtype)
    cos4 = jnp.tile(cos, (1, 4))
    sin4 = jnp.tile(sin, (1, 4))
    bq_ref[...] = ((proj(1536, 512) * cos4 + proj(2048, 512) * sin4) * scale).astype(bq_ref.dtype)
    cos2 = jnp.tile(cos, (1, 2))
    sin2 = jnp.tile(sin, (1, 2))
    bk_ref[...] = (proj(2560, 256) * cos2 + proj(2816, 256) * sin2).astype(bk_ref.dtype)
    bv_ref[...] = proj(3072, 256).astype(bv_ref.dtype)


def _proj_ab(x, mod, mod_row, g, cos, sin, tab_row, w):
    n, d = x.shape
    widths = (512, 512, 512, 512, 256, 256)
    return pl.pallas_call(
        _proj_ab_kernel,
        grid=(n // TM,),
        in_specs=[pl.BlockSpec((TM, d), lambda i: (i, 0)),
                  pl.BlockSpec((1, N_MOD, d), lambda i: (mod_row(i), 0, 0)),
                  pl.BlockSpec((1, d), lambda i: (0, 0)),
                  pl.BlockSpec((TM, LANE), lambda i: (tab_row(i), 0)),
                  pl.BlockSpec((TM, LANE), lambda i: (tab_row(i), 0)),
                  pl.BlockSpec(w.shape, lambda i: (0, 0))],
        out_specs=[pl.BlockSpec((TM, wd), lambda i: (i, 0)) for wd in widths],
        out_shape=[jax.ShapeDtypeStruct((n, wd), _MXU) for wd in widths],
        compiler_params=_cparams(("parallel",)),
        name="proj_ab",
    )(x, mod, g, cos, sin, w)


def _na_kernel(q_ref, k_ref, v_ref, kc_ref, vc_ref, bias_ref, o_ref, *, rows):
    g = pl.program_id(1)
    ks = jnp.clip(NA_GROUP_ROWS * g - NA_KR // 2, 0, rows - NA_WIN_ROWS)
    start = pl.multiple_of(ks * GRID_W, GRID_W)
    nwin = NA_WIN_ROWS * GRID_W
    hi = _lane_half((TQ_LOCAL, LANE))
    for m in range(4):
        cols = slice(m * LANE, (m + 1) * LANE)
        q2 = q_ref[:, cols]
        k2 = k_ref[pl.ds(start, nwin), cols]
        v2 = v_ref[pl.ds(start, nwin), cols]
        kc2 = kc_ref[:, cols]
        vc2 = vc_ref[:, cols]
        outs = []
        for half in range(2):
            qh = jnp.where(hi == (half == 1), q2, jnp.zeros_like(q2))
            s_lat = _dot_nt(qh, k2) + bias_ref[0, 2 * m + half]
            s_ctx = _dot_nt(qh, kc2)
            mx = jnp.maximum(jnp.max(s_lat, axis=-1, keepdims=True), jnp.max(s_ctx, axis=-1, keepdims=True))
            p_lat = jnp.exp(s_lat - mx)
            p_ctx = jnp.exp(s_ctx - mx)
            den = jnp.sum(p_lat, axis=-1, keepdims=True) + jnp.sum(p_ctx, axis=-1, keepdims=True)
            o = _dot(p_lat.astype(_MXU), v2) + _dot(p_ctx.astype(_MXU), vc2)
            outs.append(o / den)
        o_ref[:, cols] = jnp.where(hi, outs[1], outs[0]).astype(o_ref.dtype)


def _na_bias_tables(rpb, rows):
    gq = NA_GROUP_ROWS
    n_groups = rows // gq
    ql = jnp.arange(gq)[:, None]
    kl = jnp.arange(NA_WIN_ROWS)[None, :]
    qc = jnp.arange(GRID_W)[:, None]
    kc = jnp.arange(GRID_W)[None, :]
    cs = jnp.clip(qc - NA_KC // 2, 0, GRID_W - NA_KC)
    col_valid = (kc >= cs) & (kc < cs + NA_KC)
    col_idx = jnp.clip(kc - qc + NA_KC - 1, 0, 2 * NA_KC - 2)
    exact = lax.Precision.HIGHEST
    oh_c = jax.nn.one_hot(col_idx, 2 * NA_KC - 1, dtype=_F32)
    rpb_cols = jnp.einsum("hab,xyb->haxy", rpb.astype(_F32), oh_c, precision=exact)
    tabs = []
    for g in (0, 1, n_groups - 1):
        ks = min(max(gq * g - NA_KR // 2, 0), rows - NA_WIN_ROWS)
        rq = gq * g + ql
        rk = ks + kl
        r0 = jnp.clip(rq - NA_KR // 2, 0, rows - NA_KR)
        row_valid = (rk >= r0) & (rk < r0 + NA_KR)
        row_idx = jnp.clip(rk - rq + NA_KR - 1, 0, 2 * NA_KR - 2)
        oh_r = jax.nn.one_hot(row_idx, 2 * NA_KR - 1, dtype=_F32)
        b = jnp.einsum("qka,haxy->hqxky", oh_r, rpb_cols, precision=exact)
        valid = row_valid[:, None, :, None] & col_valid[None, :, None, :]
        b = jnp.where(valid[None], b, _NEG)
        tabs.append(b.reshape(rpb.shape[0], gq * GRID_W, NA_WIN_ROWS * GRID_W))
    return jnp.stack(tabs)


def _na_attention(aq, ak, av, cak, cav, bias, batch, seq, ctx_len):
    rows = seq // GRID_W
    n_groups = seq // TQ_LOCAL
    nwin = NA_WIN_ROWS * GRID_W

    def bias_row(b, g):
        return (jnp.where(g == 0, 0, jnp.where(g == n_groups - 1, 2, 1)), 0, 0, 0)

    return pl.pallas_call(
        functools.partial(_na_kernel, rows=rows),
        grid=(batch, n_groups),
        in_specs=[pl.BlockSpec((TQ_LOCAL, 512), lambda b, g: (b * n_groups + g, 0)),
                  pl.BlockSpec((seq, 512), lambda b, g: (b, 0)),
                  pl.BlockSpec((seq, 512), lambda b, g: (b, 0)),
                  pl.BlockSpec((ctx_len, 512), lambda b, g: (b, 0)),
                  pl.BlockSpec((ctx_len, 512), lambda b, g: (b, 0)),
                  pl.BlockSpec((1, 8, TQ_LOCAL, nwin), bias_row)],
        out_specs=pl.BlockSpec((TQ_LOCAL, 512), lambda b, g: (b * n_groups + g, 0)),
        out_shape=jax.ShapeDtypeStruct((batch * seq, 512), _MXU),
        compiler_params=_cparams(("parallel", "arbitrary")),
        name="na_attention",
    )(aq, ak, av, cak, cav, bias)


def _swa_kernel(sink_ref, q_ref, k_ref, v_ref, kc_ref, vc_ref, o_ref, *, seq):
    t = pl.program_id(1)
    kwin = TQ_LOCAL + 2 * SWA_WINDOW
    start = t * TQ_LOCAL
    kstart = pl.multiple_of(jnp.clip(start - SWA_WINDOW, 0, seq - kwin), LANE)
    qpos = start + lax.broadcasted_iota(jnp.int32, (TQ_LOCAL, kwin), 0)
    kpos = kstart + lax.broadcasted_iota(jnp.int32, (TQ_LOCAL, kwin), 1)
    mask = jnp.where(jnp.abs(kpos - qpos) <= SWA_WINDOW, 0.0, _NEG).astype(_F32)
    hi = _lane_half((TQ_LOCAL, LANE))
    for m in range(4):
        cols = slice(m * LANE, (m + 1) * LANE)
        kv = m // 2
        kcols = slice(kv * LANE, (kv + 1) * LANE)
        q2 = q_ref[:, cols]
        k2 = k_ref[pl.ds(kstart, kwin), kcols]
        v2 = v_ref[pl.ds(kstart, kwin), kcols]
        kc2 = kc_ref[:, kcols]
        vc2 = vc_ref[:, kcols]
        outs = []
        for half in range(2):
            sink = sink_ref[2 * m + half]
            qh = jnp.where(hi == (half == 1), q2, jnp.zeros_like(q2))
            s_lat = _dot_nt(qh, k2) + mask
            s_ctx = _dot_nt(qh, kc2)
            mx = jnp.maximum(jnp.max(s_lat, axis=-1, keepdims=True), jnp.max(s_ctx, axis=-1, keepdims=True))
            mx = jnp.maximum(mx, sink)
            p_lat = jnp.exp(s_lat - mx)
            p_ctx = jnp.exp(s_ctx - mx)
            den = (jnp.sum(p_lat, axis=-1, keepdims=True) + jnp.sum(p_ctx, axis=-1, keepdims=True)
                   + jnp.exp(sink - mx))
            o = _dot(p_lat.astype(_MXU), v2) + _dot(p_ctx.astype(_MXU), vc2)
            outs.append(o / den)
        o_ref[:, cols] = jnp.where(hi, outs[1], outs[0]).astype(o_ref.dtype)


def _swa_attention(sink, bq, bk2, bv2, cbk2, cbv2, batch, seq, ctx_len):
    n_t = seq // TQ_LOCAL
    return pl.pallas_call(
        functools.partial(_swa_kernel, seq=seq),
        grid=(batch, n_t),
        in_specs=[pl.BlockSpec(memory_space=pltpu.SMEM),
                  pl.BlockSpec((TQ_LOCAL, 512), lambda b, t: (b * n_t + t, 0)),
                  pl.BlockSpec((seq, 256), lambda b, t: (b, 0)),
                  pl.BlockSpec((seq, 256), lambda b, t: (b, 0)),
                  pl.BlockSpec((ctx_len, 256), lambda b, t: (b, 0)),
                  pl.BlockSpec((ctx_len, 256), lambda b, t: (b, 0))],
        out_specs=pl.BlockSpec((TQ_LOCAL, 512), lambda b, t: (b * n_t + t, 0)),
        out_shape=jax.ShapeDtypeStruct((batch * seq, 512), _MXU),
        compiler_params=_cparams(("parallel", "arbitrary")),
        name="swa_attention",
    )(sink, bq, bk2, bv2, cbk2, cbv2)


def _flash_kernel(*refs, heads, outs, has_ctx, has_sink, diff_cfg, n_kv):
    it = iter(refs)
    sink_ref = next(it) if has_sink else None
    q_ref, k_ref, v_ref = next(it), next(it), next(it)
    kc_ref = next(it) if has_ctx else None
    vc_ref = next(it) if has_ctx else None
    lam_ref = next(it) if diff_cfg else None
    sg_ref = next(it) if diff_cfg else None
    o_ref = next(it)
    m_ref, l_ref, acc_ref = next(it), next(it), next(it)
    kv = pl.program_id(2)
    tq = q_ref.shape[0]
    hi = _lane_half((tq, LANE))

    @pl.when(kv == 0)
    def _init():
        for h in range(len(heads)):
            if has_sink:
                m_ref[h] = jnp.full((tq, LANE), sink_ref[h], _F32)
                l_ref[h] = jnp.ones((tq, LANE), _F32)
            else:
                m_ref[h] = jnp.full((tq, LANE), _NEG, _F32)
                l_ref[h] = jnp.zeros((tq, LANE), _F32)
            acc_ref[h] = jnp.zeros((tq, LANE), _F32)

    def attend(kr, vr):
        for h, (qb, qhalf, kb, vb) in enumerate(heads):
            q2 = q_ref[:, qb * LANE:(qb + 1) * LANE]
            if qhalf is not None:
                q2 = jnp.where(hi == (qhalf == 1), q2, jnp.zeros_like(q2))
            s = _dot_nt(q2, kr[:, kb * LANE:(kb + 1) * LANE])
            m_old = m_ref[h][:, :1]
            m_new = jnp.maximum(m_old, jnp.max(s, axis=-1, keepdims=True))
            alpha = jnp.exp(m_old - m_new)
            p = jnp.exp(s - m_new)
            l_ref[h] = jnp.broadcast_to(alpha * l_ref[h][:, :1] + jnp.sum(p, axis=-1, keepdims=True), (tq, LANE))
            acc_ref[h] = alpha * acc_ref[h] + _dot(p.astype(_MXU), vr[:, vb * LANE:(vb + 1) * LANE])
            m_ref[h] = jnp.broadcast_to(m_new, (tq, LANE))

    if has_ctx:
        @pl.when(kv == 0)
        def _ctx():
            attend(kc_ref, vc_ref)

    attend(k_ref, v_ref)

    @pl.when(kv == n_kv - 1)
    def _fin():
        def head_out(h):
            return acc_ref[h] / l_ref[h][:, :1]

        for j, spec in enumerate(outs):
            if spec[0] == "full":
                o = head_out(spec[1])
            elif spec[0] == "pair":
                o = jnp.where(hi, head_out(spec[2]), head_out(spec[1]))
            else:
                lv = lam_ref[...]
                lam = (jnp.exp(jnp.sum(lv[0:1] * lv[1:2], axis=-1, keepdims=True))
                       - jnp.exp(jnp.sum(lv[2:3] * lv[3:4], axis=-1, keepdims=True)) + diff_cfg)
                o = _rms(head_out(spec[1]) - lam * head_out(spec[2]), sg_ref[...]) * (1.0 - diff_cfg)
            o_ref[:, j * LANE:(j + 1) * LANE] = o.astype(o_ref.dtype)


def _flash_attention(q, k, v, *, heads, outs, batch, q_per_batch, kv_per_batch, tq, tk,
                     q_row0=0, kv_row0=0, ctx=None, ctx_len=0, ctx_row0=0, sink=None, diff=None, name="flash"):
    n_q = q_per_batch // tq
    n_kv = kv_per_batch // tk
    qw, kw, vw = q.shape[1], k.shape[1], v.shape[1]
    args, in_specs = [], []
    if sink is not None:
        args.append(sink)
        in_specs.append(pl.BlockSpec(memory_space=pltpu.SMEM))
    q0, k0 = q_row0 // tq, kv_row0 // tk
    args += [q, k, v]
    in_specs += [pl.BlockSpec((tq, qw), lambda b, i, j: (q0 + b * n_q + i, 0)),
                 pl.BlockSpec((tk, kw), lambda b, i, j: (k0 + b * n_kv + j, 0)),
                 pl.BlockSpec((tk, vw), lambda b, i, j: (k0 + b * n_kv + j, 0))]
    if ctx is not None:
        c0 = ctx_row0 // ctx_len
        args += [ctx[0], ctx[1]]
        in_specs += [pl.BlockSpec((ctx_len, kw), lambda b, i, j: (c0 + b, 0)),
                     pl.BlockSpec((ctx_len, vw), lambda b, i, j: (c0 + b, 0))]
    diff_cfg = None
    if diff is not None:
        lam_vecs, subln_g, diff_cfg = diff
        args += [lam_vecs, subln_g]
        in_specs += [pl.BlockSpec(lam_vecs.shape, lambda b, i, j: (0, 0)),
                     pl.BlockSpec(subln_g.shape, lambda b, i, j: (0, 0))]
    nh = len(heads)
    ow = len(outs) * LANE
    return pl.pallas_call(
        functools.partial(_flash_kernel, heads=tuple(heads), outs=tuple(outs), has_ctx=ctx is not None,
                          has_sink=sink is not None, diff_cfg=diff_cfg, n_kv=n_kv),
        grid=(batch, n_q, n_kv),
        in_specs=in_specs,
        out_specs=pl.BlockSpec((tq, ow), lambda b, i, j: (b * n_q + i, 0)),
        out_shape=jax.ShapeDtypeStruct((batch * q_per_batch, ow), _MXU),
        scratch_shapes=[pltpu.VMEM((nh, tq, LANE), _F32)] * 3,
        compiler_params=_cparams(("parallel", "parallel", "arbitrary")),
        name=name,
    )(*args)


def _flasht_kernel(*refs, heads, outs, diff_cfg, n_kv):
    it = iter(refs)
    qt_ref, k_ref, vt_ref, kc_ref, vct_ref = next(it), next(it), next(it), next(it), next(it)
    lam_ref = next(it) if diff_cfg else None
    sg_ref = next(it) if diff_cfg else None
    o_ref = next(it)
    m_ref, l_ref, acc_ref = next(it), next(it), next(it)
    kv = pl.program_id(2)
    tq = qt_ref.shape[1]
    row_hi = lax.broadcasted_iota(jnp.int32, (LANE, tq), 0) >= (LANE // 2)

    @pl.when(kv == 0)
    def _init():
        m_ref[...] = jnp.full(m_ref.shape, _NEG, _F32)
        l_ref[...] = jnp.zeros(l_ref.shape, _F32)
        acc_ref[...] = jnp.zeros(acc_ref.shape, _F32)

    def attend(kr, vtr):
        def scores(h):
            qb, qhalf, kb, _ = heads[h]
            qt = qt_ref[qb * LANE:(qb + 1) * LANE, :]
            if qhalf is not None:
                qt = jnp.where(row_hi == (qhalf == 1), qt, jnp.zeros_like(qt))
            return _dot(kr[:, kb * LANE:(kb + 1) * LANE], qt)

        s_next = scores(0)
        for h in range(len(heads)):
            s = s_next
            if h + 1 < len(heads):
                s_next = scores(h + 1)
            vb = heads[h][3]
            m_old = m_ref[h, 0:1, :]
            m_new = jnp.maximum(m_old, jnp.max(s, axis=0, keepdims=True))
            alpha = jnp.exp2(m_old - m_new)
            p = jnp.exp2(s - m_new)
            l_new = alpha * l_ref[h, 0:1, :] + jnp.sum(p, axis=0, keepdims=True)
            acc_ref[h] = alpha * acc_ref[h] + _dot(vtr[vb * LANE:(vb + 1) * LANE, :], p.astype(_MXU))
            l_ref[h] = jnp.broadcast_to(l_new, (8, tq))
            m_ref[h] = jnp.broadcast_to(m_new, (8, tq))

    @pl.when(kv == 0)
    def _ctx():
        attend(kc_ref, vct_ref)

    attend(k_ref, vt_ref)

    @pl.when(kv == n_kv - 1)
    def _fin():
        def head_out(h):
            return acc_ref[h] / l_ref[h, 0:1, :]

        for j, spec in enumerate(outs):
            if spec[0] == "full":
                o = head_out(spec[1]).T
            elif spec[0] == "pair":
                o = jnp.where(row_hi, head_out(spec[2]), head_out(spec[1])).T
            else:
                lv = lam_ref[...]
                lam = (jnp.exp(jnp.sum(lv[0:1] * lv[1:2], axis=-1, keepdims=True))
                       - jnp.exp(jnp.sum(lv[2:3] * lv[3:4], axis=-1, keepdims=True)) + diff_cfg)
                o = _rms((head_out(spec[1]) - lam * head_out(spec[2])).T, sg_ref[...]) * (1.0 - diff_cfg)
            o_ref[:, j * LANE:(j + 1) * LANE] = o.astype(o_ref.dtype)


def _flasht_attention(qt, k, vt, kc, vct, *, heads, outs, batch, seq, ctx_len, tq, tk, diff=None, name="flasht"):
    n_q, n_kv = seq // tq, seq // tk
    qw, kw, vw = qt.shape[0], k.shape[1], vt.shape[0]
    args = [qt, k, vt, kc, vct]
    in_specs = [pl.BlockSpec((qw, tq), lambda b, i, j: (0, b * n_q + i)),
                pl.BlockSpec((tk, kw), lambda b, i, j: (b * n_kv + j, 0)),
                pl.BlockSpec((vw, tk), lambda b, i, j: (0, b * n_kv + j)),
                pl.BlockSpec((ctx_len, kw), lambda b, i, j: (b, 0)),
                pl.BlockSpec((vw, ctx_len), lambda b, i, j: (0, b))]
    diff_cfg = None
    if diff is not None:
        lam_vecs, subln_g, diff_cfg = diff
        args += [lam_vecs, subln_g]
        in_specs += [pl.BlockSpec(lam_vecs.shape, lambda b, i, j: (0, 0)),
                     pl.BlockSpec(subln_g.shape, lambda b, i, j: (0, 0))]
    nh = len(heads)
    ow = len(outs) * LANE
    return pl.pallas_call(
        functools.partial(_flasht_kernel, heads=tuple(heads), outs=tuple(outs), diff_cfg=diff_cfg, n_kv=n_kv),
        grid=(batch, n_q, n_kv),
        in_specs=in_specs,
        out_specs=pl.BlockSpec((tq, ow), lambda b, i, j: (b * n_q + i, 0)),
        out_shape=jax.ShapeDtypeStruct((batch * seq, ow), _MXU),
        scratch_shapes=[pltpu.VMEM((nh, 8, tq), _F32), pltpu.VMEM((nh, 8, tq), _F32),
                        pltpu.VMEM((nh, LANE, tq), _F32)],
        compiler_params=_cparams(("parallel", "parallel", "arbitrary")),
        name=name,
    )(*args)


def _out_kernel(x_ref, ya_ref, yb_ref, mod_ref, g2_ref, wa_ref, wb_ref, xo_ref, h2_ref):
    y = _dot(ya_ref[...], wa_ref[...]) + _dot(yb_ref[...], wb_ref[...])
    x = x_ref[...] + mod_ref[0, 2:3, :] * y
    xo_ref[...] = x
    h2 = _rms(x, g2_ref[...]) * (1.0 + mod_ref[0, 4:5, :]) + mod_ref[0, 3:4, :]
    h2_ref[...] = h2.astype(h2_ref.dtype)


def _out_proj(x, ya, yb, mod, mod_row, g2, wa, wb):
    n, d = x.shape
    return pl.pallas_call(
        _out_kernel,
        grid=(n // TM,),
        in_specs=[pl.BlockSpec((TM, d), lambda i: (i, 0)),
                  pl.BlockSpec((TM, ya.shape[1]), lambda i: (i, 0)),
                  pl.BlockSpec((TM, yb.shape[1]), lambda i: (i, 0)),
                  pl.BlockSpec((1, N_MOD, d), lambda i: (mod_row(i), 0, 0)),
                  pl.BlockSpec((1, d), lambda i: (0, 0)),
                  pl.BlockSpec(wa.shape, lambda i: (0, 0)),
                  pl.BlockSpec(wb.shape, lambda i: (0, 0))],
        out_specs=[pl.BlockSpec((TM, d), lambda i: (i, 0)), pl.BlockSpec((TM, d), lambda i: (i, 0))],
        out_shape=[jax.ShapeDtypeStruct((n, d), _F32), jax.ShapeDtypeStruct((n, d), _MXU)],
        compiler_params=_cparams(("parallel",)),
        name="out_proj",
    )(x, ya, yb, mod, g2, wa, wb)


def _proj_cd_kernel(x_ref, mod_ref, g_ref, cq_ref, sq_ref, cd_ref, sd_ref, w_ref, qg_ref, kvg_ref,
                    wqa_ref, wqb_ref, wk_ref, wv_ref, place_ref,
                    qm_ref, km_ref, vm_ref, dq_ref, dk_ref, dv_ref):
    x = x_ref[...]
    h = _rms(x, g_ref[...]) * (1.0 + mod_ref[0, 1:2, :]) + mod_ref[0, 0:1, :]
    hb = h.astype(_MXU)

    def proj(lo, n):
        return _dot(hb, w_ref[:, lo:lo + n])

    cosq, sinq = cq_ref[...], sq_ref[...]
    cosd, sind = cd_ref[...], sd_ref[...]
    cqn = _rms(proj(0, 256), qg_ref[...]).astype(_MXU)
    ckvn = _rms(proj(256, 128), kvg_ref[...]).astype(_MXU)
    cos8, sin8 = jnp.tile(cosq, (1, 8)), jnp.tile(sinq, (1, 8))
    qm = _dot(cqn, wqa_ref[...]) * cos8 + _dot(cqn, wqb_ref[...]) * sin8
    qm_ref[...] = (qm * ((MLA_NOPE + MLA_ROPE) ** -0.5 * _LOG2E)).T.astype(qm_ref.dtype)
    kr = (proj(384, 128) * cosq + proj(512, 128) * sinq).astype(_MXU)
    km_ref[...] = (_dot(ckvn, wk_ref[...]) + _dot(kr, place_ref[...])).astype(km_ref.dtype)
    vm_ref[...] = _dot(ckvn, wv_ref[...]).T.astype(vm_ref.dtype)
    cos4, sin4 = jnp.tile(cosd, (1, 4)), jnp.tile(sind, (1, 4))
    dq_ref[...] = ((proj(640, 512) * cos4 + proj(1152, 512) * sin4) * (HEAD_DIM ** -0.5 * _LOG2E)).T.astype(dq_ref.dtype)
    dk_ref[...] = (proj(1664, 512) * cos4 + proj(2176, 512) * sin4).astype(dk_ref.dtype)
    dv_ref[...] = proj(2688, 512).T.astype(dv_ref.dtype)


def _proj_cd(x, mod, mod_row, g, tabs, tab_row, w, qg, kvg, wqa, wqb, wk, wv, place):
    n, d = x.shape
    outs = ((1024, True), (1024, False), (512, True), (512, True), (512, False), (512, True))
    full = lambda a: pl.BlockSpec(a.shape, lambda i: (0, 0))
    return pl.pallas_call(
        _proj_cd_kernel,
        grid=(n // TM,),
        in_specs=[pl.BlockSpec((TM, d), lambda i: (i, 0)),
                  pl.BlockSpec((1, N_MOD, d), lambda i: (mod_row(i), 0, 0)),
                  pl.BlockSpec((1, d), lambda i: (0, 0))]
                 + [pl.BlockSpec((TM, LANE), lambda i: (tab_row(i), 0))] * 4
                 + [full(a) for a in (w, qg, kvg, wqa, wqb, wk, wv, place)],
        out_specs=[pl.BlockSpec((wd, TM), lambda i: (0, i)) if fm else pl.BlockSpec((TM, wd), lambda i: (i, 0))
                   for wd, fm in outs],
        out_shape=[jax.ShapeDtypeStruct((wd, n) if fm else (n, wd), _MXU) for wd, fm in outs],
        compiler_params=_cparams(("parallel",)),
        name="proj_cd",
    )(x, mod, g, *tabs, w, qg, kvg, wqa, wqb, wk, wv, place)


_MARK = 2.0 ** 121


def _extract16(x, first_only):
    r = x.shape[0]
    row = lax.broadcasted_iota(jnp.int32, x.shape, 0)
    row16 = lax.broadcasted_iota(jnp.int32, (PEER_TOPK, LANE), 0)
    vals = jnp.zeros((PEER_TOPK, LANE), _F32)
    for k in range(PEER_TOPK):
        m = jnp.max(x, axis=0, keepdims=True)
        hit = x == m
        if first_only:
            hit = row == jnp.min(jnp.where(hit, row, r), axis=0, keepdims=True)
        x = jnp.where(hit, -_MARK * (32 + k), x)
        vals = jnp.where(row16 == k, m, vals)
    order = jnp.where(x < -16.0 * _MARK, x * (-1.0 / _MARK) - 32.0, float(PEER_TOPK))
    return order, vals


def _staircase_rows():
    groups = []
    for r in range(PEER_TOPK // 2):
        n = PEER_TOPK // (r + 1)
        for c0 in range(0, n, 8):
            groups.append((r, c0, min(8, n - c0)))
    groups.append((None, 0, 8))
    return groups


def _peer_select(sa, sb, first_only):
    ra, av = _extract16(sa, first_only)
    rb, bv = _extract16(sb, first_only)
    groups = _staircase_rows()
    sub = lax.broadcasted_iota(jnp.int32, (8, LANE), 0)
    pieces = []
    for r, c0, nv in groups:
        if r is None:
            piece = av[8:16] + bv[0:1]
        else:
            piece = av[r:r + 1] + bv[c0:c0 + 8]
            if nv < 8:
                piece = jnp.where(sub < nv, piece, -jnp.inf)
        pieces.append(piece)
    cand = jnp.concatenate(pieces, axis=0)
    e_cand = jnp.exp(cand - (av[0:1] + bv[0:1]))
    sel, _ = _extract16(cand, first_only)
    sel = jnp.where(sel < float(PEER_TOPK), 1.0, 0.0)
    z = jnp.sum(sel * e_cand, axis=0, keepdims=True)
    lr = jnp.zeros(sa.shape, _F32)
    for g, (r, c0, nv) in enumerate(groups):
        blk = sel[8 * g:8 * g + 8]
        if r is None:
            for q in range(8):
                lr = jnp.where(ra == float(8 + q), blk[q:q + 1], lr)
        elif c0 == 0:
            cnt = jnp.sum(blk, axis=0, keepdims=True)
            if PEER_TOPK // (r + 1) > 8:
                cnt = cnt + jnp.sum(sel[8 * g + 8:8 * g + 16], axis=0, keepdims=True)
            lr = jnp.where(ra == float(r), cnt, lr)
    ea = jnp.exp(sa - av[0:1]) / z
    eb = jnp.exp(sb - bv[0:1])
    n_sel = (jnp.sum(jnp.where(ra < float(PEER_TOPK), 1.0, 0.0), axis=0, keepdims=True)
             + jnp.sum(jnp.where(rb < float(PEER_TOPK), 1.0, 0.0), axis=0, keepdims=True)
             + jnp.sum(sel, axis=0, keepdims=True))
    return lr, ea, rb, eb, n_sel


def _peer_kernel(h2_ref, x_ref, mod_ref, wqt_ref, keys_ref, u_ref, vt_ref, *rest, n_eb, final):
    fg_ref = rest[0] if final else None
    (o_ref, h2t_ref, lr_ref, ea_ref, rb_ref, eb_ref, ft_ref, a0_ref, a1_ref, w0_ref, w1_ref,
     h2c_ref, flag_ref) = rest[1:] if final else rest
    s = pl.program_id(1)
    n_chunks = PEER_TM // LANE
    i_per = PEER_EB // PEER_NKEYS
    gdt = rb_ref.dtype

    @pl.when(s == 0)
    def _select():
        h2t = h2_ref[...].astype(_F32).T.astype(_MXU)
        h2t_ref[...] = h2t
        ft_ref[...] = jnp.zeros_like(ft_ref)
        a1_ref[...] = jnp.zeros_like(a1_ref)
        w0_ref[...] = jnp.zeros_like(w0_ref)

        for c in range(n_chunks):
            h2c_ref[c] = h2t[:, c * LANE:(c + 1) * LANE]
        for hp in range(2 * PEER_HEADS):
            qt = _dot(wqt_ref[hp * LANE:(hp + 1) * LANE, :], h2t).astype(_MXU)
            st = _dot(keys_ref[hp % 2], qt)
            for c in range(n_chunks):
                (lr_ref if hp % 2 == 0 else ea_ref)[c, hp // 2] = st[:, c * LANE:(c + 1) * LANE]

        def unit(c, h, sa, sb, first_only):
            lr, ea, rb, eb, n_sel = _peer_select(sa, sb, first_only)
            lr_ref[c, h] = lr
            ea_ref[c, h] = ea
            rb_ref[c, h] = rb.astype(gdt)
            eb_ref[c, h] = eb.astype(gdt)
            return jnp.max(n_sel) > 3.0 * PEER_TOPK

        per_trip = PEER_SELECT_UNROLL

        def fast(t, carry):
            c, h0 = t // (PEER_HEADS // per_trip), (t % (PEER_HEADS // per_trip)) * per_trip
            for u in range(per_trip):
                tied = unit(c, h0 + u, lr_ref[c, h0 + u], ea_ref[c, h0 + u], False)
                flag_ref[c * PEER_HEADS + h0 + u] = tied.astype(jnp.int32)
            return carry

        lax.fori_loop(0, n_chunks * PEER_HEADS // per_trip, fast, 0)

        def exact(t, carry):
            @pl.when(flag_ref[t] != 0)
            def _redo():
                c, h = t // PEER_HEADS, t % PEER_HEADS
                w2 = wqt_ref[pl.ds(pl.multiple_of(h * 2 * LANE, 2 * LANE), 2 * LANE), :]
                qt = _dot(w2, h2c_ref[c]).astype(_MXU)
                unit(c, h, _dot(keys_ref[0], qt[:LANE]), _dot(keys_ref[1], qt[LANE:]), True)
            return carry

        lax.fori_loop(0, n_chunks * PEER_HEADS, exact, 0)

    def step(a_new, a_prev, w_new, w_prev):
        blk = jnp.clip(s - 1, 0, n_eb - 1)

        def gate(ii, c):
            i = blk * i_per + ii
            rows = slice(ii * PEER_NKEYS, (ii + 1) * PEER_NKEYS)
            cols = slice(c * LANE, (c + 1) * LANE)
            g = jnp.zeros((PEER_NKEYS, LANE), gdt)
            for h in range(PEER_HEADS):
                lr_i = lr_ref[c, h, pl.ds(i, 1), :].astype(gdt)
                ea_i = ea_ref[c, h, pl.ds(i, 1), :].astype(gdt)
                g = g + jnp.where(rb_ref[c, h] < lr_i, eb_ref[c, h], jnp.zeros((), gdt)) * ea_i
            a = a_prev[rows, cols]
            gelu = 0.5 * a * (1.0 + lax.erf(a * (2.0 ** -0.5)))
            w_new[rows, cols] = gelu.astype(gdt) * g

        units = [(ii, c) for ii in range(i_per) for c in range(n_chunks)]
        kd, mr = MXU_DEPTH, PEER_MXU_ROWS
        d_model = u_ref.shape[1]
        n_pieces = (PEER_EB // mr) * (d_model // kd) + (d_model // mr) * (PEER_EB // kd)
        per = len(units) // n_pieces
        done = 0
        for r0 in range(0, PEER_EB, mr):
            acc = None
            for k0 in range(0, d_model, kd):
                part = _dot(u_ref[r0:r0 + mr, k0:k0 + kd], h2t_ref[k0:k0 + kd, :])
                acc = part if acc is None else acc + part
                for ii, c in units[done:done + per]:
                    gate(ii, c)
                done += per
            a_new[r0:r0 + mr, :] = acc
        for r0 in range(0, d_model, mr):
            acc = ft_ref[r0:r0 + mr, :]
            for k0 in range(0, PEER_EB, kd):
                acc = acc + _dot(vt_ref[0, r0:r0 + mr, k0:k0 + kd], w_prev[k0:k0 + kd, :])
                for ii, c in units[done:done + per]:
                    gate(ii, c)
                done += per
            ft_ref[r0:r0 + mr, :] = acc
        for ii, c in units[done:]:
            gate(ii, c)

    @pl.when(s % 2 == 0)
    def _even():
        step(a0_ref, a1_ref, w1_ref, w0_ref)

    @pl.when(s % 2 == 1)
    def _odd():
        step(a1_ref, a0_ref, w0_ref, w1_ref)

    @pl.when(s == n_eb + 1)
    def _fin():
        x = x_ref[...] + mod_ref[0, 5:6, :] * ft_ref[...].T
        if final:
            x = _rms(x, fg_ref[...])
        o_ref[...] = x


def _peer(h2, x, mod, mod_row, wqt, keys, u, vt, final_g=None):
    n, d = x.shape
    n_exp = u.shape[0]
    n_eb = n_exp // PEER_EB
    tm = PEER_TM
    n_chunks = tm // LANE
    final = final_g is not None
    once = pl.Buffered(1)
    args = [h2, x, mod, wqt, keys, u, vt]
    in_specs = [pl.BlockSpec((tm, d), lambda i, e: (i, 0), pipeline_mode=once),
                pl.BlockSpec((tm, d), lambda i, e: (i, 0), pipeline_mode=once),
                pl.BlockSpec((1, N_MOD, d), lambda i, e: (mod_row(i), 0, 0)),
                pl.BlockSpec(wqt.shape, lambda i, e: (0, 0), pipeline_mode=once),
                pl.BlockSpec(keys.shape, lambda i, e: (0, 0, 0)),
                pl.BlockSpec((PEER_EB, d), lambda i, e: (jnp.minimum(e, n_eb - 1), 0)),
                pl.BlockSpec((1, d, PEER_EB), lambda i, e: (jnp.clip(e - 2, 0, n_eb - 1), 0, 0))]
    if final:
        args.append(final_g)
        in_specs.append(pl.BlockSpec((1, d), lambda i, e: (0, 0)))
    tab32 = pltpu.VMEM((n_chunks, PEER_HEADS, PEER_NKEYS, LANE), _F32)
    tab16 = pltpu.VMEM((n_chunks, PEER_HEADS, PEER_NKEYS, LANE), _MXU)
    abuf = pltpu.VMEM((PEER_EB, tm), _F32)
    wbuf = pltpu.VMEM((PEER_EB, tm), _MXU)
    return pl.pallas_call(
        functools.partial(_peer_kernel, n_eb=n_eb, final=final),
        grid=(n // tm, n_eb + 2),
        in_specs=in_specs,
        out_specs=pl.BlockSpec((tm, d), lambda i, e: (i, 0)),
        out_shape=jax.ShapeDtypeStruct((n, d), _F32),
        scratch_shapes=[pltpu.VMEM((d, tm), _MXU), tab32, tab32, tab16, tab16,
                        pltpu.VMEM((d, tm), _F32), abuf, abuf, wbuf, wbuf,
                        pltpu.VMEM((n_chunks, d, LANE), _MXU), pltpu.SMEM((n_chunks * PEER_HEADS,), jnp.int32)],
        compiler_params=_cparams(("parallel", "arbitrary")),
        name="peer_ffn",
    )(*args)


def _dup_halves(w):
    a, b = w[:, :HEAD_DIM], w[:, HEAD_DIM:]
    return jnp.concatenate([a, a, b, b], axis=1)


def _prep_ab(w_in):
    aq, ak, av = w_in[:, 0:512], w_in[:, 512:1024], w_in[:, 1024:1536]
    bq, bk, bv = w_in[:, 1536:2048], w_in[:, 2048:2176], w_in[:, 2176:2304]
    cat = [aq, ak, av, bq, _rot_cols(bq, HEAD_DIM), _dup_halves(bk), _dup_halves(_rot_cols(bk, HEAD_DIM)),
           _dup_halves(bv)]
    return jnp.concatenate(cat, axis=1).astype(_MXU)


def _prep_cd(w_in, w_uq, w_ukv):
    d = w_in.shape[0]
    cq, ckv, kr = w_in[:, 0:256], w_in[:, 256:384], w_in[:, 384:416]
    dq, dk, dv = w_in[:, 416:928], w_in[:, 928:1440], w_in[:, 1440:1952]
    z64, z32 = jnp.zeros((d, 64), _F32), jnp.zeros((d, 32), _F32)
    kr128 = jnp.concatenate([z64, kr, z32], axis=1)
    krrot128 = jnp.concatenate([z64, _rot_cols(kr, MLA_ROPE), z32], axis=1)
    w = jnp.concatenate([cq, ckv, kr128, krrot128, dq, _rot_cols(dq, HEAD_DIM), dk, _rot_cols(dk, HEAD_DIM), dv],
                        axis=1).astype(_MXU)
    r = w_uq.shape[0]
    uq = w_uq.reshape(r, 8, MLA_NOPE + MLA_ROPE)
    nope, rope = uq[:, :, :MLA_NOPE], uq[:, :, MLA_NOPE:]
    rope_rot = _rot_cols(rope.reshape(r, 8 * MLA_ROPE), MLA_ROPE).reshape(r, 8, MLA_ROPE)
    zq = jnp.zeros((r, 8, 32), _F32)
    wqa = jnp.concatenate([nope, rope, zq], axis=2).reshape(r, 1024).astype(_MXU)
    wqb = jnp.concatenate([jnp.zeros_like(nope), rope_rot, zq], axis=2).reshape(r, 1024).astype(_MXU)
    rk = w_ukv.shape[0]
    ukv = w_ukv.reshape(rk, 8, 128)
    wk = jnp.concatenate([ukv[:, :, :MLA_NOPE], jnp.zeros((rk, 8, 64), _F32)], axis=2).reshape(rk, 1024).astype(_MXU)
    wv = ukv[:, :, MLA_NOPE:].reshape(rk, 512).astype(_MXU)
    lane = jnp.arange(LANE)
    src = (lane >= MLA_NOPE) & (lane < MLA_NOPE + MLA_ROPE)
    place = (src[:, None] & (lane[:, None] == (jnp.arange(1024)[None, :] % LANE))).astype(_MXU)
    return w, wqa, wqb, wk, wv, place


def kernel(x, c, ctx, c_ctx, ada_w, ada_b, norm1_g, norm2_g, w_out, peer_wq, peer_keys, peer_u, peer_v,
           ab_w_in, na_rpb, swa_sink, cd_w_in, mla_q_norm_g, mla_w_uq, mla_kv_norm_g, mla_w_ukv,
           diff_lambda, diff_subln_g, final_norm_g):
    batch, seq, d = x.shape
    ctx_len = ctx.shape[1]
    depth = ada_w.shape[0]
    assert seq % TM == 0 and (batch * ctx_len) % TM == 0 and seq % TQ_LOCAL == 0
    assert seq % PEER_TM == 0 and (batch * ctx_len) % PEER_TM == 0
    assert depth == 2, "even layers keep a context stream, the single odd layer is the last one"
    n_lat, n_ctx = batch * seq, batch * ctx_len
    xs = x.reshape(n_lat, d)
    cs = ctx.reshape(n_ctx, d)

    mod_rows = -(-(batch + 1) // 16) * 16
    cc = jnp.zeros((mod_rows, d), _F32).at[:batch].set(c).at[batch].set(c_ctx)
    mod_all = _modulation(cc, ada_w, ada_b)

    tiles_per_seq = seq // TM
    lat_row = lambda i: i // tiles_per_seq
    ctx_row = lambda i: batch
    lat_tab = lambda i: i % tiles_per_seq
    ctx_tab = lambda i: tiles_per_seq

    cos64, sin64 = _rope_tables(seq, HEAD_DIM)
    cos64p, sin64p = _pad_table(cos64, LANE, 1.0), _pad_table(sin64, LANE, 0.0)

    for l in range(depth):
        last = l == depth - 1
        j = l // 2
        mod = mod_all[l, :batch + 1].reshape(batch + 1, N_MOD, d)
        g1 = norm1_g[l].reshape(1, d)
        g2 = norm2_g[l].reshape(1, d)
        wo = w_out[l].astype(_MXU)
        wqt = peer_wq[l].T.astype(_MXU)
        keys = peer_keys[l].astype(_MXU)
        u = peer_u[l].astype(_MXU)
        vt = peer_v[l].astype(_MXU).reshape(-1, PEER_EB, d).transpose(0, 2, 1)
        if l % 2 == 0:
            w = _prep_ab(ab_w_in[j])
            aq, ak, av, bq, bk2, bv2 = _proj_ab(xs, mod, lat_row, g1, cos64p, sin64p, lat_tab, w)
            caq, cak, cav, cbq, cbk2, cbv2 = _proj_ab(cs, mod, ctx_row, g1, cos64p, sin64p, ctx_tab, w)
            bias = _na_bias_tables(na_rpb[j], seq // GRID_W)
            ya = _na_attention(aq, ak, av, cak, cav, bias, batch, seq, ctx_len)
            sink = swa_sink[j].astype(_F32)
            yb = _swa_attention(sink, bq, bk2, bv2, cbk2, cbv2, batch, seq, ctx_len)
            xs, h2 = _out_proj(xs, ya, yb, mod, lat_row, g2, wo[:512], wo[512:])
            if not last:
                pair_heads = [(m, half, m, m) for m in range(4) for half in range(2)]
                pair_outs = [("pair", 2 * m, 2 * m + 1) for m in range(4)]
                yca = _flash_attention(caq, cak, cav, heads=pair_heads, outs=pair_outs, batch=batch,
                                       q_per_batch=ctx_len, kv_per_batch=ctx_len, tq=ctx_len, tk=ctx_len,
                                       name="ctx_attn_a")
                gqa_heads = [(m, half, m // 2, m // 2) for m in range(4) for half in range(2)]
                ycb = _flash_attention(cbq, cbk2, cbv2, heads=gqa_heads, outs=pair_outs, batch=batch,
                                       q_per_batch=ctx_len, kv_per_batch=ctx_len, tq=ctx_len, tk=ctx_len,
                                       sink=sink, name="ctx_attn_b")
                cs, h2c = _out_proj(cs, yca, ycb, mod, ctx_row, g2, wo[:512], wo[512:])
        else:
            lam_init = 0.8 - 0.6 * math.exp(-0.3 * l)
            w, wqa, wqb, wk, wv, place = _prep_cd(cd_w_in[j], mla_w_uq[j], mla_w_ukv[j])
            cos32, sin32 = _rope_tables(seq, MLA_ROPE)
            ones64, zeros64 = jnp.ones((seq, 64), _F32), jnp.zeros((seq, 64), _F32)
            cosq = _pad_table(jnp.concatenate([ones64, cos32, ones64[:, :32]], 1), LANE, 1.0)
            sinq = _pad_table(jnp.concatenate([zeros64, sin32, zeros64[:, :32]], 1), LANE, 0.0)
            tabs = (cosq, sinq, cos64p, sin64p)
            qg = mla_q_norm_g[j].reshape(1, -1)
            kvg = mla_kv_norm_g[j].reshape(1, -1)
            qmt, km, vmt, dqt, dk, dvt = _proj_cd(xs, mod, lat_row, g1, tabs, lat_tab, w, qg, kvg, wqa, wqb, wk, wv, place)
            _, ckm, cvmt, _, cdk, cdvt = _proj_cd(cs, mod, ctx_row, g1, tabs, ctx_tab, w, qg, kvg, wqa, wqb, wk, wv, place)
            mla_heads = [(h, None, h, h // 2) for h in range(8)]
            pair_outs = [("pair", 2 * m, 2 * m + 1) for m in range(4)]
            yc = _flasht_attention(qmt, km, vmt, ckm, cvmt, heads=mla_heads, outs=pair_outs, batch=batch, seq=seq,
                                   ctx_len=ctx_len, tq=GLOBAL_TQ, tk=GLOBAL_TK, name="mla_attn")
            diff_heads = [(h, a, h, h) for h in range(4) for a in range(2)]
            diff_outs = [("diff", 2 * h, 2 * h + 1) for h in range(4)]
            od = _flasht_attention(dqt, dk, dvt, cdk, cdvt, heads=diff_heads, outs=diff_outs, batch=batch, seq=seq,
                                   ctx_len=ctx_len, tq=GLOBAL_TQ, tk=GLOBAL_TK,
                                   diff=(diff_lambda[j].astype(_F32), diff_subln_g[j].reshape(1, -1), lam_init),
                                   name="diff_attn")
            xs, h2 = _out_proj(xs, yc, od, mod, lat_row, g2, wo[:512], wo[512:])
        fg = final_norm_g.reshape(1, d) if last else None
        xs = _peer(h2, xs, mod, lambda i: i // (seq // PEER_TM), wqt, keys, u, vt, final_g=fg)
        if not last:
            cs = _peer(h2c, cs, mod, ctx_row, wqt, keys, u, vt)
    return xs.reshape(batch, seq, d)
```

```python
import functools
import math

import jax
import jax.numpy as jnp
from jax import lax
from jax.experimental import pallas as pl
from jax.experimental.pallas import tpu as pltpu

_F32 = jnp.float32
_MXU = jnp.bfloat16
_NEG = -1e30
_LOG2E = math.log2(math.e)

GRID_W = 64
HEAD_DIM = 64
ROPE_BASE = 10000.0
RMS_EPS = 1e-6
N_MOD = 6
NA_KR, NA_KC = 8, 16
SWA_WINDOW = 128
MLA_NOPE, MLA_ROPE = 64, 32
PEER_HEADS, PEER_NKEYS, PEER_TOPK = 8, 128, 16

LANE = 128
TM = 512
TQ_LOCAL = 256
NA_GROUP_ROWS = TQ_LOCAL // GRID_W
NA_WIN_ROWS = NA_KR + NA_GROUP_ROWS - 1
GLOBAL_TQ = 512
GLOBAL_TK = 1024
PEER_TM = 1024
PEER_EB = 512
MXU_DEPTH = 256
PEER_MXU_ROWS = 512
PEER_SELECT_UNROLL = 8
VMEM_LIMIT = 56 * 1024 * 1024


def _cparams(sem, vmem=VMEM_LIMIT):
    return pltpu.CompilerParams(dimension_semantics=sem, vmem_limit_bytes=vmem)


def _dot(a, b):
    return jnp.dot(a, b, preferred_element_type=_F32)


def _dot_nt(a, b):
    return lax.dot_general(a, b, (((1,), (1,)), ((), ())), preferred_element_type=_F32)


def _rms(x, g):
    return x * lax.rsqrt(jnp.mean(x * x, axis=-1, keepdims=True) + RMS_EPS) * g


def _lane_half(shape):
    return lax.broadcasted_iota(jnp.int32, shape, len(shape) - 1) >= (LANE // 2)


def _mod_kernel(c_ref, w_ref, b_ref, o_ref):
    c = c_ref[...]
    a = c * jax.nn.sigmoid(c)
    w = w_ref[0]
    a_hi = a.astype(_MXU)
    a_lo = (a - a_hi.astype(_F32)).astype(_MXU)
    w_hi = w.astype(_MXU)
    w_lo = (w - w_hi.astype(_F32)).astype(_MXU)
    o_ref[0] = _dot(a_hi, w_hi) + _dot(a_lo, w_hi) + _dot(a_hi, w_lo) + b_ref[0]


def _modulation(cc, ada_w, ada_b):
    depth, d, n = ada_w.shape
    rows = cc.shape[0]
    tn = 768
    return pl.pallas_call(
        _mod_kernel,
        grid=(depth, n // tn),
        in_specs=[pl.BlockSpec((rows, d), lambda l, j: (0, 0)),
                  pl.BlockSpec((1, d, tn), lambda l, j: (l, 0, j)),
                  pl.BlockSpec((1, 1, tn), lambda l, j: (l, 0, j))],
        out_specs=pl.BlockSpec((1, rows, tn), lambda l, j: (l, 0, j)),
        out_shape=jax.ShapeDtypeStruct((depth, rows, n), _F32),
        compiler_params=_cparams(("parallel", "parallel")),
        name="adaln_mod",
    )(cc, ada_w, ada_b.reshape(depth, 1, n))


def _rope_tables(seq, d):
    t = jnp.arange(seq, dtype=jnp.int32)
    q = d // 4
    freq = ROPE_BASE ** (-jnp.arange(q, dtype=_F32) / q)

    def one(pos):
        ang = pos.astype(_F32)[:, None] * freq[None, :]
        return jnp.concatenate([jnp.cos(ang)] * 2, -1), jnp.concatenate([jnp.sin(ang)] * 2, -1)

    cr, sr = one(t // GRID_W)
    cc, sc = one(t % GRID_W)
    return jnp.concatenate([cr, cc], -1), jnp.concatenate([sr, sc], -1)


def _rot_cols(w, d):
    k, n = w.shape
    w5 = w.reshape(k, n // d, 2, 2, d // 4)
    return jnp.stack([-w5[:, :, :, 1], w5[:, :, :, 0]], axis=3).reshape(k, n)


def _pad_table(tab, width, ident):
    s, w = tab.shape
    if w < width:
        reps = width // w
        tab = jnp.tile(tab, (1, reps))
    return jnp.concatenate([tab, jnp.full((TM, width), ident, _F32)], axis=0)


def _proj_ab_kernel(x_ref, mod_ref, g_ref, cos_ref, sin_ref, w_ref,
                    aq_ref, ak_ref, av_ref, bq_ref, bk_ref, bv_ref):
    x = x_ref[...]
    h = _rms(x, g_ref[...]) * (1.0 + mod_ref[0, 1:2, :]) + mod_ref[0, 0:1, :]
    hb = h.astype(_MXU)
    cos = cos_ref[...]
    sin = sin_ref[...]

    def proj(lo, n):
        return _dot(hb, w_ref[:, lo:lo + n])

    scale = HEAD_DIM ** -0.5
    aq_ref[...] = (proj(0, 512) * scale).astype(aq_ref.dtype)
    ak_ref[...] = proj(512, 512).astype(ak_ref.dtype)
    av_ref[...] = proj(1024, 512).astype(av_ref.dtype)
    cos4 = jnp.tile(cos, (1, 4))
    sin4 = jnp.tile(sin, (1, 4))
    bq_ref[...] = ((proj(1536, 512) * cos4 + proj(2048, 512) * sin4) * scale).astype(bq_ref.dtype)
    cos2 = jnp.tile(cos, (1, 2))
    sin2 = jnp.tile(sin, (1, 2))
    bk_ref[...] = (proj(2560, 256) * cos2 + proj(2816, 256) * sin2).astype(bk_ref.dtype)
    bv_ref[...] = proj(3072, 256).astype(bv_ref.dtype)


def _proj_ab(x, mod, mod_row, g, cos, sin, tab_row, w):
    n, d = x.shape
    widths = (512, 512, 512, 512, 256, 256)
    return pl.pallas_call(
        _proj_ab_kernel,
        grid=(n // TM,),
        in_specs=[pl.BlockSpec((TM, d), lambda i: (i, 0)),
                  pl.BlockSpec((1, N_MOD, d), lambda i: (mod_row(i), 0, 0)),
                  pl.BlockSpec((1, d), lambda i: (0, 0)),
                  pl.BlockSpec((TM, LANE), lambda i: (tab_row(i), 0)),
                  pl.BlockSpec((TM, LANE), lambda i: (tab_row(i), 0)),
                  pl.BlockSpec(w.shape, lambda i: (0, 0))],
        out_specs=[pl.BlockSpec((TM, wd), lambda i: (i, 0)) for wd in widths],
        out_shape=[jax.ShapeDtypeStruct((n, wd), _MXU) for wd in widths],
        compiler_params=_cparams(("parallel",)),
        name="proj_ab",
    )(x, mod, g, cos, sin, w)


def _na_kernel(q_ref, k_ref, v_ref, kc_ref, vc_ref, bias_ref, o_ref, *, rows):
    g = pl.program_id(1)
    ks = jnp.clip(NA_GROUP_ROWS * g - NA_KR // 2, 0, rows - NA_WIN_ROWS)
    start = pl.multiple_of(ks * GRID_W, GRID_W)
    nwin = NA_WIN_ROWS * GRID_W
    hi = _lane_half((TQ_LOCAL, LANE))

    def scores(h):
        cols = slice((h // 2) * LANE, (h // 2 + 1) * LANE)
        q2 = q_ref[:, cols]
        qh = jnp.where(hi == (h % 2 == 1), q2, jnp.zeros_like(q2))
        return _dot_nt(qh, k_ref[pl.ds(start, nwin), cols]) + bias_ref[0, h], _dot_nt(qh, kc_ref[:, cols])

    outs = []
    nxt = scores(0)
    for h in range(8):
        s_lat, s_ctx = nxt
        if h + 1 < 8:
            nxt = scores(h + 1)
        cols = slice((h // 2) * LANE, (h // 2 + 1) * LANE)
        mx = jnp.maximum(jnp.max(s_lat, axis=-1, keepdims=True), jnp.max(s_ctx, axis=-1, keepdims=True))
        p_lat = jnp.exp(s_lat - mx)
        p_ctx = jnp.exp(s_ctx - mx)
        den = jnp.sum(p_lat, axis=-1, keepdims=True) + jnp.sum(p_ctx, axis=-1, keepdims=True)
        o = _dot(p_lat.astype(_MXU), v_ref[pl.ds(start, nwin), cols]) + _dot(p_ctx.astype(_MXU), vc_ref[:, cols])
        outs.append(o / den)
        if h % 2 == 1:
            o_ref[:, cols] = jnp.where(hi, outs[h], outs[h - 1]).astype(o_ref.dtype)


def _na_bias_tables(rpb, rows):
    gq = NA_GROUP_ROWS
    n_groups = rows // gq
    ql = jnp.arange(gq)[:, None]
    kl = jnp.arange(NA_WIN_ROWS)[None, :]
    qc = jnp.arange(GRID_W)[:, None]
    kc = jnp.arange(GRID_W)[None, :]
    cs = jnp.clip(qc - NA_KC // 2, 0, GRID_W - NA_KC)
    col_valid = (kc >= cs) & (kc < cs + NA_KC)
    col_idx = jnp.clip(kc - qc + NA_KC - 1, 0, 2 * NA_KC - 2)
    exact = lax.Precision.HIGHEST
    oh_c = jax.nn.one_hot(col_idx, 2 * NA_KC - 1, dtype=_F32)
    rpb_cols = jnp.einsum("hab,xyb->haxy", rpb.astype(_F32), oh_c, precision=exact)
    tabs = []
    for g in (0, 1, n_groups - 1):
        ks = min(max(gq * g - NA_KR // 2, 0), rows - NA_WIN_ROWS)
        rq = gq * g + ql
        rk = ks + kl
        r0 = jnp.clip(rq - NA_KR // 2, 0, rows - NA_KR)
        row_valid = (rk >= r0) & (rk < r0 + NA_KR)
        row_idx = jnp.clip(rk - rq + NA_KR - 1, 0, 2 * NA_KR - 2)
        oh_r = jax.nn.one_hot(row_idx, 2 * NA_KR - 1, dtype=_F32)
        b = jnp.einsum("qka,haxy->hqxky", oh_r, rpb_cols, precision=exact)
        valid = row_valid[:, None, :, None] & col_valid[None, :, None, :]
        b = jnp.where(valid[None], b, _NEG)
        tabs.append(b.reshape(rpb.shape[0], gq * GRID_W, NA_WIN_ROWS * GRID_W))
    return jnp.stack(tabs)


def _na_attention(aq, ak, av, cak, cav, bias, batch, seq, ctx_len):
    rows = seq // GRID_W
    n_groups = seq // TQ_LOCAL
    nwin = NA_WIN_ROWS * GRID_W

    def bias_row(b, g):
        return (jnp.where(g == 0, 0, jnp.where(g == n_groups - 1, 2, 1)), 0, 0, 0)

    return pl.pallas_call(
        functools.partial(_na_kernel, rows=rows),
        grid=(batch, n_groups),
        in_specs=[pl.BlockSpec((TQ_LOCAL, 512), lambda b, g: (b * n_groups + g, 0)),
                  pl.BlockSpec((seq, 512), lambda b, g: (b, 0)),
                  pl.BlockSpec((seq, 512), lambda b, g: (b, 0)),
                  pl.BlockSpec((ctx_len, 512), lambda b, g: (b, 0)),
                  pl.BlockSpec((ctx_len, 512), lambda b, g: (b, 0)),
                  pl.BlockSpec((1, 8, TQ_LOCAL, nwin), bias_row)],
        out_specs=pl.BlockSpec((TQ_LOCAL, 512), lambda b, g: (b * n_groups + g, 0)),
        out_shape=jax.ShapeDtypeStruct((batch * seq, 512), _MXU),
        compiler_params=_cparams(("parallel", "arbitrary")),
        name="na_attention",
    )(aq, ak, av, cak, cav, bias)


def _swa_kernel(sink_ref, q_ref, k_ref, v_ref, kc_ref, vc_ref, o_ref, *, seq):
    t = pl.program_id(1)
    kwin = TQ_LOCAL + 2 * SWA_WINDOW
    start = t * TQ_LOCAL
    kstart = pl.multiple_of(jnp.clip(start - SWA_WINDOW, 0, seq - kwin), LANE)
    qpos = start + lax.broadcasted_iota(jnp.int32, (TQ_LOCAL, kwin), 0)
    kpos = kstart + lax.broadcasted_iota(jnp.int32, (TQ_LOCAL, kwin), 1)
    mask = jnp.where(jnp.abs(kpos - qpos) <= SWA_WINDOW, 0.0, _NEG).astype(_F32)
    hi = _lane_half((TQ_LOCAL, LANE))

    def scores(h):
        cols = slice((h // 2) * LANE, (h // 2 + 1) * LANE)
        kcols = slice((h // 4) * LANE, (h // 4 + 1) * LANE)
        q2 = q_ref[:, cols]
        qh = jnp.where(hi == (h % 2 == 1), q2, jnp.zeros_like(q2))
        return _dot_nt(qh, k_ref[pl.ds(kstart, kwin), kcols]) + mask, _dot_nt(qh, kc_ref[:, kcols])

    outs = []
    nxt = scores(0)
    for h in range(8):
        s_lat, s_ctx = nxt
        if h + 1 < 8:
            nxt = scores(h + 1)
        cols = slice((h // 2) * LANE, (h // 2 + 1) * LANE)
        kcols = slice((h // 4) * LANE, (h // 4 + 1) * LANE)
        sink = sink_ref[h]
        mx = jnp.maximum(jnp.max(s_lat, axis=-1, keepdims=True), jnp.max(s_ctx, axis=-1, keepdims=True))
        mx = jnp.maximum(mx, sink)
        p_lat = jnp.exp(s_lat - mx)
        p_ctx = jnp.exp(s_ctx - mx)
        den = (jnp.sum(p_lat, axis=-1, keepdims=True) + jnp.sum(p_ctx, axis=-1, keepdims=True)
               + jnp.exp(sink - mx))
        o = (_dot(p_lat.astype(_MXU), v_ref[pl.ds(kstart, kwin), kcols])
             + _dot(p_ctx.astype(_MXU), vc_ref[:, kcols]))
        outs.append(o / den)
        if h % 2 == 1:
            o_ref[:, cols] = jnp.where(hi, outs[h], outs[h - 1]).astype(o_ref.dtype)


def _swa_attention(sink, bq, bk2, bv2, cbk2, cbv2, batch, seq, ctx_len):
    n_t = seq // TQ_LOCAL
    return pl.pallas_call(
        functools.partial(_swa_kernel, seq=seq),
        grid=(batch, n_t),
        in_specs=[pl.BlockSpec(memory_space=pltpu.SMEM),
                  pl.BlockSpec((TQ_LOCAL, 512), lambda b, t: (b * n_t + t, 0)),
                  pl.BlockSpec((seq, 256), lambda b, t: (b, 0)),
                  pl.BlockSpec((seq, 256), lambda b, t: (b, 0)),
                  pl.BlockSpec((ctx_len, 256), lambda b, t: (b, 0)),
                  pl.BlockSpec((ctx_len, 256), lambda b, t: (b, 0))],
        out_specs=pl.BlockSpec((TQ_LOCAL, 512), lambda b, t: (b * n_t + t, 0)),
        out_shape=jax.ShapeDtypeStruct((batch * seq, 512), _MXU),
        compiler_params=_cparams(("parallel", "arbitrary")),
        name="swa_attention",
    )(sink, bq, bk2, bv2, cbk2, cbv2)


def _flash_kernel(*refs, heads, outs, has_ctx, has_sink, diff_cfg, n_kv):
    it = iter(refs)
    sink_ref = next(it) if has_sink else None
    q_ref, k_ref, v_ref = next(it), next(it), next(it)
    kc_ref = next(it) if has_ctx else None
    vc_ref = next(it) if has_ctx else None
    lam_ref = next(it) if diff_cfg else None
    sg_ref = next(it) if diff_cfg else None
    o_ref = next(it)
    m_ref, l_ref, acc_ref = next(it), next(it), next(it)
    kv = pl.program_id(2)
    tq = q_ref.shape[0]
    hi = _lane_half((tq, LANE))

    @pl.when(kv == 0)
    def _init():
        for h in range(len(heads)):
            if has_sink:
                m_ref[h] = jnp.full((tq, LANE), sink_ref[h], _F32)
                l_ref[h] = jnp.ones((tq, LANE), _F32)
            else:
                m_ref[h] = jnp.full((tq, LANE), _NEG, _F32)
                l_ref[h] = jnp.zeros((tq, LANE), _F32)
            acc_ref[h] = jnp.zeros((tq, LANE), _F32)

    def attend(kr, vr):
        for h, (qb, qhalf, kb, vb) in enumerate(heads):
            q2 = q_ref[:, qb * LANE:(qb + 1) * LANE]
            if qhalf is not None:
                q2 = jnp.where(hi == (qhalf == 1), q2, jnp.zeros_like(q2))
            s = _dot_nt(q2, kr[:, kb * LANE:(kb + 1) * LANE])
            m_old = m_ref[h][:, :1]
            m_new = jnp.maximum(m_old, jnp.max(s, axis=-1, keepdims=True))
            alpha = jnp.exp(m_old - m_new)
            p = jnp.exp(s - m_new)
            l_ref[h] = jnp.broadcast_to(alpha * l_ref[h][:, :1] + jnp.sum(p, axis=-1, keepdims=True), (tq, LANE))
            acc_ref[h] = alpha * acc_ref[h] + _dot(p.astype(_MXU), vr[:, vb * LANE:(vb + 1) * LANE])
            m_ref[h] = jnp.broadcast_to(m_new, (tq, LANE))

    if has_ctx:
        @pl.when(kv == 0)
        def _ctx():
            attend(kc_ref, vc_ref)

    attend(k_ref, v_ref)

    @pl.when(kv == n_kv - 1)
    def _fin():
        def head_out(h):
            return acc_ref[h] / l_ref[h][:, :1]

        for j, spec in enumerate(outs):
            if spec[0] == "full":
                o = head_out(spec[1])
            elif spec[0] == "pair":
                o = jnp.where(hi, head_out(spec[2]), head_out(spec[1]))
            else:
                lv = lam_ref[...]
                lam = (jnp.exp(jnp.sum(lv[0:1] * lv[1:2], axis=-1, keepdims=True))
                       - jnp.exp(jnp.sum(lv[2:3] * lv[3:4], axis=-1, keepdims=True)) + diff_cfg)
                o = _rms(head_out(spec[1]) - lam * head_out(spec[2]), sg_ref[...]) * (1.0 - diff_cfg)
            o_ref[:, j * LANE:(j + 1) * LANE] = o.astype(o_ref.dtype)


def _flash_attention(q, k, v, *, heads, outs, batch, q_per_batch, kv_per_batch, tq, tk,
                     q_row0=0, kv_row0=0, ctx=None, ctx_len=0, ctx_row0=0, sink=None, diff=None, name="flash"):
    n_q = q_per_batch // tq
    n_kv = kv_per_batch // tk
    qw, kw, vw = q.shape[1], k.shape[1], v.shape[1]
    args, in_specs = [], []
    if sink is not None:
        args.append(sink)
        in_specs.append(pl.BlockSpec(memory_space=pltpu.SMEM))
    q0, k0 = q_row0 // tq, kv_row0 // tk
    args += [q, k, v]
    in_specs += [pl.BlockSpec((tq, qw), lambda b, i, j: (q0 + b * n_q + i, 0)),
                 pl.BlockSpec((tk, kw), lambda b, i, j: (k0 + b * n_kv + j, 0)),
                 pl.BlockSpec((tk, vw), lambda b, i, j: (k0 + b * n_kv + j, 0))]
    if ctx is not None:
        c0 = ctx_row0 // ctx_len
        args += [ctx[0], ctx[1]]
        in_specs += [pl.BlockSpec((ctx_len, kw), lambda b, i, j: (c0 + b, 0)),
                     pl.BlockSpec((ctx_len, vw), lambda b, i, j: (c0 + b, 0))]
    diff_cfg = None
    if diff is not None:
        lam_vecs, subln_g, diff_cfg = diff
        args += [lam_vecs, subln_g]
        in_specs += [pl.BlockSpec(lam_vecs.shape, lambda b, i, j: (0, 0)),
                     pl.BlockSpec(subln_g.shape, lambda b, i, j: (0, 0))]
    nh = len(heads)
    ow = len(outs) * LANE
    return pl.pallas_call(
        functools.partial(_flash_kernel, heads=tuple(heads), outs=tuple(outs), has_ctx=ctx is not None,
                          has_sink=sink is not None, diff_cfg=diff_cfg, n_kv=n_kv),
        grid=(batch, n_q, n_kv),
        in_specs=in_specs,
        out_specs=pl.BlockSpec((tq, ow), lambda b, i, j: (b * n_q + i, 0)),
        out_shape=jax.ShapeDtypeStruct((batch * q_per_batch, ow), _MXU),
        scratch_shapes=[pltpu.VMEM((nh, tq, LANE), _F32)] * 3,
        compiler_params=_cparams(("parallel", "parallel", "arbitrary")),
        name=name,
    )(*args)


def _flasht_kernel(*refs, heads, outs, diff_cfg, n_kv):
    it = iter(refs)
    qt_ref, k_ref, vt_ref, kc_ref, vct_ref = next(it), next(it), next(it), next(it), next(it)
    lam_ref = next(it) if diff_cfg else None
    sg_ref = next(it) if diff_cfg else None
    o_ref = next(it)
    m_ref, l_ref, acc_ref = next(it), next(it), next(it)
    kv = pl.program_id(2)
    tq = qt_ref.shape[1]
    row_hi = lax.broadcasted_iota(jnp.int32, (LANE, tq), 0) >= (LANE // 2)

    @pl.when(kv == 0)
    def _init():
        m_ref[...] = jnp.full(m_ref.shape, _NEG, _F32)
        l_ref[...] = jnp.zeros(l_ref.shape, _F32)
        acc_ref[...] = jnp.zeros(acc_ref.shape, _F32)

    def attend(kr, vtr):
        def scores(h):
            qb, qhalf, kb, _ = heads[h]
            qt = qt_ref[qb * LANE:(qb + 1) * LANE, :]
            if qhalf is not None:
                qt = jnp.where(row_hi == (qhalf == 1), qt, jnp.zeros_like(qt))
            return _dot(kr[:, kb * LANE:(kb + 1) * LANE], qt)

        s_next = scores(0)
        for h in range(len(heads)):
            s = s_next
            if h + 1 < len(heads):
                s_next = scores(h + 1)
            vb = heads[h][3]
            m_old = m_ref[h, 0:1, :]
            m_new = jnp.maximum(m_old, jnp.max(s, axis=0, keepdims=True))
            alpha = jnp.exp2(m_old - m_new)
            p = jnp.exp2(s - m_new)
            l_new = alpha * l_ref[h, 0:1, :] + jnp.sum(p, axis=0, keepdims=True)
            acc_ref[h] = alpha * acc_ref[h] + _dot(vtr[vb * LANE:(vb + 1) * LANE, :], p.astype(_MXU))
            l_ref[h] = jnp.broadcast_to(l_new, (8, tq))
            m_ref[h] = jnp.broadcast_to(m_new, (8, tq))

    @pl.when(kv == 0)
    def _ctx():
        attend(kc_ref, vct_ref)

    attend(k_ref, vt_ref)

    @pl.when(kv == n_kv - 1)
    def _fin():
        def head_out(h):
            return acc_ref[h] / l_ref[h, 0:1, :]

        for j, spec in enumerate(outs):
            if spec[0] == "full":
                o = head_out(spec[1]).T
            elif spec[0] == "pair":
                o = jnp.where(row_hi, head_out(spec[2]), head_out(spec[1])).T
            else:
                lv = lam_ref[...]
                lam = (jnp.exp(jnp.sum(lv[0:1] * lv[1:2], axis=-1, keepdims=True))
                       - jnp.exp(jnp.sum(lv[2:3] * lv[3:4], axis=-1, keepdims=True)) + diff_cfg)
                o = _rms((head_out(spec[1]) - lam * head_out(spec[2])).T, sg_ref[...]) * (1.0 - diff_cfg)
            o_ref[:, j * LANE:(j + 1) * LANE] = o.astype(o_ref.dtype)


def _flasht_attention(qt, k, vt, kc, vct, *, heads, outs, batch, seq, ctx_len, tq, tk, diff=None, name="flasht"):
    n_q, n_kv = seq // tq, seq // tk
    qw, kw, vw = qt.shape[0], k.shape[1], vt.shape[0]
    args = [qt, k, vt, kc, vct]
    in_specs = [pl.BlockSpec((qw, tq), lambda b, i, j: (0, b * n_q + i)),
                pl.BlockSpec((tk, kw), lambda b, i, j: (b * n_kv + j, 0)),
                pl.BlockSpec((vw, tk), lambda b, i, j: (0, b * n_kv + j)),
                pl.BlockSpec((ctx_len, kw), lambda b, i, j: (b, 0)),
                pl.BlockSpec((vw, ctx_len), lambda b, i, j: (0, b))]
    diff_cfg = None
    if diff is not None:
        lam_vecs, subln_g, diff_cfg = diff
        args += [lam_vecs, subln_g]
        in_specs += [pl.BlockSpec(lam_vecs.shape, lambda b, i, j: (0, 0)),
                     pl.BlockSpec(subln_g.shape, lambda b, i, j: (0, 0))]
    nh = len(heads)
    ow = len(outs) * LANE
    return pl.pallas_call(
        functools.partial(_flasht_kernel, heads=tuple(heads), outs=tuple(outs), diff_cfg=diff_cfg, n_kv=n_kv),
        grid=(batch, n_q, n_kv),
        in_specs=in_specs,
        out_specs=pl.BlockSpec((tq, ow), lambda b, i, j: (b * n_q + i, 0)),
        out_shape=jax.ShapeDtypeStruct((batch * seq, ow), _MXU),
        scratch_shapes=[pltpu.VMEM((nh, 8, tq), _F32), pltpu.VMEM((nh, 8, tq), _F32),
                        pltpu.VMEM((nh, LANE, tq), _F32)],
        compiler_params=_cparams(("parallel", "parallel", "arbitrary")),
        name=name,
    )(*args)


def _out_kernel(x_ref, ya_ref, yb_ref, mod_ref, g2_ref, wa_ref, wb_ref, xo_ref, h2_ref):
    y = _dot(ya_ref[...], wa_ref[...]) + _dot(yb_ref[...], wb_ref[...])
    x = x_ref[...] + mod_ref[0, 2:3, :] * y
    xo_ref[...] = x
    h2 = _rms(x, g2_ref[...]) * (1.0 + mod_ref[0, 4:5, :]) + mod_ref[0, 3:4, :]
    h2_ref[...] = h2.astype(h2_ref.dtype)


def _out_proj(x, ya, yb, mod, mod_row, g2, wa, wb):
    n, d = x.shape
    return pl.pallas_call(
        _out_kernel,
        grid=(n // TM,),
        in_specs=[pl.BlockSpec((TM, d), lambda i: (i, 0)),
                  pl.BlockSpec((TM, ya.shape[1]), lambda i: (i, 0)),
                  pl.BlockSpec((TM, yb.shape[1]), lambda i: (i, 0)),
                  pl.BlockSpec((1, N_MOD, d), lambda i: (mod_row(i), 0, 0)),
                  pl.BlockSpec((1, d), lambda i: (0, 0)),
                  pl.BlockSpec(wa.shape, lambda i: (0, 0)),
                  pl.BlockSpec(wb.shape, lambda i: (0, 0))],
        out_specs=[pl.BlockSpec((TM, d), lambda i: (i, 0)), pl.BlockSpec((TM, d), lambda i: (i, 0))],
        out_shape=[jax.ShapeDtypeStruct((n, d), _F32), jax.ShapeDtypeStruct((n, d), _MXU)],
        compiler_params=_cparams(("parallel",)),
        name="out_proj",
    )(x, ya, yb, mod, g2, wa, wb)


def _proj_cd_kernel(x_ref, mod_ref, g_ref, cq_ref, sq_ref, cd_ref, sd_ref, w_ref, qg_ref, kvg_ref,
                    wqa_ref, wqb_ref, wk_ref, wv_ref, place_ref,
                    qm_ref, km_ref, vm_ref, dq_ref, dk_ref, dv_ref):
    x = x_ref[...]
    h = _rms(x, g_ref[...]) * (1.0 + mod_ref[0, 1:2, :]) + mod_ref[0, 0:1, :]
    hb = h.astype(_MXU)

    def proj(lo, n):
        return _dot(hb, w_ref[:, lo:lo + n])

    cosq, sinq = cq_ref[...], sq_ref[...]
    cosd, sind = cd_ref[...], sd_ref[...]
    cqn = _rms(proj(0, 256), qg_ref[...]).astype(_MXU)
    ckvn = _rms(proj(256, 128), kvg_ref[...]).astype(_MXU)
    cos8, sin8 = jnp.tile(cosq, (1, 8)), jnp.tile(sinq, (1, 8))
    qm = _dot(cqn, wqa_ref[...]) * cos8 + _dot(cqn, wqb_ref[...]) * sin8
    qm_ref[...] = (qm * ((MLA_NOPE + MLA_ROPE) ** -0.5 * _LOG2E)).T.astype(qm_ref.dtype)
    kr = (proj(384, 128) * cosq + proj(512, 128) * sinq).astype(_MXU)
    km_ref[...] = (_dot(ckvn, wk_ref[...]) + _dot(kr, place_ref[...])).astype(km_ref.dtype)
    vm_ref[...] = _dot(ckvn, wv_ref[...]).T.astype(vm_ref.dtype)
    cos4, sin4 = jnp.tile(cosd, (1, 4)), jnp.tile(sind, (1, 4))
    dq_ref[...] = ((proj(640, 512) * cos4 + proj(1152, 512) * sin4) * (HEAD_DIM ** -0.5 * _LOG2E)).T.astype(dq_ref.dtype)
    dk_ref[...] = (proj(1664, 512) * cos4 + proj(2176, 512) * sin4).astype(dk_ref.dtype)
    dv_ref[...] = proj(2688, 512).T.astype(dv_ref.dtype)


def _proj_cd(x, mod, mod_row, g, tabs, tab_row, w, qg, kvg, wqa, wqb, wk, wv, place):
    n, d = x.shape
    outs = ((1024, True), (1024, False), (512, True), (512, True), (512, False), (512, True))
    full = lambda a: pl.BlockSpec(a.shape, lambda i: (0, 0))
    return pl.pallas_call(
        _proj_cd_kernel,
        grid=(n // TM,),
        in_specs=[pl.BlockSpec((TM, d), lambda i: (i, 0)),
                  pl.BlockSpec((1, N_MOD, d), lambda i: (mod_row(i), 0, 0)),
                  pl.BlockSpec((1, d), lambda i: (0, 0))]
                 + [pl.BlockSpec((TM, LANE), lambda i: (tab_row(i), 0))] * 4
                 + [full(a) for a in (w, qg, kvg, wqa, wqb, wk, wv, place)],
        out_specs=[pl.BlockSpec((wd, TM), lambda i: (0, i)) if fm else pl.BlockSpec((TM, wd), lambda i: (i, 0))
                   for wd, fm in outs],
        out_shape=[jax.ShapeDtypeStruct((wd, n) if fm else (n, wd), _MXU) for wd, fm in outs],
        compiler_params=_cparams(("parallel",)),
        name="proj_cd",
    )(x, mod, g, *tabs, w, qg, kvg, wqa, wqb, wk, wv, place)


_MARK = 2.0 ** 121


def _extract16(x, first_only):
    r = x.shape[0]
    row = lax.broadcasted_iota(jnp.int32, x.shape, 0)
    row16 = lax.broadcasted_iota(jnp.int32, (PEER_TOPK, LANE), 0)
    vals = jnp.zeros((PEER_TOPK, LANE), _F32)
    for k in range(PEER_TOPK):
        m = jnp.max(x, axis=0, keepdims=True)
        hit = x == m
        if first_only:
            hit = row == jnp.min(jnp.where(hit, row, r), axis=0, keepdims=True)
        x = jnp.where(hit, -_MARK * (32 + k), x)
        vals = jnp.where(row16 == k, m, vals)
    order = jnp.where(x < -16.0 * _MARK, x * (-1.0 / _MARK) - 32.0, float(PEER_TOPK))
    return order, vals


def _staircase_rows():
    groups = []
    for r in range(PEER_TOPK // 2):
        n = PEER_TOPK // (r + 1)
        for c0 in range(0, n, 8):
            groups.append((r, c0, min(8, n - c0)))
    groups.append((None, 0, 8))
    return groups


def _peer_select(sa, sb, first_only):
    ra, av = _extract16(sa, first_only)
    rb, bv = _extract16(sb, first_only)
    groups = _staircase_rows()
    sub = lax.broadcasted_iota(jnp.int32, (8, LANE), 0)
    pieces = []
    for r, c0, nv in groups:
        if r is None:
            piece = av[8:16] + bv[0:1]
        else:
            piece = av[r:r + 1] + bv[c0:c0 + 8]
            if nv < 8:
                piece = jnp.where(sub < nv, piece, -jnp.inf)
        pieces.append(piece)
    cand = jnp.concatenate(pieces, axis=0)
    e_cand = jnp.exp(cand - (av[0:1] + bv[0:1]))
    sel, _ = _extract16(cand, first_only)
    sel = jnp.where(sel < float(PEER_TOPK), 1.0, 0.0)
    z = jnp.sum(sel * e_cand, axis=0, keepdims=True)
    lr = jnp.zeros(sa.shape, _F32)
    for g, (r, c0, nv) in enumerate(groups):
        blk = sel[8 * g:8 * g + 8]
        if r is None:
            for q in range(8):
                lr = jnp.where(ra == float(8 + q), blk[q:q + 1], lr)
        elif c0 == 0:
            cnt = jnp.sum(blk, axis=0, keepdims=True)
            if PEER_TOPK // (r + 1) > 8:
                cnt = cnt + jnp.sum(sel[8 * g + 8:8 * g + 16], axis=0, keepdims=True)
            lr = jnp.where(ra == float(r), cnt, lr)
    ea = jnp.exp(sa - av[0:1]) / z
    eb = jnp.exp(sb - bv[0:1])
    n_sel = (jnp.sum(jnp.where(ra < float(PEER_TOPK), 1.0, 0.0), axis=0, keepdims=True)
             + jnp.sum(jnp.where(rb < float(PEER_TOPK), 1.0, 0.0), axis=0, keepdims=True)
             + jnp.sum(sel, axis=0, keepdims=True))
    return lr, ea, rb, eb, n_sel


def _peer_kernel(h2_ref, x_ref, mod_ref, wqt_ref, keys_ref, u_ref, vt_ref, *rest, n_eb, final):
    fg_ref = rest[0] if final else None
    (o_ref, h2t_ref, lr_ref, ea_ref, rb_ref, eb_ref, ft_ref, a0_ref, a1_ref, w0_ref, w1_ref,
     h2c_ref, flag_ref) = rest[1:] if final else rest
    s = pl.program_id(1)
    n_chunks = PEER_TM // LANE
    i_per = PEER_EB // PEER_NKEYS
    gdt = rb_ref.dtype

    @pl.when(s == 0)
    def _select():
        h2t = h2_ref[...].astype(_F32).T.astype(_MXU)
        h2t_ref[...] = h2t
        ft_ref[...] = jnp.zeros_like(ft_ref)
        a1_ref[...] = jnp.zeros_like(a1_ref)
        w0_ref[...] = jnp.zeros_like(w0_ref)

        for c in range(n_chunks):
            h2c_ref[c] = h2t[:, c * LANE:(c + 1) * LANE]
        for hp in range(2 * PEER_HEADS):
            qt = _dot(wqt_ref[hp * LANE:(hp + 1) * LANE, :], h2t).astype(_MXU)
            st = _dot(keys_ref[hp % 2], qt)
            for c in range(n_chunks):
                (lr_ref if hp % 2 == 0 else ea_ref)[c, hp // 2] = st[:, c * LANE:(c + 1) * LANE]

        def unit(c, h, sa, sb, first_only):
            lr, ea, rb, eb, n_sel = _peer_select(sa, sb, first_only)
            lr_ref[c, h] = lr
            ea_ref[c, h] = ea
            rb_ref[c, h] = rb.astype(gdt)
            eb_ref[c, h] = eb.astype(gdt)
            return jnp.max(n_sel) > 3.0 * PEER_TOPK

        per_trip = PEER_SELECT_UNROLL

        def fast(t, carry):
            c, h0 = t // (PEER_HEADS // per_trip), (t % (PEER_HEADS // per_trip)) * per_trip
            for u in range(per_trip):
                tied = unit(c, h0 + u, lr_ref[c, h0 + u], ea_ref[c, h0 + u], False)
                flag_ref[c * PEER_HEADS + h0 + u] = tied.astype(jnp.int32)
            return carry

        lax.fori_loop(0, n_chunks * PEER_HEADS // per_trip, fast, 0)

        def exact(t, carry):
            @pl.when(flag_ref[t] != 0)
            def _redo():
                c, h = t // PEER_HEADS, t % PEER_HEADS
                w2 = wqt_ref[pl.ds(pl.multiple_of(h * 2 * LANE, 2 * LANE), 2 * LANE), :]
                qt = _dot(w2, h2c_ref[c]).astype(_MXU)
                unit(c, h, _dot(keys_ref[0], qt[:LANE]), _dot(keys_ref[1], qt[LANE:]), True)
            return carry

        lax.fori_loop(0, n_chunks * PEER_HEADS, exact, 0)

    def step(a_new, a_prev, w_new, w_prev):
        blk = jnp.clip(s - 1, 0, n_eb - 1)

        def gate(ii, c):
            i = blk * i_per + ii
            rows = slice(ii * PEER_NKEYS, (ii + 1) * PEER_NKEYS)
            cols = slice(c * LANE, (c + 1) * LANE)
            g = jnp.zeros((PEER_NKEYS, LANE), gdt)
            for h in range(PEER_HEADS):
                lr_i = lr_ref[c, h, pl.ds(i, 1), :].astype(gdt)
                ea_i = ea_ref[c, h, pl.ds(i, 1), :].astype(gdt)
                g = g + jnp.where(rb_ref[c, h] < lr_i, eb_ref[c, h], jnp.zeros((), gdt)) * ea_i
            a = a_prev[rows, cols]
            gelu = 0.5 * a * (1.0 + lax.erf(a * (2.0 ** -0.5)))
            w_new[rows, cols] = gelu.astype(gdt) * g

        units = [(ii, c) for ii in range(i_per) for c in range(n_chunks)]
        kd, mr = MXU_DEPTH, PEER_MXU_ROWS
        d_model = u_ref.shape[1]
        n_pieces = (PEER_EB // mr) * (d_model // kd) + (d_model // mr) * (PEER_EB // kd)
        per = len(units) // n_pieces
        done = 0
        for r0 in range(0, PEER_EB, mr):
            acc = None
            for k0 in range(0, d_model, kd):
                part = _dot(u_ref[r0:r0 + mr, k0:k0 + kd], h2t_ref[k0:k0 + kd, :])
                acc = part if acc is None else acc + part
                for ii, c in units[done:done + per]:
                    gate(ii, c)
                done += per
            a_new[r0:r0 + mr, :] = acc
        for r0 in range(0, d_model, mr):
            acc = ft_ref[r0:r0 + mr, :]
            for k0 in range(0, PEER_EB, kd):
                acc = acc + _dot(vt_ref[0, r0:r0 + mr, k0:k0 + kd], w_prev[k0:k0 + kd, :])
                for ii, c in units[done:done + per]:
                    gate(ii, c)
                done += per
            ft_ref[r0:r0 + mr, :] = acc
        for ii, c in units[done:]:
            gate(ii, c)

    @pl.when(s % 2 == 0)
    def _even():
        step(a0_ref, a1_ref, w1_ref, w0_ref)

    @pl.when(s % 2 == 1)
    def _odd():
        step(a1_ref, a0_ref, w0_ref, w1_ref)

    @pl.when(s == n_eb + 1)
    def _fin():
        x = x_ref[...] + mod_ref[0, 5:6, :] * ft_ref[...].T
        if final:
            x = _rms(x, fg_ref[...])
        o_ref[...] = x


def _peer(h2, x, mod, mod_row, wqt, keys, u, vt, final_g=None):
    n, d = x.shape
    n_exp = u.shape[0]
    n_eb = n_exp // PEER_EB
    tm = PEER_TM
    n_chunks = tm // LANE
    final = final_g is not None
    once = pl.Buffered(1)
    args = [h2, x, mod, wqt, keys, u, vt]
    in_specs = [pl.BlockSpec((tm, d), lambda i, e: (i, 0), pipeline_mode=once),
                pl.BlockSpec((tm, d), lambda i, e: (i, 0), pipeline_mode=once),
                pl.BlockSpec((1, N_MOD, d), lambda i, e: (mod_row(i), 0, 0)),
                pl.BlockSpec(wqt.shape, lambda i, e: (0, 0), pipeline_mode=once),
                pl.BlockSpec(keys.shape, lambda i, e: (0, 0, 0)),
                pl.BlockSpec((PEER_EB, d), lambda i, e: (jnp.minimum(e, n_eb - 1), 0)),
                pl.BlockSpec((1, d, PEER_EB), lambda i, e: (jnp.clip(e - 2, 0, n_eb - 1), 0, 0))]
    if final:
        args.append(final_g)
        in_specs.append(pl.BlockSpec((1, d), lambda i, e: (0, 0)))
    tab32 = pltpu.VMEM((n_chunks, PEER_HEADS, PEER_NKEYS, LANE), _F32)
    tab16 = pltpu.VMEM((n_chunks, PEER_HEADS, PEER_NKEYS, LANE), _MXU)
    abuf = pltpu.VMEM((PEER_EB, tm), _F32)
    wbuf = pltpu.VMEM((PEER_EB, tm), _MXU)
    return pl.pallas_call(
        functools.partial(_peer_kernel, n_eb=n_eb, final=final),
        grid=(n // tm, n_eb + 2),
        in_specs=in_specs,
        out_specs=pl.BlockSpec((tm, d), lambda i, e: (i, 0)),
        out_shape=jax.ShapeDtypeStruct((n, d), _F32),
        scratch_shapes=[pltpu.VMEM((d, tm), _MXU), tab32, tab32, tab16, tab16,
                        pltpu.VMEM((d, tm), _F32), abuf, abuf, wbuf, wbuf,
                        pltpu.VMEM((n_chunks, d, LANE), _MXU), pltpu.SMEM((n_chunks * PEER_HEADS,), jnp.int32)],
        compiler_params=_cparams(("parallel", "arbitrary")),
        name="peer_ffn",
    )(*args)


def _dup_halves(w):
    a, b = w[:, :HEAD_DIM], w[:, HEAD_DIM:]
    return jnp.concatenate([a, a, b, b], axis=1)


def _prep_ab(w_in):
    aq, ak, av = w_in[:, 0:512], w_in[:, 512:1024], w_in[:, 1024:1536]
    bq, bk, bv = w_in[:, 1536:2048], w_in[:, 2048:2176], w_in[:, 2176:2304]
    cat = [aq, ak, av, bq, _rot_cols(bq, HEAD_DIM), _dup_halves(bk), _dup_halves(_rot_cols(bk, HEAD_DIM)),
           _dup_halves(bv)]
    return jnp.concatenate(cat, axis=1).astype(_MXU)


def _prep_cd(w_in, w_uq, w_ukv):
    d = w_in.shape[0]
    cq, ckv, kr = w_in[:, 0:256], w_in[:, 256:384], w_in[:, 384:416]
    dq, dk, dv = w_in[:, 416:928], w_in[:, 928:1440], w_in[:, 1440:1952]
    z64, z32 = jnp.zeros((d, 64), _F32), jnp.zeros((d, 32), _F32)
    kr128 = jnp.concatenate([z64, kr, z32], axis=1)
    krrot128 = jnp.concatenate([z64, _rot_cols(kr, MLA_ROPE), z32], axis=1)
    w = jnp.concatenate([cq, ckv, kr128, krrot128, dq, _rot_cols(dq, HEAD_DIM), dk, _rot_cols(dk, HEAD_DIM), dv],
                        axis=1).astype(_MXU)
    r = w_uq.shape[0]
    uq = w_uq.reshape(r, 8, MLA_NOPE + MLA_ROPE)
    nope, rope = uq[:, :, :MLA_NOPE], uq[:, :, MLA_NOPE:]
    rope_rot = _rot_cols(rope.reshape(r, 8 * MLA_ROPE), MLA_ROPE).reshape(r, 8, MLA_ROPE)
    zq = jnp.zeros((r, 8, 32), _F32)
    wqa = jnp.concatenate([nope, rope, zq], axis=2).reshape(r, 1024).astype(_MXU)
    wqb = jnp.concatenate([jnp.zeros_like(nope), rope_rot, zq], axis=2).reshape(r, 1024).astype(_MXU)
    rk = w_ukv.shape[0]
    ukv = w_ukv.reshape(rk, 8, 128)
    wk = jnp.concatenate([ukv[:, :, :MLA_NOPE], jnp.zeros((rk, 8, 64), _F32)], axis=2).reshape(rk, 1024).astype(_MXU)
    wv = ukv[:, :, MLA_NOPE:].reshape(rk, 512).astype(_MXU)
    lane = jnp.arange(LANE)
    src = (lane >= MLA_NOPE) & (lane < MLA_NOPE + MLA_ROPE)
    place = (src[:, None] & (lane[:, None] == (jnp.arange(1024)[None, :] % LANE))).astype(_MXU)
    return w, wqa, wqb, wk, wv, place


def kernel(x, c, ctx, c_ctx, ada_w, ada_b, norm1_g, norm2_g, w_out, peer_wq, peer_keys, peer_u, peer_v,
           ab_w_in, na_rpb, swa_sink, cd_w_in, mla_q_norm_g, mla_w_uq, mla_kv_norm_g, mla_w_ukv,
           diff_lambda, diff_subln_g, final_norm_g):
    batch, seq, d = x.shape
    ctx_len = ctx.shape[1]
    depth = ada_w.shape[0]
    assert seq % TM == 0 and (batch * ctx_len) % TM == 0 and seq % TQ_LOCAL == 0
    assert seq % PEER_TM == 0 and (batch * ctx_len) % PEER_TM == 0
    assert depth == 2, "even layers keep a context stream, the single odd layer is the last one"
    n_lat, n_ctx = batch * seq, batch * ctx_len
    xs = x.reshape(n_lat, d)
    cs = ctx.reshape(n_ctx, d)

    mod_rows = -(-(batch + 1) // 16) * 16
    cc = jnp.zeros((mod_rows, d), _F32).at[:batch].set(c).at[batch].set(c_ctx)
    mod_all = _modulation(cc, ada_w, ada_b)

    tiles_per_seq = seq // TM
    lat_row = lambda i: i // tiles_per_seq
    ctx_row = lambda i: batch
    lat_tab = lambda i: i % tiles_per_seq
    ctx_tab = lambda i: tiles_per_seq

    cos64, sin64 = _rope_tables(seq, HEAD_DIM)
    cos64p, sin64p = _pad_table(cos64, LANE, 1.0), _pad_table(sin64, LANE, 0.0)

    for l in range(depth):
        last = l == depth - 1
        j = l // 2
        mod = mod_all[l, :batch + 1].reshape(batch + 1, N_MOD, d)
        g1 = norm1_g[l].reshape(1, d)
        g2 = norm2_g[l].reshape(1, d)
        wo = w_out[l].astype(_MXU)
        wqt = peer_wq[l].T.astype(_MXU)
        keys = peer_keys[l].astype(_MXU)
        u = peer_u[l].astype(_MXU)
        vt = peer_v[l].astype(_MXU).reshape(-1, PEER_EB, d).transpose(0, 2, 1)
        if l % 2 == 0:
            w = _prep_ab(ab_w_in[j])
            aq, ak, av, bq, bk2, bv2 = _proj_ab(xs, mod, lat_row, g1, cos64p, sin64p, lat_tab, w)
            caq, cak, cav, cbq, cbk2, cbv2 = _proj_ab(cs, mod, ctx_row, g1, cos64p, sin64p, ctx_tab, w)
            bias = _na_bias_tables(na_rpb[j], seq // GRID_W)
            ya = _na_attention(aq, ak, av, cak, cav, bias, batch, seq, ctx_len)
            sink = swa_sink[j].astype(_F32)
            yb = _swa_attention(sink, bq, bk2, bv2, cbk2, cbv2, batch, seq, ctx_len)
            xs, h2 = _out_proj(xs, ya, yb, mod, lat_row, g2, wo[:512], wo[512:])
            if not last:
                pair_heads = [(m, half, m, m) for m in range(4) for half in range(2)]
                pair_outs = [("pair", 2 * m, 2 * m + 1) for m in range(4)]
                yca = _flash_attention(caq, cak, cav, heads=pair_heads, outs=pair_outs, batch=batch,
                                       q_per_batch=ctx_len, kv_per_batch=ctx_len, tq=ctx_len, tk=ctx_len,
                                       name="ctx_attn_a")
                gqa_heads = [(m, half, m // 2, m // 2) for m in range(4) for half in range(2)]
                ycb = _flash_attention(cbq, cbk2, cbv2, heads=gqa_heads, outs=pair_outs, batch=batch,
                                       q_per_batch=ctx_len, kv_per_batch=ctx_len, tq=ctx_len, tk=ctx_len,
                                       sink=sink, name="ctx_attn_b")
                cs, h2c = _out_proj(cs, yca, ycb, mod, ctx_row, g2, wo[:512], wo[512:])
        else:
            lam_init = 0.8 - 0.6 * math.exp(-0.3 * l)
            w, wqa, wqb, wk, wv, place = _prep_cd(cd_w_in[j], mla_w_uq[j], mla_w_ukv[j])
            cos32, sin32 = _rope_tables(seq, MLA_ROPE)
            ones64, zeros64 = jnp.ones((seq, 64), _F32), jnp.zeros((seq, 64), _F32)
            cosq = _pad_table(jnp.concatenate([ones64, cos32, ones64[:, :32]], 1), LANE, 1.0)
            sinq = _pad_table(jnp.concatenate([zeros64, sin32, zeros64[:, :32]], 1), LANE, 0.0)
            tabs = (cosq, sinq, cos64p, sin64p)
            qg = mla_q_norm_g[j].reshape(1, -1)
            kvg = mla_kv_norm_g[j].reshape(1, -1)
            qmt, km, vmt, dqt, dk, dvt = _proj_cd(xs, mod, lat_row, g1, tabs, lat_tab, w, qg, kvg, wqa, wqb, wk, wv, place)
            _, ckm, cvmt, _, cdk, cdvt = _proj_cd(cs, mod, ctx_row, g1, tabs, ctx_tab, w, qg, kvg, wqa, wqb, wk, wv, place)
            mla_heads = [(h, None, h, h // 2) for h in range(8)]
            pair_outs = [("pair", 2 * m, 2 * m + 1) for m in range(4)]
            yc = _flasht_attention(qmt, km, vmt, ckm, cvmt, heads=mla_heads, outs=pair_outs, batch=batch, seq=seq,
                                   ctx_len=ctx_len, tq=GLOBAL_TQ, tk=GLOBAL_TK, name="mla_attn")
            diff_heads = [(h, a, h, h) for h in range(4) for a in range(2)]
            diff_outs = [("diff", 2 * h, 2 * h + 1) for h in range(4)]
            od = _flasht_attention(dqt, dk, dvt, cdk, cdvt, heads=diff_heads, outs=diff_outs, batch=batch, seq=seq,
                                   ctx_len=ctx_len, tq=GLOBAL_TQ, tk=GLOBAL_TK,
                                   diff=(diff_lambda[j].astype(_F32), diff_subln_g[j].reshape(1, -1), lam_init),
                                   name="diff_attn")
            xs, h2 = _out_proj(xs, yc, od, mod, lat_row, g2, wo[:512], wo[512:])
        fg = final_norm_g.reshape(1, d) if last else None
        xs = _peer(h2, xs, mod, lambda i: i // (seq // PEER_TM), wqt, keys, u, vt, final_g=fg)
        if not last:
            cs = _peer(h2c, cs, mod, ctx_row, wqt, keys, u, vt)
    return xs.reshape(batch, seq, d)
```

```python
import functools
import math

import jax
import jax.numpy as jnp
from jax import lax
from jax.experimental import pallas as pl
from jax.experimental.pallas import tpu as pltpu

_F32 = jnp.float32
_MXU = jnp.bfloat16
_NEG = -1e30
_LOG2E = math.log2(math.e)

GRID_W = 64
HEAD_DIM = 64
ROPE_BASE = 10000.0
RMS_EPS = 1e-6
N_MOD = 6
NA_KR, NA_KC = 8, 16
SWA_WINDOW = 128
MLA_NOPE, MLA_ROPE = 64, 32
PEER_HEADS, PEER_NKEYS, PEER_TOPK = 8, 128, 16

LANE = 128
TM = 512
TQ_LOCAL = 256
NA_GROUP_ROWS = TQ_LOCAL // GRID_W
NA_WIN_ROWS = NA_KR + NA_GROUP_ROWS - 1
GLOBAL_TQ = 512
GLOBAL_TK = 2048
PEER_TM = 1024
PEER_EB = 512
MXU_DEPTH = 256
PEER_MXU_ROWS = 512
PEER_SELECT_UNROLL = 8
VMEM_LIMIT = 56 * 1024 * 1024


def _cparams(sem, vmem=VMEM_LIMIT):
    return pltpu.CompilerParams(dimension_semantics=sem, vmem_limit_bytes=vmem)


def _dot(a, b):
    return jnp.dot(a, b, preferred_element_type=_F32)


def _dot_nt(a, b):
    return lax.dot_general(a, b, (((1,), (1,)), ((), ())), preferred_element_type=_F32)


def _rms(x, g):
    return x * lax.rsqrt(jnp.mean(x * x, axis=-1, keepdims=True) + RMS_EPS) * g


def _lane_half(shape):
    return lax.broadcasted_iota(jnp.int32, shape, len(shape) - 1) >= (LANE // 2)


def _mod_kernel(c_ref, w_ref, b_ref, o_ref):
    c = c_ref[...]
    a = c * jax.nn.sigmoid(c)
    w = w_ref[0]
    a_hi = a.astype(_MXU)
    a_lo = (a - a_hi.astype(_F32)).astype(_MXU)
    w_hi = w.astype(_MXU)
    w_lo = (w - w_hi.astype(_F32)).astype(_MXU)
    o_ref[0] = _dot(a_hi, w_hi) + _dot(a_lo, w_hi) + _dot(a_hi, w_lo) + b_ref[0]


def _modulation(cc, ada_w, ada_b):
    depth, d, n = ada_w.shape
    rows = cc.shape[0]
    tn = 768
    return pl.pallas_call(
        _mod_kernel,
        grid=(depth, n // tn),
        in_specs=[pl.BlockSpec((rows, d), lambda l, j: (0, 0)),
                  pl.BlockSpec((1, d, tn), lambda l, j: (l, 0, j)),
                  pl.BlockSpec((1, 1, tn), lambda l, j: (l, 0, j))],
        out_specs=pl.BlockSpec((1, rows, tn), lambda l, j: (l, 0, j)),
        out_shape=jax.ShapeDtypeStruct((depth, rows, n), _F32),
        compiler_params=_cparams(("parallel", "parallel")),
        name="adaln_mod",
    )(cc, ada_w, ada_b.reshape(depth, 1, n))


def _rope_tables(seq, d):
    t = jnp.arange(seq, dtype=jnp.int32)
    q = d // 4
    freq = ROPE_BASE ** (-jnp.arange(q, dtype=_F32) / q)

    def one(pos):
        ang = pos.astype(_F32)[:, None] * freq[None, :]
        return jnp.concatenate([jnp.cos(ang)] * 2, -1), jnp.concatenate([jnp.sin(ang)] * 2, -1)

    cr, sr = one(t // GRID_W)
    cc, sc = one(t % GRID_W)
    return jnp.concatenate([cr, cc], -1), jnp.concatenate([sr, sc], -1)


def _rot_cols(w, d):
    k, n = w.shape
    w5 = w.reshape(k, n // d, 2, 2, d // 4)
    return jnp.stack([-w5[:, :, :, 1], w5[:, :, :, 0]], axis=3).reshape(k, n)


def _pad_table(tab, width, ident):
    s, w = tab.shape
    if w < width:
        reps = width // w
        tab = jnp.tile(tab, (1, reps))
    return jnp.concatenate([tab, jnp.full((TM, width), ident, _F32)], axis=0)


def _proj_ab_kernel(x_ref, mod_ref, g_ref, cos_ref, sin_ref, w_ref,
                    aq_ref, ak_ref, av_ref, bq_ref, bk_ref, bv_ref):
    x = x_ref[...]
    h = _rms(x, g_ref[...]) * (1.0 + mod_ref[0, 1:2, :]) + mod_ref[0, 0:1, :]
    hb = h.astype(_MXU)
    cos = cos_ref[...]
    sin = sin_ref[...]

    def proj(lo, n):
        return _dot(hb, w_ref[:, lo:lo + n])

    scale = HEAD_DIM ** -0.5
    aq_ref[...] = (proj(0, 512) * scale).astype(aq_ref.dtype)
    ak_ref[...] = proj(512, 512).astype(ak_ref.dtype)
    av_ref[...] = proj(1024, 512).astype(av_ref.dtype)
    cos4 = jnp.tile(cos, (1, 4))
    sin4 = jnp.tile(sin, (1, 4))
    bq_ref[...] = ((proj(1536, 512) * cos4 + proj(2048, 512) * sin4) * scale).astype(bq_ref.dtype)
    cos2 = jnp.tile(cos, (1, 2))
    sin2 = jnp.tile(sin, (1, 2))
    bk_ref[...] = (proj(2560, 256) * cos2 + proj(2816, 256) * sin2).astype(bk_ref.dtype)
    bv_ref[...] = proj(3072, 256).astype(bv_ref.dtype)


def _proj_ab(x, mod, mod_row, g, cos, sin, tab_row, w):
    n, d = x.shape
    widths = (512, 512, 512, 512, 256, 256)
    return pl.pallas_call(
        _proj_ab_kernel,
        grid=(n // TM,),
        in_specs=[pl.BlockSpec((TM, d), lambda i: (i, 0)),
                  pl.BlockSpec((1, N_MOD, d), lambda i: (mod_row(i), 0, 0)),
                  pl.BlockSpec((1, d), lambda i: (0, 0)),
                  pl.BlockSpec((TM, LANE), lambda i: (tab_row(i), 0)),
                  pl.BlockSpec((TM, LANE), lambda i: (tab_row(i), 0)),
                  pl.BlockSpec(w.shape, lambda i: (0, 0))],
        out_specs=[pl.BlockSpec((TM, wd), lambda i: (i, 0)) for wd in widths],
        out_shape=[jax.ShapeDtypeStruct((n, wd), _MXU) for wd in widths],
        compiler_params=_cparams(("parallel",)),
        name="proj_ab",
    )(x, mod, g, cos, sin, w)


def _na_kernel(q_ref, k_ref, v_ref, kc_ref, vc_ref, bias_ref, o_ref, *, rows):
    g = pl.program_id(1)
    ks = jnp.clip(NA_GROUP_ROWS * g - NA_KR // 2, 0, rows - NA_WIN_ROWS)
    start = pl.multiple_of(ks * GRID_W, GRID_W)
    nwin = NA_WIN_ROWS * GRID_W
    hi = _lane_half((TQ_LOCAL, LANE))

    def scores(h):
        cols = slice((h // 2) * LANE, (h // 2 + 1) * LANE)
        q2 = q_ref[:, cols]
        qh = jnp.where(hi == (h % 2 == 1), q2, jnp.zeros_like(q2))
        return _dot_nt(qh, k_ref[pl.ds(start, nwin), cols]) + bias_ref[0, h], _dot_nt(qh, kc_ref[:, cols])

    outs = []
    nxt = scores(0)
    for h in range(8):
        s_lat, s_ctx = nxt
        if h + 1 < 8:
            nxt = scores(h + 1)
        cols = slice((h // 2) * LANE, (h // 2 + 1) * LANE)
        mx = jnp.maximum(jnp.max(s_lat, axis=-1, keepdims=True), jnp.max(s_ctx, axis=-1, keepdims=True))
        p_lat = jnp.exp(s_lat - mx)
        p_ctx = jnp.exp(s_ctx - mx)
        den = jnp.sum(p_lat, axis=-1, keepdims=True) + jnp.sum(p_ctx, axis=-1, keepdims=True)
        o = _dot(p_lat.astype(_MXU), v_ref[pl.ds(start, nwin), cols]) + _dot(p_ctx.astype(_MXU), vc_ref[:, cols])
        outs.append(o / den)
        if h % 2 == 1:
            o_ref[:, cols] = jnp.where(hi, outs[h], outs[h - 1]).astype(o_ref.dtype)


def _na_bias_tables(rpb, rows):
    gq = NA_GROUP_ROWS
    n_groups = rows // gq
    ql = jnp.arange(gq)[:, None]
    kl = jnp.arange(NA_WIN_ROWS)[None, :]
    qc = jnp.arange(GRID_W)[:, None]
    kc = jnp.arange(GRID_W)[None, :]
    cs = jnp.clip(qc - NA_KC // 2, 0, GRID_W - NA_KC)
    col_valid = (kc >= cs) & (kc < cs + NA_KC)
    col_idx = jnp.clip(kc - qc + NA_KC - 1, 0, 2 * NA_KC - 2)
    exact = lax.Precision.HIGHEST
    oh_c = jax.nn.one_hot(col_idx, 2 * NA_KC - 1, dtype=_F32)
    rpb_cols = jnp.einsum("hab,xyb->haxy", rpb.astype(_F32), oh_c, precision=exact)
    tabs = []
    for g in (0, 1, n_groups - 1):
        ks = min(max(gq * g - NA_KR // 2, 0), rows - NA_WIN_ROWS)
        rq = gq * g + ql
        rk = ks + kl
        r0 = jnp.clip(rq - NA_KR // 2, 0, rows - NA_KR)
        row_valid = (rk >= r0) & (rk < r0 + NA_KR)
        row_idx = jnp.clip(rk - rq + NA_KR - 1, 0, 2 * NA_KR - 2)
        oh_r = jax.nn.one_hot(row_idx, 2 * NA_KR - 1, dtype=_F32)
        b = jnp.einsum("qka,haxy->hqxky", oh_r, rpb_cols, precision=exact)
        valid = row_valid[:, None, :, None] & col_valid[None, :, None, :]
        b = jnp.where(valid[None], b, _NEG)
        tabs.append(b.reshape(rpb.shape[0], gq * GRID_W, NA_WIN_ROWS * GRID_W))
    return jnp.stack(tabs)


def _na_attention(aq, ak, av, cak, cav, bias, batch, seq, ctx_len):
    rows = seq // GRID_W
    n_groups = seq // TQ_LOCAL
    nwin = NA_WIN_ROWS * GRID_W

    def bias_row(b, g):
        return (jnp.where(g == 0, 0, jnp.where(g == n_groups - 1, 2, 1)), 0, 0, 0)

    return pl.pallas_call(
        functools.partial(_na_kernel, rows=rows),
        grid=(batch, n_groups),
        in_specs=[pl.BlockSpec((TQ_LOCAL, 512), lambda b, g: (b * n_groups + g, 0)),
                  pl.BlockSpec((seq, 512), lambda b, g: (b, 0)),
                  pl.BlockSpec((seq, 512), lambda b, g: (b, 0)),
                  pl.BlockSpec((ctx_len, 512), lambda b, g: (b, 0)),
                  pl.BlockSpec((ctx_len, 512), lambda b, g: (b, 0)),
                  pl.BlockSpec((1, 8, TQ_LOCAL, nwin), bias_row)],
        out_specs=pl.BlockSpec((TQ_LOCAL, 512), lambda b, g: (b * n_groups + g, 0)),
        out_shape=jax.ShapeDtypeStruct((batch * seq, 512), _MXU),
        compiler_params=_cparams(("parallel", "arbitrary")),
        name="na_attention",
    )(aq, ak, av, cak, cav, bias)


def _swa_kernel(sink_ref, q_ref, k_ref, v_ref, kc_ref, vc_ref, o_ref, *, seq):
    t = pl.program_id(1)
    kwin = TQ_LOCAL + 2 * SWA_WINDOW
    start = t * TQ_LOCAL
    kstart = pl.multiple_of(jnp.clip(start - SWA_WINDOW, 0, seq - kwin), LANE)
    qpos = start + lax.broadcasted_iota(jnp.int32, (TQ_LOCAL, kwin), 0)
    kpos = kstart + lax.broadcasted_iota(jnp.int32, (TQ_LOCAL, kwin), 1)
    mask = jnp.where(jnp.abs(kpos - qpos) <= SWA_WINDOW, 0.0, _NEG).astype(_F32)
    hi = _lane_half((TQ_LOCAL, LANE))

    def scores(h):
        cols = slice((h // 2) * LANE, (h // 2 + 1) * LANE)
        kcols = slice((h // 4) * LANE, (h // 4 + 1) * LANE)
        q2 = q_ref[:, cols]
        qh = jnp.where(hi == (h % 2 == 1), q2, jnp.zeros_like(q2))
        return _dot_nt(qh, k_ref[pl.ds(kstart, kwin), kcols]) + mask, _dot_nt(qh, kc_ref[:, kcols])

    outs = []
    nxt = scores(0)
    for h in range(8):
        s_lat, s_ctx = nxt
        if h + 1 < 8:
            nxt = scores(h + 1)
        cols = slice((h // 2) * LANE, (h // 2 + 1) * LANE)
        kcols = slice((h // 4) * LANE, (h // 4 + 1) * LANE)
        sink = sink_ref[h]
        mx = jnp.maximum(jnp.max(s_lat, axis=-1, keepdims=True), jnp.max(s_ctx, axis=-1, keepdims=True))
        mx = jnp.maximum(mx, sink)
        p_lat = jnp.exp(s_lat - mx)
        p_ctx = jnp.exp(s_ctx - mx)
        den = (jnp.sum(p_lat, axis=-1, keepdims=True) + jnp.sum(p_ctx, axis=-1, keepdims=True)
               + jnp.exp(sink - mx))
        o = (_dot(p_lat.astype(_MXU), v_ref[pl.ds(kstart, kwin), kcols])
             + _dot(p_ctx.astype(_MXU), vc_ref[:, kcols]))
        outs.append(o / den)
        if h % 2 == 1:
            o_ref[:, cols] = jnp.where(hi, outs[h], outs[h - 1]).astype(o_ref.dtype)


def _swa_attention(sink, bq, bk2, bv2, cbk2, cbv2, batch, seq, ctx_len):
    n_t = seq // TQ_LOCAL
    return pl.pallas_call(
        functools.partial(_swa_kernel, seq=seq),
        grid=(batch, n_t),
        in_specs=[pl.BlockSpec(memory_space=pltpu.SMEM),
                  pl.BlockSpec((TQ_LOCAL, 512), lambda b, t: (b * n_t + t, 0)),
                  pl.BlockSpec((seq, 256), lambda b, t: (b, 0)),
                  pl.BlockSpec((seq, 256), lambda b, t: (b, 0)),
                  pl.BlockSpec((ctx_len, 256), lambda b, t: (b, 0)),
                  pl.BlockSpec((ctx_len, 256), lambda b, t: (b, 0))],
        out_specs=pl.BlockSpec((TQ_LOCAL, 512), lambda b, t: (b * n_t + t, 0)),
        out_shape=jax.ShapeDtypeStruct((batch * seq, 512), _MXU),
        compiler_params=_cparams(("parallel", "arbitrary")),
        name="swa_attention",
    )(sink, bq, bk2, bv2, cbk2, cbv2)


def _flash_kernel(*refs, heads, outs, has_ctx, has_sink, diff_cfg, n_kv):
    it = iter(refs)
    sink_ref = next(it) if has_sink else None
    q_ref, k_ref, v_ref = next(it), next(it), next(it)
    kc_ref = next(it) if has_ctx else None
    vc_ref = next(it) if has_ctx else None
    lam_ref = next(it) if diff_cfg else None
    sg_ref = next(it) if diff_cfg else None
    o_ref = next(it)
    m_ref, l_ref, acc_ref = next(it), next(it), next(it)
    kv = pl.program_id(2)
    tq = q_ref.shape[0]
    hi = _lane_half((tq, LANE))

    @pl.when(kv == 0)
    def _init():
        for h in range(len(heads)):
            if has_sink:
                m_ref[h] = jnp.full((tq, LANE), sink_ref[h], _F32)
                l_ref[h] = jnp.ones((tq, LANE), _F32)
            else:
                m_ref[h] = jnp.full((tq, LANE), _NEG, _F32)
                l_ref[h] = jnp.zeros((tq, LANE), _F32)
            acc_ref[h] = jnp.zeros((tq, LANE), _F32)

    def attend(kr, vr):
        for h, (qb, qhalf, kb, vb) in enumerate(heads):
            q2 = q_ref[:, qb * LANE:(qb + 1) * LANE]
            if qhalf is not None:
                q2 = jnp.where(hi == (qhalf == 1), q2, jnp.zeros_like(q2))
            s = _dot_nt(q2, kr[:, kb * LANE:(kb + 1) * LANE])
            m_old = m_ref[h][:, :1]
            m_new = jnp.maximum(m_old, jnp.max(s, axis=-1, keepdims=True))
            alpha = jnp.exp(m_old - m_new)
            p = jnp.exp(s - m_new)
            l_ref[h] = jnp.broadcast_to(alpha * l_ref[h][:, :1] + jnp.sum(p, axis=-1, keepdims=True), (tq, LANE))
            acc_ref[h] = alpha * acc_ref[h] + _dot(p.astype(_MXU), vr[:, vb * LANE:(vb + 1) * LANE])
            m_ref[h] = jnp.broadcast_to(m_new, (tq, LANE))

    if has_ctx:
        @pl.when(kv == 0)
        def _ctx():
            attend(kc_ref, vc_ref)

    attend(k_ref, v_ref)

    @pl.when(kv == n_kv - 1)
    def _fin():
        def head_out(h):
            return acc_ref[h] / l_ref[h][:, :1]

        for j, spec in enumerate(outs):
            if spec[0] == "full":
                o = head_out(spec[1])
            elif spec[0] == "pair":
                o = jnp.where(hi, head_out(spec[2]), head_out(spec[1]))
            else:
                lv = lam_ref[...]
                lam = (jnp.exp(jnp.sum(lv[0:1] * lv[1:2], axis=-1, keepdims=True))
                       - jnp.exp(jnp.sum(lv[2:3] * lv[3:4], axis=-1, keepdims=True)) + diff_cfg)
                o = _rms(head_out(spec[1]) - lam * head_out(spec[2]), sg_ref[...]) * (1.0 - diff_cfg)
            o_ref[:, j * LANE:(j + 1) * LANE] = o.astype(o_ref.dtype)


def _flash_attention(q, k, v, *, heads, outs, batch, q_per_batch, kv_per_batch, tq, tk,
                     q_row0=0, kv_row0=0, ctx=None, ctx_len=0, ctx_row0=0, sink=None, diff=None, name="flash"):
    n_q = q_per_batch // tq
    n_kv = kv_per_batch // tk
    qw, kw, vw = q.shape[1], k.shape[1], v.shape[1]
    args, in_specs = [], []
    if sink is not None:
        args.append(sink)
        in_specs.append(pl.BlockSpec(memory_space=pltpu.SMEM))
    q0, k0 = q_row0 // tq, kv_row0 // tk
    args += [q, k, v]
    in_specs += [pl.BlockSpec((tq, qw), lambda b, i, j: (q0 + b * n_q + i, 0)),
                 pl.BlockSpec((tk, kw), lambda b, i, j: (k0 + b * n_kv + j, 0)),
                 pl.BlockSpec((tk, vw), lambda b, i, j: (k0 + b * n_kv + j, 0))]
    if ctx is not None:
        c0 = ctx_row0 // ctx_len
        args += [ctx[0], ctx[1]]
        in_specs += [pl.BlockSpec((ctx_len, kw), lambda b, i, j: (c0 + b, 0)),
                     pl.BlockSpec((ctx_len, vw), lambda b, i, j: (c0 + b, 0))]
    diff_cfg = None
    if diff is not None:
        lam_vecs, subln_g, diff_cfg = diff
        args += [lam_vecs, subln_g]
        in_specs += [pl.BlockSpec(lam_vecs.shape, lambda b, i, j: (0, 0)),
                     pl.BlockSpec(subln_g.shape, lambda b, i, j: (0, 0))]
    nh = len(heads)
    ow = len(outs) * LANE
    return pl.pallas_call(
        functools.partial(_flash_kernel, heads=tuple(heads), outs=tuple(outs), has_ctx=ctx is not None,
                          has_sink=sink is not None, diff_cfg=diff_cfg, n_kv=n_kv),
        grid=(batch, n_q, n_kv),
        in_specs=in_specs,
        out_specs=pl.BlockSpec((tq, ow), lambda b, i, j: (b * n_q + i, 0)),
        out_shape=jax.ShapeDtypeStruct((batch * q_per_batch, ow), _MXU),
        scratch_shapes=[pltpu.VMEM((nh, tq, LANE), _F32)] * 3,
        compiler_params=_cparams(("parallel", "parallel", "arbitrary")),
        name=name,
    )(*args)


def _flasht_kernel(*refs, heads, outs, diff_cfg, n_kv):
    it = iter(refs)
    qt_ref, k_ref, vt_ref, kc_ref, vct_ref = next(it), next(it), next(it), next(it), next(it)
    lam_ref = next(it) if diff_cfg else None
    sg_ref = next(it) if diff_cfg else None
    o_ref = next(it)
    m_ref, l_ref, acc_ref = next(it), next(it), next(it)
    kv = pl.program_id(2)
    tq = qt_ref.shape[1]
    row_hi = lax.broadcasted_iota(jnp.int32, (LANE, tq), 0) >= (LANE // 2)

    @pl.when(kv == 0)
    def _init():
        m_ref[...] = jnp.full(m_ref.shape, _NEG, _F32)
        l_ref[...] = jnp.zeros(l_ref.shape, _F32)
        acc_ref[...] = jnp.zeros(acc_ref.shape, _F32)

    def attend(kr, vtr):
        def scores(h):
            qb, qhalf, kb, _ = heads[h]
            qt = qt_ref[qb * LANE:(qb + 1) * LANE, :]
            if qhalf is not None:
                qt = jnp.where(row_hi == (qhalf == 1), qt, jnp.zeros_like(qt))
            return _dot(kr[:, kb * LANE:(kb + 1) * LANE], qt)

        s_next = scores(0)
        for h in range(len(heads)):
            s = s_next
            if h + 1 < len(heads):
                s_next = scores(h + 1)
            vb = heads[h][3]
            m_old = m_ref[h, 0:1, :]
            m_new = jnp.maximum(m_old, jnp.max(s, axis=0, keepdims=True))
            alpha = jnp.exp2(m_old - m_new)
            p = jnp.exp2(s - m_new)
            l_new = alpha * l_ref[h, 0:1, :] + jnp.sum(p, axis=0, keepdims=True)
            acc_ref[h] = alpha * acc_ref[h] + _dot(vtr[vb * LANE:(vb + 1) * LANE, :], p.astype(_MXU))
            l_ref[h] = jnp.broadcast_to(l_new, (8, tq))
            m_ref[h] = jnp.broadcast_to(m_new, (8, tq))

    @pl.when(kv == 0)
    def _ctx():
        attend(kc_ref, vct_ref)

    attend(k_ref, vt_ref)

    @pl.when(kv == n_kv - 1)
    def _fin():
        def head_out(h):
            return acc_ref[h] / l_ref[h, 0:1, :]

        for j, spec in enumerate(outs):
            if spec[0] == "full":
                o = head_out(spec[1]).T
            elif spec[0] == "pair":
                o = jnp.where(row_hi, head_out(spec[2]), head_out(spec[1])).T
            else:
                lv = lam_ref[...]
                lam = (jnp.exp(jnp.sum(lv[0:1] * lv[1:2], axis=-1, keepdims=True))
                       - jnp.exp(jnp.sum(lv[2:3] * lv[3:4], axis=-1, keepdims=True)) + diff_cfg)
                o = _rms((head_out(spec[1]) - lam * head_out(spec[2])).T, sg_ref[...]) * (1.0 - diff_cfg)
            o_ref[:, j * LANE:(j + 1) * LANE] = o.astype(o_ref.dtype)


def _flasht_attention(qt, k, vt, kc, vct, *, heads, outs, batch, seq, ctx_len, tq, tk, diff=None, name="flasht"):
    n_q, n_kv = seq // tq, seq // tk
    qw, kw, vw = qt.shape[0], k.shape[1], vt.shape[0]
    args = [qt, k, vt, kc, vct]
    in_specs = [pl.BlockSpec((qw, tq), lambda b, i, j: (0, b * n_q + i)),
                pl.BlockSpec((tk, kw), lambda b, i, j: (b * n_kv + j, 0)),
                pl.BlockSpec((vw, tk), lambda b, i, j: (0, b * n_kv + j)),
                pl.BlockSpec((ctx_len, kw), lambda b, i, j: (b, 0)),
                pl.BlockSpec((vw, ctx_len), lambda b, i, j: (0, b))]
    diff_cfg = None
    if diff is not None:
        lam_vecs, subln_g, diff_cfg = diff
        args += [lam_vecs, subln_g]
        in_specs += [pl.BlockSpec(lam_vecs.shape, lambda b, i, j: (0, 0)),
                     pl.BlockSpec(subln_g.shape, lambda b, i, j: (0, 0))]
    nh = len(heads)
    ow = len(outs) * LANE
    return pl.pallas_call(
        functools.partial(_flasht_kernel, heads=tuple(heads), outs=tuple(outs), diff_cfg=diff_cfg, n_kv=n_kv),
        grid=(batch, n_q, n_kv),
        in_specs=in_specs,
        out_specs=pl.BlockSpec((tq, ow), lambda b, i, j: (b * n_q + i, 0)),
        out_shape=jax.ShapeDtypeStruct((batch * seq, ow), _MXU),
        scratch_shapes=[pltpu.VMEM((nh, 8, tq), _F32), pltpu.VMEM((nh, 8, tq), _F32),
                        pltpu.VMEM((nh, LANE, tq), _F32)],
        compiler_params=_cparams(("parallel", "parallel", "arbitrary")),
        name=name,
    )(*args)


def _out_kernel(x_ref, ya_ref, yb_ref, mod_ref, g2_ref, wa_ref, wb_ref, xo_ref, h2_ref):
    y = _dot(ya_ref[...], wa_ref[...]) + _dot(yb_ref[...], wb_ref[...])
    x = x_ref[...] + mod_ref[0, 2:3, :] * y
    xo_ref[...] = x
    h2 = _rms(x, g2_ref[...]) * (1.0 + mod_ref[0, 4:5, :]) + mod_ref[0, 3:4, :]
    h2_ref[...] = h2.astype(h2_ref.dtype)


def _out_proj(x, ya, yb, mod, mod_row, g2, wa, wb):
    n, d = x.shape
    return pl.pallas_call(
        _out_kernel,
        grid=(n // TM,),
        in_specs=[pl.BlockSpec((TM, d), lambda i: (i, 0)),
                  pl.BlockSpec((TM, ya.shape[1]), lambda i: (i, 0)),
                  pl.BlockSpec((TM, yb.shape[1]), lambda i: (i, 0)),
                  pl.BlockSpec((1, N_MOD, d), lambda i: (mod_row(i), 0, 0)),
                  pl.BlockSpec((1, d), lambda i: (0, 0)),
                  pl.BlockSpec(wa.shape, lambda i: (0, 0)),
                  pl.BlockSpec(wb.shape, lambda i: (0, 0))],
        out_specs=[pl.BlockSpec((TM, d), lambda i: (i, 0)), pl.BlockSpec((TM, d), lambda i: (i, 0))],
        out_shape=[jax.ShapeDtypeStruct((n, d), _F32), jax.ShapeDtypeStruct((n, d), _MXU)],
        compiler_params=_cparams(("parallel",)),
        name="out_proj",
    )(x, ya, yb, mod, g2, wa, wb)


def _proj_cd_kernel(x_ref, mod_ref, g_ref, cq_ref, sq_ref, cd_ref, sd_ref, w_ref, qg_ref, kvg_ref,
                    wqa_ref, wqb_ref, wk_ref, wv_ref, place_ref,
                    qm_ref, km_ref, vm_ref, dq_ref, dk_ref, dv_ref):
    x = x_ref[...]
    h = _rms(x, g_ref[...]) * (1.0 + mod_ref[0, 1:2, :]) + mod_ref[0, 0:1, :]
    hb = h.astype(_MXU)

    def proj(lo, n):
        return _dot(hb, w_ref[:, lo:lo + n])

    cosq, sinq = cq_ref[...], sq_ref[...]
    cosd, sind = cd_ref[...], sd_ref[...]
    cqn = _rms(proj(0, 256), qg_ref[...]).astype(_MXU)
    ckvn = _rms(proj(256, 128), kvg_ref[...]).astype(_MXU)
    cos8, sin8 = jnp.tile(cosq, (1, 8)), jnp.tile(sinq, (1, 8))
    qm = _dot(cqn, wqa_ref[...]) * cos8 + _dot(cqn, wqb_ref[...]) * sin8
    qm_ref[...] = (qm * ((MLA_NOPE + MLA_ROPE) ** -0.5 * _LOG2E)).T.astype(qm_ref.dtype)
    kr = (proj(384, 128) * cosq + proj(512, 128) * sinq).astype(_MXU)
    km_ref[...] = (_dot(ckvn, wk_ref[...]) + _dot(kr, place_ref[...])).astype(km_ref.dtype)
    vm_ref[...] = _dot(ckvn, wv_ref[...]).T.astype(vm_ref.dtype)
    cos4, sin4 = jnp.tile(cosd, (1, 4)), jnp.tile(sind, (1, 4))
    dq_ref[...] = ((proj(640, 512) * cos4 + proj(1152, 512) * sin4) * (HEAD_DIM ** -0.5 * _LOG2E)).T.astype(dq_ref.dtype)
    dk_ref[...] = (proj(1664, 512) * cos4 + proj(2176, 512) * sin4).astype(dk_ref.dtype)
    dv_ref[...] = proj(2688, 512).T.astype(dv_ref.dtype)


def _proj_cd(x, mod, mod_row, g, tabs, tab_row, w, qg, kvg, wqa, wqb, wk, wv, place):
    n, d = x.shape
    outs = ((1024, True), (1024, False), (512, True), (512, True), (512, False), (512, True))
    full = lambda a: pl.BlockSpec(a.shape, lambda i: (0, 0))
    return pl.pallas_call(
        _proj_cd_kernel,
        grid=(n // TM,),
        in_specs=[pl.BlockSpec((TM, d), lambda i: (i, 0)),
                  pl.BlockSpec((1, N_MOD, d), lambda i: (mod_row(i), 0, 0)),
                  pl.BlockSpec((1, d), lambda i: (0, 0))]
                 + [pl.BlockSpec((TM, LANE), lambda i: (tab_row(i), 0))] * 4
                 + [full(a) for a in (w, qg, kvg, wqa, wqb, wk, wv, place)],
        out_specs=[pl.BlockSpec((wd, TM), lambda i: (0, i)) if fm else pl.BlockSpec((TM, wd), lambda i: (i, 0))
                   for wd, fm in outs],
        out_shape=[jax.ShapeDtypeStruct((wd, n) if fm else (n, wd), _MXU) for wd, fm in outs],
        compiler_params=_cparams(("parallel",)),
        name="proj_cd",
    )(x, mod, g, *tabs, w, qg, kvg, wqa, wqb, wk, wv, place)


_MARK = 2.0 ** 121


def _extract16(x, first_only):
    r = x.shape[0]
    row = lax.broadcasted_iota(jnp.int32, x.shape, 0)
    row16 = lax.broadcasted_iota(jnp.int32, (PEER_TOPK, LANE), 0)
    vals = jnp.zeros((PEER_TOPK, LANE), _F32)
    for k in range(PEER_TOPK):
        m = jnp.max(x, axis=0, keepdims=True)
        hit = x == m
        if first_only:
            hit = row == jnp.min(jnp.where(hit, row, r), axis=0, keepdims=True)
        x = jnp.where(hit, -_MARK * (32 + k), x)
        vals = jnp.where(row16 == k, m, vals)
    order = jnp.where(x < -16.0 * _MARK, x * (-1.0 / _MARK) - 32.0, float(PEER_TOPK))
    return order, vals


def _staircase_rows():
    groups = []
    for r in range(PEER_TOPK // 2):
        n = PEER_TOPK // (r + 1)
        for c0 in range(0, n, 8):
            groups.append((r, c0, min(8, n - c0)))
    groups.append((None, 0, 8))
    return groups


def _peer_select(sa, sb, first_only):
    ra, av = _extract16(sa, first_only)
    rb, bv = _extract16(sb, first_only)
    groups = _staircase_rows()
    sub = lax.broadcasted_iota(jnp.int32, (8, LANE), 0)
    pieces = []
    for r, c0, nv in groups:
        if r is None:
            piece = av[8:16] + bv[0:1]
        else:
            piece = av[r:r + 1] + bv[c0:c0 + 8]
            if nv < 8:
                piece = jnp.where(sub < nv, piece, -jnp.inf)
        pieces.append(piece)
    cand = jnp.concatenate(pieces, axis=0)
    e_cand = jnp.exp(cand - (av[0:1] + bv[0:1]))
    sel, _ = _extract16(cand, first_only)
    sel = jnp.where(sel < float(PEER_TOPK), 1.0, 0.0)
    z = jnp.sum(sel * e_cand, axis=0, keepdims=True)
    lr = jnp.zeros(sa.shape, _F32)
    for g, (r, c0, nv) in enumerate(groups):
        blk = sel[8 * g:8 * g + 8]
        if r is None:
            for q in range(8):
                lr = jnp.where(ra == float(8 + q), blk[q:q + 1], lr)
        elif c0 == 0:
            cnt = jnp.sum(blk, axis=0, keepdims=True)
            if PEER_TOPK // (r + 1) > 8:
                cnt = cnt + jnp.sum(sel[8 * g + 8:8 * g + 16], axis=0, keepdims=True)
            lr = jnp.where(ra == float(r), cnt, lr)
    ea = jnp.exp(sa - av[0:1]) / z
    eb = jnp.exp(sb - bv[0:1])
    n_sel = (jnp.sum(jnp.where(ra < float(PEER_TOPK), 1.0, 0.0), axis=0, keepdims=True)
             + jnp.sum(jnp.where(rb < float(PEER_TOPK), 1.0, 0.0), axis=0, keepdims=True)
             + jnp.sum(sel, axis=0, keepdims=True))
    return lr, ea, rb, eb, n_sel


def _peer_kernel(h2_ref, x_ref, mod_ref, wqt_ref, keys_ref, u_ref, vt_ref, *rest, n_eb, final):
    fg_ref = rest[0] if final else None
    (o_ref, h2t_ref, lr_ref, ea_ref, rb_ref, eb_ref, ft_ref, a0_ref, a1_ref, w0_ref, w1_ref,
     h2c_ref, flag_ref) = rest[1:] if final else rest
    s = pl.program_id(1)
    n_chunks = PEER_TM // LANE
    i_per = PEER_EB // PEER_NKEYS
    gdt = rb_ref.dtype

    @pl.when(s == 0)
    def _select():
        h2t = h2_ref[...].astype(_F32).T.astype(_MXU)
        h2t_ref[...] = h2t
        ft_ref[...] = jnp.zeros_like(ft_ref)
        a1_ref[...] = jnp.zeros_like(a1_ref)
        w0_ref[...] = jnp.zeros_like(w0_ref)

        for c in range(n_chunks):
            h2c_ref[c] = h2t[:, c * LANE:(c + 1) * LANE]
        for hp in range(2 * PEER_HEADS):
            qt = _dot(wqt_ref[hp * LANE:(hp + 1) * LANE, :], h2t).astype(_MXU)
            st = _dot(keys_ref[hp % 2], qt)
            for c in range(n_chunks):
                (lr_ref if hp % 2 == 0 else ea_ref)[c, hp // 2] = st[:, c * LANE:(c + 1) * LANE]

        def unit(c, h, sa, sb, first_only):
            lr, ea, rb, eb, n_sel = _peer_select(sa, sb, first_only)
            lr_ref[c, h] = lr
            ea_ref[c, h] = ea
            rb_ref[c, h] = rb.astype(gdt)
            eb_ref[c, h] = eb.astype(gdt)
            return jnp.max(n_sel) > 3.0 * PEER_TOPK

        per_trip = PEER_SELECT_UNROLL

        def fast(t, carry):
            c, h0 = t // (PEER_HEADS // per_trip), (t % (PEER_HEADS // per_trip)) * per_trip
            for u in range(per_trip):
                tied = unit(c, h0 + u, lr_ref[c, h0 + u], ea_ref[c, h0 + u], False)
                flag_ref[c * PEER_HEADS + h0 + u] = tied.astype(jnp.int32)
            return carry

        lax.fori_loop(0, n_chunks * PEER_HEADS // per_trip, fast, 0)

        def exact(t, carry):
            @pl.when(flag_ref[t] != 0)
            def _redo():
                c, h = t // PEER_HEADS, t % PEER_HEADS
                w2 = wqt_ref[pl.ds(pl.multiple_of(h * 2 * LANE, 2 * LANE), 2 * LANE), :]
                qt = _dot(w2, h2c_ref[c]).astype(_MXU)
                unit(c, h, _dot(keys_ref[0], qt[:LANE]), _dot(keys_ref[1], qt[LANE:]), True)
            return carry

        lax.fori_loop(0, n_chunks * PEER_HEADS, exact, 0)

    def step(a_new, a_prev, w_new, w_prev):
        blk = jnp.clip(s - 1, 0, n_eb - 1)

        def gate(ii, c):
            i = blk * i_per + ii
            rows = slice(ii * PEER_NKEYS, (ii + 1) * PEER_NKEYS)
            cols = slice(c * LANE, (c + 1) * LANE)
            g = jnp.zeros((PEER_NKEYS, LANE), gdt)
            for h in range(PEER_HEADS):
                lr_i = lr_ref[c, h, pl.ds(i, 1), :].astype(gdt)
                ea_i = ea_ref[c, h, pl.ds(i, 1), :].astype(gdt)
                g = g + jnp.where(rb_ref[c, h] < lr_i, eb_ref[c, h], jnp.zeros((), gdt)) * ea_i
            a = a_prev[rows, cols]
            gelu = 0.5 * a * (1.0 + lax.erf(a * (2.0 ** -0.5)))
            w_new[rows, cols] = gelu.astype(gdt) * g

        units = [(ii, c) for ii in range(i_per) for c in range(n_chunks)]
        kd, mr = MXU_DEPTH, PEER_MXU_ROWS
        d_model = u_ref.shape[1]
        n_pieces = (PEER_EB // mr) * (d_model // kd) + (d_model // mr) * (PEER_EB // kd)
        per = len(units) // n_pieces
        done = 0
        for r0 in range(0, PEER_EB, mr):
            acc = None
            for k0 in range(0, d_model, kd):
                part = _dot(u_ref[r0:r0 + mr, k0:k0 + kd], h2t_ref[k0:k0 + kd, :])
                acc = part if acc is None else acc + part
                for ii, c in units[done:done + per]:
                    gate(ii, c)
                done += per
            a_new[r0:r0 + mr, :] = acc
        for r0 in range(0, d_model, mr):
            acc = ft_ref[r0:r0 + mr, :]
            for k0 in range(0, PEER_EB, kd):
                acc = acc + _dot(vt_ref[0, r0:r0 + mr, k0:k0 + kd], w_prev[k0:k0 + kd, :])
                for ii, c in units[done:done + per]:
                    gate(ii, c)
                done += per
            ft_ref[r0:r0 + mr, :] = acc
        for ii, c in units[done:]:
            gate(ii, c)

    @pl.when(s % 2 == 0)
    def _even():
        step(a0_ref, a1_ref, w1_ref, w0_ref)

    @pl.when(s % 2 == 1)
    def _odd():
        step(a1_ref, a0_ref, w0_ref, w1_ref)

    @pl.when(s == n_eb + 1)
    def _fin():
        x = x_ref[...] + mod_ref[0, 5:6, :] * ft_ref[...].T
        if final:
            x = _rms(x, fg_ref[...])
        o_ref[...] = x


def _peer(h2, x, mod, mod_row, wqt, keys, u, vt, final_g=None):
    n, d = x.shape
    n_exp = u.shape[0]
    n_eb = n_exp // PEER_EB
    tm = PEER_TM
    n_chunks = tm // LANE
    final = final_g is not None
    once = pl.Buffered(1)
    args = [h2, x, mod, wqt, keys, u, vt]
    in_specs = [pl.BlockSpec((tm, d), lambda i, e: (i, 0), pipeline_mode=once),
                pl.BlockSpec((tm, d), lambda i, e: (i, 0), pipeline_mode=once),
                pl.BlockSpec((1, N_MOD, d), lambda i, e: (mod_row(i), 0, 0)),
                pl.BlockSpec(wqt.shape, lambda i, e: (0, 0), pipeline_mode=once),
                pl.BlockSpec(keys.shape, lambda i, e: (0, 0, 0)),
                pl.BlockSpec((PEER_EB, d), lambda i, e: (jnp.minimum(e, n_eb - 1), 0)),
                pl.BlockSpec((1, d, PEER_EB), lambda i, e: (jnp.clip(e - 2, 0, n_eb - 1), 0, 0))]
    if final:
        args.append(final_g)
        in_specs.append(pl.BlockSpec((1, d), lambda i, e: (0, 0)))
    tab32 = pltpu.VMEM((n_chunks, PEER_HEADS, PEER_NKEYS, LANE), _F32)
    tab16 = pltpu.VMEM((n_chunks, PEER_HEADS, PEER_NKEYS, LANE), _MXU)
    abuf = pltpu.VMEM((PEER_EB, tm), _F32)
    wbuf = pltpu.VMEM((PEER_EB, tm), _MXU)
    return pl.pallas_call(
        functools.partial(_peer_kernel, n_eb=n_eb, final=final),
        grid=(n // tm, n_eb + 2),
        in_specs=in_specs,
        out_specs=pl.BlockSpec((tm, d), lambda i, e: (i, 0)),
        out_shape=jax.ShapeDtypeStruct((n, d), _F32),
        scratch_shapes=[pltpu.VMEM((d, tm), _MXU), tab32, tab32, tab16, tab16,
                        pltpu.VMEM((d, tm), _F32), abuf, abuf, wbuf, wbuf,
                        pltpu.VMEM((n_chunks, d, LANE), _MXU), pltpu.SMEM((n_chunks * PEER_HEADS,), jnp.int32)],
        compiler_params=_cparams(("parallel", "arbitrary")),
        name="peer_ffn",
    )(*args)


def _dup_halves(w):
    a, b = w[:, :HEAD_DIM], w[:, HEAD_DIM:]
    return jnp.concatenate([a, a, b, b], axis=1)


def _prep_ab(w_in):
    aq, ak, av = w_in[:, 0:512], w_in[:, 512:1024], w_in[:, 1024:1536]
    bq, bk, bv = w_in[:, 1536:2048], w_in[:, 2048:2176], w_in[:, 2176:2304]
    cat = [aq, ak, av, bq, _rot_cols(bq, HEAD_DIM), _dup_halves(bk), _dup_halves(_rot_cols(bk, HEAD_DIM)),
           _dup_halves(bv)]
    return jnp.concatenate(cat, axis=1).astype(_MXU)


def _prep_cd(w_in, w_uq, w_ukv):
    d = w_in.shape[0]
    cq, ckv, kr = w_in[:, 0:256], w_in[:, 256:384], w_in[:, 384:416]
    dq, dk, dv = w_in[:, 416:928], w_in[:, 928:1440], w_in[:, 1440:1952]
    z64, z32 = jnp.zeros((d, 64), _F32), jnp.zeros((d, 32), _F32)
    kr128 = jnp.concatenate([z64, kr, z32], axis=1)
    krrot128 = jnp.concatenate([z64, _rot_cols(kr, MLA_ROPE), z32], axis=1)
    w = jnp.concatenate([cq, ckv, kr128, krrot128, dq, _rot_cols(dq, HEAD_DIM), dk, _rot_cols(dk, HEAD_DIM), dv],
                        axis=1).astype(_MXU)
    r = w_uq.shape[0]
    uq = w_uq.reshape(r, 8, MLA_NOPE + MLA_ROPE)
    nope, rope = uq[:, :, :MLA_NOPE], uq[:, :, MLA_NOPE:]
    rope_rot = _rot_cols(rope.reshape(r, 8 * MLA_ROPE), MLA_ROPE).reshape(r, 8, MLA_ROPE)
    zq = jnp.zeros((r, 8, 32), _F32)
    wqa = jnp.concatenate([nope, rope, zq], axis=2).reshape(r, 1024).astype(_MXU)
    wqb = jnp.concatenate([jnp.zeros_like(nope), rope_rot, zq], axis=2).reshape(r, 1024).astype(_MXU)
    rk = w_ukv.shape[0]
    ukv = w_ukv.reshape(rk, 8, 128)
    wk = jnp.concatenate([ukv[:, :, :MLA_NOPE], jnp.zeros((rk, 8, 64), _F32)], axis=2).reshape(rk, 1024).astype(_MXU)
    wv = ukv[:, :, MLA_NOPE:].reshape(rk, 512).astype(_MXU)
    lane = jnp.arange(LANE)
    src = (lane >= MLA_NOPE) & (lane < MLA_NOPE + MLA_ROPE)
    place = (src[:, None] & (lane[:, None] == (jnp.arange(1024)[None, :] % LANE))).astype(_MXU)
    return w, wqa, wqb, wk, wv, place


def kernel(x, c, ctx, c_ctx, ada_w, ada_b, norm1_g, norm2_g, w_out, peer_wq, peer_keys, peer_u, peer_v,
           ab_w_in, na_rpb, swa_sink, cd_w_in, mla_q_norm_g, mla_w_uq, mla_kv_norm_g, mla_w_ukv,
           diff_lambda, diff_subln_g, final_norm_g):
    batch, seq, d = x.shape
    ctx_len = ctx.shape[1]
    depth = ada_w.shape[0]
    assert seq % TM == 0 and (batch * ctx_len) % TM == 0 and seq % TQ_LOCAL == 0
    assert seq % PEER_TM == 0 and (batch * ctx_len) % PEER_TM == 0
    assert depth == 2, "even layers keep a context stream, the single odd layer is the last one"
    n_lat, n_ctx = batch * seq, batch * ctx_len
    xs = x.reshape(n_lat, d)
    cs = ctx.reshape(n_ctx, d)

    mod_rows = -(-(batch + 1) // 16) * 16
    cc = jnp.zeros((mod_rows, d), _F32).at[:batch].set(c).at[batch].set(c_ctx)
    mod_all = _modulation(cc, ada_w, ada_b)

    tiles_per_seq = seq // TM
    lat_row = lambda i: i // tiles_per_seq
    ctx_row = lambda i: batch
    lat_tab = lambda i: i % tiles_per_seq
    ctx_tab = lambda i: tiles_per_seq

    cos64, sin64 = _rope_tables(seq, HEAD_DIM)
    cos64p, sin64p = _pad_table(cos64, LANE, 1.0), _pad_table(sin64, LANE, 0.0)

    for l in range(depth):
        last = l == depth - 1
        j = l // 2
        mod = mod_all[l, :batch + 1].reshape(batch + 1, N_MOD, d)
        g1 = norm1_g[l].reshape(1, d)
        g2 = norm2_g[l].reshape(1, d)
        wo = w_out[l].astype(_MXU)
        wqt = peer_wq[l].T.astype(_MXU)
        keys = peer_keys[l].astype(_MXU)
        u = peer_u[l].astype(_MXU)
        vt = peer_v[l].astype(_MXU).reshape(-1, PEER_EB, d).transpose(0, 2, 1)
        if l % 2 == 0:
            w = _prep_ab(ab_w_in[j])
            aq, ak, av, bq, bk2, bv2 = _proj_ab(xs, mod, lat_row, g1, cos64p, sin64p, lat_tab, w)
            caq, cak, cav, cbq, cbk2, cbv2 = _proj_ab(cs, mod, ctx_row, g1, cos64p, sin64p, ctx_tab, w)
            bias = _na_bias_tables(na_rpb[j], seq // GRID_W)
            ya = _na_attention(aq, ak, av, cak, cav, bias, batch, seq, ctx_len)
            sink = swa_sink[j].astype(_F32)
            yb = _swa_attention(sink, bq, bk2, bv2, cbk2, cbv2, batch, seq, ctx_len)
            xs, h2 = _out_proj(xs, ya, yb, mod, lat_row, g2, wo[:512], wo[512:])
            if not last:
                pair_heads = [(m, half, m, m) for m in range(4) for half in range(2)]
                pair_outs = [("pair", 2 * m, 2 * m + 1) for m in range(4)]
                yca = _flash_attention(caq, cak, cav, heads=pair_heads, outs=pair_outs, batch=batch,
                                       q_per_batch=ctx_len, kv_per_batch=ctx_len, tq=ctx_len, tk=ctx_len,
                                       name="ctx_attn_a")
                gqa_heads = [(m, half, m // 2, m // 2) for m in range(4) for half in range(2)]
                ycb = _flash_attention(cbq, cbk2, cbv2, heads=gqa_heads, outs=pair_outs, batch=batch,
                                       q_per_batch=ctx_len, kv_per_batch=ctx_len, tq=ctx_len, tk=ctx_len,
                                       sink=sink, name="ctx_attn_b")
                cs, h2c = _out_proj(cs, yca, ycb, mod, ctx_row, g2, wo[:512], wo[512:])
        else:
            lam_init = 0.8 - 0.6 * math.exp(-0.3 * l)
            w, wqa, wqb, wk, wv, place = _prep_cd(cd_w_in[j], mla_w_uq[j], mla_w_ukv[j])
            cos32, sin32 = _rope_tables(seq, MLA_ROPE)
            ones64, zeros64 = jnp.ones((seq, 64), _F32), jnp.zeros((seq, 64), _F32)
            cosq = _pad_table(jnp.concatenate([ones64, cos32, ones64[:, :32]], 1), LANE, 1.0)
            sinq = _pad_table(jnp.concatenate([zeros64, sin32, zeros64[:, :32]], 1), LANE, 0.0)
            tabs = (cosq, sinq, cos64p, sin64p)
            qg = mla_q_norm_g[j].reshape(1, -1)
            kvg = mla_kv_norm_g[j].reshape(1, -1)
            qmt, km, vmt, dqt, dk, dvt = _proj_cd(xs, mod, lat_row, g1, tabs, lat_tab, w, qg, kvg, wqa, wqb, wk, wv, place)
            _, ckm, cvmt, _, cdk, cdvt = _proj_cd(cs, mod, ctx_row, g1, tabs, ctx_tab, w, qg, kvg, wqa, wqb, wk, wv, place)
            mla_heads = [(h, None, h, h // 2) for h in range(8)]
            pair_outs = [("pair", 2 * m, 2 * m + 1) for m in range(4)]
            yc = _flasht_attention(qmt, km, vmt, ckm, cvmt, heads=mla_heads, outs=pair_outs, batch=batch, seq=seq,
                                   ctx_len=ctx_len, tq=GLOBAL_TQ, tk=GLOBAL_TK, name="mla_attn")
            diff_heads = [(h, a, h, h) for h in range(4) for a in range(2)]
            diff_outs = [("diff", 2 * h, 2 * h + 1) for h in range(4)]
            od = _flasht_attention(dqt, dk, dvt, cdk, cdvt, heads=diff_heads, outs=diff_outs, batch=batch, seq=seq,
                                   ctx_len=ctx_len, tq=GLOBAL_TQ, tk=GLOBAL_TK,
                                   diff=(diff_lambda[j].astype(_F32), diff_subln_g[j].reshape(1, -1), lam_init),
                                   name="diff_attn")
            xs, h2 = _out_proj(xs, yc, od, mod, lat_row, g2, wo[:512], wo[512:])
        fg = final_norm_g.reshape(1, d) if last else None
        xs = _peer(h2, xs, mod, lambda i: i // (seq // PEER_TM), wqt, keys, u, vt, final_g=fg)
        if not last:
            cs = _peer(h2c, cs, mod, ctx_row, wqt, keys, u, vt)
    return xs.reshape(batch, seq, d)
```

```python
import functools
import math

import jax
import jax.numpy as jnp
from jax import lax
from jax.experimental import pallas as pl
from jax.experimental.pallas import tpu as pltpu

_F32 = jnp.float32
_MXU = jnp.bfloat16
_NEG = -1e30
_LOG2E = math.log2(math.e)

GRID_W = 64
HEAD_DIM = 64
ROPE_BASE = 10000.0
RMS_EPS = 1e-6
N_MOD = 6
NA_KR, NA_KC = 8, 16
SWA_WINDOW = 128
MLA_NOPE, MLA_ROPE = 64, 32
PEER_HEADS, PEER_NKEYS, PEER_TOPK = 8, 128, 16

LANE = 128
TM = 512
TQ_LOCAL = 256
NA_GROUP_ROWS = TQ_LOCAL // GRID_W
NA_WIN_ROWS = NA_KR + NA_GROUP_ROWS - 1
GLOBAL_TQ = 1024
GLOBAL_TK = 1024
PEER_TM = 1024
PEER_EB = 512
MXU_DEPTH = 256
PEER_MXU_ROWS = 512
PEER_SELECT_UNROLL = 8
VMEM_LIMIT = 56 * 1024 * 1024


def _cparams(sem, vmem=VMEM_LIMIT):
    return pltpu.CompilerParams(dimension_semantics=sem, vmem_limit_bytes=vmem)


def _dot(a, b):
    return jnp.dot(a, b, preferred_element_type=_F32)


def _dot_nt(a, b):
    return lax.dot_general(a, b, (((1,), (1,)), ((), ())), preferred_element_type=_F32)


def _rms(x, g):
    return x * lax.rsqrt(jnp.mean(x * x, axis=-1, keepdims=True) + RMS_EPS) * g


def _lane_half(shape):
    return lax.broadcasted_iota(jnp.int32, shape, len(shape) - 1) >= (LANE // 2)


def _mod_kernel(c_ref, w_ref, b_ref, o_ref):
    c = c_ref[...]
    a = c * jax.nn.sigmoid(c)
    w = w_ref[0]
    a_hi = a.astype(_MXU)
    a_lo = (a - a_hi.astype(_F32)).astype(_MXU)
    w_hi = w.astype(_MXU)
    w_lo = (w - w_hi.astype(_F32)).astype(_MXU)
    o_ref[0] = _dot(a_hi, w_hi) + _dot(a_lo, w_hi) + _dot(a_hi, w_lo) + b_ref[0]


def _modulation(cc, ada_w, ada_b):
    depth, d, n = ada_w.shape
    rows = cc.shape[0]
    tn = 768
    return pl.pallas_call(
        _mod_kernel,
        grid=(depth, n // tn),
        in_specs=[pl.BlockSpec((rows, d), lambda l, j: (0, 0)),
                  pl.BlockSpec((1, d, tn), lambda l, j: (l, 0, j)),
                  pl.BlockSpec((1, 1, tn), lambda l, j: (l, 0, j))],
        out_specs=pl.BlockSpec((1, rows, tn), lambda l, j: (l, 0, j)),
        out_shape=jax.ShapeDtypeStruct((depth, rows, n), _F32),
        compiler_params=_cparams(("parallel", "parallel")),
        name="adaln_mod",
    )(cc, ada_w, ada_b.reshape(depth, 1, n))


def _rope_tables(seq, d):
    t = jnp.arange(seq, dtype=jnp.int32)
    q = d // 4
    freq = ROPE_BASE ** (-jnp.arange(q, dtype=_F32) / q)

    def one(pos):
        ang = pos.astype(_F32)[:, None] * freq[None, :]
        return jnp.concatenate([jnp.cos(ang)] * 2, -1), jnp.concatenate([jnp.sin(ang)] * 2, -1)

    cr, sr = one(t // GRID_W)
    cc, sc = one(t % GRID_W)
    return jnp.concatenate([cr, cc], -1), jnp.concatenate([sr, sc], -1)


def _rot_cols(w, d):
    k, n = w.shape
    w5 = w.reshape(k, n // d, 2, 2, d // 4)
    return jnp.stack([-w5[:, :, :, 1], w5[:, :, :, 0]], axis=3).reshape(k, n)


def _pad_table(tab, width, ident):
    s, w = tab.shape
    if w < width:
        reps = width // w
        tab = jnp.tile(tab, (1, reps))
    return jnp.concatenate([tab, jnp.full((TM, width), ident, _F32)], axis=0)


def _proj_ab_kernel(x_ref, mod_ref, g_ref, cos_ref, sin_ref, w_ref,
                    aq_ref, ak_ref, av_ref, bq_ref, bk_ref, bv_ref):
    x = x_ref[...]
    h = _rms(x, g_ref[...]) * (1.0 + mod_ref[0, 1:2, :]) + mod_ref[0, 0:1, :]
    hb = h.astype(_MXU)
    cos = cos_ref[...]
    sin = sin_ref[...]

    def proj(lo, n):
        return _dot(hb, w_ref[:, lo:lo + n])

    scale = HEAD_DIM ** -0.5
    aq_ref[...] = (proj(0, 512) * scale).astype(aq_ref.dtype)
    ak_ref[...] = proj(512, 512).astype(ak_ref.dtype)
    av_ref[...] = proj(1024, 512).astype(av_ref.dtype)
    cos4 = jnp.tile(cos, (1, 4))
    sin4 = jnp.tile(sin, (1, 4))
    bq_ref[...] = ((proj(1536, 512) * cos4 + proj(2048, 512) * sin4) * scale).astype(bq_ref.dtype)
    cos2 = jnp.tile(cos, (1, 2))
    sin2 = jnp.tile(sin, (1, 2))
    bk_ref[...] = (proj(2560, 256) * cos2 + proj(2816, 256) * sin2).astype(bk_ref.dtype)
    bv_ref[...] = proj(3072, 256).astype(bv_ref.dtype)


def _proj_ab(x, mod, mod_row, g, cos, sin, tab_row, w):
    n, d = x.shape
    widths = (512, 512, 512, 512, 256, 256)
    return pl.pallas_call(
        _proj_ab_kernel,
        grid=(n // TM,),
        in_specs=[pl.BlockSpec((TM, d), lambda i: (i, 0)),
                  pl.BlockSpec((1, N_MOD, d), lambda i: (mod_row(i), 0, 0)),
                  pl.BlockSpec((1, d), lambda i: (0, 0)),
                  pl.BlockSpec((TM, LANE), lambda i: (tab_row(i), 0)),
                  pl.BlockSpec((TM, LANE), lambda i: (tab_row(i), 0)),
                  pl.BlockSpec(w.shape, lambda i: (0, 0))],
        out_specs=[pl.BlockSpec((TM, wd), lambda i: (i, 0)) for wd in widths],
        out_shape=[jax.ShapeDtypeStruct((n, wd), _MXU) for wd in widths],
        compiler_params=_cparams(("parallel",)),
        name="proj_ab",
    )(x, mod, g, cos, sin, w)


def _na_kernel(q_ref, k_ref, v_ref, kc_ref, vc_ref, bias_ref, o_ref, *, rows):
    g = pl.program_id(1)
    ks = jnp.clip(NA_GROUP_ROWS * g - NA_KR // 2, 0, rows - NA_WIN_ROWS)
    start = pl.multiple_of(ks * GRID_W, GRID_W)
    nwin = NA_WIN_ROWS * GRID_W
    hi = _lane_half((TQ_LOCAL, LANE))

    def scores(h):
        cols = slice((h // 2) * LANE, (h // 2 + 1) * LANE)
        q2 = q_ref[:, cols]
        qh = jnp.where(hi == (h % 2 == 1), q2, jnp.zeros_like(q2))
        return _dot_nt(qh, k_ref[pl.ds(start, nwin), cols]) + bias_ref[0, h], _dot_nt(qh, kc_ref[:, cols])

    outs = []
    nxt = scores(0)
    for h in range(8):
        s_lat, s_ctx = nxt
        if h + 1 < 8:
            nxt = scores(h + 1)
        cols = slice((h // 2) * LANE, (h // 2 + 1) * LANE)
        mx = jnp.maximum(jnp.max(s_lat, axis=-1, keepdims=True), jnp.max(s_ctx, axis=-1, keepdims=True))
        p_lat = jnp.exp(s_lat - mx)
        p_ctx = jnp.exp(s_ctx - mx)
        den = jnp.sum(p_lat, axis=-1, keepdims=True) + jnp.sum(p_ctx, axis=-1, keepdims=True)
        o = _dot(p_lat.astype(_MXU), v_ref[pl.ds(start, nwin), cols]) + _dot(p_ctx.astype(_MXU), vc_ref[:, cols])
        outs.append(o / den)
        if h % 2 == 1:
            o_ref[:, cols] = jnp.where(hi, outs[h], outs[h - 1]).astype(o_ref.dtype)


def _na_bias_tables(rpb, rows):
    gq = NA_GROUP_ROWS
    n_groups = rows // gq
    ql = jnp.arange(gq)[:, None]
    kl = jnp.arange(NA_WIN_ROWS)[None, :]
    qc = jnp.arange(GRID_W)[:, None]
    kc = jnp.arange(GRID_W)[None, :]
    cs = jnp.clip(qc - NA_KC // 2, 0, GRID_W - NA_KC)
    col_valid = (kc >= cs) & (kc < cs + NA_KC)
    col_idx = jnp.clip(kc - qc + NA_KC - 1, 0, 2 * NA_KC - 2)
    exact = lax.Precision.HIGHEST
    oh_c = jax.nn.one_hot(col_idx, 2 * NA_KC - 1, dtype=_F32)
    rpb_cols = jnp.einsum("hab,xyb->haxy", rpb.astype(_F32), oh_c, precision=exact)
    tabs = []
    for g in (0, 1, n_groups - 1):
        ks = min(max(gq * g - NA_KR // 2, 0), rows - NA_WIN_ROWS)
        rq = gq * g + ql
        rk = ks + kl
        r0 = jnp.clip(rq - NA_KR // 2, 0, rows - NA_KR)
        row_valid = (rk >= r0) & (rk < r0 + NA_KR)
        row_idx = jnp.clip(rk - rq + NA_KR - 1, 0, 2 * NA_KR - 2)
        oh_r = jax.nn.one_hot(row_idx, 2 * NA_KR - 1, dtype=_F32)
        b = jnp.einsum("qka,haxy->hqxky", oh_r, rpb_cols, precision=exact)
        valid = row_valid[:, None, :, None] & col_valid[None, :, None, :]
        b = jnp.where(valid[None], b, _NEG)
        tabs.append(b.reshape(rpb.shape[0], gq * GRID_W, NA_WIN_ROWS * GRID_W))
    return jnp.stack(tabs)


def _na_attention(aq, ak, av, cak, cav, bias, batch, seq, ctx_len):
    rows = seq // GRID_W
    n_groups = seq // TQ_LOCAL
    nwin = NA_WIN_ROWS * GRID_W

    def bias_row(b, g):
        return (jnp.where(g == 0, 0, jnp.where(g == n_groups - 1, 2, 1)), 0, 0, 0)

    return pl.pallas_call(
        functools.partial(_na_kernel, rows=rows),
        grid=(batch, n_groups),
        in_specs=[pl.BlockSpec((TQ_LOCAL, 512), lambda b, g: (b * n_groups + g, 0)),
                  pl.BlockSpec((seq, 512), lambda b, g: (b, 0)),
                  pl.BlockSpec((seq, 512), lambda b, g: (b, 0)),
                  pl.BlockSpec((ctx_len, 512), lambda b, g: (b, 0)),
                  pl.BlockSpec((ctx_len, 512), lambda b, g: (b, 0)),
                  pl.BlockSpec((1, 8, TQ_LOCAL, nwin), bias_row)],
        out_specs=pl.BlockSpec((TQ_LOCAL, 512), lambda b, g: (b * n_groups + g, 0)),
        out_shape=jax.ShapeDtypeStruct((batch * seq, 512), _MXU),
        compiler_params=_cparams(("parallel", "arbitrary")),
        name="na_attention",
    )(aq, ak, av, cak, cav, bias)


def _swa_kernel(sink_ref, q_ref, k_ref, v_ref, kc_ref, vc_ref, o_ref, *, seq):
    t = pl.program_id(1)
    kwin = TQ_LOCAL + 2 * SWA_WINDOW
    start = t * TQ_LOCAL
    kstart = pl.multiple_of(jnp.clip(start - SWA_WINDOW, 0, seq - kwin), LANE)
    qpos = start + lax.broadcasted_iota(jnp.int32, (TQ_LOCAL, kwin), 0)
    kpos = kstart + lax.broadcasted_iota(jnp.int32, (TQ_LOCAL, kwin), 1)
    mask = jnp.where(jnp.abs(kpos - qpos) <= SWA_WINDOW, 0.0, _NEG).astype(_F32)
    hi = _lane_half((TQ_LOCAL, LANE))

    def scores(h):
        cols = slice((h // 2) * LANE, (h // 2 + 1) * LANE)
        kcols = slice((h // 4) * LANE, (h // 4 + 1) * LANE)
        q2 = q_ref[:, cols]
        qh = jnp.where(hi == (h % 2 == 1), q2, jnp.zeros_like(q2))
        return _dot_nt(qh, k_ref[pl.ds(kstart, kwin), kcols]) + mask, _dot_nt(qh, kc_ref[:, kcols])

    outs = []
    nxt = scores(0)
    for h in range(8):
        s_lat, s_ctx = nxt
        if h + 1 < 8:
            nxt = scores(h + 1)
        cols = slice((h // 2) * LANE, (h // 2 + 1) * LANE)
        kcols = slice((h // 4) * LANE, (h // 4 + 1) * LANE)
        sink = sink_ref[h]
        mx = jnp.maximum(jnp.max(s_lat, axis=-1, keepdims=True), jnp.max(s_ctx, axis=-1, keepdims=True))
        mx = jnp.maximum(mx, sink)
        p_lat = jnp.exp(s_lat - mx)
        p_ctx = jnp.exp(s_ctx - mx)
        den = (jnp.sum(p_lat, axis=-1, keepdims=True) + jnp.sum(p_ctx, axis=-1, keepdims=True)
               + jnp.exp(sink - mx))
        o = (_dot(p_lat.astype(_MXU), v_ref[pl.ds(kstart, kwin), kcols])
             + _dot(p_ctx.astype(_MXU), vc_ref[:, kcols]))
        outs.append(o / den)
        if h % 2 == 1:
            o_ref[:, cols] = jnp.where(hi, outs[h], outs[h - 1]).astype(o_ref.dtype)


def _swa_attention(sink, bq, bk2, bv2, cbk2, cbv2, batch, seq, ctx_len):
    n_t = seq // TQ_LOCAL
    return pl.pallas_call(
        functools.partial(_swa_kernel, seq=seq),
        grid=(batch, n_t),
        in_specs=[pl.BlockSpec(memory_space=pltpu.SMEM),
                  pl.BlockSpec((TQ_LOCAL, 512), lambda b, t: (b * n_t + t, 0)),
                  pl.BlockSpec((seq, 256), lambda b, t: (b, 0)),
                  pl.BlockSpec((seq, 256), lambda b, t: (b, 0)),
                  pl.BlockSpec((ctx_len, 256), lambda b, t: (b, 0)),
                  pl.BlockSpec((ctx_len, 256), lambda b, t: (b, 0))],
        out_specs=pl.BlockSpec((TQ_LOCAL, 512), lambda b, t: (b * n_t + t, 0)),
        out_shape=jax.ShapeDtypeStruct((batch * seq, 512), _MXU),
        compiler_params=_cparams(("parallel", "arbitrary")),
        name="swa_attention",
    )(sink, bq, bk2, bv2, cbk2, cbv2)


def _flash_kernel(*refs, heads, outs, has_ctx, has_sink, diff_cfg, n_kv):
    it = iter(refs)
    sink_ref = next(it) if has_sink else None
    q_ref, k_ref, v_ref = next(it), next(it), next(it)
    kc_ref = next(it) if has_ctx else None
    vc_ref = next(it) if has_ctx else None
    lam_ref = next(it) if diff_cfg else None
    sg_ref = next(it) if diff_cfg else None
    o_ref = next(it)
    m_ref, l_ref, acc_ref = next(it), next(it), next(it)
    kv = pl.program_id(2)
    tq = q_ref.shape[0]
    hi = _lane_half((tq, LANE))

    @pl.when(kv == 0)
    def _init():
        for h in range(len(heads)):
            if has_sink:
                m_ref[h] = jnp.full((tq, LANE), sink_ref[h], _F32)
                l_ref[h] = jnp.ones((tq, LANE), _F32)
            else:
                m_ref[h] = jnp.full((tq, LANE), _NEG, _F32)
                l_ref[h] = jnp.zeros((tq, LANE), _F32)
            acc_ref[h] = jnp.zeros((tq, LANE), _F32)

    def attend(kr, vr):
        for h, (qb, qhalf, kb, vb) in enumerate(heads):
            q2 = q_ref[:, qb * LANE:(qb + 1) * LANE]
            if qhalf is not None:
                q2 = jnp.where(hi == (qhalf == 1), q2, jnp.zeros_like(q2))
            s = _dot_nt(q2, kr[:, kb * LANE:(kb + 1) * LANE])
            m_old = m_ref[h][:, :1]
            m_new = jnp.maximum(m_old, jnp.max(s, axis=-1, keepdims=True))
            alpha = jnp.exp(m_old - m_new)
            p = jnp.exp(s - m_new)
            l_ref[h] = jnp.broadcast_to(alpha * l_ref[h][:, :1] + jnp.sum(p, axis=-1, keepdims=True), (tq, LANE))
            acc_ref[h] = alpha * acc_ref[h] + _dot(p.astype(_MXU), vr[:, vb * LANE:(vb + 1) * LANE])
            m_ref[h] = jnp.broadcast_to(m_new, (tq, LANE))

    if has_ctx:
        @pl.when(kv == 0)
        def _ctx():
            attend(kc_ref, vc_ref)

    attend(k_ref, v_ref)

    @pl.when(kv == n_kv - 1)
    def _fin():
        def head_out(h):
            return acc_ref[h] / l_ref[h][:, :1]

        for j, spec in enumerate(outs):
            if spec[0] == "full":
                o = head_out(spec[1])
            elif spec[0] == "pair":
                o = jnp.where(hi, head_out(spec[2]), head_out(spec[1]))
            else:
                lv = lam_ref[...]
                lam = (jnp.exp(jnp.sum(lv[0:1] * lv[1:2], axis=-1, keepdims=True))
                       - jnp.exp(jnp.sum(lv[2:3] * lv[3:4], axis=-1, keepdims=True)) + diff_cfg)
                o = _rms(head_out(spec[1]) - lam * head_out(spec[2]), sg_ref[...]) * (1.0 - diff_cfg)
            o_ref[:, j * LANE:(j + 1) * LANE] = o.astype(o_ref.dtype)


def _flash_attention(q, k, v, *, heads, outs, batch, q_per_batch, kv_per_batch, tq, tk,
                     q_row0=0, kv_row0=0, ctx=None, ctx_len=0, ctx_row0=0, sink=None, diff=None, name="flash"):
    n_q = q_per_batch // tq
    n_kv = kv_per_batch // tk
    qw, kw, vw = q.shape[1], k.shape[1], v.shape[1]
    args, in_specs = [], []
    if sink is not None:
        args.append(sink)
        in_specs.append(pl.BlockSpec(memory_space=pltpu.SMEM))
    q0, k0 = q_row0 // tq, kv_row0 // tk
    args += [q, k, v]
    in_specs += [pl.BlockSpec((tq, qw), lambda b, i, j: (q0 + b * n_q + i, 0)),
                 pl.BlockSpec((tk, kw), lambda b, i, j: (k0 + b * n_kv + j, 0)),
                 pl.BlockSpec((tk, vw), lambda b, i, j: (k0 + b * n_kv + j, 0))]
    if ctx is not None:
        c0 = ctx_row0 // ctx_len
        args += [ctx[0], ctx[1]]
        in_specs += [pl.BlockSpec((ctx_len, kw), lambda b, i, j: (c0 + b, 0)),
                     pl.BlockSpec((ctx_len, vw), lambda b, i, j: (c0 + b, 0))]
    diff_cfg = None
    if diff is not None:
        lam_vecs, subln_g, diff_cfg = diff
        args += [lam_vecs, subln_g]
        in_specs += [pl.BlockSpec(lam_vecs.shape, lambda b, i, j: (0, 0)),
                     pl.BlockSpec(subln_g.shape, lambda b, i, j: (0, 0))]
    nh = len(heads)
    ow = len(outs) * LANE
    return pl.pallas_call(
        functools.partial(_flash_kernel, heads=tuple(heads), outs=tuple(outs), has_ctx=ctx is not None,
                          has_sink=sink is not None, diff_cfg=diff_cfg, n_kv=n_kv),
        grid=(batch, n_q, n_kv),
        in_specs=in_specs,
        out_specs=pl.BlockSpec((tq, ow), lambda b, i, j: (b * n_q + i, 0)),
        out_shape=jax.ShapeDtypeStruct((batch * q_per_batch, ow), _MXU),
        scratch_shapes=[pltpu.VMEM((nh, tq, LANE), _F32)] * 3,
        compiler_params=_cparams(("parallel", "parallel", "arbitrary")),
        name=name,
    )(*args)


def _flasht_kernel(*refs, heads, outs, diff_cfg, n_kv):
    it = iter(refs)
    qt_ref, k_ref, vt_ref, kc_ref, vct_ref = next(it), next(it), next(it), next(it), next(it)
    lam_ref = next(it) if diff_cfg else None
    sg_ref = next(it) if diff_cfg else None
    o_ref = next(it)
    m_ref, l_ref, acc_ref = next(it), next(it), next(it)
    kv = pl.program_id(2)
    tq = qt_ref.shape[1]
    row_hi = lax.broadcasted_iota(jnp.int32, (LANE, tq), 0) >= (LANE // 2)

    @pl.when(kv == 0)
    def _init():
        m_ref[...] = jnp.full(m_ref.shape, _NEG, _F32)
        l_ref[...] = jnp.zeros(l_ref.shape, _F32)
        acc_ref[...] = jnp.zeros(acc_ref.shape, _F32)

    def attend(kr, vtr):
        def scores(h):
            qb, qhalf, kb, _ = heads[h]
            qt = qt_ref[qb * LANE:(qb + 1) * LANE, :]
            if qhalf is not None:
                qt = jnp.where(row_hi == (qhalf == 1), qt, jnp.zeros_like(qt))
            return _dot(kr[:, kb * LANE:(kb + 1) * LANE], qt)

        s_next = scores(0)
        for h in range(len(heads)):
            s = s_next
            if h + 1 < len(heads):
                s_next = scores(h + 1)
            vb = heads[h][3]
            m_old = m_ref[h, 0:1, :]
            m_new = jnp.maximum(m_old, jnp.max(s, axis=0, keepdims=True))
            alpha = jnp.exp2(m_old - m_new)
            p = jnp.exp2(s - m_new)
            l_new = alpha * l_ref[h, 0:1, :] + jnp.sum(p, axis=0, keepdims=True)
            acc_ref[h] = alpha * acc_ref[h] + _dot(vtr[vb * LANE:(vb + 1) * LANE, :], p.astype(_MXU))
            l_ref[h] = jnp.broadcast_to(l_new, (8, tq))
            m_ref[h] = jnp.broadcast_to(m_new, (8, tq))

    @pl.when(kv == 0)
    def _ctx():
        attend(kc_ref, vct_ref)

    attend(k_ref, vt_ref)

    @pl.when(kv == n_kv - 1)
    def _fin():
        def head_out(h):
            return acc_ref[h] / l_ref[h, 0:1, :]

        for j, spec in enumerate(outs):
            if spec[0] == "full":
                o = head_out(spec[1]).T
            elif spec[0] == "pair":
                o = jnp.where(row_hi, head_out(spec[2]), head_out(spec[1])).T
            else:
                lv = lam_ref[...]
                lam = (jnp.exp(jnp.sum(lv[0:1] * lv[1:2], axis=-1, keepdims=True))
                       - jnp.exp(jnp.sum(lv[2:3] * lv[3:4], axis=-1, keepdims=True)) + diff_cfg)
                o = _rms((head_out(spec[1]) - lam * head_out(spec[2])).T, sg_ref[...]) * (1.0 - diff_cfg)
            o_ref[:, j * LANE:(j + 1) * LANE] = o.astype(o_ref.dtype)


def _flasht_attention(qt, k, vt, kc, vct, *, heads, outs, batch, seq, ctx_len, tq, tk, diff=None, name="flasht"):
    n_q, n_kv = seq // tq, seq // tk
    qw, kw, vw = qt.shape[0], k.shape[1], vt.shape[0]
    args = [qt, k, vt, kc, vct]
    in_specs = [pl.BlockSpec((qw, tq), lambda b, i, j: (0, b * n_q + i)),
                pl.BlockSpec((tk, kw), lambda b, i, j: (b * n_kv + j, 0)),
                pl.BlockSpec((vw, tk), lambda b, i, j: (0, b * n_kv + j)),
                pl.BlockSpec((ctx_len, kw), lambda b, i, j: (b, 0)),
                pl.BlockSpec((vw, ctx_len), lambda b, i, j: (0, b))]
    diff_cfg = None
    if diff is not None:
        lam_vecs, subln_g, diff_cfg = diff
        args += [lam_vecs, subln_g]
        in_specs += [pl.BlockSpec(lam_vecs.shape, lambda b, i, j: (0, 0)),
                     pl.BlockSpec(subln_g.shape, lambda b, i, j: (0, 0))]
    nh = len(heads)
    ow = len(outs) * LANE
    return pl.pallas_call(
        functools.partial(_flasht_kernel, heads=tuple(heads), outs=tuple(outs), diff_cfg=diff_cfg, n_kv=n_kv),
        grid=(batch, n_q, n_kv),
        in_specs=in_specs,
        out_specs=pl.BlockSpec((tq, ow), lambda b, i, j: (b * n_q + i, 0)),
        out_shape=jax.ShapeDtypeStruct((batch * seq, ow), _MXU),
        scratch_shapes=[pltpu.VMEM((nh, 8, tq), _F32), pltpu.VMEM((nh, 8, tq), _F32),
                        pltpu.VMEM((nh, LANE, tq), _F32)],
        compiler_params=_cparams(("parallel", "parallel", "arbitrary")),
        name=name,
    )(*args)


def _out_kernel(x_ref, ya_ref, yb_ref, mod_ref, g2_ref, wa_ref, wb_ref, xo_ref, h2_ref):
    y = _dot(ya_ref[...], wa_ref[...]) + _dot(yb_ref[...], wb_ref[...])
    x = x_ref[...] + mod_ref[0, 2:3, :] * y
    xo_ref[...] = x
    h2 = _rms(x, g2_ref[...]) * (1.0 + mod_ref[0, 4:5, :]) + mod_ref[0, 3:4, :]
    h2_ref[...] = h2.astype(h2_ref.dtype)


def _out_proj(x, ya, yb, mod, mod_row, g2, wa, wb):
    n, d = x.shape
    return pl.pallas_call(
        _out_kernel,
        grid=(n // TM,),
        in_specs=[pl.BlockSpec((TM, d), lambda i: (i, 0)),
                  pl.BlockSpec((TM, ya.shape[1]), lambda i: (i, 0)),
                  pl.BlockSpec((TM, yb.shape[1]), lambda i: (i, 0)),
                  pl.BlockSpec((1, N_MOD, d), lambda i: (mod_row(i), 0, 0)),
                  pl.BlockSpec((1, d), lambda i: (0, 0)),
                  pl.BlockSpec(wa.shape, lambda i: (0, 0)),
                  pl.BlockSpec(wb.shape, lambda i: (0, 0))],
        out_specs=[pl.BlockSpec((TM, d), lambda i: (i, 0)), pl.BlockSpec((TM, d), lambda i: (i, 0))],
        out_shape=[jax.ShapeDtypeStruct((n, d), _F32), jax.ShapeDtypeStruct((n, d), _MXU)],
        compiler_params=_cparams(("parallel",)),
        name="out_proj",
    )(x, ya, yb, mod, g2, wa, wb)


def _proj_cd_kernel(x_ref, mod_ref, g_ref, cq_ref, sq_ref, cd_ref, sd_ref, w_ref, qg_ref, kvg_ref,
                    wqa_ref, wqb_ref, wk_ref, wv_ref, place_ref,
                    qm_ref, km_ref, vm_ref, dq_ref, dk_ref, dv_ref):
    x = x_ref[...]
    h = _rms(x, g_ref[...]) * (1.0 + mod_ref[0, 1:2, :]) + mod_ref[0, 0:1, :]
    hb = h.astype(_MXU)

    def proj(lo, n):
        return _dot(hb, w_ref[:, lo:lo + n])

    cosq, sinq = cq_ref[...], sq_ref[...]
    cosd, sind = cd_ref[...], sd_ref[...]
    cqn = _rms(proj(0, 256), qg_ref[...]).astype(_MXU)
    ckvn = _rms(proj(256, 128), kvg_ref[...]).astype(_MXU)
    cos8, sin8 = jnp.tile(cosq, (1, 8)), jnp.tile(sinq, (1, 8))
    qm = _dot(cqn, wqa_ref[...]) * cos8 + _dot(cqn, wqb_ref[...]) * sin8
    qm_ref[...] = (qm * ((MLA_NOPE + MLA_ROPE) ** -0.5 * _LOG2E)).T.astype(qm_ref.dtype)
    kr = (proj(384, 128) * cosq + proj(512, 128) * sinq).astype(_MXU)
    km_ref[...] = (_dot(ckvn, wk_ref[...]) + _dot(kr, place_ref[...])).astype(km_ref.dtype)
    vm_ref[...] = _dot(ckvn, wv_ref[...]).T.astype(vm_ref.dtype)
    cos4, sin4 = jnp.tile(cosd, (1, 4)), jnp.tile(sind, (1, 4))
    dq_ref[...] = ((proj(640, 512) * cos4 + proj(1152, 512) * sin4) * (HEAD_DIM ** -0.5 * _LOG2E)).T.astype(dq_ref.dtype)
    dk_ref[...] = (proj(1664, 512) * cos4 + proj(2176, 512) * sin4).astype(dk_ref.dtype)
    dv_ref[...] = proj(2688, 512).T.astype(dv_ref.dtype)


def _proj_cd(x, mod, mod_row, g, tabs, tab_row, w, qg, kvg, wqa, wqb, wk, wv, place):
    n, d = x.shape
    outs = ((1024, True), (1024, False), (512, True), (512, True), (512, False), (512, True))
    full = lambda a: pl.BlockSpec(a.shape, lambda i: (0, 0))
    return pl.pallas_call(
        _proj_cd_kernel,
        grid=(n // TM,),
        in_specs=[pl.BlockSpec((TM, d), lambda i: (i, 0)),
                  pl.BlockSpec((1, N_MOD, d), lambda i: (mod_row(i), 0, 0)),
                  pl.BlockSpec((1, d), lambda i: (0, 0))]
                 + [pl.BlockSpec((TM, LANE), lambda i: (tab_row(i), 0))] * 4
                 + [full(a) for a in (w, qg, kvg, wqa, wqb, wk, wv, place)],
        out_specs=[pl.BlockSpec((wd, TM), lambda i: (0, i)) if fm else pl.BlockSpec((TM, wd), lambda i: (i, 0))
                   for wd, fm in outs],
        out_shape=[jax.ShapeDtypeStruct((wd, n) if fm else (n, wd), _MXU) for wd, fm in outs],
        compiler_params=_cparams(("parallel",)),
        name="proj_cd",
    )(x, mod, g, *tabs, w, qg, kvg, wqa, wqb, wk, wv, place)


_MARK = 2.0 ** 121


def _extract16(x, first_only):
    r = x.shape[0]
    row = lax.broadcasted_iota(jnp.int32, x.shape, 0)
    row16 = lax.broadcasted_iota(jnp.int32, (PEER_TOPK, LANE), 0)
    vals = jnp.zeros((PEER_TOPK, LANE), _F32)
    for k in range(PEER_TOPK):
        m = jnp.max(x, axis=0, keepdims=True)
        hit = x == m
        if first_only:
            hit = row == jnp.min(jnp.where(hit, row, r), axis=0, keepdims=True)
        x = jnp.where(hit, -_MARK * (32 + k), x)
        vals = jnp.where(row16 == k, m, vals)
    order = jnp.where(x < -16.0 * _MARK, x * (-1.0 / _MARK) - 32.0, float(PEER_TOPK))
    return order, vals


def _staircase_rows():
    groups = []
    for r in range(PEER_TOPK // 2):
        n = PEER_TOPK // (r + 1)
        for c0 in range(0, n, 8):
            groups.append((r, c0, min(8, n - c0)))
    groups.append((None, 0, 8))
    return groups


def _peer_select(sa, sb, first_only):
    ra, av = _extract16(sa, first_only)
    rb, bv = _extract16(sb, first_only)
    groups = _staircase_rows()
    sub = lax.broadcasted_iota(jnp.int32, (8, LANE), 0)
    pieces = []
    for r, c0, nv in groups:
        if r is None:
            piece = av[8:16] + bv[0:1]
        else:
            piece = av[r:r + 1] + bv[c0:c0 + 8]
            if nv < 8:
                piece = jnp.where(sub < nv, piece, -jnp.inf)
        pieces.append(piece)
    cand = jnp.concatenate(pieces, axis=0)
    e_cand = jnp.exp(cand - (av[0:1] + bv[0:1]))
    sel, _ = _extract16(cand, first_only)
    sel = jnp.where(sel < float(PEER_TOPK), 1.0, 0.0)
    z = jnp.sum(sel * e_cand, axis=0, keepdims=True)
    lr = jnp.zeros(sa.shape, _F32)
    for g, (r, c0, nv) in enumerate(groups):
        blk = sel[8 * g:8 * g + 8]
        if r is None:
            for q in range(8):
                lr = jnp.where(ra == float(8 + q), blk[q:q + 1], lr)
        elif c0 == 0:
            cnt = jnp.sum(blk, axis=0, keepdims=True)
            if PEER_TOPK // (r + 1) > 8:
                cnt = cnt + jnp.sum(sel[8 * g + 8:8 * g + 16], axis=0, keepdims=True)
            lr = jnp.where(ra == float(r), cnt, lr)
    ea = jnp.exp(sa - av[0:1]) / z
    eb = jnp.exp(sb - bv[0:1])
    n_sel = (jnp.sum(jnp.where(ra < float(PEER_TOPK), 1.0, 0.0), axis=0, keepdims=True)
             + jnp.sum(jnp.where(rb < float(PEER_TOPK), 1.0, 0.0), axis=0, keepdims=True)
             + jnp.sum(sel, axis=0, keepdims=True))
    return lr, ea, rb, eb, n_sel


def _peer_kernel(h2_ref, x_ref, mod_ref, wqt_ref, keys_ref, u_ref, vt_ref, *rest, n_eb, final):
    fg_ref = rest[0] if final else None
    (o_ref, h2t_ref, lr_ref, ea_ref, rb_ref, eb_ref, ft_ref, a0_ref, a1_ref, w0_ref, w1_ref,
     h2c_ref, flag_ref) = rest[1:] if final else rest
    s = pl.program_id(1)
    n_chunks = PEER_TM // LANE
    i_per = PEER_EB // PEER_NKEYS
    gdt = rb_ref.dtype

    @pl.when(s == 0)
    def _select():
        h2t = h2_ref[...].astype(_F32).T.astype(_MXU)
        h2t_ref[...] = h2t
        ft_ref[...] = jnp.zeros_like(ft_ref)
        a1_ref[...] = jnp.zeros_like(a1_ref)
        w0_ref[...] = jnp.zeros_like(w0_ref)

        for c in range(n_chunks):
            h2c_ref[c] = h2t[:, c * LANE:(c + 1) * LANE]
        for hp in range(2 * PEER_HEADS):
            qt = _dot(wqt_ref[hp * LANE:(hp + 1) * LANE, :], h2t).astype(_MXU)
            st = _dot(keys_ref[hp % 2], qt)
            for c in range(n_chunks):
                (lr_ref if hp % 2 == 0 else ea_ref)[c, hp // 2] = st[:, c * LANE:(c + 1) * LANE]

        def unit(c, h, sa, sb, first_only):
            lr, ea, rb, eb, n_sel = _peer_select(sa, sb, first_only)
            lr_ref[c, h] = lr
            ea_ref[c, h] = ea
            rb_ref[c, h] = rb.astype(gdt)
            eb_ref[c, h] = eb.astype(gdt)
            return jnp.max(n_sel) > 3.0 * PEER_TOPK

        per_trip = PEER_SELECT_UNROLL

        def fast(t, carry):
            c, h0 = t // (PEER_HEADS // per_trip), (t % (PEER_HEADS // per_trip)) * per_trip
            for u in range(per_trip):
                tied = unit(c, h0 + u, lr_ref[c, h0 + u], ea_ref[c, h0 + u], False)
                flag_ref[c * PEER_HEADS + h0 + u] = tied.astype(jnp.int32)
            return carry

        lax.fori_loop(0, n_chunks * PEER_HEADS // per_trip, fast, 0)

        def exact(t, carry):
            @pl.when(flag_ref[t] != 0)
            def _redo():
                c, h = t // PEER_HEADS, t % PEER_HEADS
                w2 = wqt_ref[pl.ds(pl.multiple_of(h * 2 * LANE, 2 * LANE), 2 * LANE), :]
                qt = _dot(w2, h2c_ref[c]).astype(_MXU)
                unit(c, h, _dot(keys_ref[0], qt[:LANE]), _dot(keys_ref[1], qt[LANE:]), True)
            return carry

        lax.fori_loop(0, n_chunks * PEER_HEADS, exact, 0)

    def step(a_new, a_prev, w_new, w_prev):
        blk = jnp.clip(s - 1, 0, n_eb - 1)

        def gate(ii, c):
            i = blk * i_per + ii
            rows = slice(ii * PEER_NKEYS, (ii + 1) * PEER_NKEYS)
            cols = slice(c * LANE, (c + 1) * LANE)
            g = jnp.zeros((PEER_NKEYS, LANE), gdt)
            for h in range(PEER_HEADS):
                lr_i = lr_ref[c, h, pl.ds(i, 1), :].astype(gdt)
                ea_i = ea_ref[c, h, pl.ds(i, 1), :].astype(gdt)
                g = g + jnp.where(rb_ref[c, h] < lr_i, eb_ref[c, h], jnp.zeros((), gdt)) * ea_i
            a = a_prev[rows, cols]
            gelu = 0.5 * a * (1.0 + lax.erf(a * (2.0 ** -0.5)))
            w_new[rows, cols] = gelu.astype(gdt) * g

        units = [(ii, c) for ii in range(i_per) for c in range(n_chunks)]
        kd, mr = MXU_DEPTH, PEER_MXU_ROWS
        d_model = u_ref.shape[1]
        n_pieces = (PEER_EB // mr) * (d_model // kd) + (d_model // mr) * (PEER_EB // kd)
        per = len(units) // n_pieces
        done = 0
        for r0 in range(0, PEER_EB, mr):
            acc = None
            for k0 in range(0, d_model, kd):
                part = _dot(u_ref[r0:r0 + mr, k0:k0 + kd], h2t_ref[k0:k0 + kd, :])
                acc = part if acc is None else acc + part
                for ii, c in units[done:done + per]:
                    gate(ii, c)
                done += per
            a_new[r0:r0 + mr, :] = acc
        for r0 in range(0, d_model, mr):
            acc = ft_ref[r0:r0 + mr, :]
            for k0 in range(0, PEER_EB, kd):
                acc = acc + _dot(vt_ref[0, r0:r0 + mr, k0:k0 + kd], w_prev[k0:k0 + kd, :])
                for ii, c in units[done:done + per]:
                    gate(ii, c)
                done += per
            ft_ref[r0:r0 + mr, :] = acc
        for ii, c in units[done:]:
            gate(ii, c)

    @pl.when(s % 2 == 0)
    def _even():
        step(a0_ref, a1_ref, w1_ref, w0_ref)

    @pl.when(s % 2 == 1)
    def _odd():
        step(a1_ref, a0_ref, w0_ref, w1_ref)

    @pl.when(s == n_eb + 1)
    def _fin():
        x = x_ref[...] + mod_ref[0, 5:6, :] * ft_ref[...].T
        if final:
            x = _rms(x, fg_ref[...])
        o_ref[...] = x


def _peer(h2, x, mod, mod_row, wqt, keys, u, vt, final_g=None):
    n, d = x.shape
    n_exp = u.shape[0]
    n_eb = n_exp // PEER_EB
    tm = PEER_TM
    n_chunks = tm // LANE
    final = final_g is not None
    once = pl.Buffered(1)
    args = [h2, x, mod, wqt, keys, u, vt]
    in_specs = [pl.BlockSpec((tm, d), lambda i, e: (i, 0), pipeline_mode=once),
                pl.BlockSpec((tm, d), lambda i, e: (i, 0), pipeline_mode=once),
                pl.BlockSpec((1, N_MOD, d), lambda i, e: (mod_row(i), 0, 0)),
                pl.BlockSpec(wqt.shape, lambda i, e: (0, 0), pipeline_mode=once),
                pl.BlockSpec(keys.shape, lambda i, e: (0, 0, 0)),
                pl.BlockSpec((PEER_EB, d), lambda i, e: (jnp.minimum(e, n_eb - 1), 0)),
                pl.BlockSpec((1, d, PEER_EB), lambda i, e: (jnp.clip(e - 2, 0, n_eb - 1), 0, 0))]
    if final:
        args.append(final_g)
        in_specs.append(pl.BlockSpec((1, d), lambda i, e: (0, 0)))
    tab32 = pltpu.VMEM((n_chunks, PEER_HEADS, PEER_NKEYS, LANE), _F32)
    tab16 = pltpu.VMEM((n_chunks, PEER_HEADS, PEER_NKEYS, LANE), _MXU)
    abuf = pltpu.VMEM((PEER_EB, tm), _F32)
    wbuf = pltpu.VMEM((PEER_EB, tm), _MXU)
    return pl.pallas_call(
        functools.partial(_peer_kernel, n_eb=n_eb, final=final),
        grid=(n // tm, n_eb + 2),
        in_specs=in_specs,
        out_specs=pl.BlockSpec((tm, d), lambda i, e: (i, 0)),
        out_shape=jax.ShapeDtypeStruct((n, d), _F32),
        scratch_shapes=[pltpu.VMEM((d, tm), _MXU), tab32, tab32, tab16, tab16,
                        pltpu.VMEM((d, tm), _F32), abuf, abuf, wbuf, wbuf,
                        pltpu.VMEM((n_chunks, d, LANE), _MXU), pltpu.SMEM((n_chunks * PEER_HEADS,), jnp.int32)],
        compiler_params=_cparams(("parallel", "arbitrary")),
        name="peer_ffn",
    )(*args)


def _dup_halves(w):
    a, b = w[:, :HEAD_DIM], w[:, HEAD_DIM:]
    return jnp.concatenate([a, a, b, b], axis=1)


def _prep_ab(w_in):
    aq, ak, av = w_in[:, 0:512], w_in[:, 512:1024], w_in[:, 1024:1536]
    bq, bk, bv = w_in[:, 1536:2048], w_in[:, 2048:2176], w_in[:, 2176:2304]
    cat = [aq, ak, av, bq, _rot_cols(bq, HEAD_DIM), _dup_halves(bk), _dup_halves(_rot_cols(bk, HEAD_DIM)),
           _dup_halves(bv)]
    return jnp.concatenate(cat, axis=1).astype(_MXU)


def _prep_cd(w_in, w_uq, w_ukv):
    d = w_in.shape[0]
    cq, ckv, kr = w_in[:, 0:256], w_in[:, 256:384], w_in[:, 384:416]
    dq, dk, dv = w_in[:, 416:928], w_in[:, 928:1440], w_in[:, 1440:1952]
    z64, z32 = jnp.zeros((d, 64), _F32), jnp.zeros((d, 32), _F32)
    kr128 = jnp.concatenate([z64, kr, z32], axis=1)
    krrot128 = jnp.concatenate([z64, _rot_cols(kr, MLA_ROPE), z32], axis=1)
    w = jnp.concatenate([cq, ckv, kr128, krrot128, dq, _rot_cols(dq, HEAD_DIM), dk, _rot_cols(dk, HEAD_DIM), dv],
                        axis=1).astype(_MXU)
    r = w_uq.shape[0]
    uq = w_uq.reshape(r, 8, MLA_NOPE + MLA_ROPE)
    nope, rope = uq[:, :, :MLA_NOPE], uq[:, :, MLA_NOPE:]
    rope_rot = _rot_cols(rope.reshape(r, 8 * MLA_ROPE), MLA_ROPE).reshape(r, 8, MLA_ROPE)
    zq = jnp.zeros((r, 8, 32), _F32)
    wqa = jnp.concatenate([nope, rope, zq], axis=2).reshape(r, 1024).astype(_MXU)
    wqb = jnp.concatenate([jnp.zeros_like(nope), rope_rot, zq], axis=2).reshape(r, 1024).astype(_MXU)
    rk = w_ukv.shape[0]
    ukv = w_ukv.reshape(rk, 8, 128)
    wk = jnp.concatenate([ukv[:, :, :MLA_NOPE], jnp.zeros((rk, 8, 64), _F32)], axis=2).reshape(rk, 1024).astype(_MXU)
    wv = ukv[:, :, MLA_NOPE:].reshape(rk, 512).astype(_MXU)
    lane = jnp.arange(LANE)
    src = (lane >= MLA_NOPE) & (lane < MLA_NOPE + MLA_ROPE)
    place = (src[:, None] & (lane[:, None] == (jnp.arange(1024)[None, :] % LANE))).astype(_MXU)
    return w, wqa, wqb, wk, wv, place


def kernel(x, c, ctx, c_ctx, ada_w, ada_b, norm1_g, norm2_g, w_out, peer_wq, peer_keys, peer_u, peer_v,
           ab_w_in, na_rpb, swa_sink, cd_w_in, mla_q_norm_g, mla_w_uq, mla_kv_norm_g, mla_w_ukv,
           diff_lambda, diff_subln_g, final_norm_g):
    batch, seq, d = x.shape
    ctx_len = ctx.shape[1]
    depth = ada_w.shape[0]
    assert seq % TM == 0 and (batch * ctx_len) % TM == 0 and seq % TQ_LOCAL == 0
    assert seq % PEER_TM == 0 and (batch * ctx_len) % PEER_TM == 0
    assert depth == 2, "even layers keep a context stream, the single odd layer is the last one"
    n_lat, n_ctx = batch * seq, batch * ctx_len
    xs = x.reshape(n_lat, d)
    cs = ctx.reshape(n_ctx, d)

    mod_rows = -(-(batch + 1) // 16) * 16
    cc = jnp.zeros((mod_rows, d), _F32).at[:batch].set(c).at[batch].set(c_ctx)
    mod_all = _modulation(cc, ada_w, ada_b)

    tiles_per_seq = seq // TM
    lat_row = lambda i: i // tiles_per_seq
    ctx_row = lambda i: batch
    lat_tab = lambda i: i % tiles_per_seq
    ctx_tab = lambda i: tiles_per_seq

    cos64, sin64 = _rope_tables(seq, HEAD_DIM)
    cos64p, sin64p = _pad_table(cos64, LANE, 1.0), _pad_table(sin64, LANE, 0.0)

    for l in range(depth):
        last = l == depth - 1
        j = l // 2
        mod = mod_all[l, :batch + 1].reshape(batch + 1, N_MOD, d)
        g1 = norm1_g[l].reshape(1, d)
        g2 = norm2_g[l].reshape(1, d)
        wo = w_out[l].astype(_MXU)
        wqt = peer_wq[l].T.astype(_MXU)
        keys = peer_keys[l].astype(_MXU)
        u = peer_u[l].astype(_MXU)
        vt = peer_v[l].astype(_MXU).reshape(-1, PEER_EB, d).transpose(0, 2, 1)
        if l % 2 == 0:
            w = _prep_ab(ab_w_in[j])
            aq, ak, av, bq, bk2, bv2 = _proj_ab(xs, mod, lat_row, g1, cos64p, sin64p, lat_tab, w)
            caq, cak, cav, cbq, cbk2, cbv2 = _proj_ab(cs, mod, ctx_row, g1, cos64p, sin64p, ctx_tab, w)
            bias = _na_bias_tables(na_rpb[j], seq // GRID_W)
            ya = _na_attention(aq, ak, av, cak, cav, bias, batch, seq, ctx_len)
            sink = swa_sink[j].astype(_F32)
            yb = _swa_attention(sink, bq, bk2, bv2, cbk2, cbv2, batch, seq, ctx_len)
            xs, h2 = _out_proj(xs, ya, yb, mod, lat_row, g2, wo[:512], wo[512:])
            if not last:
                pair_heads = [(m, half, m, m) for m in range(4) for half in range(2)]
                pair_outs = [("pair", 2 * m, 2 * m + 1) for m in range(4)]
                yca = _flash_attention(caq, cak, cav, heads=pair_heads, outs=pair_outs, batch=batch,
                                       q_per_batch=ctx_len, kv_per_batch=ctx_len, tq=ctx_len, tk=ctx_len,
                                       name="ctx_attn_a")
                gqa_heads = [(m, half, m // 2, m // 2) for m in range(4) for half in range(2)]
                ycb = _flash_attention(cbq, cbk2, cbv2, heads=gqa_heads, outs=pair_outs, batch=batch,
                                       q_per_batch=ctx_len, kv_per_batch=ctx_len, tq=ctx_len, tk=ctx_len,
                                       sink=sink, name="ctx_attn_b")
                cs, h2c = _out_proj(cs, yca, ycb, mod, ctx_row, g2, wo[:512], wo[512:])
        else:
            lam_init = 0.8 - 0.6 * math.exp(-0.3 * l)
            w, wqa, wqb, wk, wv, place = _prep_cd(cd_w_in[j], mla_w_uq[j], mla_w_ukv[j])
            cos32, sin32 = _rope_tables(seq, MLA_ROPE)
            ones64, zeros64 = jnp.ones((seq, 64), _F32), jnp.zeros((seq, 64), _F32)
            cosq = _pad_table(jnp.concatenate([ones64, cos32, ones64[:, :32]], 1), LANE, 1.0)
            sinq = _pad_table(jnp.concatenate([zeros64, sin32, zeros64[:, :32]], 1), LANE, 0.0)
            tabs = (cosq, sinq, cos64p, sin64p)
            qg = mla_q_norm_g[j].reshape(1, -1)
            kvg = mla_kv_norm_g[j].reshape(1, -1)
            qmt, km, vmt, dqt, dk, dvt = _proj_cd(xs, mod, lat_row, g1, tabs, lat_tab, w, qg, kvg, wqa, wqb, wk, wv, place)
            _, ckm, cvmt, _, cdk, cdvt = _proj_cd(cs, mod, ctx_row, g1, tabs, ctx_tab, w, qg, kvg, wqa, wqb, wk, wv, place)
            mla_heads = [(h, None, h, h // 2) for h in range(8)]
            pair_outs = [("pair", 2 * m, 2 * m + 1) for m in range(4)]
            yc = _flasht_attention(qmt, km, vmt, ckm, cvmt, heads=mla_heads, outs=pair_outs, batch=batch, seq=seq,
                                   ctx_len=ctx_len, tq=GLOBAL_TQ, tk=GLOBAL_TK, name="mla_attn")
            diff_heads = [(h, a, h, h) for h in range(4) for a in range(2)]
            diff_outs = [("diff", 2 * h, 2 * h + 1) for h in range(4)]
            od = _flasht_attention(dqt, dk, dvt, cdk, cdvt, heads=diff_heads, outs=diff_outs, batch=batch, seq=seq,
                                   ctx_len=ctx_len, tq=GLOBAL_TQ, tk=GLOBAL_TK,
                                   diff=(diff_lambda[j].astype(_F32), diff_subln_g[j].reshape(1, -1), lam_init),
                                   name="diff_attn")
            xs, h2 = _out_proj(xs, yc, od, mod, lat_row, g2, wo[:512], wo[512:])
        fg = final_norm_g.reshape(1, d) if last else None
        xs = _peer(h2, xs, mod, lambda i: i // (seq // PEER_TM), wqt, keys, u, vt, final_g=fg)
        if not last:
            cs = _peer(h2c, cs, mod, ctx_row, wqt, keys, u, vt)
    return xs.reshape(batch, seq, d)
```

```python
import functools
import math

import jax
import jax.numpy as jnp
from jax import lax
from jax.experimental import pallas as pl
from jax.experimental.pallas import tpu as pltpu

_F32 = jnp.float32
_MXU = jnp.bfloat16
_NEG = -1e30
_LOG2E = math.log2(math.e)

GRID_W = 64
HEAD_DIM = 64
ROPE_BASE = 10000.0
RMS_EPS = 1e-6
N_MOD = 6
NA_KR, NA_KC = 8, 16
SWA_WINDOW = 128
MLA_NOPE, MLA_ROPE = 64, 32
PEER_HEADS, PEER_NKEYS, PEER_TOPK = 8, 128, 16

LANE = 128
TM = 512
TQ_LOCAL = 256
NA_GROUP_ROWS = TQ_LOCAL // GRID_W
NA_WIN_ROWS = NA_KR + NA_GROUP_ROWS - 1
GLOBAL_TQ = 1024
GLOBAL_TK = 1024
PEER_TM = 1024
PEER_EB = 512
MXU_DEPTH = 256
PEER_MXU_ROWS = 512
PEER_SELECT_UNROLL = 8
VMEM_LIMIT = 56 * 1024 * 1024
PEER_VMEM_LIMIT = 56 * 1024 * 1024


def _cparams(sem, vmem=VMEM_LIMIT):
    return pltpu.CompilerParams(dimension_semantics=sem, vmem_limit_bytes=vmem)


def _dot(a, b):
    return jnp.dot(a, b, preferred_element_type=_F32)


def _dot_nt(a, b):
    return lax.dot_general(a, b, (((1,), (1,)), ((), ())), preferred_element_type=_F32)


def _rms(x, g):
    return x * lax.rsqrt(jnp.mean(x * x, axis=-1, keepdims=True) + RMS_EPS) * g


def _lane_half(shape):
    return lax.broadcasted_iota(jnp.int32, shape, len(shape) - 1) >= (LANE // 2)


def _mod_kernel(c_ref, w_ref, b_ref, o_ref):
    c = c_ref[...]
    a = c * jax.nn.sigmoid(c)
    w = w_ref[0]
    a_hi = a.astype(_MXU)
    a_lo = (a - a_hi.astype(_F32)).astype(_MXU)
    w_hi = w.astype(_MXU)
    w_lo = (w - w_hi.astype(_F32)).astype(_MXU)
    o_ref[0] = _dot(a_hi, w_hi) + _dot(a_lo, w_hi) + _dot(a_hi, w_lo) + b_ref[0]


def _modulation(cc, ada_w, ada_b):
    depth, d, n = ada_w.shape
    rows = cc.shape[0]
    tn = 768
    return pl.pallas_call(
        _mod_kernel,
        grid=(depth, n // tn),
        in_specs=[pl.BlockSpec((rows, d), lambda l, j: (0, 0)),
                  pl.BlockSpec((1, d, tn), lambda l, j: (l, 0, j)),
                  pl.BlockSpec((1, 1, tn), lambda l, j: (l, 0, j))],
        out_specs=pl.BlockSpec((1, rows, tn), lambda l, j: (l, 0, j)),
        out_shape=jax.ShapeDtypeStruct((depth, rows, n), _F32),
        compiler_params=_cparams(("parallel", "parallel")),
        name="adaln_mod",
    )(cc, ada_w, ada_b.reshape(depth, 1, n))


def _rope_tables(seq, d):
    t = jnp.arange(seq, dtype=jnp.int32)
    q = d // 4
    freq = ROPE_BASE ** (-jnp.arange(q, dtype=_F32) / q)

    def one(pos):
        ang = pos.astype(_F32)[:, None] * freq[None, :]
        return jnp.concatenate([jnp.cos(ang)] * 2, -1), jnp.concatenate([jnp.sin(ang)] * 2, -1)

    cr, sr = one(t // GRID_W)
    cc, sc = one(t % GRID_W)
    return jnp.concatenate([cr, cc], -1), jnp.concatenate([sr, sc], -1)


def _rot_cols(w, d):
    k, n = w.shape
    w5 = w.reshape(k, n // d, 2, 2, d // 4)
    return jnp.stack([-w5[:, :, :, 1], w5[:, :, :, 0]], axis=3).reshape(k, n)


def _pad_table(tab, width, ident):
    s, w = tab.shape
    if w < width:
        reps = width // w
        tab = jnp.tile(tab, (1, reps))
    return jnp.concatenate([tab, jnp.full((TM, width), ident, _F32)], axis=0)


def _proj_ab_kernel(x_ref, mod_ref, g_ref, cos_ref, sin_ref, w_ref,
                    aq_ref, ak_ref, av_ref, bq_ref, bk_ref, bv_ref):
    x = x_ref[...]
    h = _rms(x, g_ref[...]) * (1.0 + mod_ref[0, 1:2, :]) + mod_ref[0, 0:1, :]
    hb = h.astype(_MXU)
    cos = cos_ref[...]
    sin = sin_ref[...]

    def proj(lo, n):
        return _dot(hb, w_ref[:, lo:lo + n])

    scale = HEAD_DIM ** -0.5
    aq_ref[...] = (proj(0, 512) * scale).astype(aq_ref.dtype)
    ak_ref[...] = proj(512, 512).astype(ak_ref.dtype)
    av_ref[...] = proj(1024, 512).astype(av_ref.dtype)
    cos4 = jnp.tile(cos, (1, 4))
    sin4 = jnp.tile(sin, (1, 4))
    bq_ref[...] = ((proj(1536, 512) * cos4 + proj(2048, 512) * sin4) * scale).astype(bq_ref.dtype)
    cos2 = jnp.tile(cos, (1, 2))
    sin2 = jnp.tile(sin, (1, 2))
    bk_ref[...] = (proj(2560, 256) * cos2 + proj(2816, 256) * sin2).astype(bk_ref.dtype)
    bv_ref[...] = proj(3072, 256).astype(bv_ref.dtype)


def _proj_ab(x, mod, mod_row, g, cos, sin, tab_row, w):
    n, d = x.shape
    widths = (512, 512, 512, 512, 256, 256)
    return pl.pallas_call(
        _proj_ab_kernel,
        grid=(n // TM,),
        in_specs=[pl.BlockSpec((TM, d), lambda i: (i, 0)),
                  pl.BlockSpec((1, N_MOD, d), lambda i: (mod_row(i), 0, 0)),
                  pl.BlockSpec((1, d), lambda i: (0, 0)),
                  pl.BlockSpec((TM, LANE), lambda i: (tab_row(i), 0)),
                  pl.BlockSpec((TM, LANE), lambda i: (tab_row(i), 0)),
                  pl.BlockSpec(w.shape, lambda i: (0, 0))],
        out_specs=[pl.BlockSpec((TM, wd), lambda i: (i, 0)) for wd in widths],
        out_shape=[jax.ShapeDtypeStruct((n, wd), _MXU) for wd in widths],
        compiler_params=_cparams(("parallel",)),
        name="proj_ab",
    )(x, mod, g, cos, sin, w)


def _na_kernel(q_ref, k_ref, v_ref, kc_ref, vc_ref, bias_ref, o_ref, *, rows):
    g = pl.program_id(1)
    ks = jnp.clip(NA_GROUP_ROWS * g - NA_KR // 2, 0, rows - NA_WIN_ROWS)
    start = pl.multiple_of(ks * GRID_W, GRID_W)
    nwin = NA_WIN_ROWS * GRID_W
    hi = _lane_half((TQ_LOCAL, LANE))

    def scores(h):
        cols = slice((h // 2) * LANE, (h // 2 + 1) * LANE)
        q2 = q_ref[:, cols]
        qh = jnp.where(hi == (h % 2 == 1), q2, jnp.zeros_like(q2))
        return _dot_nt(qh, k_ref[pl.ds(start, nwin), cols]) + bias_ref[0, h], _dot_nt(qh, kc_ref[:, cols])

    outs = []
    nxt = scores(0)
    for h in range(8):
        s_lat, s_ctx = nxt
        if h + 1 < 8:
            nxt = scores(h + 1)
        cols = slice((h // 2) * LANE, (h // 2 + 1) * LANE)
        mx = jnp.maximum(jnp.max(s_lat, axis=-1, keepdims=True), jnp.max(s_ctx, axis=-1, keepdims=True))
        p_lat = jnp.exp(s_lat - mx)
        p_ctx = jnp.exp(s_ctx - mx)
        den = jnp.sum(p_lat, axis=-1, keepdims=True) + jnp.sum(p_ctx, axis=-1, keepdims=True)
        o = _dot(p_lat.astype(_MXU), v_ref[pl.ds(start, nwin), cols]) + _dot(p_ctx.astype(_MXU), vc_ref[:, cols])
        outs.append(o / den)
        if h % 2 == 1:
            o_ref[:, cols] = jnp.where(hi, outs[h], outs[h - 1]).astype(o_ref.dtype)


def _na_bias_tables(rpb, rows):
    gq = NA_GROUP_ROWS
    n_groups = rows // gq
    ql = jnp.arange(gq)[:, None]
    kl = jnp.arange(NA_WIN_ROWS)[None, :]
    qc = jnp.arange(GRID_W)[:, None]
    kc = jnp.arange(GRID_W)[None, :]
    cs = jnp.clip(qc - NA_KC // 2, 0, GRID_W - NA_KC)
    col_valid = (kc >= cs) & (kc < cs + NA_KC)
    col_idx = jnp.clip(kc - qc + NA_KC - 1, 0, 2 * NA_KC - 2)
    exact = lax.Precision.HIGHEST
    oh_c = jax.nn.one_hot(col_idx, 2 * NA_KC - 1, dtype=_F32)
    rpb_cols = jnp.einsum("hab,xyb->haxy", rpb.astype(_F32), oh_c, precision=exact)
    tabs = []
    for g in (0, 1, n_groups - 1):
        ks = min(max(gq * g - NA_KR // 2, 0), rows - NA_WIN_ROWS)
        rq = gq * g + ql
        rk = ks + kl
        r0 = jnp.clip(rq - NA_KR // 2, 0, rows - NA_KR)
        row_valid = (rk >= r0) & (rk < r0 + NA_KR)
        row_idx = jnp.clip(rk - rq + NA_KR - 1, 0, 2 * NA_KR - 2)
        oh_r = jax.nn.one_hot(row_idx, 2 * NA_KR - 1, dtype=_F32)
        b = jnp.einsum("qka,haxy->hqxky", oh_r, rpb_cols, precision=exact)
        valid = row_valid[:, None, :, None] & col_valid[None, :, None, :]
        b = jnp.where(valid[None], b, _NEG)
        tabs.append(b.reshape(rpb.shape[0], gq * GRID_W, NA_WIN_ROWS * GRID_W))
    return jnp.stack(tabs)


def _na_attention(aq, ak, av, cak, cav, bias, batch, seq, ctx_len):
    rows = seq // GRID_W
    n_groups = seq // TQ_LOCAL
    nwin = NA_WIN_ROWS * GRID_W

    def bias_row(b, g):
        return (jnp.where(g == 0, 0, jnp.where(g == n_groups - 1, 2, 1)), 0, 0, 0)

    return pl.pallas_call(
        functools.partial(_na_kernel, rows=rows),
        grid=(batch, n_groups),
        in_specs=[pl.BlockSpec((TQ_LOCAL, 512), lambda b, g: (b * n_groups + g, 0)),
                  pl.BlockSpec((seq, 512), lambda b, g: (b, 0)),
                  pl.BlockSpec((seq, 512), lambda b, g: (b, 0)),
                  pl.BlockSpec((ctx_len, 512), lambda b, g: (b, 0)),
                  pl.BlockSpec((ctx_len, 512), lambda b, g: (b, 0)),
                  pl.BlockSpec((1, 8, TQ_LOCAL, nwin), bias_row)],
        out_specs=pl.BlockSpec((TQ_LOCAL, 512), lambda b, g: (b * n_groups + g, 0)),
        out_shape=jax.ShapeDtypeStruct((batch * seq, 512), _MXU),
        compiler_params=_cparams(("parallel", "arbitrary")),
        name="na_attention",
    )(aq, ak, av, cak, cav, bias)


def _swa_kernel(sink_ref, q_ref, k_ref, v_ref, kc_ref, vc_ref, o_ref, *, seq):
    t = pl.program_id(1)
    kwin = TQ_LOCAL + 2 * SWA_WINDOW
    start = t * TQ_LOCAL
    kstart = pl.multiple_of(jnp.clip(start - SWA_WINDOW, 0, seq - kwin), LANE)
    qpos = start + lax.broadcasted_iota(jnp.int32, (TQ_LOCAL, kwin), 0)
    kpos = kstart + lax.broadcasted_iota(jnp.int32, (TQ_LOCAL, kwin), 1)
    mask = jnp.where(jnp.abs(kpos - qpos) <= SWA_WINDOW, 0.0, _NEG).astype(_F32)
    hi = _lane_half((TQ_LOCAL, LANE))

    def scores(h):
        cols = slice((h // 2) * LANE, (h // 2 + 1) * LANE)
        kcols = slice((h // 4) * LANE, (h // 4 + 1) * LANE)
        q2 = q_ref[:, cols]
        qh = jnp.where(hi == (h % 2 == 1), q2, jnp.zeros_like(q2))
        return _dot_nt(qh, k_ref[pl.ds(kstart, kwin), kcols]) + mask, _dot_nt(qh, kc_ref[:, kcols])

    outs = []
    nxt = scores(0)
    for h in range(8):
        s_lat, s_ctx = nxt
        if h + 1 < 8:
            nxt = scores(h + 1)
        cols = slice((h // 2) * LANE, (h // 2 + 1) * LANE)
        kcols = slice((h // 4) * LANE, (h // 4 + 1) * LANE)
        sink = sink_ref[h]
        mx = jnp.maximum(jnp.max(s_lat, axis=-1, keepdims=True), jnp.max(s_ctx, axis=-1, keepdims=True))
        mx = jnp.maximum(mx, sink)
        p_lat = jnp.exp(s_lat - mx)
        p_ctx = jnp.exp(s_ctx - mx)
        den = (jnp.sum(p_lat, axis=-1, keepdims=True) + jnp.sum(p_ctx, axis=-1, keepdims=True)
               + jnp.exp(sink - mx))
        o = (_dot(p_lat.astype(_MXU), v_ref[pl.ds(kstart, kwin), kcols])
             + _dot(p_ctx.astype(_MXU), vc_ref[:, kcols]))
        outs.append(o / den)
        if h % 2 == 1:
            o_ref[:, cols] = jnp.where(hi, outs[h], outs[h - 1]).astype(o_ref.dtype)


def _swa_attention(sink, bq, bk2, bv2, cbk2, cbv2, batch, seq, ctx_len):
    n_t = seq // TQ_LOCAL
    return pl.pallas_call(
        functools.partial(_swa_kernel, seq=seq),
        grid=(batch, n_t),
        in_specs=[pl.BlockSpec(memory_space=pltpu.SMEM),
                  pl.BlockSpec((TQ_LOCAL, 512), lambda b, t: (b * n_t + t, 0)),
                  pl.BlockSpec((seq, 256), lambda b, t: (b, 0)),
                  pl.BlockSpec((seq, 256), lambda b, t: (b, 0)),
                  pl.BlockSpec((ctx_len, 256), lambda b, t: (b, 0)),
                  pl.BlockSpec((ctx_len, 256), lambda b, t: (b, 0))],
        out_specs=pl.BlockSpec((TQ_LOCAL, 512), lambda b, t: (b * n_t + t, 0)),
        out_shape=jax.ShapeDtypeStruct((batch * seq, 512), _MXU),
        compiler_params=_cparams(("parallel", "arbitrary")),
        name="swa_attention",
    )(sink, bq, bk2, bv2, cbk2, cbv2)


def _flash_kernel(*refs, heads, outs, has_ctx, has_sink, diff_cfg, n_kv):
    it = iter(refs)
    sink_ref = next(it) if has_sink else None
    q_ref, k_ref, v_ref = next(it), next(it), next(it)
    kc_ref = next(it) if has_ctx else None
    vc_ref = next(it) if has_ctx else None
    lam_ref = next(it) if diff_cfg else None
    sg_ref = next(it) if diff_cfg else None
    o_ref = next(it)
    m_ref, l_ref, acc_ref = next(it), next(it), next(it)
    kv = pl.program_id(2)
    tq = q_ref.shape[0]
    hi = _lane_half((tq, LANE))

    @pl.when(kv == 0)
    def _init():
        for h in range(len(heads)):
            if has_sink:
                m_ref[h] = jnp.full((tq, LANE), sink_ref[h], _F32)
                l_ref[h] = jnp.ones((tq, LANE), _F32)
            else:
                m_ref[h] = jnp.full((tq, LANE), _NEG, _F32)
                l_ref[h] = jnp.zeros((tq, LANE), _F32)
            acc_ref[h] = jnp.zeros((tq, LANE), _F32)

    def attend(kr, vr):
        for h, (qb, qhalf, kb, vb) in enumerate(heads):
            q2 = q_ref[:, qb * LANE:(qb + 1) * LANE]
            if qhalf is not None:
                q2 = jnp.where(hi == (qhalf == 1), q2, jnp.zeros_like(q2))
            s = _dot_nt(q2, kr[:, kb * LANE:(kb + 1) * LANE])
            m_old = m_ref[h][:, :1]
            m_new = jnp.maximum(m_old, jnp.max(s, axis=-1, keepdims=True))
            alpha = jnp.exp(m_old - m_new)
            p = jnp.exp(s - m_new)
            l_ref[h] = jnp.broadcast_to(alpha * l_ref[h][:, :1] + jnp.sum(p, axis=-1, keepdims=True), (tq, LANE))
            acc_ref[h] = alpha * acc_ref[h] + _dot(p.astype(_MXU), vr[:, vb * LANE:(vb + 1) * LANE])
            m_ref[h] = jnp.broadcast_to(m_new, (tq, LANE))

    if has_ctx:
        @pl.when(kv == 0)
        def _ctx():
            attend(kc_ref, vc_ref)

    attend(k_ref, v_ref)

    @pl.when(kv == n_kv - 1)
    def _fin():
        def head_out(h):
            return acc_ref[h] / l_ref[h][:, :1]

        for j, spec in enumerate(outs):
            if spec[0] == "full":
                o = head_out(spec[1])
            elif spec[0] == "pair":
                o = jnp.where(hi, head_out(spec[2]), head_out(spec[1]))
            else:
                lv = lam_ref[...]
                lam = (jnp.exp(jnp.sum(lv[0:1] * lv[1:2], axis=-1, keepdims=True))
                       - jnp.exp(jnp.sum(lv[2:3] * lv[3:4], axis=-1, keepdims=True)) + diff_cfg)
                o = _rms(head_out(spec[1]) - lam * head_out(spec[2]), sg_ref[...]) * (1.0 - diff_cfg)
            o_ref[:, j * LANE:(j + 1) * LANE] = o.astype(o_ref.dtype)


def _flash_attention(q, k, v, *, heads, outs, batch, q_per_batch, kv_per_batch, tq, tk,
                     q_row0=0, kv_row0=0, ctx=None, ctx_len=0, ctx_row0=0, sink=None, diff=None, name="flash"):
    n_q = q_per_batch // tq
    n_kv = kv_per_batch // tk
    qw, kw, vw = q.shape[1], k.shape[1], v.shape[1]
    args, in_specs = [], []
    if sink is not None:
        args.append(sink)
        in_specs.append(pl.BlockSpec(memory_space=pltpu.SMEM))
    q0, k0 = q_row0 // tq, kv_row0 // tk
    args += [q, k, v]
    in_specs += [pl.BlockSpec((tq, qw), lambda b, i, j: (q0 + b * n_q + i, 0)),
                 pl.BlockSpec((tk, kw), lambda b, i, j: (k0 + b * n_kv + j, 0)),
                 pl.BlockSpec((tk, vw), lambda b, i, j: (k0 + b * n_kv + j, 0))]
    if ctx is not None:
        c0 = ctx_row0 // ctx_len
        args += [ctx[0], ctx[1]]
        in_specs += [pl.BlockSpec((ctx_len, kw), lambda b, i, j: (c0 + b, 0)),
                     pl.BlockSpec((ctx_len, vw), lambda b, i, j: (c0 + b, 0))]
    diff_cfg = None
    if diff is not None:
        lam_vecs, subln_g, diff_cfg = diff
        args += [lam_vecs, subln_g]
        in_specs += [pl.BlockSpec(lam_vecs.shape, lambda b, i, j: (0, 0)),
                     pl.BlockSpec(subln_g.shape, lambda b, i, j: (0, 0))]
    nh = len(heads)
    ow = len(outs) * LANE
    return pl.pallas_call(
        functools.partial(_flash_kernel, heads=tuple(heads), outs=tuple(outs), has_ctx=ctx is not None,
                          has_sink=sink is not None, diff_cfg=diff_cfg, n_kv=n_kv),
        grid=(batch, n_q, n_kv),
        in_specs=in_specs,
        out_specs=pl.BlockSpec((tq, ow), lambda b, i, j: (b * n_q + i, 0)),
        out_shape=jax.ShapeDtypeStruct((batch * q_per_batch, ow), _MXU),
        scratch_shapes=[pltpu.VMEM((nh, tq, LANE), _F32)] * 3,
        compiler_params=_cparams(("parallel", "parallel", "arbitrary")),
        name=name,
    )(*args)


def _flasht_kernel(*refs, heads, outs, diff_cfg, n_kv):
    it = iter(refs)
    qt_ref, k_ref, vt_ref, kc_ref, vct_ref = next(it), next(it), next(it), next(it), next(it)
    lam_ref = next(it) if diff_cfg else None
    sg_ref = next(it) if diff_cfg else None
    o_ref = next(it)
    m_ref, l_ref, acc_ref = next(it), next(it), next(it)
    kv = pl.program_id(2)
    tq = qt_ref.shape[1]
    row_hi = lax.broadcasted_iota(jnp.int32, (LANE, tq), 0) >= (LANE // 2)

    @pl.when(kv == 0)
    def _init():
        m_ref[...] = jnp.full(m_ref.shape, _NEG, _F32)
        l_ref[...] = jnp.zeros(l_ref.shape, _F32)
        acc_ref[...] = jnp.zeros(acc_ref.shape, _F32)

    def attend(kr, vtr):
        def scores(h):
            qb, qhalf, kb, _ = heads[h]
            qt = qt_ref[qb * LANE:(qb + 1) * LANE, :]
            if qhalf is not None:
                qt = jnp.where(row_hi == (qhalf == 1), qt, jnp.zeros_like(qt))
            return _dot(kr[:, kb * LANE:(kb + 1) * LANE], qt)

        s_next = scores(0)
        for h in range(len(heads)):
            s = s_next
            if h + 1 < len(heads):
                s_next = scores(h + 1)
            vb = heads[h][3]
            m_old = m_ref[h, 0:1, :]
            m_new = jnp.maximum(m_old, jnp.max(s, axis=0, keepdims=True))
            alpha = jnp.exp2(m_old - m_new)
            p = jnp.exp2(s - m_new)
            l_new = alpha * l_ref[h, 0:1, :] + jnp.sum(p, axis=0, keepdims=True)
            acc_ref[h] = alpha * acc_ref[h] + _dot(vtr[vb * LANE:(vb + 1) * LANE, :], p.astype(_MXU))
            l_ref[h] = jnp.broadcast_to(l_new, (8, tq))
            m_ref[h] = jnp.broadcast_to(m_new, (8, tq))

    @pl.when(kv == 0)
    def _ctx():
        attend(kc_ref, vct_ref)

    attend(k_ref, vt_ref)

    @pl.when(kv == n_kv - 1)
    def _fin():
        def head_out(h):
            return acc_ref[h] / l_ref[h, 0:1, :]

        for j, spec in enumerate(outs):
            if spec[0] == "full":
                o = head_out(spec[1]).T
            elif spec[0] == "pair":
                o = jnp.where(row_hi, head_out(spec[2]), head_out(spec[1])).T
            else:
                lv = lam_ref[...]
                lam = (jnp.exp(jnp.sum(lv[0:1] * lv[1:2], axis=-1, keepdims=True))
                       - jnp.exp(jnp.sum(lv[2:3] * lv[3:4], axis=-1, keepdims=True)) + diff_cfg)
                o = _rms((head_out(spec[1]) - lam * head_out(spec[2])).T, sg_ref[...]) * (1.0 - diff_cfg)
            o_ref[:, j * LANE:(j + 1) * LANE] = o.astype(o_ref.dtype)


def _flasht_attention(qt, k, vt, kc, vct, *, heads, outs, batch, seq, ctx_len, tq, tk, diff=None, name="flasht"):
    n_q, n_kv = seq // tq, seq // tk
    qw, kw, vw = qt.shape[0], k.shape[1], vt.shape[0]
    args = [qt, k, vt, kc, vct]
    in_specs = [pl.BlockSpec((qw, tq), lambda b, i, j: (0, b * n_q + i)),
                pl.BlockSpec((tk, kw), lambda b, i, j: (b * n_kv + j, 0)),
                pl.BlockSpec((vw, tk), lambda b, i, j: (0, b * n_kv + j)),
                pl.BlockSpec((ctx_len, kw), lambda b, i, j: (b, 0)),
                pl.BlockSpec((vw, ctx_len), lambda b, i, j: (0, b))]
    diff_cfg = None
    if diff is not None:
        lam_vecs, subln_g, diff_cfg = diff
        args += [lam_vecs, subln_g]
        in_specs += [pl.BlockSpec(lam_vecs.shape, lambda b, i, j: (0, 0)),
                     pl.BlockSpec(subln_g.shape, lambda b, i, j: (0, 0))]
    nh = len(heads)
    ow = len(outs) * LANE
    return pl.pallas_call(
        functools.partial(_flasht_kernel, heads=tuple(heads), outs=tuple(outs), diff_cfg=diff_cfg, n_kv=n_kv),
        grid=(batch, n_q, n_kv),
        in_specs=in_specs,
        out_specs=pl.BlockSpec((tq, ow), lambda b, i, j: (b * n_q + i, 0)),
        out_shape=jax.ShapeDtypeStruct((batch * seq, ow), _MXU),
        scratch_shapes=[pltpu.VMEM((nh, 8, tq), _F32), pltpu.VMEM((nh, 8, tq), _F32),
                        pltpu.VMEM((nh, LANE, tq), _F32)],
        compiler_params=_cparams(("parallel", "parallel", "arbitrary")),
        name=name,
    )(*args)


def _out_kernel(x_ref, ya_ref, yb_ref, mod_ref, g2_ref, wa_ref, wb_ref, xo_ref, h2_ref):
    y = _dot(ya_ref[...], wa_ref[...]) + _dot(yb_ref[...], wb_ref[...])
    x = x_ref[...] + mod_ref[0, 2:3, :] * y
    xo_ref[...] = x
    h2 = _rms(x, g2_ref[...]) * (1.0 + mod_ref[0, 4:5, :]) + mod_ref[0, 3:4, :]
    h2_ref[...] = h2.astype(h2_ref.dtype)


def _out_proj(x, ya, yb, mod, mod_row, g2, wa, wb):
    n, d = x.shape
    return pl.pallas_call(
        _out_kernel,
        grid=(n // TM,),
        in_specs=[pl.BlockSpec((TM, d), lambda i: (i, 0)),
                  pl.BlockSpec((TM, ya.shape[1]), lambda i: (i, 0)),
                  pl.BlockSpec((TM, yb.shape[1]), lambda i: (i, 0)),
                  pl.BlockSpec((1, N_MOD, d), lambda i: (mod_row(i), 0, 0)),
                  pl.BlockSpec((1, d), lambda i: (0, 0)),
                  pl.BlockSpec(wa.shape, lambda i: (0, 0)),
                  pl.BlockSpec(wb.shape, lambda i: (0, 0))],
        out_specs=[pl.BlockSpec((TM, d), lambda i: (i, 0)), pl.BlockSpec((TM, d), lambda i: (i, 0))],
        out_shape=[jax.ShapeDtypeStruct((n, d), _F32), jax.ShapeDtypeStruct((n, d), _MXU)],
        compiler_params=_cparams(("parallel",)),
        name="out_proj",
    )(x, ya, yb, mod, g2, wa, wb)


def _proj_cd_kernel(x_ref, mod_ref, g_ref, cq_ref, sq_ref, cd_ref, sd_ref, w_ref, qg_ref, kvg_ref,
                    wqa_ref, wqb_ref, wk_ref, wv_ref, place_ref,
                    qm_ref, km_ref, vm_ref, dq_ref, dk_ref, dv_ref):
    x = x_ref[...]
    h = _rms(x, g_ref[...]) * (1.0 + mod_ref[0, 1:2, :]) + mod_ref[0, 0:1, :]
    hb = h.astype(_MXU)

    def proj(lo, n):
        return _dot(hb, w_ref[:, lo:lo + n])

    cosq, sinq = cq_ref[...], sq_ref[...]
    cosd, sind = cd_ref[...], sd_ref[...]
    cqn = _rms(proj(0, 256), qg_ref[...]).astype(_MXU)
    ckvn = _rms(proj(256, 128), kvg_ref[...]).astype(_MXU)
    cos8, sin8 = jnp.tile(cosq, (1, 8)), jnp.tile(sinq, (1, 8))
    qm = _dot(cqn, wqa_ref[...]) * cos8 + _dot(cqn, wqb_ref[...]) * sin8
    qm_ref[...] = (qm * ((MLA_NOPE + MLA_ROPE) ** -0.5 * _LOG2E)).T.astype(qm_ref.dtype)
    kr = (proj(384, 128) * cosq + proj(512, 128) * sinq).astype(_MXU)
    km_ref[...] = (_dot(ckvn, wk_ref[...]) + _dot(kr, place_ref[...])).astype(km_ref.dtype)
    vm_ref[...] = _dot(ckvn, wv_ref[...]).T.astype(vm_ref.dtype)
    cos4, sin4 = jnp.tile(cosd, (1, 4)), jnp.tile(sind, (1, 4))
    dq_ref[...] = ((proj(640, 512) * cos4 + proj(1152, 512) * sin4) * (HEAD_DIM ** -0.5 * _LOG2E)).T.astype(dq_ref.dtype)
    dk_ref[...] = (proj(1664, 512) * cos4 + proj(2176, 512) * sin4).astype(dk_ref.dtype)
    dv_ref[...] = proj(2688, 512).T.astype(dv_ref.dtype)


def _proj_cd(x, mod, mod_row, g, tabs, tab_row, w, qg, kvg, wqa, wqb, wk, wv, place):
    n, d = x.shape
    outs = ((1024, True), (1024, False), (512, True), (512, True), (512, False), (512, True))
    full = lambda a: pl.BlockSpec(a.shape, lambda i: (0, 0))
    return pl.pallas_call(
        _proj_cd_kernel,
        grid=(n // TM,),
        in_specs=[pl.BlockSpec((TM, d), lambda i: (i, 0)),
                  pl.BlockSpec((1, N_MOD, d), lambda i: (mod_row(i), 0, 0)),
                  pl.BlockSpec((1, d), lambda i: (0, 0))]
                 + [pl.BlockSpec((TM, LANE), lambda i: (tab_row(i), 0))] * 4
                 + [full(a) for a in (w, qg, kvg, wqa, wqb, wk, wv, place)],
        out_specs=[pl.BlockSpec((wd, TM), lambda i: (0, i)) if fm else pl.BlockSpec((TM, wd), lambda i: (i, 0))
                   for wd, fm in outs],
        out_shape=[jax.ShapeDtypeStruct((wd, n) if fm else (n, wd), _MXU) for wd, fm in outs],
        compiler_params=_cparams(("parallel",)),
        name="proj_cd",
    )(x, mod, g, *tabs, w, qg, kvg, wqa, wqb, wk, wv, place)


_MARK = 2.0 ** 121


def _extract16(x, first_only):
    r = x.shape[0]
    row = lax.broadcasted_iota(jnp.int32, x.shape, 0)
    row16 = lax.broadcasted_iota(jnp.int32, (PEER_TOPK, LANE), 0)
    vals = jnp.zeros((PEER_TOPK, LANE), _F32)
    for k in range(PEER_TOPK):
        m = jnp.max(x, axis=0, keepdims=True)
        hit = x == m
        if first_only:
            hit = row == jnp.min(jnp.where(hit, row, r), axis=0, keepdims=True)
        x = jnp.where(hit, -_MARK * (32 + k), x)
        vals = jnp.where(row16 == k, m, vals)
    order = jnp.where(x < -16.0 * _MARK, x * (-1.0 / _MARK) - 32.0, float(PEER_TOPK))
    return order, vals


def _staircase_rows():
    groups = []
    for r in range(PEER_TOPK // 2):
        n = PEER_TOPK // (r + 1)
        for c0 in range(0, n, 8):
            groups.append((r, c0, min(8, n - c0)))
    groups.append((None, 0, 8))
    return groups


def _peer_select(sa, sb, first_only):
    ra, av = _extract16(sa, first_only)
    rb, bv = _extract16(sb, first_only)
    groups = _staircase_rows()
    sub = lax.broadcasted_iota(jnp.int32, (8, LANE), 0)
    pieces = []
    for r, c0, nv in groups:
        if r is None:
            piece = av[8:16] + bv[0:1]
        else:
            piece = av[r:r + 1] + bv[c0:c0 + 8]
            if nv < 8:
                piece = jnp.where(sub < nv, piece, -jnp.inf)
        pieces.append(piece)
    cand = jnp.concatenate(pieces, axis=0)
    e_cand = jnp.exp(cand - (av[0:1] + bv[0:1]))
    sel, _ = _extract16(cand, first_only)
    sel = jnp.where(sel < float(PEER_TOPK), 1.0, 0.0)
    z = jnp.sum(sel * e_cand, axis=0, keepdims=True)
    lr = jnp.zeros(sa.shape, _F32)
    for g, (r, c0, nv) in enumerate(groups):
        blk = sel[8 * g:8 * g + 8]
        if r is None:
            for q in range(8):
                lr = jnp.where(ra == float(8 + q), blk[q:q + 1], lr)
        elif c0 == 0:
            cnt = jnp.sum(blk, axis=0, keepdims=True)
            if PEER_TOPK // (r + 1) > 8:
                cnt = cnt + jnp.sum(sel[8 * g + 8:8 * g + 16], axis=0, keepdims=True)
            lr = jnp.where(ra == float(r), cnt, lr)
    ea = jnp.exp(sa - av[0:1]) / z
    eb = jnp.exp(sb - bv[0:1])
    n_sel = (jnp.sum(jnp.where(ra < float(PEER_TOPK), 1.0, 0.0), axis=0, keepdims=True)
             + jnp.sum(jnp.where(rb < float(PEER_TOPK), 1.0, 0.0), axis=0, keepdims=True)
             + jnp.sum(sel, axis=0, keepdims=True))
    return lr, ea, rb, eb, n_sel


def _peer_kernel(h2_ref, x_ref, mod_ref, wqt_ref, keys_ref, u_ref, vt_ref, *rest, n_eb, final):
    fg_ref = rest[0] if final else None
    (o_ref, h2t_ref, lr_ref, ea_ref, rb_ref, eb_ref, ft_ref, a0_ref, a1_ref, w0_ref, w1_ref,
     flag_ref) = rest[1:] if final else rest
    s = pl.program_id(1)
    n_chunks = PEER_TM // LANE
    i_per = PEER_EB // PEER_NKEYS
    gdt = rb_ref.dtype

    @pl.when(s == 0)
    def _select():
        h2t = h2_ref[...].astype(_F32).T.astype(_MXU)
        h2t_ref[...] = h2t
        ft_ref[...] = jnp.zeros_like(ft_ref)
        a1_ref[...] = jnp.zeros_like(a1_ref)
        w0_ref[...] = jnp.zeros_like(w0_ref)

        for hp in range(2 * PEER_HEADS):
            qt = _dot(wqt_ref[hp * LANE:(hp + 1) * LANE, :], h2t).astype(_MXU)
            st = _dot(keys_ref[hp % 2], qt)
            for c in range(n_chunks):
                (lr_ref if hp % 2 == 0 else ea_ref)[c, hp // 2] = st[:, c * LANE:(c + 1) * LANE]

        def unit(c, h, sa, sb, first_only):
            lr, ea, rb, eb, n_sel = _peer_select(sa, sb, first_only)
            lr_ref[c, h] = lr
            ea_ref[c, h] = ea
            rb_ref[c, h] = rb.astype(gdt)
            eb_ref[c, h] = eb.astype(gdt)
            return jnp.max(n_sel) > 3.0 * PEER_TOPK

        per_trip = PEER_SELECT_UNROLL

        def fast(t, carry):
            c, h0 = t // (PEER_HEADS // per_trip), (t % (PEER_HEADS // per_trip)) * per_trip
            for u in range(per_trip):
                tied = unit(c, h0 + u, lr_ref[c, h0 + u], ea_ref[c, h0 + u], False)
                flag_ref[c * PEER_HEADS + h0 + u] = tied.astype(jnp.int32)
            return carry

        lax.fori_loop(0, n_chunks * PEER_HEADS // per_trip, fast, 0)

        def exact(t, carry):
            @pl.when(flag_ref[t] != 0)
            def _redo():
                c, h = t // PEER_HEADS, t % PEER_HEADS
                w2 = wqt_ref[pl.ds(pl.multiple_of(h * 2 * LANE, 2 * LANE), 2 * LANE), :]
                h2c = h2_ref[pl.ds(pl.multiple_of(c * LANE, LANE), LANE), :]
                qt = _dot_nt(w2, h2c).astype(_MXU)
                unit(c, h, _dot(keys_ref[0], qt[:LANE]), _dot(keys_ref[1], qt[LANE:]), True)
            return carry

        lax.fori_loop(0, n_chunks * PEER_HEADS, exact, 0)

    def step(half, a_new, a_prev, w_new, w_prev):
        blk = jnp.clip(2 * s + half - 1, 0, n_eb - 1)
        e0 = half * PEER_EB

        def gate(ii, c):
            i = blk * i_per + ii
            rows = slice(ii * PEER_NKEYS, (ii + 1) * PEER_NKEYS)
            cols = slice(c * LANE, (c + 1) * LANE)
            g = jnp.zeros((PEER_NKEYS, LANE), gdt)
            for h in range(PEER_HEADS):
                lr_i = lr_ref[c, h, pl.ds(i, 1), :].astype(gdt)
                ea_i = ea_ref[c, h, pl.ds(i, 1), :].astype(gdt)
                g = g + jnp.where(rb_ref[c, h] < lr_i, eb_ref[c, h], jnp.zeros((), gdt)) * ea_i
            a = a_prev[rows, cols]
            gelu = 0.5 * a * (1.0 + lax.erf(a * (2.0 ** -0.5)))
            w_new[rows, cols] = gelu.astype(gdt) * g

        units = [(ii, c) for ii in range(i_per) for c in range(n_chunks)]
        kd, mr = MXU_DEPTH, PEER_MXU_ROWS
        d_model = u_ref.shape[1]
        n_pieces = (PEER_EB // mr) * (d_model // kd) + (d_model // mr) * (PEER_EB // kd)
        per = len(units) // n_pieces
        done = 0
        for r0 in range(0, PEER_EB, mr):
            acc = None
            for k0 in range(0, d_model, kd):
                part = _dot(u_ref[e0 + r0:e0 + r0 + mr, k0:k0 + kd], h2t_ref[k0:k0 + kd, :])
                acc = part if acc is None else acc + part
                for ii, c in units[done:done + per]:
                    gate(ii, c)
                done += per
            a_new[r0:r0 + mr, :] = acc
        for r0 in range(0, d_model, mr):
            acc = ft_ref[r0:r0 + mr, :]
            for k0 in range(0, PEER_EB, kd):
                acc = acc + _dot(vt_ref[0, r0:r0 + mr, e0 + k0:e0 + k0 + kd], w_prev[k0:k0 + kd, :])
                for ii, c in units[done:done + per]:
                    gate(ii, c)
                done += per
            ft_ref[r0:r0 + mr, :] = acc
        for ii, c in units[done:]:
            gate(ii, c)

    @pl.when(s <= n_eb // 2)
    def _first():
        step(0, a0_ref, a1_ref, w1_ref, w0_ref)

    @pl.when(s >= 0)
    def _second():
        step(1, a1_ref, a0_ref, w0_ref, w1_ref)

    @pl.when(s == n_eb // 2)
    def _fin():
        x = x_ref[...] + mod_ref[0, 5:6, :] * ft_ref[...].T
        if final:
            x = _rms(x, fg_ref[...])
        o_ref[...] = x


def _peer(h2, x, mod, mod_row, wqt, keys, u, vt, final_g=None):
    n, d = x.shape
    n_exp = u.shape[0]
    n_eb = n_exp // PEER_EB
    tm = PEER_TM
    n_chunks = tm // LANE
    final = final_g is not None
    once = pl.Buffered(1)
    args = [h2, x, mod, wqt, keys, u, vt]
    in_specs = [pl.BlockSpec((tm, d), lambda i, e: (i, 0), pipeline_mode=once),
                pl.BlockSpec((tm, d), lambda i, e: (i, 0), pipeline_mode=once),
                pl.BlockSpec((1, N_MOD, d), lambda i, e: (mod_row(i), 0, 0)),
                pl.BlockSpec(wqt.shape, lambda i, e: (0, 0), pipeline_mode=once),
                pl.BlockSpec(keys.shape, lambda i, e: (0, 0, 0)),
                pl.BlockSpec((2 * PEER_EB, d), lambda i, e: (jnp.minimum(e, n_eb // 2 - 1), 0)),
                pl.BlockSpec((1, d, 2 * PEER_EB), lambda i, e: (jnp.clip(e - 1, 0, n_eb // 2 - 1), 0, 0))]
    if final:
        args.append(final_g)
        in_specs.append(pl.BlockSpec((1, d), lambda i, e: (0, 0)))
    tab32 = pltpu.VMEM((n_chunks, PEER_HEADS, PEER_NKEYS, LANE), _F32)
    tab16 = pltpu.VMEM((n_chunks, PEER_HEADS, PEER_NKEYS, LANE), _MXU)
    abuf = pltpu.VMEM((PEER_EB, tm), _F32)
    wbuf = pltpu.VMEM((PEER_EB, tm), _MXU)
    return pl.pallas_call(
        functools.partial(_peer_kernel, n_eb=n_eb, final=final),
        grid=(n // tm, n_eb // 2 + 1),
        in_specs=in_specs,
        out_specs=pl.BlockSpec((tm, d), lambda i, e: (i, 0), pipeline_mode=once),
        out_shape=jax.ShapeDtypeStruct((n, d), _F32),
        scratch_shapes=[pltpu.VMEM((d, tm), _MXU), tab32, tab32, tab16, tab16,
                        pltpu.VMEM((d, tm), _F32), abuf, abuf, wbuf, wbuf,
                        pltpu.SMEM((n_chunks * PEER_HEADS,), jnp.int32)],
        compiler_params=_cparams(("parallel", "arbitrary"), PEER_VMEM_LIMIT),
        name="peer_ffn",
    )(*args)


def _dup_halves(w):
    a, b = w[:, :HEAD_DIM], w[:, HEAD_DIM:]
    return jnp.concatenate([a, a, b, b], axis=1)


def _prep_ab(w_in):
    aq, ak, av = w_in[:, 0:512], w_in[:, 512:1024], w_in[:, 1024:1536]
    bq, bk, bv = w_in[:, 1536:2048], w_in[:, 2048:2176], w_in[:, 2176:2304]
    cat = [aq, ak, av, bq, _rot_cols(bq, HEAD_DIM), _dup_halves(bk), _dup_halves(_rot_cols(bk, HEAD_DIM)),
           _dup_halves(bv)]
    return jnp.concatenate(cat, axis=1).astype(_MXU)


def _prep_cd(w_in, w_uq, w_ukv):
    d = w_in.shape[0]
    cq, ckv, kr = w_in[:, 0:256], w_in[:, 256:384], w_in[:, 384:416]
    dq, dk, dv = w_in[:, 416:928], w_in[:, 928:1440], w_in[:, 1440:1952]
    z64, z32 = jnp.zeros((d, 64), _F32), jnp.zeros((d, 32), _F32)
    kr128 = jnp.concatenate([z64, kr, z32], axis=1)
    krrot128 = jnp.concatenate([z64, _rot_cols(kr, MLA_ROPE), z32], axis=1)
    w = jnp.concatenate([cq, ckv, kr128, krrot128, dq, _rot_cols(dq, HEAD_DIM), dk, _rot_cols(dk, HEAD_DIM), dv],
                        axis=1).astype(_MXU)
    r = w_uq.shape[0]
    uq = w_uq.reshape(r, 8, MLA_NOPE + MLA_ROPE)
    nope, rope = uq[:, :, :MLA_NOPE], uq[:, :, MLA_NOPE:]
    rope_rot = _rot_cols(rope.reshape(r, 8 * MLA_ROPE), MLA_ROPE).reshape(r, 8, MLA_ROPE)
    zq = jnp.zeros((r, 8, 32), _F32)
    wqa = jnp.concatenate([nope, rope, zq], axis=2).reshape(r, 1024).astype(_MXU)
    wqb = jnp.concatenate([jnp.zeros_like(nope), rope_rot, zq], axis=2).reshape(r, 1024).astype(_MXU)
    rk = w_ukv.shape[0]
    ukv = w_ukv.reshape(rk, 8, 128)
    wk = jnp.concatenate([ukv[:, :, :MLA_NOPE], jnp.zeros((rk, 8, 64), _F32)], axis=2).reshape(rk, 1024).astype(_MXU)
    wv = ukv[:, :, MLA_NOPE:].reshape(rk, 512).astype(_MXU)
    lane = jnp.arange(LANE)
    src = (lane >= MLA_NOPE) & (lane < MLA_NOPE + MLA_ROPE)
    place = (src[:, None] & (lane[:, None] == (jnp.arange(1024)[None, :] % LANE))).astype(_MXU)
    return w, wqa, wqb, wk, wv, place


def kernel(x, c, ctx, c_ctx, ada_w, ada_b, norm1_g, norm2_g, w_out, peer_wq, peer_keys, peer_u, peer_v,
           ab_w_in, na_rpb, swa_sink, cd_w_in, mla_q_norm_g, mla_w_uq, mla_kv_norm_g, mla_w_ukv,
           diff_lambda, diff_subln_g, final_norm_g):
    batch, seq, d = x.shape
    ctx_len = ctx.shape[1]
    depth = ada_w.shape[0]
    assert seq % TM == 0 and (batch * ctx_len) % TM == 0 and seq % TQ_LOCAL == 0
    assert seq % PEER_TM == 0 and (batch * ctx_len) % PEER_TM == 0
    assert depth == 2, "even layers keep a context stream, the single odd layer is the last one"
    n_lat, n_ctx = batch * seq, batch * ctx_len
    xs = x.reshape(n_lat, d)
    cs = ctx.reshape(n_ctx, d)

    mod_rows = -(-(batch + 1) // 16) * 16
    cc = jnp.zeros((mod_rows, d), _F32).at[:batch].set(c).at[batch].set(c_ctx)
    mod_all = _modulation(cc, ada_w, ada_b)

    tiles_per_seq = seq // TM
    lat_row = lambda i: i // tiles_per_seq
    ctx_row = lambda i: batch
    lat_tab = lambda i: i % tiles_per_seq
    ctx_tab = lambda i: tiles_per_seq

    cos64, sin64 = _rope_tables(seq, HEAD_DIM)
    cos64p, sin64p = _pad_table(cos64, LANE, 1.0), _pad_table(sin64, LANE, 0.0)

    for l in range(depth):
        last = l == depth - 1
        j = l // 2
        mod = mod_all[l, :batch + 1].reshape(batch + 1, N_MOD, d)
        g1 = norm1_g[l].reshape(1, d)
        g2 = norm2_g[l].reshape(1, d)
        wo = w_out[l].astype(_MXU)
        wqt = peer_wq[l].T.astype(_MXU)
        keys = peer_keys[l].astype(_MXU)
        u = peer_u[l].astype(_MXU)
        vt = peer_v[l].astype(_MXU).reshape(-1, 2 * PEER_EB, d).transpose(0, 2, 1)
        if l % 2 == 0:
            w = _prep_ab(ab_w_in[j])
            aq, ak, av, bq, bk2, bv2 = _proj_ab(xs, mod, lat_row, g1, cos64p, sin64p, lat_tab, w)
            caq, cak, cav, cbq, cbk2, cbv2 = _proj_ab(cs, mod, ctx_row, g1, cos64p, sin64p, ctx_tab, w)
            bias = _na_bias_tables(na_rpb[j], seq // GRID_W)
            ya = _na_attention(aq, ak, av, cak, cav, bias, batch, seq, ctx_len)
            sink = swa_sink[j].astype(_F32)
            yb = _swa_attention(sink, bq, bk2, bv2, cbk2, cbv2, batch, seq, ctx_len)
            xs, h2 = _out_proj(xs, ya, yb, mod, lat_row, g2, wo[:512], wo[512:])
            if not last:
                pair_heads = [(m, half, m, m) for m in range(4) for half in range(2)]
                pair_outs = [("pair", 2 * m, 2 * m + 1) for m in range(4)]
                yca = _flash_attention(caq, cak, cav, heads=pair_heads, outs=pair_outs, batch=batch,
                                       q_per_batch=ctx_len, kv_per_batch=ctx_len, tq=ctx_len, tk=ctx_len,
                                       name="ctx_attn_a")
                gqa_heads = [(m, half, m // 2, m // 2) for m in range(4) for half in range(2)]
                ycb = _flash_attention(cbq, cbk2, cbv2, heads=gqa_heads, outs=pair_outs, batch=batch,
                                       q_per_batch=ctx_len, kv_per_batch=ctx_len, tq=ctx_len, tk=ctx_len,
                                       sink=sink, name="ctx_attn_b")
                cs, h2c = _out_proj(cs, yca, ycb, mod, ctx_row, g2, wo[:512], wo[512:])
        else:
            lam_init = 0.8 - 0.6 * math.exp(-0.3 * l)
            w, wqa, wqb, wk, wv, place = _prep_cd(cd_w_in[j], mla_w_uq[j], mla_w_ukv[j])
            cos32, sin32 = _rope_tables(seq, MLA_ROPE)
            ones64, zeros64 = jnp.ones((seq, 64), _F32), jnp.zeros((seq, 64), _F32)
            cosq = _pad_table(jnp.concatenate([ones64, cos32, ones64[:, :32]], 1), LANE, 1.0)
            sinq = _pad_table(jnp.concatenate([zeros64, sin32, zeros64[:, :32]], 1), LANE, 0.0)
            tabs = (cosq, sinq, cos64p, sin64p)
            qg = mla_q_norm_g[j].reshape(1, -1)
            kvg = mla_kv_norm_g[j].reshape(1, -1)
            qmt, km, vmt, dqt, dk, dvt = _proj_cd(xs, mod, lat_row, g1, tabs, lat_tab, w, qg, kvg, wqa, wqb, wk, wv, place)
            _, ckm, cvmt, _, cdk, cdvt = _proj_cd(cs, mod, ctx_row, g1, tabs, ctx_tab, w, qg, kvg, wqa, wqb, wk, wv, place)
            mla_heads = [(h, None, h, h // 2) for h in range(8)]
            pair_outs = [("pair", 2 * m, 2 * m + 1) for m in range(4)]
            yc = _flasht_attention(qmt, km, vmt, ckm, cvmt, heads=mla_heads, outs=pair_outs, batch=batch, seq=seq,
                                   ctx_len=ctx_len, tq=GLOBAL_TQ, tk=GLOBAL_TK, name="mla_attn")
            diff_heads = [(h, a, h, h) for h in range(4) for a in range(2)]
            diff_outs = [("diff", 2 * h, 2 * h + 1) for h in range(4)]
            od = _flasht_attention(dqt, dk, dvt, cdk, cdvt, heads=diff_heads, outs=diff_outs, batch=batch, seq=seq,
                                   ctx_len=ctx_len, tq=GLOBAL_TQ, tk=GLOBAL_TK,
                                   diff=(diff_lambda[j].astype(_F32), diff_subln_g[j].reshape(1, -1), lam_init),
                                   name="diff_attn")
            xs, h2 = _out_proj(xs, yc, od, mod, lat_row, g2, wo[:512], wo[512:])
        fg = final_norm_g.reshape(1, d) if last else None
        xs = _peer(h2, xs, mod, lambda i: i // (seq // PEER_TM), wqt, keys, u, vt, final_g=fg)
        if not last:
            cs = _peer(h2c, cs, mod, ctx_row, wqt, keys, u, vt)
    return xs.reshape(batch, seq, d)
```

```python
import functools
import math

import jax
import jax.numpy as jnp
from jax import lax
from jax.experimental import pallas as pl
from jax.experimental.pallas import tpu as pltpu

_F32 = jnp.float32
_MXU = jnp.bfloat16
_NEG = -1e30
_LOG2E = math.log2(math.e)

GRID_W = 64
HEAD_DIM = 64
ROPE_BASE = 10000.0
RMS_EPS = 1e-6
N_MOD = 6
NA_KR, NA_KC = 8, 16
SWA_WINDOW = 128
MLA_NOPE, MLA_ROPE = 64, 32
PEER_HEADS, PEER_NKEYS, PEER_TOPK = 8, 128, 16

LANE = 128
TM = 512
TQ_LOCAL = 256
NA_GROUP_ROWS = TQ_LOCAL // GRID_W
NA_WIN_ROWS = NA_KR + NA_GROUP_ROWS - 1
GLOBAL_TQ = 1024
GLOBAL_TK = 1024
PEER_TM = 1024
PEER_EB = 512
MXU_DEPTH = 256
PEER_MXU_ROWS = 512
PEER_SELECT_UNROLL = 8
VMEM_LIMIT = 56 * 1024 * 1024
PEER_VMEM_LIMIT = 56 * 1024 * 1024


def _cparams(sem, vmem=VMEM_LIMIT):
    return pltpu.CompilerParams(dimension_semantics=sem, vmem_limit_bytes=vmem)


def _dot(a, b):
    return jnp.dot(a, b, preferred_element_type=_F32)


def _dot_nt(a, b):
    return lax.dot_general(a, b, (((1,), (1,)), ((), ())), preferred_element_type=_F32)


def _rms(x, g):
    return x * lax.rsqrt(jnp.mean(x * x, axis=-1, keepdims=True) + RMS_EPS) * g


def _lane_half(shape):
    return lax.broadcasted_iota(jnp.int32, shape, len(shape) - 1) >= (LANE // 2)


def _mod_kernel(c_ref, w_ref, b_ref, o_ref):
    c = c_ref[...]
    a = c * jax.nn.sigmoid(c)
    w = w_ref[0]
    a_hi = a.astype(_MXU)
    a_lo = (a - a_hi.astype(_F32)).astype(_MXU)
    w_hi = w.astype(_MXU)
    w_lo = (w - w_hi.astype(_F32)).astype(_MXU)
    o_ref[0] = _dot(a_hi, w_hi) + _dot(a_lo, w_hi) + _dot(a_hi, w_lo) + b_ref[0]


def _modulation(cc, ada_w, ada_b):
    depth, d, n = ada_w.shape
    rows = cc.shape[0]
    tn = 768
    return pl.pallas_call(
        _mod_kernel,
        grid=(depth, n // tn),
        in_specs=[pl.BlockSpec((rows, d), lambda l, j: (0, 0)),
                  pl.BlockSpec((1, d, tn), lambda l, j: (l, 0, j)),
                  pl.BlockSpec((1, 1, tn), lambda l, j: (l, 0, j))],
        out_specs=pl.BlockSpec((1, rows, tn), lambda l, j: (l, 0, j)),
        out_shape=jax.ShapeDtypeStruct((depth, rows, n), _F32),
        compiler_params=_cparams(("parallel", "parallel")),
        name="adaln_mod",
    )(cc, ada_w, ada_b.reshape(depth, 1, n))


def _rope_tables(seq, d):
    t = jnp.arange(seq, dtype=jnp.int32)
    q = d // 4
    freq = ROPE_BASE ** (-jnp.arange(q, dtype=_F32) / q)

    def one(pos):
        ang = pos.astype(_F32)[:, None] * freq[None, :]
        return jnp.concatenate([jnp.cos(ang)] * 2, -1), jnp.concatenate([jnp.sin(ang)] * 2, -1)

    cr, sr = one(t // GRID_W)
    cc, sc = one(t % GRID_W)
    return jnp.concatenate([cr, cc], -1), jnp.concatenate([sr, sc], -1)


def _rot_cols(w, d):
    k, n = w.shape
    w5 = w.reshape(k, n // d, 2, 2, d // 4)
    return jnp.stack([-w5[:, :, :, 1], w5[:, :, :, 0]], axis=3).reshape(k, n)


def _pad_table(tab, width, ident):
    s, w = tab.shape
    if w < width:
        reps = width // w
        tab = jnp.tile(tab, (1, reps))
    return jnp.concatenate([tab, jnp.full((TM, width), ident, _F32)], axis=0)


def _proj_ab_kernel(x_ref, mod_ref, g_ref, cos_ref, sin_ref, w_ref,
                    aq_ref, ak_ref, av_ref, bq_ref, bk_ref, bv_ref):
    x = x_ref[...]
    h = _rms(x, g_ref[...]) * (1.0 + mod_ref[0, 1:2, :]) + mod_ref[0, 0:1, :]
    hb = h.astype(_MXU)
    cos = cos_ref[...]
    sin = sin_ref[...]

    def proj(lo, n):
        return _dot(hb, w_ref[:, lo:lo + n])

    scale = HEAD_DIM ** -0.5
    aq_ref[...] = (proj(0, 512) * scale).astype(aq_ref.dtype)
    ak_ref[...] = proj(512, 512).astype(ak_ref.dtype)
    av_ref[...] = proj(1024, 512).astype(av_ref.dtype)
    cos4 = jnp.tile(cos, (1, 4))
    sin4 = jnp.tile(sin, (1, 4))
    bq_ref[...] = ((proj(1536, 512) * cos4 + proj(2048, 512) * sin4) * scale).astype(bq_ref.dtype)
    cos2 = jnp.tile(cos, (1, 2))
    sin2 = jnp.tile(sin, (1, 2))
    bk_ref[...] = (proj(2560, 256) * cos2 + proj(2816, 256) * sin2).astype(bk_ref.dtype)
    bv_ref[...] = proj(3072, 256).astype(bv_ref.dtype)


def _proj_ab(x, mod, mod_row, g, cos, sin, tab_row, w):
    n, d = x.shape
    widths = (512, 512, 512, 512, 256, 256)
    return pl.pallas_call(
        _proj_ab_kernel,
        grid=(n // TM,),
        in_specs=[pl.BlockSpec((TM, d), lambda i: (i, 0)),
                  pl.BlockSpec((1, N_MOD, d), lambda i: (mod_row(i), 0, 0)),
                  pl.BlockSpec((1, d), lambda i: (0, 0)),
                  pl.BlockSpec((TM, LANE), lambda i: (tab_row(i), 0)),
                  pl.BlockSpec((TM, LANE), lambda i: (tab_row(i), 0)),
                  pl.BlockSpec(w.shape, lambda i: (0, 0))],
        out_specs=[pl.BlockSpec((TM, wd), lambda i: (i, 0)) for wd in widths],
        out_shape=[jax.ShapeDtypeStruct((n, wd), _MXU) for wd in widths],
        compiler_params=_cparams(("parallel",)),
        name="proj_ab",
    )(x, mod, g, cos, sin, w)


def _na_kernel(q_ref, k_ref, v_ref, kc_ref, vc_ref, bias_ref, o_ref, *, rows):
    g = pl.program_id(1)
    ks = jnp.clip(NA_GROUP_ROWS * g - NA_KR // 2, 0, rows - NA_WIN_ROWS)
    start = pl.multiple_of(ks * GRID_W, GRID_W)
    nwin = NA_WIN_ROWS * GRID_W
    hi = _lane_half((TQ_LOCAL, LANE))

    def scores(h):
        cols = slice((h // 2) * LANE, (h // 2 + 1) * LANE)
        q2 = q_ref[:, cols]
        qh = jnp.where(hi == (h % 2 == 1), q2, jnp.zeros_like(q2))
        return _dot_nt(qh, k_ref[pl.ds(start, nwin), cols]) + bias_ref[0, h], _dot_nt(qh, kc_ref[:, cols])

    outs = []
    nxt = scores(0)
    for h in range(8):
        s_lat, s_ctx = nxt
        if h + 1 < 8:
            nxt = scores(h + 1)
        cols = slice((h // 2) * LANE, (h // 2 + 1) * LANE)
        mx = jnp.maximum(jnp.max(s_lat, axis=-1, keepdims=True), jnp.max(s_ctx, axis=-1, keepdims=True))
        p_lat = jnp.exp(s_lat - mx)
        p_ctx = jnp.exp(s_ctx - mx)
        den = jnp.sum(p_lat, axis=-1, keepdims=True) + jnp.sum(p_ctx, axis=-1, keepdims=True)
        o = _dot(p_lat.astype(_MXU), v_ref[pl.ds(start, nwin), cols]) + _dot(p_ctx.astype(_MXU), vc_ref[:, cols])
        outs.append(o / den)
        if h % 2 == 1:
            o_ref[:, cols] = jnp.where(hi, outs[h], outs[h - 1]).astype(o_ref.dtype)


def _na_bias_tables(rpb, rows):
    gq = NA_GROUP_ROWS
    n_groups = rows // gq
    ql = jnp.arange(gq)[:, None]
    kl = jnp.arange(NA_WIN_ROWS)[None, :]
    qc = jnp.arange(GRID_W)[:, None]
    kc = jnp.arange(GRID_W)[None, :]
    cs = jnp.clip(qc - NA_KC // 2, 0, GRID_W - NA_KC)
    col_valid = (kc >= cs) & (kc < cs + NA_KC)
    col_idx = jnp.clip(kc - qc + NA_KC - 1, 0, 2 * NA_KC - 2)
    exact = lax.Precision.HIGHEST
    oh_c = jax.nn.one_hot(col_idx, 2 * NA_KC - 1, dtype=_F32)
    rpb_cols = jnp.einsum("hab,xyb->haxy", rpb.astype(_F32), oh_c, precision=exact)
    tabs = []
    for g in (0, 1, n_groups - 1):
        ks = min(max(gq * g - NA_KR // 2, 0), rows - NA_WIN_ROWS)
        rq = gq * g + ql
        rk = ks + kl
        r0 = jnp.clip(rq - NA_KR // 2, 0, rows - NA_KR)
        row_valid = (rk >= r0) & (rk < r0 + NA_KR)
        row_idx = jnp.clip(rk - rq + NA_KR - 1, 0, 2 * NA_KR - 2)
        oh_r = jax.nn.one_hot(row_idx, 2 * NA_KR - 1, dtype=_F32)
        b = jnp.einsum("qka,haxy->hqxky", oh_r, rpb_cols, precision=exact)
        valid = row_valid[:, None, :, None] & col_valid[None, :, None, :]
        b = jnp.where(valid[None], b, _NEG)
        tabs.append(b.reshape(rpb.shape[0], gq * GRID_W, NA_WIN_ROWS * GRID_W))
    return jnp.stack(tabs)


def _na_attention(aq, ak, av, cak, cav, bias, batch, seq, ctx_len):
    rows = seq // GRID_W
    n_groups = seq // TQ_LOCAL
    nwin = NA_WIN_ROWS * GRID_W

    def bias_row(b, g):
        return (jnp.where(g == 0, 0, jnp.where(g == n_groups - 1, 2, 1)), 0, 0, 0)

    return pl.pallas_call(
        functools.partial(_na_kernel, rows=rows),
        grid=(batch, n_groups),
        in_specs=[pl.BlockSpec((TQ_LOCAL, 512), lambda b, g: (b * n_groups + g, 0)),
                  pl.BlockSpec((seq, 512), lambda b, g: (b, 0)),
                  pl.BlockSpec((seq, 512), lambda b, g: (b, 0)),
                  pl.BlockSpec((ctx_len, 512), lambda b, g: (b, 0)),
                  pl.BlockSpec((ctx_len, 512), lambda b, g: (b, 0)),
                  pl.BlockSpec((1, 8, TQ_LOCAL, nwin), bias_row)],
        out_specs=pl.BlockSpec((TQ_LOCAL, 512), lambda b, g: (b * n_groups + g, 0)),
        out_shape=jax.ShapeDtypeStruct((batch * seq, 512), _MXU),
        compiler_params=_cparams(("parallel", "arbitrary")),
        name="na_attention",
    )(aq, ak, av, cak, cav, bias)


def _swa_kernel(sink_ref, q_ref, k_ref, v_ref, kc_ref, vc_ref, o_ref, *, seq):
    t = pl.program_id(1)
    kwin = TQ_LOCAL + 2 * SWA_WINDOW
    start = t * TQ_LOCAL
    kstart = pl.multiple_of(jnp.clip(start - SWA_WINDOW, 0, seq - kwin), LANE)
    qpos = start + lax.broadcasted_iota(jnp.int32, (TQ_LOCAL, kwin), 0)
    kpos = kstart + lax.broadcasted_iota(jnp.int32, (TQ_LOCAL, kwin), 1)
    mask = jnp.where(jnp.abs(kpos - qpos) <= SWA_WINDOW, 0.0, _NEG).astype(_F32)
    hi = _lane_half((TQ_LOCAL, LANE))

    def scores(h):
        cols = slice((h // 2) * LANE, (h // 2 + 1) * LANE)
        kcols = slice((h // 4) * LANE, (h // 4 + 1) * LANE)
        q2 = q_ref[:, cols]
        qh = jnp.where(hi == (h % 2 == 1), q2, jnp.zeros_like(q2))
        return _dot_nt(qh, k_ref[pl.ds(kstart, kwin), kcols]) + mask, _dot_nt(qh, kc_ref[:, kcols])

    outs = []
    nxt = scores(0)
    for h in range(8):
        s_lat, s_ctx = nxt
        if h + 1 < 8:
            nxt = scores(h + 1)
        cols = slice((h // 2) * LANE, (h // 2 + 1) * LANE)
        kcols = slice((h // 4) * LANE, (h // 4 + 1) * LANE)
        sink = sink_ref[h]
        mx = jnp.maximum(jnp.max(s_lat, axis=-1, keepdims=True), jnp.max(s_ctx, axis=-1, keepdims=True))
        mx = jnp.maximum(mx, sink)
        p_lat = jnp.exp(s_lat - mx)
        p_ctx = jnp.exp(s_ctx - mx)
        den = (jnp.sum(p_lat, axis=-1, keepdims=True) + jnp.sum(p_ctx, axis=-1, keepdims=True)
               + jnp.exp(sink - mx))
        o = (_dot(p_lat.astype(_MXU), v_ref[pl.ds(kstart, kwin), kcols])
             + _dot(p_ctx.astype(_MXU), vc_ref[:, kcols]))
        outs.append(o / den)
        if h % 2 == 1:
            o_ref[:, cols] = jnp.where(hi, outs[h], outs[h - 1]).astype(o_ref.dtype)


def _swa_attention(sink, bq, bk2, bv2, cbk2, cbv2, batch, seq, ctx_len):
    n_t = seq // TQ_LOCAL
    return pl.pallas_call(
        functools.partial(_swa_kernel, seq=seq),
        grid=(batch, n_t),
        in_specs=[pl.BlockSpec(memory_space=pltpu.SMEM),
                  pl.BlockSpec((TQ_LOCAL, 512), lambda b, t: (b * n_t + t, 0)),
                  pl.BlockSpec((seq, 256), lambda b, t: (b, 0)),
                  pl.BlockSpec((seq, 256), lambda b, t: (b, 0)),
                  pl.BlockSpec((ctx_len, 256), lambda b, t: (b, 0)),
                  pl.BlockSpec((ctx_len, 256), lambda b, t: (b, 0))],
        out_specs=pl.BlockSpec((TQ_LOCAL, 512), lambda b, t: (b * n_t + t, 0)),
        out_shape=jax.ShapeDtypeStruct((batch * seq, 512), _MXU),
        compiler_params=_cparams(("parallel", "arbitrary")),
        name="swa_attention",
    )(sink, bq, bk2, bv2, cbk2, cbv2)


def _flash_kernel(*refs, heads, outs, has_ctx, has_sink, diff_cfg, n_kv):
    it = iter(refs)
    sink_ref = next(it) if has_sink else None
    q_ref, k_ref, v_ref = next(it), next(it), next(it)
    kc_ref = next(it) if has_ctx else None
    vc_ref = next(it) if has_ctx else None
    lam_ref = next(it) if diff_cfg else None
    sg_ref = next(it) if diff_cfg else None
    o_ref = next(it)
    m_ref, l_ref, acc_ref = next(it), next(it), next(it)
    kv = pl.program_id(2)
    tq = q_ref.shape[0]
    hi = _lane_half((tq, LANE))

    @pl.when(kv == 0)
    def _init():
        for h in range(len(heads)):
            if has_sink:
                m_ref[h] = jnp.full((tq, LANE), sink_ref[h], _F32)
                l_ref[h] = jnp.ones((tq, LANE), _F32)
            else:
                m_ref[h] = jnp.full((tq, LANE), _NEG, _F32)
                l_ref[h] = jnp.zeros((tq, LANE), _F32)
            acc_ref[h] = jnp.zeros((tq, LANE), _F32)

    def attend(kr, vr):
        for h, (qb, qhalf, kb, vb) in enumerate(heads):
            q2 = q_ref[:, qb * LANE:(qb + 1) * LANE]
            if qhalf is not None:
                q2 = jnp.where(hi == (qhalf == 1), q2, jnp.zeros_like(q2))
            s = _dot_nt(q2, kr[:, kb * LANE:(kb + 1) * LANE])
            m_old = m_ref[h][:, :1]
            m_new = jnp.maximum(m_old, jnp.max(s, axis=-1, keepdims=True))
            alpha = jnp.exp(m_old - m_new)
            p = jnp.exp(s - m_new)
            l_ref[h] = jnp.broadcast_to(alpha * l_ref[h][:, :1] + jnp.sum(p, axis=-1, keepdims=True), (tq, LANE))
            acc_ref[h] = alpha * acc_ref[h] + _dot(p.astype(_MXU), vr[:, vb * LANE:(vb + 1) * LANE])
            m_ref[h] = jnp.broadcast_to(m_new, (tq, LANE))

    if has_ctx:
        @pl.when(kv == 0)
        def _ctx():
            attend(kc_ref, vc_ref)

    attend(k_ref, v_ref)

    @pl.when(kv == n_kv - 1)
    def _fin():
        def head_out(h):
            return acc_ref[h] / l_ref[h][:, :1]

        for j, spec in enumerate(outs):
            if spec[0] == "full":
                o = head_out(spec[1])
            elif spec[0] == "pair":
                o = jnp.where(hi, head_out(spec[2]), head_out(spec[1]))
            else:
                lv = lam_ref[...]
                lam = (jnp.exp(jnp.sum(lv[0:1] * lv[1:2], axis=-1, keepdims=True))
                       - jnp.exp(jnp.sum(lv[2:3] * lv[3:4], axis=-1, keepdims=True)) + diff_cfg)
                o = _rms(head_out(spec[1]) - lam * head_out(spec[2]), sg_ref[...]) * (1.0 - diff_cfg)
            o_ref[:, j * LANE:(j + 1) * LANE] = o.astype(o_ref.dtype)


def _flash_attention(q, k, v, *, heads, outs, batch, q_per_batch, kv_per_batch, tq, tk,
                     q_row0=0, kv_row0=0, ctx=None, ctx_len=0, ctx_row0=0, sink=None, diff=None, name="flash"):
    n_q = q_per_batch // tq
    n_kv = kv_per_batch // tk
    qw, kw, vw = q.shape[1], k.shape[1], v.shape[1]
    args, in_specs = [], []
    if sink is not None:
        args.append(sink)
        in_specs.append(pl.BlockSpec(memory_space=pltpu.SMEM))
    q0, k0 = q_row0 // tq, kv_row0 // tk
    args += [q, k, v]
    in_specs += [pl.BlockSpec((tq, qw), lambda b, i, j: (q0 + b * n_q + i, 0)),
                 pl.BlockSpec((tk, kw), lambda b, i, j: (k0 + b * n_kv + j, 0)),
                 pl.BlockSpec((tk, vw), lambda b, i, j: (k0 + b * n_kv + j, 0))]
    if ctx is not None:
        c0 = ctx_row0 // ctx_len
        args += [ctx[0], ctx[1]]
        in_specs += [pl.BlockSpec((ctx_len, kw), lambda b, i, j: (c0 + b, 0)),
                     pl.BlockSpec((ctx_len, vw), lambda b, i, j: (c0 + b, 0))]
    diff_cfg = None
    if diff is not None:
        lam_vecs, subln_g, diff_cfg = diff
        args += [lam_vecs, subln_g]
        in_specs += [pl.BlockSpec(lam_vecs.shape, lambda b, i, j: (0, 0)),
                     pl.BlockSpec(subln_g.shape, lambda b, i, j: (0, 0))]
    nh = len(heads)
    ow = len(outs) * LANE
    return pl.pallas_call(
        functools.partial(_flash_kernel, heads=tuple(heads), outs=tuple(outs), has_ctx=ctx is not None,
                          has_sink=sink is not None, diff_cfg=diff_cfg, n_kv=n_kv),
        grid=(batch, n_q, n_kv),
        in_specs=in_specs,
        out_specs=pl.BlockSpec((tq, ow), lambda b, i, j: (b * n_q + i, 0)),
        out_shape=jax.ShapeDtypeStruct((batch * q_per_batch, ow), _MXU),
        scratch_shapes=[pltpu.VMEM((nh, tq, LANE), _F32)] * 3,
        compiler_params=_cparams(("parallel", "parallel", "arbitrary")),
        name=name,
    )(*args)


def _flasht_kernel(*refs, heads, outs, diff_cfg, n_kv):
    it = iter(refs)
    qt_ref, k_ref, vt_ref, kc_ref, vct_ref = next(it), next(it), next(it), next(it), next(it)
    lam_ref = next(it) if diff_cfg else None
    sg_ref = next(it) if diff_cfg else None
    o_ref = next(it)
    m_ref, l_ref, acc_ref = next(it), next(it), next(it)
    kv = pl.program_id(2)
    tq = qt_ref.shape[1]
    row_hi = lax.broadcasted_iota(jnp.int32, (LANE, tq), 0) >= (LANE // 2)

    @pl.when(kv == 0)
    def _init():
        m_ref[...] = jnp.full(m_ref.shape, _NEG, _F32)
        l_ref[...] = jnp.zeros(l_ref.shape, _F32)
        acc_ref[...] = jnp.zeros(acc_ref.shape, _F32)

    def attend(kr, vtr):
        def scores(h):
            qb, qhalf, kb, _ = heads[h]
            qt = qt_ref[qb * LANE:(qb + 1) * LANE, :]
            if qhalf is not None:
                qt = jnp.where(row_hi == (qhalf == 1), qt, jnp.zeros_like(qt))
            return _dot(kr[:, kb * LANE:(kb + 1) * LANE], qt)

        s_next = scores(0)
        for h in range(len(heads)):
            s = s_next
            if h + 1 < len(heads):
                s_next = scores(h + 1)
            vb = heads[h][3]
            m_old = m_ref[h, 0:1, :]
            m_new = jnp.maximum(m_old, jnp.max(s, axis=0, keepdims=True))
            alpha = jnp.exp2(m_old - m_new)
            p = jnp.exp2(s - m_new)
            l_new = alpha * l_ref[h, 0:1, :] + jnp.sum(p, axis=0, keepdims=True)
            acc_ref[h] = alpha * acc_ref[h] + _dot(vtr[vb * LANE:(vb + 1) * LANE, :], p.astype(_MXU))
            l_ref[h] = jnp.broadcast_to(l_new, (8, tq))
            m_ref[h] = jnp.broadcast_to(m_new, (8, tq))

    @pl.when(kv == 0)
    def _ctx():
        attend(kc_ref, vct_ref)

    attend(k_ref, vt_ref)

    @pl.when(kv == n_kv - 1)
    def _fin():
        def head_out(h):
            return acc_ref[h] / l_ref[h, 0:1, :]

        for j, spec in enumerate(outs):
            if spec[0] == "full":
                o = head_out(spec[1]).T
            elif spec[0] == "pair":
                o = jnp.where(row_hi, head_out(spec[2]), head_out(spec[1])).T
            else:
                lv = lam_ref[...]
                lam = (jnp.exp(jnp.sum(lv[0:1] * lv[1:2], axis=-1, keepdims=True))
                       - jnp.exp(jnp.sum(lv[2:3] * lv[3:4], axis=-1, keepdims=True)) + diff_cfg)
                o = _rms((head_out(spec[1]) - lam * head_out(spec[2])).T, sg_ref[...]) * (1.0 - diff_cfg)
            o_ref[:, j * LANE:(j + 1) * LANE] = o.astype(o_ref.dtype)


def _flasht_attention(qt, k, vt, kc, vct, *, heads, outs, batch, seq, ctx_len, tq, tk, diff=None, name="flasht"):
    n_q, n_kv = seq // tq, seq // tk
    qw, kw, vw = qt.shape[0], k.shape[1], vt.shape[0]
    args = [qt, k, vt, kc, vct]
    in_specs = [pl.BlockSpec((qw, tq), lambda b, i, j: (0, b * n_q + i)),
                pl.BlockSpec((tk, kw), lambda b, i, j: (b * n_kv + j, 0)),
                pl.BlockSpec((vw, tk), lambda b, i, j: (0, b * n_kv + j)),
                pl.BlockSpec((ctx_len, kw), lambda b, i, j: (b, 0)),
                pl.BlockSpec((vw, ctx_len), lambda b, i, j: (0, b))]
    diff_cfg = None
    if diff is not None:
        lam_vecs, subln_g, diff_cfg = diff
        args += [lam_vecs, subln_g]
        in_specs += [pl.BlockSpec(lam_vecs.shape, lambda b, i, j: (0, 0)),
                     pl.BlockSpec(subln_g.shape, lambda b, i, j: (0, 0))]
    nh = len(heads)
    ow = len(outs) * LANE
    return pl.pallas_call(
        functools.partial(_flasht_kernel, heads=tuple(heads), outs=tuple(outs), diff_cfg=diff_cfg, n_kv=n_kv),
        grid=(batch, n_q, n_kv),
        in_specs=in_specs,
        out_specs=pl.BlockSpec((tq, ow), lambda b, i, j: (b * n_q + i, 0)),
        out_shape=jax.ShapeDtypeStruct((batch * seq, ow), _MXU),
        scratch_shapes=[pltpu.VMEM((nh, 8, tq), _F32), pltpu.VMEM((nh, 8, tq), _F32),
                        pltpu.VMEM((nh, LANE, tq), _F32)],
        compiler_params=_cparams(("parallel", "parallel", "arbitrary")),
        name=name,
    )(*args)


def _out_kernel(x_ref, ya_ref, yb_ref, mod_ref, g2_ref, wa_ref, wb_ref, xo_ref, h2_ref):
    y = _dot(ya_ref[...], wa_ref[...]) + _dot(yb_ref[...], wb_ref[...])
    x = x_ref[...] + mod_ref[0, 2:3, :] * y
    xo_ref[...] = x
    h2 = _rms(x, g2_ref[...]) * (1.0 + mod_ref[0, 4:5, :]) + mod_ref[0, 3:4, :]
    h2_ref[...] = h2.astype(h2_ref.dtype)


def _out_proj(x, ya, yb, mod, mod_row, g2, wa, wb):
    n, d = x.shape
    return pl.pallas_call(
        _out_kernel,
        grid=(n // TM,),
        in_specs=[pl.BlockSpec((TM, d), lambda i: (i, 0)),
                  pl.BlockSpec((TM, ya.shape[1]), lambda i: (i, 0)),
                  pl.BlockSpec((TM, yb.shape[1]), lambda i: (i, 0)),
                  pl.BlockSpec((1, N_MOD, d), lambda i: (mod_row(i), 0, 0)),
                  pl.BlockSpec((1, d), lambda i: (0, 0)),
                  pl.BlockSpec(wa.shape, lambda i: (0, 0)),
                  pl.BlockSpec(wb.shape, lambda i: (0, 0))],
        out_specs=[pl.BlockSpec((TM, d), lambda i: (i, 0)), pl.BlockSpec((TM, d), lambda i: (i, 0))],
        out_shape=[jax.ShapeDtypeStruct((n, d), _F32), jax.ShapeDtypeStruct((n, d), _MXU)],
        compiler_params=_cparams(("parallel",)),
        name="out_proj",
    )(x, ya, yb, mod, g2, wa, wb)


def _proj_cd_kernel(x_ref, mod_ref, g_ref, cq_ref, sq_ref, cd_ref, sd_ref, w_ref, qg_ref, kvg_ref,
                    wqa_ref, wqb_ref, wk_ref, wv_ref, place_ref,
                    qm_ref, km_ref, vm_ref, dq_ref, dk_ref, dv_ref):
    x = x_ref[...]
    h = _rms(x, g_ref[...]) * (1.0 + mod_ref[0, 1:2, :]) + mod_ref[0, 0:1, :]
    hb = h.astype(_MXU)

    def proj(lo, n):
        return _dot(hb, w_ref[:, lo:lo + n])

    cosq, sinq = cq_ref[...], sq_ref[...]
    cosd, sind = cd_ref[...], sd_ref[...]
    cqn = _rms(proj(0, 256), qg_ref[...]).astype(_MXU)
    ckvn = _rms(proj(256, 128), kvg_ref[...]).astype(_MXU)
    cos8, sin8 = jnp.tile(cosq, (1, 8)), jnp.tile(sinq, (1, 8))
    qm = _dot(cqn, wqa_ref[...]) * cos8 + _dot(cqn, wqb_ref[...]) * sin8
    qm_ref[...] = (qm * ((MLA_NOPE + MLA_ROPE) ** -0.5 * _LOG2E)).T.astype(qm_ref.dtype)
    kr = (proj(384, 128) * cosq + proj(512, 128) * sinq).astype(_MXU)
    km_ref[...] = (_dot(ckvn, wk_ref[...]) + _dot(kr, place_ref[...])).astype(km_ref.dtype)
    vm_ref[...] = _dot(ckvn, wv_ref[...]).T.astype(vm_ref.dtype)
    cos4, sin4 = jnp.tile(cosd, (1, 4)), jnp.tile(sind, (1, 4))
    dq_ref[...] = ((proj(640, 512) * cos4 + proj(1152, 512) * sin4) * (HEAD_DIM ** -0.5 * _LOG2E)).T.astype(dq_ref.dtype)
    dk_ref[...] = (proj(1664, 512) * cos4 + proj(2176, 512) * sin4).astype(dk_ref.dtype)
    dv_ref[...] = proj(2688, 512).T.astype(dv_ref.dtype)


def _proj_cd(x, mod, mod_row, g, tabs, tab_row, w, qg, kvg, wqa, wqb, wk, wv, place):
    n, d = x.shape
    outs = ((1024, True), (1024, False), (512, True), (512, True), (512, False), (512, True))
    full = lambda a: pl.BlockSpec(a.shape, lambda i: (0, 0))
    return pl.pallas_call(
        _proj_cd_kernel,
        grid=(n // TM,),
        in_specs=[pl.BlockSpec((TM, d), lambda i: (i, 0)),
                  pl.BlockSpec((1, N_MOD, d), lambda i: (mod_row(i), 0, 0)),
                  pl.BlockSpec((1, d), lambda i: (0, 0))]
                 + [pl.BlockSpec((TM, LANE), lambda i: (tab_row(i), 0))] * 4
                 + [full(a) for a in (w, qg, kvg, wqa, wqb, wk, wv, place)],
        out_specs=[pl.BlockSpec((wd, TM), lambda i: (0, i)) if fm else pl.BlockSpec((TM, wd), lambda i: (i, 0))
                   for wd, fm in outs],
        out_shape=[jax.ShapeDtypeStruct((wd, n) if fm else (n, wd), _MXU) for wd, fm in outs],
        compiler_params=_cparams(("parallel",)),
        name="proj_cd",
    )(x, mod, g, *tabs, w, qg, kvg, wqa, wqb, wk, wv, place)


_MARK = 2.0 ** 121


def _extract16(x, first_only):
    r = x.shape[0]
    row = lax.broadcasted_iota(jnp.int32, x.shape, 0)
    row16 = lax.broadcasted_iota(jnp.int32, (PEER_TOPK, LANE), 0)
    vals = jnp.zeros((PEER_TOPK, LANE), _F32)
    for k in range(PEER_TOPK):
        m = jnp.max(x, axis=0, keepdims=True)
        hit = x == m
        if first_only:
            hit = row == jnp.min(jnp.where(hit, row, r), axis=0, keepdims=True)
        x = jnp.where(hit, -_MARK * (32 + k), x)
        vals = jnp.where(row16 == k, m, vals)
    order = jnp.where(x < -16.0 * _MARK, x * (-1.0 / _MARK) - 32.0, float(PEER_TOPK))
    return order, vals


def _staircase_rows():
    groups = []
    for r in range(PEER_TOPK // 2):
        n = PEER_TOPK // (r + 1)
        for c0 in range(0, n, 8):
            groups.append((r, c0, min(8, n - c0)))
    groups.append((None, 0, 8))
    return groups


def _peer_select(sa, sb, first_only):
    ra, av = _extract16(sa, first_only)
    rb, bv = _extract16(sb, first_only)
    groups = _staircase_rows()
    sub = lax.broadcasted_iota(jnp.int32, (8, LANE), 0)
    pieces = []
    for r, c0, nv in groups:
        if r is None:
            piece = av[8:16] + bv[0:1]
        else:
            piece = av[r:r + 1] + bv[c0:c0 + 8]
            if nv < 8:
                piece = jnp.where(sub < nv, piece, -jnp.inf)
        pieces.append(piece)
    cand = jnp.concatenate(pieces, axis=0)
    e_cand = jnp.exp(cand - (av[0:1] + bv[0:1]))
    sel, _ = _extract16(cand, first_only)
    sel = jnp.where(sel < float(PEER_TOPK), 1.0, 0.0)
    z = jnp.sum(sel * e_cand, axis=0, keepdims=True)
    lr = jnp.zeros(sa.shape, _F32)
    for g, (r, c0, nv) in enumerate(groups):
        blk = sel[8 * g:8 * g + 8]
        if r is None:
            for q in range(8):
                lr = jnp.where(ra == float(8 + q), blk[q:q + 1], lr)
        elif c0 == 0:
            cnt = jnp.sum(blk, axis=0, keepdims=True)
            if PEER_TOPK // (r + 1) > 8:
                cnt = cnt + jnp.sum(sel[8 * g + 8:8 * g + 16], axis=0, keepdims=True)
            lr = jnp.where(ra == float(r), cnt, lr)
    ea = jnp.exp(sa - av[0:1]) / z
    eb = jnp.exp(sb - bv[0:1])
    n_sel = (jnp.sum(jnp.where(ra < float(PEER_TOPK), 1.0, 0.0), axis=0, keepdims=True)
             + jnp.sum(jnp.where(rb < float(PEER_TOPK), 1.0, 0.0), axis=0, keepdims=True)
             + jnp.sum(sel, axis=0, keepdims=True))
    return lr, ea, rb, eb, n_sel


def _peer_kernel(h2_ref, x_ref, mod_ref, wqt_ref, keys_ref, u_ref, vt_ref, *rest, n_eb, final):
    fg_ref = rest[0] if final else None
    (o_ref, h2t_ref, lr_ref, ea_ref, rb_ref, eb_ref, ft_ref, a0_ref, a1_ref, w0_ref, w1_ref,
     flag_ref) = rest[1:] if final else rest
    s = pl.program_id(1)
    n_chunks = PEER_TM // LANE
    i_per = PEER_EB // PEER_NKEYS
    gdt = rb_ref.dtype

    @pl.when(s == 0)
    def _select():
        h2t = h2_ref[...].astype(_F32).T.astype(_MXU)
        h2t_ref[...] = h2t
        ft_ref[...] = jnp.zeros_like(ft_ref)
        a1_ref[...] = jnp.zeros_like(a1_ref)
        w0_ref[...] = jnp.zeros_like(w0_ref)

        for hp in range(2 * PEER_HEADS):
            qt = _dot(wqt_ref[hp * LANE:(hp + 1) * LANE, :], h2t).astype(_MXU)
            st = _dot(keys_ref[hp % 2], qt)
            for c in range(n_chunks):
                (lr_ref if hp % 2 == 0 else ea_ref)[c, hp // 2] = st[:, c * LANE:(c + 1) * LANE]

        def unit(c, h, sa, sb, first_only):
            lr, ea, rb, eb, n_sel = _peer_select(sa, sb, first_only)
            lr_ref[c, h] = lr
            ea_ref[c, h] = ea
            rb_ref[c, h] = rb.astype(gdt)
            eb_ref[c, h] = eb.astype(gdt)
            return jnp.max(n_sel) > 3.0 * PEER_TOPK

        per_trip = PEER_SELECT_UNROLL

        def fast(t, carry):
            c, h0 = t // (PEER_HEADS // per_trip), (t % (PEER_HEADS // per_trip)) * per_trip
            for u in range(per_trip):
                tied = unit(c, h0 + u, lr_ref[c, h0 + u], ea_ref[c, h0 + u], False)
                flag_ref[c * PEER_HEADS + h0 + u] = tied.astype(jnp.int32)
            return carry

        lax.fori_loop(0, n_chunks * PEER_HEADS // per_trip, fast, 0)

        def exact(t, carry):
            @pl.when(flag_ref[t] != 0)
            def _redo():
                c, h = t // PEER_HEADS, t % PEER_HEADS
                w2 = wqt_ref[pl.ds(pl.multiple_of(h * 2 * LANE, 2 * LANE), 2 * LANE), :]
                h2c = h2_ref[pl.ds(pl.multiple_of(c * LANE, LANE), LANE), :]
                qt = _dot_nt(w2, h2c).astype(_MXU)
                unit(c, h, _dot(keys_ref[0], qt[:LANE]), _dot(keys_ref[1], qt[LANE:]), True)
            return carry

        lax.fori_loop(0, n_chunks * PEER_HEADS, exact, 0)

    def step(half, a_new, a_prev, w_new, w_prev):
        blk = jnp.clip(2 * s + half - 1, 0, n_eb - 1)
        e0 = half * PEER_EB

        def gate(ii, c):
            i = blk * i_per + ii
            rows = slice(ii * PEER_NKEYS, (ii + 1) * PEER_NKEYS)
            cols = slice(c * LANE, (c + 1) * LANE)
            g = jnp.zeros((PEER_NKEYS, LANE), gdt)
            for h in range(PEER_HEADS):
                lr_i = lr_ref[c, h, pl.ds(i, 1), :].astype(gdt)
                ea_i = ea_ref[c, h, pl.ds(i, 1), :].astype(gdt)
                g = g + jnp.where(rb_ref[c, h] < lr_i, eb_ref[c, h], jnp.zeros((), gdt)) * ea_i
            a = a_prev[rows, cols]
            gelu = 0.5 * a * (1.0 + lax.erf(a * (2.0 ** -0.5)))
            w_new[rows, cols] = gelu.astype(gdt) * g

        units = [(ii, c) for ii in range(i_per) for c in range(n_chunks)]
        kd, mr = MXU_DEPTH, PEER_MXU_ROWS
        d_model = u_ref.shape[1]
        n_pieces = (PEER_EB // mr) * (d_model // kd) + (d_model // mr) * (PEER_EB // kd)
        per = len(units) // n_pieces
        done = 0
        for r0 in range(0, PEER_EB, mr):
            acc = None
            for k0 in range(0, d_model, kd):
                part = _dot(u_ref[e0 + r0:e0 + r0 + mr, k0:k0 + kd], h2t_ref[k0:k0 + kd, :])
                acc = part if acc is None else acc + part
                for ii, c in units[done:done + per]:
                    gate(ii, c)
                done += per
            a_new[r0:r0 + mr, :] = acc
        for r0 in range(0, d_model, mr):
            acc = ft_ref[r0:r0 + mr, :]
            for k0 in range(0, PEER_EB, kd):
                acc = acc + _dot(vt_ref[0, r0:r0 + mr, e0 + k0:e0 + k0 + kd], w_prev[k0:k0 + kd, :])
                for ii, c in units[done:done + per]:
                    gate(ii, c)
                done += per
            ft_ref[r0:r0 + mr, :] = acc
        for ii, c in units[done:]:
            gate(ii, c)

    @pl.when(s <= n_eb // 2)
    def _first():
        step(0, a0_ref, a1_ref, w1_ref, w0_ref)

    @pl.when(s >= 0)
    def _second():
        step(1, a1_ref, a0_ref, w0_ref, w1_ref)

    @pl.when(s == n_eb // 2)
    def _fin():
        x = x_ref[...] + mod_ref[0, 5:6, :] * ft_ref[...].T
        if final:
            x = _rms(x, fg_ref[...])
        o_ref[...] = x


def _peer(h2, x, mod, mod_row, wqt, keys, u, vt, final_g=None):
    n, d = x.shape
    n_exp = u.shape[0]
    n_eb = n_exp // PEER_EB
    tm = PEER_TM
    n_chunks = tm // LANE
    final = final_g is not None
    once = pl.Buffered(1)
    args = [h2, x, mod, wqt, keys, u, vt]
    in_specs = [pl.BlockSpec((tm, d), lambda i, e: (i, 0)),
                pl.BlockSpec((tm, d), lambda i, e: (i, 0), pipeline_mode=once),
                pl.BlockSpec((1, N_MOD, d), lambda i, e: (mod_row(i), 0, 0)),
                pl.BlockSpec(wqt.shape, lambda i, e: (0, 0), pipeline_mode=once),
                pl.BlockSpec(keys.shape, lambda i, e: (0, 0, 0)),
                pl.BlockSpec((2 * PEER_EB, d), lambda i, e: (jnp.minimum(e, n_eb // 2 - 1), 0)),
                pl.BlockSpec((1, d, 2 * PEER_EB), lambda i, e: (jnp.clip(e - 1, 0, n_eb // 2 - 1), 0, 0))]
    if final:
        args.append(final_g)
        in_specs.append(pl.BlockSpec((1, d), lambda i, e: (0, 0)))
    tab32 = pltpu.VMEM((n_chunks, PEER_HEADS, PEER_NKEYS, LANE), _F32)
    tab16 = pltpu.VMEM((n_chunks, PEER_HEADS, PEER_NKEYS, LANE), _MXU)
    abuf = pltpu.VMEM((PEER_EB, tm), _F32)
    wbuf = pltpu.VMEM((PEER_EB, tm), _MXU)
    return pl.pallas_call(
        functools.partial(_peer_kernel, n_eb=n_eb, final=final),
        grid=(n // tm, n_eb // 2 + 1),
        in_specs=in_specs,
        out_specs=pl.BlockSpec((tm, d), lambda i, e: (i, 0), pipeline_mode=once),
        out_shape=jax.ShapeDtypeStruct((n, d), _F32),
        scratch_shapes=[pltpu.VMEM((d, tm), _MXU), tab32, tab32, tab16, tab16,
                        pltpu.VMEM((d, tm), _F32), abuf, abuf, wbuf, wbuf,
                        pltpu.SMEM((n_chunks * PEER_HEADS,), jnp.int32)],
        compiler_params=_cparams(("parallel", "arbitrary"), PEER_VMEM_LIMIT),
        name="peer_ffn",
    )(*args)


def _dup_halves(w):
    a, b = w[:, :HEAD_DIM], w[:, HEAD_DIM:]
    return jnp.concatenate([a, a, b, b], axis=1)


def _prep_ab(w_in):
    aq, ak, av = w_in[:, 0:512], w_in[:, 512:1024], w_in[:, 1024:1536]
    bq, bk, bv = w_in[:, 1536:2048], w_in[:, 2048:2176], w_in[:, 2176:2304]
    cat = [aq, ak, av, bq, _rot_cols(bq, HEAD_DIM), _dup_halves(bk), _dup_halves(_rot_cols(bk, HEAD_DIM)),
           _dup_halves(bv)]
    return jnp.concatenate(cat, axis=1).astype(_MXU)


def _prep_cd(w_in, w_uq, w_ukv):
    d = w_in.shape[0]
    cq, ckv, kr = w_in[:, 0:256], w_in[:, 256:384], w_in[:, 384:416]
    dq, dk, dv = w_in[:, 416:928], w_in[:, 928:1440], w_in[:, 1440:1952]
    z64, z32 = jnp.zeros((d, 64), _F32), jnp.zeros((d, 32), _F32)
    kr128 = jnp.concatenate([z64, kr, z32], axis=1)
    krrot128 = jnp.concatenate([z64, _rot_cols(kr, MLA_ROPE), z32], axis=1)
    w = jnp.concatenate([cq, ckv, kr128, krrot128, dq, _rot_cols(dq, HEAD_DIM), dk, _rot_cols(dk, HEAD_DIM), dv],
                        axis=1).astype(_MXU)
    r = w_uq.shape[0]
    uq = w_uq.reshape(r, 8, MLA_NOPE + MLA_ROPE)
    nope, rope = uq[:, :, :MLA_NOPE], uq[:, :, MLA_NOPE:]
    rope_rot = _rot_cols(rope.reshape(r, 8 * MLA_ROPE), MLA_ROPE).reshape(r, 8, MLA_ROPE)
    zq = jnp.zeros((r, 8, 32), _F32)
    wqa = jnp.concatenate([nope, rope, zq], axis=2).reshape(r, 1024).astype(_MXU)
    wqb = jnp.concatenate([jnp.zeros_like(nope), rope_rot, zq], axis=2).reshape(r, 1024).astype(_MXU)
    rk = w_ukv.shape[0]
    ukv = w_ukv.reshape(rk, 8, 128)
    wk = jnp.concatenate([ukv[:, :, :MLA_NOPE], jnp.zeros((rk, 8, 64), _F32)], axis=2).reshape(rk, 1024).astype(_MXU)
    wv = ukv[:, :, MLA_NOPE:].reshape(rk, 512).astype(_MXU)
    lane = jnp.arange(LANE)
    src = (lane >= MLA_NOPE) & (lane < MLA_NOPE + MLA_ROPE)
    place = (src[:, None] & (lane[:, None] == (jnp.arange(1024)[None, :] % LANE))).astype(_MXU)
    return w, wqa, wqb, wk, wv, place


def kernel(x, c, ctx, c_ctx, ada_w, ada_b, norm1_g, norm2_g, w_out, peer_wq, peer_keys, peer_u, peer_v,
           ab_w_in, na_rpb, swa_sink, cd_w_in, mla_q_norm_g, mla_w_uq, mla_kv_norm_g, mla_w_ukv,
           diff_lambda, diff_subln_g, final_norm_g):
    batch, seq, d = x.shape
    ctx_len = ctx.shape[1]
    depth = ada_w.shape[0]
    assert seq % TM == 0 and (batch * ctx_len) % TM == 0 and seq % TQ_LOCAL == 0
    assert seq % PEER_TM == 0 and (batch * ctx_len) % PEER_TM == 0
    assert depth == 2, "even layers keep a context stream, the single odd layer is the last one"
    n_lat, n_ctx = batch * seq, batch * ctx_len
    xs = x.reshape(n_lat, d)
    cs = ctx.reshape(n_ctx, d)

    mod_rows = -(-(batch + 1) // 16) * 16
    cc = jnp.zeros((mod_rows, d), _F32).at[:batch].set(c).at[batch].set(c_ctx)
    mod_all = _modulation(cc, ada_w, ada_b)

    tiles_per_seq = seq // TM
    lat_row = lambda i: i // tiles_per_seq
    ctx_row = lambda i: batch
    lat_tab = lambda i: i % tiles_per_seq
    ctx_tab = lambda i: tiles_per_seq

    cos64, sin64 = _rope_tables(seq, HEAD_DIM)
    cos64p, sin64p = _pad_table(cos64, LANE, 1.0), _pad_table(sin64, LANE, 0.0)

    for l in range(depth):
        last = l == depth - 1
        j = l // 2
        mod = mod_all[l, :batch + 1].reshape(batch + 1, N_MOD, d)
        g1 = norm1_g[l].reshape(1, d)
        g2 = norm2_g[l].reshape(1, d)
        wo = w_out[l].astype(_MXU)
        wqt = peer_wq[l].T.astype(_MXU)
        keys = peer_keys[l].astype(_MXU)
        u = peer_u[l].astype(_MXU)
        vt = peer_v[l].astype(_MXU).reshape(-1, 2 * PEER_EB, d).transpose(0, 2, 1)
        if l % 2 == 0:
            w = _prep_ab(ab_w_in[j])
            aq, ak, av, bq, bk2, bv2 = _proj_ab(xs, mod, lat_row, g1, cos64p, sin64p, lat_tab, w)
            caq, cak, cav, cbq, cbk2, cbv2 = _proj_ab(cs, mod, ctx_row, g1, cos64p, sin64p, ctx_tab, w)
            bias = _na_bias_tables(na_rpb[j], seq // GRID_W)
            ya = _na_attention(aq, ak, av, cak, cav, bias, batch, seq, ctx_len)
            sink = swa_sink[j].astype(_F32)
            yb = _swa_attention(sink, bq, bk2, bv2, cbk2, cbv2, batch, seq, ctx_len)
            xs, h2 = _out_proj(xs, ya, yb, mod, lat_row, g2, wo[:512], wo[512:])
            if not last:
                pair_heads = [(m, half, m, m) for m in range(4) for half in range(2)]
                pair_outs = [("pair", 2 * m, 2 * m + 1) for m in range(4)]
                yca = _flash_attention(caq, cak, cav, heads=pair_heads, outs=pair_outs, batch=batch,
                                       q_per_batch=ctx_len, kv_per_batch=ctx_len, tq=ctx_len, tk=ctx_len,
                                       name="ctx_attn_a")
                gqa_heads = [(m, half, m // 2, m // 2) for m in range(4) for half in range(2)]
                ycb = _flash_attention(cbq, cbk2, cbv2, heads=gqa_heads, outs=pair_outs, batch=batch,
                                       q_per_batch=ctx_len, kv_per_batch=ctx_len, tq=ctx_len, tk=ctx_len,
                                       sink=sink, name="ctx_attn_b")
                cs, h2c = _out_proj(cs, yca, ycb, mod, ctx_row, g2, wo[:512], wo[512:])
        else:
            lam_init = 0.8 - 0.6 * math.exp(-0.3 * l)
            w, wqa, wqb, wk, wv, place = _prep_cd(cd_w_in[j], mla_w_uq[j], mla_w_ukv[j])
            cos32, sin32 = _rope_tables(seq, MLA_ROPE)
            ones64, zeros64 = jnp.ones((seq, 64), _F32), jnp.zeros((seq, 64), _F32)
            cosq = _pad_table(jnp.concatenate([ones64, cos32, ones64[:, :32]], 1), LANE, 1.0)
            sinq = _pad_table(jnp.concatenate([zeros64, sin32, zeros64[:, :32]], 1), LANE, 0.0)
            tabs = (cosq, sinq, cos64p, sin64p)
            qg = mla_q_norm_g[j].reshape(1, -1)
            kvg = mla_kv_norm_g[j].reshape(1, -1)
            qmt, km, vmt, dqt, dk, dvt = _proj_cd(xs, mod, lat_row, g1, tabs, lat_tab, w, qg, kvg, wqa, wqb, wk, wv, place)
            _, ckm, cvmt, _, cdk, cdvt = _proj_cd(cs, mod, ctx_row, g1, tabs, ctx_tab, w, qg, kvg, wqa, wqb, wk, wv, place)
            mla_heads = [(h, None, h, h // 2) for h in range(8)]
            pair_outs = [("pair", 2 * m, 2 * m + 1) for m in range(4)]
            yc = _flasht_attention(qmt, km, vmt, ckm, cvmt, heads=mla_heads, outs=pair_outs, batch=batch, seq=seq,
                                   ctx_len=ctx_len, tq=GLOBAL_TQ, tk=GLOBAL_TK, name="mla_attn")
            diff_heads = [(h, a, h, h) for h in range(4) for a in range(2)]
            diff_outs = [("diff", 2 * h, 2 * h + 1) for h in range(4)]
            od = _flasht_attention(dqt, dk, dvt, cdk, cdvt, heads=diff_heads, outs=diff_outs, batch=batch, seq=seq,
                                   ctx_len=ctx_len, tq=GLOBAL_TQ, tk=GLOBAL_TK,
                                   diff=(diff_lambda[j].astype(_F32), diff_subln_g[j].reshape(1, -1), lam_init),
                                   name="diff_attn")
            xs, h2 = _out_proj(xs, yc, od, mod, lat_row, g2, wo[:512], wo[512:])
        fg = final_norm_g.reshape(1, d) if last else None
        xs = _peer(h2, xs, mod, lambda i: i // (seq // PEER_TM), wqt, keys, u, vt, final_g=fg)
        if not last:
            cs = _peer(h2c, cs, mod, ctx_row, wqt, keys, u, vt)
    return xs.reshape(batch, seq, d)
```

```python
import functools
import math

import jax
import jax.numpy as jnp
from jax import lax
from jax.experimental import pallas as pl
from jax.experimental.pallas import tpu as pltpu

_F32 = jnp.float32
_MXU = jnp.bfloat16
_NEG = -1e30
_LOG2E = math.log2(math.e)

GRID_W = 64
HEAD_DIM = 64
ROPE_BASE = 10000.0
RMS_EPS = 1e-6
N_MOD = 6
NA_KR, NA_KC = 8, 16
SWA_WINDOW = 128
MLA_NOPE, MLA_ROPE = 64, 32
PEER_HEADS, PEER_NKEYS, PEER_TOPK = 8, 128, 16

LANE = 128
TM = 1024
TQ_LOCAL = 256
NA_GROUP_ROWS = TQ_LOCAL // GRID_W
NA_WIN_ROWS = NA_KR + NA_GROUP_ROWS - 1
GLOBAL_TQ = 1024
GLOBAL_TK = 1024
PEER_TM = 1024
PEER_EB = 512
MXU_DEPTH = 256
PEER_MXU_ROWS = 512
PEER_SELECT_UNROLL = 8
VMEM_LIMIT = 56 * 1024 * 1024
PEER_VMEM_LIMIT = 56 * 1024 * 1024


def _cparams(sem, vmem=VMEM_LIMIT):
    return pltpu.CompilerParams(dimension_semantics=sem, vmem_limit_bytes=vmem)


def _dot(a, b):
    return jnp.dot(a, b, preferred_element_type=_F32)


def _dot_nt(a, b):
    return lax.dot_general(a, b, (((1,), (1,)), ((), ())), preferred_element_type=_F32)


def _rms(x, g):
    return x * lax.rsqrt(jnp.mean(x * x, axis=-1, keepdims=True) + RMS_EPS) * g


def _lane_half(shape):
    return lax.broadcasted_iota(jnp.int32, shape, len(shape) - 1) >= (LANE // 2)


def _mod_kernel(c_ref, w_ref, b_ref, o_ref):
    c = c_ref[...]
    a = c * jax.nn.sigmoid(c)
    w = w_ref[0]
    a_hi = a.astype(_MXU)
    a_lo = (a - a_hi.astype(_F32)).astype(_MXU)
    w_hi = w.astype(_MXU)
    w_lo = (w - w_hi.astype(_F32)).astype(_MXU)
    o_ref[0] = _dot(a_hi, w_hi) + _dot(a_lo, w_hi) + _dot(a_hi, w_lo) + b_ref[0]


def _modulation(cc, ada_w, ada_b):
    depth, d, n = ada_w.shape
    rows = cc.shape[0]
    tn = 768
    return pl.pallas_call(
        _mod_kernel,
        grid=(depth, n // tn),
        in_specs=[pl.BlockSpec((rows, d), lambda l, j: (0, 0)),
                  pl.BlockSpec((1, d, tn), lambda l, j: (l, 0, j)),
                  pl.BlockSpec((1, 1, tn), lambda l, j: (l, 0, j))],
        out_specs=pl.BlockSpec((1, rows, tn), lambda l, j: (l, 0, j)),
        out_shape=jax.ShapeDtypeStruct((depth, rows, n), _F32),
        compiler_params=_cparams(("parallel", "parallel")),
        name="adaln_mod",
    )(cc, ada_w, ada_b.reshape(depth, 1, n))


def _rope_tables(seq, d):
    t = jnp.arange(seq, dtype=jnp.int32)
    q = d // 4
    freq = ROPE_BASE ** (-jnp.arange(q, dtype=_F32) / q)

    def one(pos):
        ang = pos.astype(_F32)[:, None] * freq[None, :]
        return jnp.concatenate([jnp.cos(ang)] * 2, -1), jnp.concatenate([jnp.sin(ang)] * 2, -1)

    cr, sr = one(t // GRID_W)
    cc, sc = one(t % GRID_W)
    return jnp.concatenate([cr, cc], -1), jnp.concatenate([sr, sc], -1)


def _rot_cols(w, d):
    k, n = w.shape
    w5 = w.reshape(k, n // d, 2, 2, d // 4)
    return jnp.stack([-w5[:, :, :, 1], w5[:, :, :, 0]], axis=3).reshape(k, n)


def _pad_table(tab, width, ident):
    s, w = tab.shape
    if w < width:
        reps = width // w
        tab = jnp.tile(tab, (1, reps))
    return jnp.concatenate([tab, jnp.full((TM, width), ident, _F32)], axis=0)


def _proj_ab_kernel(x_ref, mod_ref, g_ref, cos_ref, sin_ref, w_ref,
                    aq_ref, ak_ref, av_ref, bq_ref, bk_ref, bv_ref):
    x = x_ref[...]
    h = _rms(x, g_ref[...]) * (1.0 + mod_ref[0, 1:2, :]) + mod_ref[0, 0:1, :]
    hb = h.astype(_MXU)
    cos = cos_ref[...]
    sin = sin_ref[...]

    def proj(lo, n):
        return _dot(hb, w_ref[:, lo:lo + n])

    scale = HEAD_DIM ** -0.5
    aq_ref[...] = (proj(0, 512) * scale).astype(aq_ref.dtype)
    ak_ref[...] = proj(512, 512).astype(ak_ref.dtype)
    av_ref[...] = proj(1024, 512).astype(av_ref.dtype)
    cos4 = jnp.tile(cos, (1, 4))
    sin4 = jnp.tile(sin, (1, 4))
    bq_ref[...] = ((proj(1536, 512) * cos4 + proj(2048, 512) * sin4) * scale).astype(bq_ref.dtype)
    cos2 = jnp.tile(cos, (1, 2))
    sin2 = jnp.tile(sin, (1, 2))
    bk_ref[...] = (proj(2560, 256) * cos2 + proj(2816, 256) * sin2).astype(bk_ref.dtype)
    bv_ref[...] = proj(3072, 256).astype(bv_ref.dtype)


def _proj_ab(x, mod, mod_row, g, cos, sin, tab_row, w):
    n, d = x.shape
    widths = (512, 512, 512, 512, 256, 256)
    return pl.pallas_call(
        _proj_ab_kernel,
        grid=(n // TM,),
        in_specs=[pl.BlockSpec((TM, d), lambda i: (i, 0)),
                  pl.BlockSpec((1, N_MOD, d), lambda i: (mod_row(i), 0, 0)),
                  pl.BlockSpec((1, d), lambda i: (0, 0)),
                  pl.BlockSpec((TM, LANE), lambda i: (tab_row(i), 0)),
                  pl.BlockSpec((TM, LANE), lambda i: (tab_row(i), 0)),
                  pl.BlockSpec(w.shape, lambda i: (0, 0))],
        out_specs=[pl.BlockSpec((TM, wd), lambda i: (i, 0)) for wd in widths],
        out_shape=[jax.ShapeDtypeStruct((n, wd), _MXU) for wd in widths],
        compiler_params=_cparams(("parallel",)),
        name="proj_ab",
    )(x, mod, g, cos, sin, w)


def _na_kernel(q_ref, k_ref, v_ref, kc_ref, vc_ref, bias_ref, o_ref, *, rows):
    g = pl.program_id(1)
    ks = jnp.clip(NA_GROUP_ROWS * g - NA_KR // 2, 0, rows - NA_WIN_ROWS)
    start = pl.multiple_of(ks * GRID_W, GRID_W)
    nwin = NA_WIN_ROWS * GRID_W
    hi = _lane_half((TQ_LOCAL, LANE))

    def scores(h):
        cols = slice((h // 2) * LANE, (h // 2 + 1) * LANE)
        q2 = q_ref[:, cols]
        qh = jnp.where(hi == (h % 2 == 1), q2, jnp.zeros_like(q2))
        return _dot_nt(qh, k_ref[pl.ds(start, nwin), cols]) + bias_ref[0, h], _dot_nt(qh, kc_ref[:, cols])

    outs = []
    nxt = scores(0)
    for h in range(8):
        s_lat, s_ctx = nxt
        if h + 1 < 8:
            nxt = scores(h + 1)
        cols = slice((h // 2) * LANE, (h // 2 + 1) * LANE)
        mx = jnp.maximum(jnp.max(s_lat, axis=-1, keepdims=True), jnp.max(s_ctx, axis=-1, keepdims=True))
        p_lat = jnp.exp(s_lat - mx)
        p_ctx = jnp.exp(s_ctx - mx)
        den = jnp.sum(p_lat, axis=-1, keepdims=True) + jnp.sum(p_ctx, axis=-1, keepdims=True)
        o = _dot(p_lat.astype(_MXU), v_ref[pl.ds(start, nwin), cols]) + _dot(p_ctx.astype(_MXU), vc_ref[:, cols])
        outs.append(o / den)
        if h % 2 == 1:
            o_ref[:, cols] = jnp.where(hi, outs[h], outs[h - 1]).astype(o_ref.dtype)


def _na_bias_tables(rpb, rows):
    gq = NA_GROUP_ROWS
    n_groups = rows // gq
    ql = jnp.arange(gq)[:, None]
    kl = jnp.arange(NA_WIN_ROWS)[None, :]
    qc = jnp.arange(GRID_W)[:, None]
    kc = jnp.arange(GRID_W)[None, :]
    cs = jnp.clip(qc - NA_KC // 2, 0, GRID_W - NA_KC)
    col_valid = (kc >= cs) & (kc < cs + NA_KC)
    col_idx = jnp.clip(kc - qc + NA_KC - 1, 0, 2 * NA_KC - 2)
    exact = lax.Precision.HIGHEST
    oh_c = jax.nn.one_hot(col_idx, 2 * NA_KC - 1, dtype=_F32)
    rpb_cols = jnp.einsum("hab,xyb->haxy", rpb.astype(_F32), oh_c, precision=exact)
    tabs = []
    for g in (0, 1, n_groups - 1):
        ks = min(max(gq * g - NA_KR // 2, 0), rows - NA_WIN_ROWS)
        rq = gq * g + ql
        rk = ks + kl
        r0 = jnp.clip(rq - NA_KR // 2, 0, rows - NA_KR)
        row_valid = (rk >= r0) & (rk < r0 + NA_KR)
        row_idx = jnp.clip(rk - rq + NA_KR - 1, 0, 2 * NA_KR - 2)
        oh_r = jax.nn.one_hot(row_idx, 2 * NA_KR - 1, dtype=_F32)
        b = jnp.einsum("qka,haxy->hqxky", oh_r, rpb_cols, precision=exact)
        valid = row_valid[:, None, :, None] & col_valid[None, :, None, :]
        b = jnp.where(valid[None], b, _NEG)
        tabs.append(b.reshape(rpb.shape[0], gq * GRID_W, NA_WIN_ROWS * GRID_W))
    return jnp.stack(tabs)


def _na_attention(aq, ak, av, cak, cav, bias, batch, seq, ctx_len):
    rows = seq // GRID_W
    n_groups = seq // TQ_LOCAL
    nwin = NA_WIN_ROWS * GRID_W

    def bias_row(b, g):
        return (jnp.where(g == 0, 0, jnp.where(g == n_groups - 1, 2, 1)), 0, 0, 0)

    return pl.pallas_call(
        functools.partial(_na_kernel, rows=rows),
        grid=(batch, n_groups),
        in_specs=[pl.BlockSpec((TQ_LOCAL, 512), lambda b, g: (b * n_groups + g, 0)),
                  pl.BlockSpec((seq, 512), lambda b, g: (b, 0)),
                  pl.BlockSpec((seq, 512), lambda b, g: (b, 0)),
                  pl.BlockSpec((ctx_len, 512), lambda b, g: (b, 0)),
                  pl.BlockSpec((ctx_len, 512), lambda b, g: (b, 0)),
                  pl.BlockSpec((1, 8, TQ_LOCAL, nwin), bias_row)],
        out_specs=pl.BlockSpec((TQ_LOCAL, 512), lambda b, g: (b * n_groups + g, 0)),
        out_shape=jax.ShapeDtypeStruct((batch * seq, 512), _MXU),
        compiler_params=_cparams(("parallel", "arbitrary")),
        name="na_attention",
    )(aq, ak, av, cak, cav, bias)


def _swa_kernel(sink_ref, q_ref, k_ref, v_ref, kc_ref, vc_ref, o_ref, *, seq):
    t = pl.program_id(1)
    kwin = TQ_LOCAL + 2 * SWA_WINDOW
    start = t * TQ_LOCAL
    kstart = pl.multiple_of(jnp.clip(start - SWA_WINDOW, 0, seq - kwin), LANE)
    qpos = start + lax.broadcasted_iota(jnp.int32, (TQ_LOCAL, kwin), 0)
    kpos = kstart + lax.broadcasted_iota(jnp.int32, (TQ_LOCAL, kwin), 1)
    mask = jnp.where(jnp.abs(kpos - qpos) <= SWA_WINDOW, 0.0, _NEG).astype(_F32)
    hi = _lane_half((TQ_LOCAL, LANE))

    def scores(h):
        cols = slice((h // 2) * LANE, (h // 2 + 1) * LANE)
        kcols = slice((h // 4) * LANE, (h // 4 + 1) * LANE)
        q2 = q_ref[:, cols]
        qh = jnp.where(hi == (h % 2 == 1), q2, jnp.zeros_like(q2))
        return _dot_nt(qh, k_ref[pl.ds(kstart, kwin), kcols]) + mask, _dot_nt(qh, kc_ref[:, kcols])

    outs = []
    nxt = scores(0)
    for h in range(8):
        s_lat, s_ctx = nxt
        if h + 1 < 8:
            nxt = scores(h + 1)
        cols = slice((h // 2) * LANE, (h // 2 + 1) * LANE)
        kcols = slice((h // 4) * LANE, (h // 4 + 1) * LANE)
        sink = sink_ref[h]
        mx = jnp.maximum(jnp.max(s_lat, axis=-1, keepdims=True), jnp.max(s_ctx, axis=-1, keepdims=True))
        mx = jnp.maximum(mx, sink)
        p_lat = jnp.exp(s_lat - mx)
        p_ctx = jnp.exp(s_ctx - mx)
        den = (jnp.sum(p_lat, axis=-1, keepdims=True) + jnp.sum(p_ctx, axis=-1, keepdims=True)
               + jnp.exp(sink - mx))
        o = (_dot(p_lat.astype(_MXU), v_ref[pl.ds(kstart, kwin), kcols])
             + _dot(p_ctx.astype(_MXU), vc_ref[:, kcols]))
        outs.append(o / den)
        if h % 2 == 1:
            o_ref[:, cols] = jnp.where(hi, outs[h], outs[h - 1]).astype(o_ref.dtype)


def _swa_attention(sink, bq, bk2, bv2, cbk2, cbv2, batch, seq, ctx_len):
    n_t = seq // TQ_LOCAL
    return pl.pallas_call(
        functools.partial(_swa_kernel, seq=seq),
        grid=(batch, n_t),
        in_specs=[pl.BlockSpec(memory_space=pltpu.SMEM),
                  pl.BlockSpec((TQ_LOCAL, 512), lambda b, t: (b * n_t + t, 0)),
                  pl.BlockSpec((seq, 256), lambda b, t: (b, 0)),
                  pl.BlockSpec((seq, 256), lambda b, t: (b, 0)),
                  pl.BlockSpec((ctx_len, 256), lambda b, t: (b, 0)),
                  pl.BlockSpec((ctx_len, 256), lambda b, t: (b, 0))],
        out_specs=pl.BlockSpec((TQ_LOCAL, 512), lambda b, t: (b * n_t + t, 0)),
        out_shape=jax.ShapeDtypeStruct((batch * seq, 512), _MXU),
        compiler_params=_cparams(("parallel", "arbitrary")),
        name="swa_attention",
    )(sink, bq, bk2, bv2, cbk2, cbv2)


def _flash_kernel(*refs, heads, outs, has_ctx, has_sink, diff_cfg, n_kv):
    it = iter(refs)
    sink_ref = next(it) if has_sink else None
    q_ref, k_ref, v_ref = next(it), next(it), next(it)
    kc_ref = next(it) if has_ctx else None
    vc_ref = next(it) if has_ctx else None
    lam_ref = next(it) if diff_cfg else None
    sg_ref = next(it) if diff_cfg else None
    o_ref = next(it)
    m_ref, l_ref, acc_ref = next(it), next(it), next(it)
    kv = pl.program_id(2)
    tq = q_ref.shape[0]
    hi = _lane_half((tq, LANE))

    @pl.when(kv == 0)
    def _init():
        for h in range(len(heads)):
            if has_sink:
                m_ref[h] = jnp.full((tq, LANE), sink_ref[h], _F32)
                l_ref[h] = jnp.ones((tq, LANE), _F32)
            else:
                m_ref[h] = jnp.full((tq, LANE), _NEG, _F32)
                l_ref[h] = jnp.zeros((tq, LANE), _F32)
            acc_ref[h] = jnp.zeros((tq, LANE), _F32)

    def attend(kr, vr):
        for h, (qb, qhalf, kb, vb) in enumerate(heads):
            q2 = q_ref[:, qb * LANE:(qb + 1) * LANE]
            if qhalf is not None:
                q2 = jnp.where(hi == (qhalf == 1), q2, jnp.zeros_like(q2))
            s = _dot_nt(q2, kr[:, kb * LANE:(kb + 1) * LANE])
            m_old = m_ref[h][:, :1]
            m_new = jnp.maximum(m_old, jnp.max(s, axis=-1, keepdims=True))
            alpha = jnp.exp(m_old - m_new)
            p = jnp.exp(s - m_new)
            l_ref[h] = jnp.broadcast_to(alpha * l_ref[h][:, :1] + jnp.sum(p, axis=-1, keepdims=True), (tq, LANE))
            acc_ref[h] = alpha * acc_ref[h] + _dot(p.astype(_MXU), vr[:, vb * LANE:(vb + 1) * LANE])
            m_ref[h] = jnp.broadcast_to(m_new, (tq, LANE))

    if has_ctx:
        @pl.when(kv == 0)
        def _ctx():
            attend(kc_ref, vc_ref)

    attend(k_ref, v_ref)

    @pl.when(kv == n_kv - 1)
    def _fin():
        def head_out(h):
            return acc_ref[h] / l_ref[h][:, :1]

        for j, spec in enumerate(outs):
            if spec[0] == "full":
                o = head_out(spec[1])
            elif spec[0] == "pair":
                o = jnp.where(hi, head_out(spec[2]), head_out(spec[1]))
            else:
                lv = lam_ref[...]
                lam = (jnp.exp(jnp.sum(lv[0:1] * lv[1:2], axis=-1, keepdims=True))
                       - jnp.exp(jnp.sum(lv[2:3] * lv[3:4], axis=-1, keepdims=True)) + diff_cfg)
                o = _rms(head_out(spec[1]) - lam * head_out(spec[2]), sg_ref[...]) * (1.0 - diff_cfg)
            o_ref[:, j * LANE:(j + 1) * LANE] = o.astype(o_ref.dtype)


def _flash_attention(q, k, v, *, heads, outs, batch, q_per_batch, kv_per_batch, tq, tk,
                     q_row0=0, kv_row0=0, ctx=None, ctx_len=0, ctx_row0=0, sink=None, diff=None, name="flash"):
    n_q = q_per_batch // tq
    n_kv = kv_per_batch // tk
    qw, kw, vw = q.shape[1], k.shape[1], v.shape[1]
    args, in_specs = [], []
    if sink is not None:
        args.append(sink)
        in_specs.append(pl.BlockSpec(memory_space=pltpu.SMEM))
    q0, k0 = q_row0 // tq, kv_row0 // tk
    args += [q, k, v]
    in_specs += [pl.BlockSpec((tq, qw), lambda b, i, j: (q0 + b * n_q + i, 0)),
                 pl.BlockSpec((tk, kw), lambda b, i, j: (k0 + b * n_kv + j, 0)),
                 pl.BlockSpec((tk, vw), lambda b, i, j: (k0 + b * n_kv + j, 0))]
    if ctx is not None:
        c0 = ctx_row0 // ctx_len
        args += [ctx[0], ctx[1]]
        in_specs += [pl.BlockSpec((ctx_len, kw), lambda b, i, j: (c0 + b, 0)),
                     pl.BlockSpec((ctx_len, vw), lambda b, i, j: (c0 + b, 0))]
    diff_cfg = None
    if diff is not None:
        lam_vecs, subln_g, diff_cfg = diff
        args += [lam_vecs, subln_g]
        in_specs += [pl.BlockSpec(lam_vecs.shape, lambda b, i, j: (0, 0)),
                     pl.BlockSpec(subln_g.shape, lambda b, i, j: (0, 0))]
    nh = len(heads)
    ow = len(outs) * LANE
    return pl.pallas_call(
        functools.partial(_flash_kernel, heads=tuple(heads), outs=tuple(outs), has_ctx=ctx is not None,
                          has_sink=sink is not None, diff_cfg=diff_cfg, n_kv=n_kv),
        grid=(batch, n_q, n_kv),
        in_specs=in_specs,
        out_specs=pl.BlockSpec((tq, ow), lambda b, i, j: (b * n_q + i, 0)),
        out_shape=jax.ShapeDtypeStruct((batch * q_per_batch, ow), _MXU),
        scratch_shapes=[pltpu.VMEM((nh, tq, LANE), _F32)] * 3,
        compiler_params=_cparams(("parallel", "parallel", "arbitrary")),
        name=name,
    )(*args)


def _flasht_kernel(*refs, heads, outs, diff_cfg, n_kv):
    it = iter(refs)
    qt_ref, k_ref, vt_ref, kc_ref, vct_ref = next(it), next(it), next(it), next(it), next(it)
    lam_ref = next(it) if diff_cfg else None
    sg_ref = next(it) if diff_cfg else None
    o_ref = next(it)
    m_ref, l_ref, acc_ref = next(it), next(it), next(it)
    kv = pl.program_id(2)
    tq = qt_ref.shape[1]
    row_hi = lax.broadcasted_iota(jnp.int32, (LANE, tq), 0) >= (LANE // 2)

    @pl.when(kv == 0)
    def _init():
        m_ref[...] = jnp.full(m_ref.shape, _NEG, _F32)
        l_ref[...] = jnp.zeros(l_ref.shape, _F32)
        acc_ref[...] = jnp.zeros(acc_ref.shape, _F32)

    def attend(kr, vtr):
        def scores(h):
            qb, qhalf, kb, _ = heads[h]
            qt = qt_ref[qb * LANE:(qb + 1) * LANE, :]
            if qhalf is not None:
                qt = jnp.where(row_hi == (qhalf == 1), qt, jnp.zeros_like(qt))
            return _dot(kr[:, kb * LANE:(kb + 1) * LANE], qt)

        s_next = scores(0)
        for h in range(len(heads)):
            s = s_next
            if h + 1 < len(heads):
                s_next = scores(h + 1)
            vb = heads[h][3]
            m_old = m_ref[h, 0:1, :]
            m_new = jnp.maximum(m_old, jnp.max(s, axis=0, keepdims=True))
            alpha = jnp.exp2(m_old - m_new)
            p = jnp.exp2(s - m_new)
            l_new = alpha * l_ref[h, 0:1, :] + jnp.sum(p, axis=0, keepdims=True)
            acc_ref[h] = alpha * acc_ref[h] + _dot(vtr[vb * LANE:(vb + 1) * LANE, :], p.astype(_MXU))
            l_ref[h] = jnp.broadcast_to(l_new, (8, tq))
            m_ref[h] = jnp.broadcast_to(m_new, (8, tq))

    @pl.when(kv == 0)
    def _ctx():
        attend(kc_ref, vct_ref)

    attend(k_ref, vt_ref)

    @pl.when(kv == n_kv - 1)
    def _fin():
        def head_out(h):
            return acc_ref[h] / l_ref[h, 0:1, :]

        for j, spec in enumerate(outs):
            if spec[0] == "full":
                o = head_out(spec[1]).T
            elif spec[0] == "pair":
                o = jnp.where(row_hi, head_out(spec[2]), head_out(spec[1])).T
            else:
                lv = lam_ref[...]
                lam = (jnp.exp(jnp.sum(lv[0:1] * lv[1:2], axis=-1, keepdims=True))
                       - jnp.exp(jnp.sum(lv[2:3] * lv[3:4], axis=-1, keepdims=True)) + diff_cfg)
                o = _rms((head_out(spec[1]) - lam * head_out(spec[2])).T, sg_ref[...]) * (1.0 - diff_cfg)
            o_ref[:, j * LANE:(j + 1) * LANE] = o.astype(o_ref.dtype)


def _flasht_attention(qt, k, vt, kc, vct, *, heads, outs, batch, seq, ctx_len, tq, tk, diff=None, name="flasht"):
    n_q, n_kv = seq // tq, seq // tk
    qw, kw, vw = qt.shape[0], k.shape[1], vt.shape[0]
    args = [qt, k, vt, kc, vct]
    in_specs = [pl.BlockSpec((qw, tq), lambda b, i, j: (0, b * n_q + i)),
                pl.BlockSpec((tk, kw), lambda b, i, j: (b * n_kv + j, 0)),
                pl.BlockSpec((vw, tk), lambda b, i, j: (0, b * n_kv + j)),
                pl.BlockSpec((ctx_len, kw), lambda b, i, j: (b, 0)),
                pl.BlockSpec((vw, ctx_len), lambda b, i, j: (0, b))]
    diff_cfg = None
    if diff is not None:
        lam_vecs, subln_g, diff_cfg = diff
        args += [lam_vecs, subln_g]
        in_specs += [pl.BlockSpec(lam_vecs.shape, lambda b, i, j: (0, 0)),
                     pl.BlockSpec(subln_g.shape, lambda b, i, j: (0, 0))]
    nh = len(heads)
    ow = len(outs) * LANE
    return pl.pallas_call(
        functools.partial(_flasht_kernel, heads=tuple(heads), outs=tuple(outs), diff_cfg=diff_cfg, n_kv=n_kv),
        grid=(batch, n_q, n_kv),
        in_specs=in_specs,
        out_specs=pl.BlockSpec((tq, ow), lambda b, i, j: (b * n_q + i, 0)),
        out_shape=jax.ShapeDtypeStruct((batch * seq, ow), _MXU),
        scratch_shapes=[pltpu.VMEM((nh, 8, tq), _F32), pltpu.VMEM((nh, 8, tq), _F32),
                        pltpu.VMEM((nh, LANE, tq), _F32)],
        compiler_params=_cparams(("parallel", "parallel", "arbitrary")),
        name=name,
    )(*args)


def _out_kernel(x_ref, ya_ref, yb_ref, mod_ref, g2_ref, wa_ref, wb_ref, xo_ref, h2_ref):
    y = _dot(ya_ref[...], wa_ref[...]) + _dot(yb_ref[...], wb_ref[...])
    x = x_ref[...] + mod_ref[0, 2:3, :] * y
    xo_ref[...] = x
    h2 = _rms(x, g2_ref[...]) * (1.0 + mod_ref[0, 4:5, :]) + mod_ref[0, 3:4, :]
    h2_ref[...] = h2.astype(h2_ref.dtype)


def _out_proj(x, ya, yb, mod, mod_row, g2, wa, wb):
    n, d = x.shape
    return pl.pallas_call(
        _out_kernel,
        grid=(n // TM,),
        in_specs=[pl.BlockSpec((TM, d), lambda i: (i, 0)),
                  pl.BlockSpec((TM, ya.shape[1]), lambda i: (i, 0)),
                  pl.BlockSpec((TM, yb.shape[1]), lambda i: (i, 0)),
                  pl.BlockSpec((1, N_MOD, d), lambda i: (mod_row(i), 0, 0)),
                  pl.BlockSpec((1, d), lambda i: (0, 0)),
                  pl.BlockSpec(wa.shape, lambda i: (0, 0)),
                  pl.BlockSpec(wb.shape, lambda i: (0, 0))],
        out_specs=[pl.BlockSpec((TM, d), lambda i: (i, 0)), pl.BlockSpec((TM, d), lambda i: (i, 0))],
        out_shape=[jax.ShapeDtypeStruct((n, d), _F32), jax.ShapeDtypeStruct((n, d), _MXU)],
        compiler_params=_cparams(("parallel",)),
        name="out_proj",
    )(x, ya, yb, mod, g2, wa, wb)


def _proj_cd_kernel(x_ref, mod_ref, g_ref, cq_ref, sq_ref, cd_ref, sd_ref, w_ref, qg_ref, kvg_ref,
                    wqa_ref, wqb_ref, wk_ref, wv_ref, place_ref,
                    qm_ref, km_ref, vm_ref, dq_ref, dk_ref, dv_ref):
    x = x_ref[...]
    h = _rms(x, g_ref[...]) * (1.0 + mod_ref[0, 1:2, :]) + mod_ref[0, 0:1, :]
    hb = h.astype(_MXU)

    def proj(lo, n):
        return _dot(hb, w_ref[:, lo:lo + n])

    cosq, sinq = cq_ref[...], sq_ref[...]
    cosd, sind = cd_ref[...], sd_ref[...]
    cqn = _rms(proj(0, 256), qg_ref[...]).astype(_MXU)
    ckvn = _rms(proj(256, 128), kvg_ref[...]).astype(_MXU)
    cos8, sin8 = jnp.tile(cosq, (1, 8)), jnp.tile(sinq, (1, 8))
    qm = _dot(cqn, wqa_ref[...]) * cos8 + _dot(cqn, wqb_ref[...]) * sin8
    qm_ref[...] = (qm * ((MLA_NOPE + MLA_ROPE) ** -0.5 * _LOG2E)).T.astype(qm_ref.dtype)
    kr = (proj(384, 128) * cosq + proj(512, 128) * sinq).astype(_MXU)
    km_ref[...] = (_dot(ckvn, wk_ref[...]) + _dot(kr, place_ref[...])).astype(km_ref.dtype)
    vm_ref[...] = _dot(ckvn, wv_ref[...]).T.astype(vm_ref.dtype)
    cos4, sin4 = jnp.tile(cosd, (1, 4)), jnp.tile(sind, (1, 4))
    dq_ref[...] = ((proj(640, 512) * cos4 + proj(1152, 512) * sin4) * (HEAD_DIM ** -0.5 * _LOG2E)).T.astype(dq_ref.dtype)
    dk_ref[...] = (proj(1664, 512) * cos4 + proj(2176, 512) * sin4).astype(dk_ref.dtype)
    dv_ref[...] = proj(2688, 512).T.astype(dv_ref.dtype)


def _proj_cd(x, mod, mod_row, g, tabs, tab_row, w, qg, kvg, wqa, wqb, wk, wv, place):
    n, d = x.shape
    outs = ((1024, True), (1024, False), (512, True), (512, True), (512, False), (512, True))
    full = lambda a: pl.BlockSpec(a.shape, lambda i: (0, 0))
    return pl.pallas_call(
        _proj_cd_kernel,
        grid=(n // TM,),
        in_specs=[pl.BlockSpec((TM, d), lambda i: (i, 0)),
                  pl.BlockSpec((1, N_MOD, d), lambda i: (mod_row(i), 0, 0)),
                  pl.BlockSpec((1, d), lambda i: (0, 0))]
                 + [pl.BlockSpec((TM, LANE), lambda i: (tab_row(i), 0))] * 4
                 + [full(a) for a in (w, qg, kvg, wqa, wqb, wk, wv, place)],
        out_specs=[pl.BlockSpec((wd, TM), lambda i: (0, i)) if fm else pl.BlockSpec((TM, wd), lambda i: (i, 0))
                   for wd, fm in outs],
        out_shape=[jax.ShapeDtypeStruct((wd, n) if fm else (n, wd), _MXU) for wd, fm in outs],
        compiler_params=_cparams(("parallel",)),
        name="proj_cd",
    )(x, mod, g, *tabs, w, qg, kvg, wqa, wqb, wk, wv, place)


_MARK = 2.0 ** 121


def _extract16(x, first_only):
    r = x.shape[0]
    row = lax.broadcasted_iota(jnp.int32, x.shape, 0)
    row16 = lax.broadcasted_iota(jnp.int32, (PEER_TOPK, LANE), 0)
    vals = jnp.zeros((PEER_TOPK, LANE), _F32)
    for k in range(PEER_TOPK):
        m = jnp.max(x, axis=0, keepdims=True)
        hit = x == m
        if first_only:
            hit = row == jnp.min(jnp.where(hit, row, r), axis=0, keepdims=True)
        x = jnp.where(hit, -_MARK * (32 + k), x)
        vals = jnp.where(row16 == k, m, vals)
    order = jnp.where(x < -16.0 * _MARK, x * (-1.0 / _MARK) - 32.0, float(PEER_TOPK))
    return order, vals


def _staircase_rows():
    groups = []
    for r in range(PEER_TOPK // 2):
        n = PEER_TOPK // (r + 1)
        for c0 in range(0, n, 8):
            groups.append((r, c0, min(8, n - c0)))
    groups.append((None, 0, 8))
    return groups


def _peer_select(sa, sb, first_only):
    ra, av = _extract16(sa, first_only)
    rb, bv = _extract16(sb, first_only)
    groups = _staircase_rows()
    sub = lax.broadcasted_iota(jnp.int32, (8, LANE), 0)
    pieces = []
    for r, c0, nv in groups:
        if r is None:
            piece = av[8:16] + bv[0:1]
        else:
            piece = av[r:r + 1] + bv[c0:c0 + 8]
            if nv < 8:
                piece = jnp.where(sub < nv, piece, -jnp.inf)
        pieces.append(piece)
    cand = jnp.concatenate(pieces, axis=0)
    e_cand = jnp.exp(cand - (av[0:1] + bv[0:1]))
    sel, _ = _extract16(cand, first_only)
    sel = jnp.where(sel < float(PEER_TOPK), 1.0, 0.0)
    z = jnp.sum(sel * e_cand, axis=0, keepdims=True)
    lr = jnp.zeros(sa.shape, _F32)
    for g, (r, c0, nv) in enumerate(groups):
        blk = sel[8 * g:8 * g + 8]
        if r is None:
            for q in range(8):
                lr = jnp.where(ra == float(8 + q), blk[q:q + 1], lr)
        elif c0 == 0:
            cnt = jnp.sum(blk, axis=0, keepdims=True)
            if PEER_TOPK // (r + 1) > 8:
                cnt = cnt + jnp.sum(sel[8 * g + 8:8 * g + 16], axis=0, keepdims=True)
            lr = jnp.where(ra == float(r), cnt, lr)
    ea = jnp.exp(sa - av[0:1]) / z
    eb = jnp.exp(sb - bv[0:1])
    n_sel = (jnp.sum(jnp.where(ra < float(PEER_TOPK), 1.0, 0.0), axis=0, keepdims=True)
             + jnp.sum(jnp.where(rb < float(PEER_TOPK), 1.0, 0.0), axis=0, keepdims=True)
             + jnp.sum(sel, axis=0, keepdims=True))
    return lr, ea, rb, eb, n_sel


def _peer_kernel(h2_ref, x_ref, mod_ref, wqt_ref, keys_ref, u_ref, vt_ref, *rest, n_eb, final):
    fg_ref = rest[0] if final else None
    (o_ref, h2t_ref, lr_ref, ea_ref, rb_ref, eb_ref, ft_ref, a0_ref, a1_ref, w0_ref, w1_ref,
     flag_ref) = rest[1:] if final else rest
    s = pl.program_id(1)
    n_chunks = PEER_TM // LANE
    i_per = PEER_EB // PEER_NKEYS
    gdt = rb_ref.dtype

    @pl.when(s == 0)
    def _select():
        h2t = h2_ref[...].astype(_F32).T.astype(_MXU)
        h2t_ref[...] = h2t
        ft_ref[...] = jnp.zeros_like(ft_ref)
        a1_ref[...] = jnp.zeros_like(a1_ref)
        w0_ref[...] = jnp.zeros_like(w0_ref)

        for hp in range(2 * PEER_HEADS):
            qt = _dot(wqt_ref[hp * LANE:(hp + 1) * LANE, :], h2t).astype(_MXU)
            st = _dot(keys_ref[hp % 2], qt)
            for c in range(n_chunks):
                (lr_ref if hp % 2 == 0 else ea_ref)[c, hp // 2] = st[:, c * LANE:(c + 1) * LANE]

        def unit(c, h, sa, sb, first_only):
            lr, ea, rb, eb, n_sel = _peer_select(sa, sb, first_only)
            lr_ref[c, h] = lr
            ea_ref[c, h] = ea
            rb_ref[c, h] = rb.astype(gdt)
            eb_ref[c, h] = eb.astype(gdt)
            return jnp.max(n_sel) > 3.0 * PEER_TOPK

        per_trip = PEER_SELECT_UNROLL

        def fast(t, carry):
            c, h0 = t // (PEER_HEADS // per_trip), (t % (PEER_HEADS // per_trip)) * per_trip
            for u in range(per_trip):
                tied = unit(c, h0 + u, lr_ref[c, h0 + u], ea_ref[c, h0 + u], False)
                flag_ref[c * PEER_HEADS + h0 + u] = tied.astype(jnp.int32)
            return carry

        lax.fori_loop(0, n_chunks * PEER_HEADS // per_trip, fast, 0)

        def exact(t, carry):
            @pl.when(flag_ref[t] != 0)
            def _redo():
                c, h = t // PEER_HEADS, t % PEER_HEADS
                w2 = wqt_ref[pl.ds(pl.multiple_of(h * 2 * LANE, 2 * LANE), 2 * LANE), :]
                h2c = h2_ref[pl.ds(pl.multiple_of(c * LANE, LANE), LANE), :]
                qt = _dot_nt(w2, h2c).astype(_MXU)
                unit(c, h, _dot(keys_ref[0], qt[:LANE]), _dot(keys_ref[1], qt[LANE:]), True)
            return carry

        lax.fori_loop(0, n_chunks * PEER_HEADS, exact, 0)

    def step(half, a_new, a_prev, w_new, w_prev):
        blk = jnp.clip(2 * s + half - 1, 0, n_eb - 1)
        e0 = half * PEER_EB

        def gate(ii, c):
            i = blk * i_per + ii
            rows = slice(ii * PEER_NKEYS, (ii + 1) * PEER_NKEYS)
            cols = slice(c * LANE, (c + 1) * LANE)
            g = jnp.zeros((PEER_NKEYS, LANE), gdt)
            for h in range(PEER_HEADS):
                lr_i = lr_ref[c, h, pl.ds(i, 1), :].astype(gdt)
                ea_i = ea_ref[c, h, pl.ds(i, 1), :].astype(gdt)
                g = g + jnp.where(rb_ref[c, h] < lr_i, eb_ref[c, h], jnp.zeros((), gdt)) * ea_i
            a = a_prev[rows, cols]
            gelu = 0.5 * a * (1.0 + lax.erf(a * (2.0 ** -0.5)))
            w_new[rows, cols] = gelu.astype(gdt) * g

        units = [(ii, c) for ii in range(i_per) for c in range(n_chunks)]
        kd, mr = MXU_DEPTH, PEER_MXU_ROWS
        d_model = u_ref.shape[1]
        n_pieces = (PEER_EB // mr) * (d_model // kd) + (d_model // mr) * (PEER_EB // kd)
        per = len(units) // n_pieces
        done = 0
        for r0 in range(0, PEER_EB, mr):
            acc = None
            for k0 in range(0, d_model, kd):
                part = _dot(u_ref[e0 + r0:e0 + r0 + mr, k0:k0 + kd], h2t_ref[k0:k0 + kd, :])
                acc = part if acc is None else acc + part
                for ii, c in units[done:done + per]:
                    gate(ii, c)
                done += per
            a_new[r0:r0 + mr, :] = acc
        for r0 in range(0, d_model, mr):
            acc = ft_ref[r0:r0 + mr, :]
            for k0 in range(0, PEER_EB, kd):
                acc = acc + _dot(vt_ref[0, r0:r0 + mr, e0 + k0:e0 + k0 + kd], w_prev[k0:k0 + kd, :])
                for ii, c in units[done:done + per]:
                    gate(ii, c)
                done += per
            ft_ref[r0:r0 + mr, :] = acc
        for ii, c in units[done:]:
            gate(ii, c)

    @pl.when(s <= n_eb // 2)
    def _first():
        step(0, a0_ref, a1_ref, w1_ref, w0_ref)

    @pl.when(s >= 0)
    def _second():
        step(1, a1_ref, a0_ref, w0_ref, w1_ref)

    @pl.when(s == n_eb // 2)
    def _fin():
        x = x_ref[...] + mod_ref[0, 5:6, :] * ft_ref[...].T
        if final:
            x = _rms(x, fg_ref[...])
        o_ref[...] = x


def _peer(h2, x, mod, mod_row, wqt, keys, u, vt, final_g=None):
    n, d = x.shape
    n_exp = u.shape[0]
    n_eb = n_exp // PEER_EB
    tm = PEER_TM
    n_chunks = tm // LANE
    final = final_g is not None
    once = pl.Buffered(1)
    args = [h2, x, mod, wqt, keys, u, vt]
    in_specs = [pl.BlockSpec((tm, d), lambda i, e: (i, 0), pipeline_mode=once),
                pl.BlockSpec((tm, d), lambda i, e: (i, 0), pipeline_mode=once),
                pl.BlockSpec((1, N_MOD, d), lambda i, e: (mod_row(i), 0, 0)),
                pl.BlockSpec(wqt.shape, lambda i, e: (0, 0), pipeline_mode=once),
                pl.BlockSpec(keys.shape, lambda i, e: (0, 0, 0)),
                pl.BlockSpec((2 * PEER_EB, d), lambda i, e: (jnp.minimum(e, n_eb // 2 - 1), 0)),
                pl.BlockSpec((1, d, 2 * PEER_EB), lambda i, e: (jnp.clip(e - 1, 0, n_eb // 2 - 1), 0, 0))]
    if final:
        args.append(final_g)
        in_specs.append(pl.BlockSpec((1, d), lambda i, e: (0, 0)))
    tab32 = pltpu.VMEM((n_chunks, PEER_HEADS, PEER_NKEYS, LANE), _F32)
    tab16 = pltpu.VMEM((n_chunks, PEER_HEADS, PEER_NKEYS, LANE), _MXU)
    abuf = pltpu.VMEM((PEER_EB, tm), _F32)
    wbuf = pltpu.VMEM((PEER_EB, tm), _MXU)
    return pl.pallas_call(
        functools.partial(_peer_kernel, n_eb=n_eb, final=final),
        grid=(n // tm, n_eb // 2 + 1),
        in_specs=in_specs,
        out_specs=pl.BlockSpec((tm, d), lambda i, e: (i, 0), pipeline_mode=once),
        out_shape=jax.ShapeDtypeStruct((n, d), _F32),
        scratch_shapes=[pltpu.VMEM((d, tm), _MXU), tab32, tab32, tab16, tab16,
                        pltpu.VMEM((d, tm), _F32), abuf, abuf, wbuf, wbuf,
                        pltpu.SMEM((n_chunks * PEER_HEADS,), jnp.int32)],
        compiler_params=_cparams(("parallel", "arbitrary"), PEER_VMEM_LIMIT),
        name="peer_ffn",
    )(*args)


def _dup_halves(w):
    a, b = w[:, :HEAD_DIM], w[:, HEAD_DIM:]
    return jnp.concatenate([a, a, b, b], axis=1)


def _prep_ab(w_in):
    aq, ak, av = w_in[:, 0:512], w_in[:, 512:1024], w_in[:, 1024:1536]
    bq, bk, bv = w_in[:, 1536:2048], w_in[:, 2048:2176], w_in[:, 2176:2304]
    cat = [aq, ak, av, bq, _rot_cols(bq, HEAD_DIM), _dup_halves(bk), _dup_halves(_rot_cols(bk, HEAD_DIM)),
           _dup_halves(bv)]
    return jnp.concatenate(cat, axis=1).astype(_MXU)


def _prep_cd(w_in, w_uq, w_ukv):
    d = w_in.shape[0]
    cq, ckv, kr = w_in[:, 0:256], w_in[:, 256:384], w_in[:, 384:416]
    dq, dk, dv = w_in[:, 416:928], w_in[:, 928:1440], w_in[:, 1440:1952]
    z64, z32 = jnp.zeros((d, 64), _F32), jnp.zeros((d, 32), _F32)
    kr128 = jnp.concatenate([z64, kr, z32], axis=1)
    krrot128 = jnp.concatenate([z64, _rot_cols(kr, MLA_ROPE), z32], axis=1)
    w = jnp.concatenate([cq, ckv, kr128, krrot128, dq, _rot_cols(dq, HEAD_DIM), dk, _rot_cols(dk, HEAD_DIM), dv],
                        axis=1).astype(_MXU)
    r = w_uq.shape[0]
    uq = w_uq.reshape(r, 8, MLA_NOPE + MLA_ROPE)
    nope, rope = uq[:, :, :MLA_NOPE], uq[:, :, MLA_NOPE:]
    rope_rot = _rot_cols(rope.reshape(r, 8 * MLA_ROPE), MLA_ROPE).reshape(r, 8, MLA_ROPE)
    zq = jnp.zeros((r, 8, 32), _F32)
    wqa = jnp.concatenate([nope, rope, zq], axis=2).reshape(r, 1024).astype(_MXU)
    wqb = jnp.concatenate([jnp.zeros_like(nope), rope_rot, zq], axis=2).reshape(r, 1024).astype(_MXU)
    rk = w_ukv.shape[0]
    ukv = w_ukv.reshape(rk, 8, 128)
    wk = jnp.concatenate([ukv[:, :, :MLA_NOPE], jnp.zeros((rk, 8, 64), _F32)], axis=2).reshape(rk, 1024).astype(_MXU)
    wv = ukv[:, :, MLA_NOPE:].reshape(rk, 512).astype(_MXU)
    lane = jnp.arange(LANE)
    src = (lane >= MLA_NOPE) & (lane < MLA_NOPE + MLA_ROPE)
    place = (src[:, None] & (lane[:, None] == (jnp.arange(1024)[None, :] % LANE))).astype(_MXU)
    return w, wqa, wqb, wk, wv, place


def kernel(x, c, ctx, c_ctx, ada_w, ada_b, norm1_g, norm2_g, w_out, peer_wq, peer_keys, peer_u, peer_v,
           ab_w_in, na_rpb, swa_sink, cd_w_in, mla_q_norm_g, mla_w_uq, mla_kv_norm_g, mla_w_ukv,
           diff_lambda, diff_subln_g, final_norm_g):
    batch, seq, d = x.shape
    ctx_len = ctx.shape[1]
    depth = ada_w.shape[0]
    assert seq % TM == 0 and (batch * ctx_len) % TM == 0 and seq % TQ_LOCAL == 0
    assert seq % PEER_TM == 0 and (batch * ctx_len) % PEER_TM == 0
    assert depth == 2, "even layers keep a context stream, the single odd layer is the last one"
    n_lat, n_ctx = batch * seq, batch * ctx_len
    xs = x.reshape(n_lat, d)
    cs = ctx.reshape(n_ctx, d)

    mod_rows = -(-(batch + 1) // 16) * 16
    cc = jnp.zeros((mod_rows, d), _F32).at[:batch].set(c).at[batch].set(c_ctx)
    mod_all = _modulation(cc, ada_w, ada_b)

    tiles_per_seq = seq // TM
    lat_row = lambda i: i // tiles_per_seq
    ctx_row = lambda i: batch
    lat_tab = lambda i: i % tiles_per_seq
    ctx_tab = lambda i: tiles_per_seq

    cos64, sin64 = _rope_tables(seq, HEAD_DIM)
    cos64p, sin64p = _pad_table(cos64, LANE, 1.0), _pad_table(sin64, LANE, 0.0)

    for l in range(depth):
        last = l == depth - 1
        j = l // 2
        mod = mod_all[l, :batch + 1].reshape(batch + 1, N_MOD, d)
        g1 = norm1_g[l].reshape(1, d)
        g2 = norm2_g[l].reshape(1, d)
        wo = w_out[l].astype(_MXU)
        wqt = peer_wq[l].T.astype(_MXU)
        keys = peer_keys[l].astype(_MXU)
        u = peer_u[l].astype(_MXU)
        vt = peer_v[l].astype(_MXU).reshape(-1, 2 * PEER_EB, d).transpose(0, 2, 1)
        if l % 2 == 0:
            w = _prep_ab(ab_w_in[j])
            aq, ak, av, bq, bk2, bv2 = _proj_ab(xs, mod, lat_row, g1, cos64p, sin64p, lat_tab, w)
            caq, cak, cav, cbq, cbk2, cbv2 = _proj_ab(cs, mod, ctx_row, g1, cos64p, sin64p, ctx_tab, w)
            bias = _na_bias_tables(na_rpb[j], seq // GRID_W)
            ya = _na_attention(aq, ak, av, cak, cav, bias, batch, seq, ctx_len)
            sink = swa_sink[j].astype(_F32)
            yb = _swa_attention(sink, bq, bk2, bv2, cbk2, cbv2, batch, seq, ctx_len)
            xs, h2 = _out_proj(xs, ya, yb, mod, lat_row, g2, wo[:512], wo[512:])
            if not last:
                pair_heads = [(m, half, m, m) for m in range(4) for half in range(2)]
                pair_outs = [("pair", 2 * m, 2 * m + 1) for m in range(4)]
                yca = _flash_attention(caq, cak, cav, heads=pair_heads, outs=pair_outs, batch=batch,
                                       q_per_batch=ctx_len, kv_per_batch=ctx_len, tq=ctx_len, tk=ctx_len,
                                       name="ctx_attn_a")
                gqa_heads = [(m, half, m // 2, m // 2) for m in range(4) for half in range(2)]
                ycb = _flash_attention(cbq, cbk2, cbv2, heads=gqa_heads, outs=pair_outs, batch=batch,
                                       q_per_batch=ctx_len, kv_per_batch=ctx_len, tq=ctx_len, tk=ctx_len,
                                       sink=sink, name="ctx_attn_b")
                cs, h2c = _out_proj(cs, yca, ycb, mod, ctx_row, g2, wo[:512], wo[512:])
        else:
            lam_init = 0.8 - 0.6 * math.exp(-0.3 * l)
            w, wqa, wqb, wk, wv, place = _prep_cd(cd_w_in[j], mla_w_uq[j], mla_w_ukv[j])
            cos32, sin32 = _rope_tables(seq, MLA_ROPE)
            ones64, zeros64 = jnp.ones((seq, 64), _F32), jnp.zeros((seq, 64), _F32)
            cosq = _pad_table(jnp.concatenate([ones64, cos32, ones64[:, :32]], 1), LANE, 1.0)
            sinq = _pad_table(jnp.concatenate([zeros64, sin32, zeros64[:, :32]], 1), LANE, 0.0)
            tabs = (cosq, sinq, cos64p, sin64p)
            qg = mla_q_norm_g[j].reshape(1, -1)
            kvg = mla_kv_norm_g[j].reshape(1, -1)
            qmt, km, vmt, dqt, dk, dvt = _proj_cd(xs, mod, lat_row, g1, tabs, lat_tab, w, qg, kvg, wqa, wqb, wk, wv, place)
            _, ckm, cvmt, _, cdk, cdvt = _proj_cd(cs, mod, ctx_row, g1, tabs, ctx_tab, w, qg, kvg, wqa, wqb, wk, wv, place)
            mla_heads = [(h, None, h, h // 2) for h in range(8)]
            pair_outs = [("pair", 2 * m, 2 * m + 1) for m in range(4)]
            yc = _flasht_attention(qmt, km, vmt, ckm, cvmt, heads=mla_heads, outs=pair_outs, batch=batch, seq=seq,
                                   ctx_len=ctx_len, tq=GLOBAL_TQ, tk=GLOBAL_TK, name="mla_attn")
            diff_heads = [(h, a, h, h) for h in range(4) for a in range(2)]
            diff_outs = [("diff", 2 * h, 2 * h + 1) for h in range(4)]
            od = _flasht_attention(dqt, dk, dvt, cdk, cdvt, heads=diff_heads, outs=diff_outs, batch=batch, seq=seq,
                                   ctx_len=ctx_len, tq=GLOBAL_TQ, tk=GLOBAL_TK,
                                   diff=(diff_lambda[j].astype(_F32), diff_subln_g[j].reshape(1, -1), lam_init),
                                   name="diff_attn")
            xs, h2 = _out_proj(xs, yc, od, mod, lat_row, g2, wo[:512], wo[512:])
        fg = final_norm_g.reshape(1, d) if last else None
        xs = _peer(h2, xs, mod, lambda i: i // (seq // PEER_TM), wqt, keys, u, vt, final_g=fg)
        if not last:
            cs = _peer(h2c, cs, mod, ctx_row, wqt, keys, u, vt)
    return xs.reshape(batch, seq, d)
```
